```python
import math
import jax, jax.numpy as jnp
from jax import lax
import numpy as np

D_MODEL = 1024
BATCH = 8
SEQ = 2048
DEPTH = 4
DEC_BATCH = 16
DEC_SEQ = 32
PAST_LEN = 4096

CHUNK = 64
Q_BLOCK = 128
N_A_LAYERS = DEPTH // 2
N_B_LAYERS = DEPTH - N_A_LAYERS
MIX_IN = 3 * D_MODEL // 4
MEM_HEADS = 4
MEM_HEAD_DIM = 64
MEM_WIDTH = MEM_HEADS * MEM_HEAD_DIM
N_MEM = 256
SSM_GROUP = 16
SSM_GROUPS = MIX_IN // SSM_GROUP
SSM_STATE = 64
MLA_HEADS = 12
NOPE_DIM = 64
ROPE_DIM = 32
V_DIM = 64
KV_LORA = 256
Q_LORA = MIX_IN
ROPE_BASE = 10000.0
MLA_SCALE = (NOPE_DIM + ROPE_DIM) ** -0.5
MEM_SCALE = MEM_HEAD_DIM ** -0.5
D_FF = 2816
CONV_W = 3
EPS = 1e-6
NEG_INF = -1e30

kernel_name = "yoco_s5_mla_streaming_step"


def rms_norm(x, g):
    xf = x.astype(jnp.float32)
    y = xf * lax.rsqrt(jnp.mean(xf * xf, axis=-1, keepdims=True) + EPS)
    return (y * g.astype(jnp.float32)).astype(x.dtype)


def rope_angles(pos):
    inv_freq = 1.0 / (ROPE_BASE ** (jnp.arange(0, ROPE_DIM, 2, dtype=jnp.float32) / ROPE_DIM))
    ang = pos.astype(jnp.float32)[:, None] * inv_freq[None, :]
    return jnp.cos(ang), jnp.sin(ang)


def apply_rope(x, cos, sin):
    shape = (cos.shape[0],) + (1,) * (x.ndim - 3) + (cos.shape[1],)
    c, s = cos.reshape(shape), sin.reshape(shape)
    xf = x.astype(jnp.float32)
    half = ROPE_DIM // 2
    x1, x2 = xf[..., :half], xf[..., half:]
    return jnp.concatenate([x1 * c - x2 * s, x2 * c + x1 * s], axis=-1).astype(x.dtype)


def conv_ffn(x, ctx, w_in, conv_w, conv_b, w_out):
    u = x @ w_in
    L = u.shape[1]
    padded = jnp.concatenate([ctx.astype(u.dtype), u], axis=1)
    y = sum((padded[:, k:k + L] * conv_w[k] for k in range(CONV_W)), conv_b)
    a, g = jnp.split(y, 2, axis=-1)
    return (jax.nn.silu(g) * a) @ w_out, padded[:, L:]


def _complex_affine_combine(e1, e2):
    a1r, a1i, b1r, b1i = e1
    a2r, a2i, b2r, b2i = e2
    return (a2r * a1r - a2i * a1i, a2r * a1i + a2i * a1r,
            a2r * b1r - a2i * b1i + b2r, a2r * b1i + a2i * b1r + b2i)


def s5_mixer(u, h0_re, h0_im, a_re, a_im, log_dt, b_re, b_im, c_re, c_im, d_skip, w_glu, b_glu):
    f32 = jnp.float32
    Bsz, L, _ = u.shape
    a_re, a_im = a_re.astype(f32), a_im.astype(f32)
    dt = jnp.exp(log_dt.astype(f32))[:, None]
    mag = jnp.exp(a_re * dt)
    lam_re, lam_im = mag * jnp.cos(a_im * dt), mag * jnp.sin(a_im * dt)
    den = a_re * a_re + a_im * a_im
    x_re = lam_re - 1.0
    f_re = (x_re * a_re + lam_im * a_im) / den
    f_im = (lam_im * a_re - x_re * a_im) / den
    b_re, b_im = b_re.astype(f32), b_im.astype(f32)
    bb_re = f_re[..., None] * b_re - f_im[..., None] * b_im
    bb_im = f_re[..., None] * b_im + f_im[..., None] * b_re
    ug = u.astype(f32).reshape(Bsz, L, SSM_GROUPS, SSM_GROUP)
    bu_re = jnp.einsum('blgc,gnc->blgn', ug, bb_re)
    bu_im = jnp.einsum('blgc,gnc->blgn', ug, bb_im)
    if h0_re is not None:
        h0_re, h0_im = h0_re.astype(f32), h0_im.astype(f32)
        bu_re = bu_re.at[:, 0].add(lam_re * h0_re - lam_im * h0_im)
        bu_im = bu_im.at[:, 0].add(lam_re * h0_im + lam_im * h0_re)
    lam_re_t = jnp.broadcast_to(lam_re, (1, L) + lam_re.shape)
    lam_im_t = jnp.broadcast_to(lam_im, (1, L) + lam_im.shape)
    _, _, h_re, h_im = lax.associative_scan(
        _complex_affine_combine, (lam_re_t, lam_im_t, bu_re, bu_im), axis=1)
    y = (jnp.einsum('blgn,gcn->blgc', h_re, c_re.astype(f32))
         - jnp.einsum('blgn,gcn->blgc', h_im, c_im.astype(f32)))
    y = y.reshape(Bsz, L, MIX_IN) + d_skip.astype(f32) * u.astype(f32)
    y = jax.nn.gelu(y).astype(u.dtype)
    out = y * jax.nn.sigmoid(y @ w_glu + b_glu)
    return out, h_re[:, -1], h_im[:, -1]


def memory_kv(mem, mem_norm_g, w_mem_kv, mem_k_norm_g):
    Bsz, M, _ = mem.shape
    m = rms_norm(mem[None], mem_norm_g[:, None, None, :])
    kv = jnp.einsum('lbmd,ldk->lbmk', m, w_mem_kv)
    k, v = jnp.split(kv, 2, axis=-1)
    k = rms_norm(k.reshape(DEPTH, Bsz, M, MEM_HEADS, MEM_HEAD_DIM), mem_k_norm_g[:, None, None, None, :])
    return k, v.reshape(DEPTH, Bsz, M, MEM_HEADS, MEM_HEAD_DIM)


def memory_attend(q, k, v):
    s = jnp.einsum('bqhd,bkhd->bhqk', q, k, preferred_element_type=jnp.float32) * MEM_SCALE
    p = jax.nn.softmax(s, axis=-1).astype(v.dtype)
    return jnp.einsum('bhqk,bkhd->bqhd', p, v)


def mla_attend(q_nope, q_rope, k_nope, k_rope, v, q_chunk, k_chunk):
    s = (jnp.einsum('bqhd,bkhd->bhqk', q_nope, k_nope, preferred_element_type=jnp.float32)
         + jnp.einsum('bqhd,bkd->bhqk', q_rope, k_rope, preferred_element_type=jnp.float32)) * MLA_SCALE
    s = jnp.where(k_chunk[None, :] <= q_chunk[:, None], s, NEG_INF)
    p = jax.nn.softmax(s, axis=-1).astype(v.dtype)
    return jnp.einsum('bhqk,bkhd->bqhd', p, v)


def mla_attention(q_nope, q_rope, k_nope, k_rope, v, q_pos, k_pos):
    q_chunk, k_chunk = q_pos // CHUNK, k_pos // CHUNK
    Bsz, L = q_nope.shape[:2]
    if L % Q_BLOCK != 0:
        return mla_attend(q_nope, q_rope, k_nope, k_rope, v, q_chunk, k_chunk)
    nb = L // Q_BLOCK

    def to_blocks(t):
        return jnp.moveaxis(t.reshape((Bsz, nb, Q_BLOCK) + t.shape[2:]), 1, 0)

    def one_block(args):
        qn, qr, qc = args
        return mla_attend(qn, qr, k_nope, k_rope, v, qc, k_chunk)

    o = lax.map(one_block, (to_blocks(q_nope), to_blocks(q_rope), q_chunk.reshape(nb, Q_BLOCK)))
    return jnp.moveaxis(o, 0, 1).reshape(Bsz, L, MLA_HEADS, V_DIM)


def trunk(x, mem_k, mem_v, ssm_h0_re, ssm_h0_im, conv_ctx, past_latent, past_krope,
          norm_mix_g, w_mix_in, w_mix_out, norm_ffn_g, w_ffn_in, ffn_conv_w, ffn_conv_b, w_ffn_out,
          mem_q_norm_g,
          ssm_a_re, ssm_a_im, ssm_log_dt, ssm_b_re, ssm_b_im, ssm_c_re, ssm_c_im, ssm_d, w_glu, b_glu,
          kv_norm_g, w_dkv, latent_norm_g, krope_norm_g, w_uk, w_uv, k_nope_norm_g,
          q_latent_norm_g, w_uq, q_nope_norm_g, q_rope_norm_g):
    Bsz, L, _ = x.shape
    past = 0 if past_latent is None else past_latent.shape[1]
    q_pos = past + jnp.arange(L, dtype=jnp.int32)
    cos, sin = rope_angles(q_pos)
    if conv_ctx is None:
        conv_ctx = jnp.zeros((DEPTH, Bsz, CONV_W - 1, 2 * D_FF), x.dtype)
    h = x
    ssm_re_out, ssm_im_out, conv_out = [], [], []
    for layer in range(DEPTH):
        z = rms_norm(h, norm_mix_g[layer]) @ w_mix_in[layer]
        z_mix, z_mem = z[..., :MIX_IN], z[..., MIX_IN:]
        mq = rms_norm(z_mem.reshape(Bsz, L, MEM_HEADS, MEM_HEAD_DIM), mem_q_norm_g[layer])
        mem_out = memory_attend(mq, mem_k[layer], mem_v[layer]).reshape(Bsz, L, MEM_WIDTH)
        if layer < N_A_LAYERS:
            i = layer
            mix_out, hr, hi = s5_mixer(
                z_mix,
                None if ssm_h0_re is None else ssm_h0_re[i],
                None if ssm_h0_im is None else ssm_h0_im[i],
                ssm_a_re[i], ssm_a_im[i], ssm_log_dt[i], ssm_b_re[i], ssm_b_im[i],
                ssm_c_re[i], ssm_c_im[i], ssm_d[i], w_glu[i], b_glu[i])
            ssm_re_out.append(hr)
            ssm_im_out.append(hi)
        else:
            if layer == N_A_LAYERS:
                ckv = rms_norm(h, kv_norm_g) @ w_dkv
                new_latent = rms_norm(ckv[..., :KV_LORA], latent_norm_g)
                new_krope = apply_rope(rms_norm(ckv[..., KV_LORA:], krope_norm_g), cos, sin)
                if past_latent is None:
                    latent_all, krope_all = new_latent, new_krope
                else:
                    latent_all = jnp.concatenate([past_latent.astype(new_latent.dtype), new_latent], axis=1)
                    krope_all = jnp.concatenate([past_krope.astype(new_krope.dtype), new_krope], axis=1)
                Lk = latent_all.shape[1]
                k_pos = jnp.arange(Lk, dtype=jnp.int32)
                k_nope = rms_norm((latent_all @ w_uk).reshape(Bsz, Lk, MLA_HEADS, NOPE_DIM), k_nope_norm_g)
                v_all = (latent_all @ w_uv).reshape(Bsz, Lk, MLA_HEADS, V_DIM)
            j = layer - N_A_LAYERS
            q = (rms_norm(z_mix, q_latent_norm_g[j]) @ w_uq[j]).reshape(Bsz, L, MLA_HEADS, NOPE_DIM + ROPE_DIM)
            q_nope = rms_norm(q[..., :NOPE_DIM], q_nope_norm_g[j])
            q_rope = apply_rope(rms_norm(q[..., NOPE_DIM:], q_rope_norm_g[j]), cos, sin)
            mix_out = mla_attention(q_nope, q_rope, k_nope, krope_all, v_all, q_pos, k_pos).reshape(Bsz, L, MIX_IN)
        h = h + jnp.concatenate([mix_out, mem_out], axis=-1) @ w_mix_out[layer]
        f, ctx = conv_ffn(rms_norm(h, norm_ffn_g[layer]), conv_ctx[layer], w_ffn_in[layer],
                          ffn_conv_w[layer], ffn_conv_b[layer], w_ffn_out[layer])
        conv_out.append(ctx)
        h = h + f
    return h, new_latent, new_krope, jnp.stack(ssm_re_out), jnp.stack(ssm_im_out), jnp.stack(conv_out)


def setup_inputs(seed: int = 0) -> dict:
    key = jax.random.key(seed)
    ks = iter(jax.random.split(key, 64))
    f32 = jnp.float32

    def nrm(shape, scale=1.0):
        return scale * jax.random.normal(next(ks), shape, f32)

    def gain(shape):
        return 1.0 + nrm(shape, 0.05)

    D = D_MODEL
    inp = {}
    inp["x_prompt"] = nrm((BATCH, SEQ, D))
    inp["x_sample"] = nrm((DEC_BATCH, DEC_SEQ, D))
    inp["cache_mla_latent"] = nrm((DEC_BATCH, PAST_LEN, KV_LORA))
    inp["cache_mla_krope"] = nrm((DEC_BATCH, PAST_LEN, ROPE_DIM))
    inp["cache_mem_k"] = nrm((DEPTH, DEC_BATCH, N_MEM, MEM_HEADS, MEM_HEAD_DIM))
    inp["cache_mem_v"] = nrm((DEPTH, DEC_BATCH, N_MEM, MEM_HEADS, MEM_HEAD_DIM))
    inp["state_ssm_re"] = nrm((N_A_LAYERS, DEC_BATCH, SSM_GROUPS, SSM_STATE), 0.1)
    inp["state_ssm_im"] = nrm((N_A_LAYERS, DEC_BATCH, SSM_GROUPS, SSM_STATE), 0.1)
    inp["state_conv"] = nrm((DEPTH, DEC_BATCH, CONV_W - 1, 2 * D_FF))
    inp["mem_prompt"] = nrm((BATCH, N_MEM, D))
    inp["norm_mix_g"] = gain((DEPTH, D))
    inp["w_mix_in"] = nrm((DEPTH, D, MIX_IN + MEM_WIDTH), D ** -0.5)
    inp["w_mix_out"] = nrm((DEPTH, MIX_IN + MEM_WIDTH, D), (MIX_IN + MEM_WIDTH) ** -0.5)
    inp["norm_ffn_g"] = gain((DEPTH, D))
    inp["w_ffn_in"] = nrm((DEPTH, D, 2 * D_FF), D ** -0.5)
    inp["ffn_conv_w"] = nrm((DEPTH, CONV_W, 2 * D_FF), CONV_W ** -0.5)
    inp["ffn_conv_b"] = nrm((DEPTH, 2 * D_FF), 0.01)
    inp["w_ffn_out"] = nrm((DEPTH, D_FF, D), D_FF ** -0.5)
    inp["mem_norm_g"] = gain((DEPTH, D))
    inp["w_mem_kv"] = nrm((DEPTH, D, 2 * MEM_WIDTH), D ** -0.5)
    inp["mem_q_norm_g"] = gain((DEPTH, MEM_HEAD_DIM))
    inp["mem_k_norm_g"] = gain((DEPTH, MEM_HEAD_DIM))
    n_idx = jnp.arange(SSM_STATE, dtype=f32)
    inp["ssm_a_re"] = -0.5 + nrm((N_A_LAYERS, SSM_GROUPS, SSM_STATE), 0.01)
    inp["ssm_a_im"] = math.pi * n_idx + nrm((N_A_LAYERS, SSM_GROUPS, SSM_STATE), 0.01)
    inp["ssm_log_dt"] = jax.random.uniform(next(ks), (N_A_LAYERS, SSM_GROUPS), f32,
                                           math.log(1e-3), math.log(1e-1))
    inp["ssm_b_re"] = nrm((N_A_LAYERS, SSM_GROUPS, SSM_STATE, SSM_GROUP), (2 * SSM_GROUP) ** -0.5)
    inp["ssm_b_im"] = nrm((N_A_LAYERS, SSM_GROUPS, SSM_STATE, SSM_GROUP), (2 * SSM_GROUP) ** -0.5)
    inp["ssm_c_re"] = nrm((N_A_LAYERS, SSM_GROUPS, SSM_GROUP, SSM_STATE), SSM_STATE ** -0.5)
    inp["ssm_c_im"] = nrm((N_A_LAYERS, SSM_GROUPS, SSM_GROUP, SSM_STATE), SSM_STATE ** -0.5)
    inp["ssm_d"] = nrm((N_A_LAYERS, MIX_IN))
    inp["w_glu"] = nrm((N_A_LAYERS, MIX_IN, MIX_IN), MIX_IN ** -0.5)
    inp["b_glu"] = nrm((N_A_LAYERS, MIX_IN), 0.01)
    inp["kv_norm_g"] = gain((D,))
    inp["w_dkv"] = nrm((D, KV_LORA + ROPE_DIM), D ** -0.5)
    inp["latent_norm_g"] = gain((KV_LORA,))
    inp["krope_norm_g"] = gain((ROPE_DIM,))
    inp["w_uk"] = nrm((KV_LORA, MLA_HEADS * NOPE_DIM), KV_LORA ** -0.5)
    inp["w_uv"] = nrm((KV_LORA, MLA_HEADS * V_DIM), KV_LORA ** -0.5)
    inp["k_nope_norm_g"] = gain((NOPE_DIM,))
    inp["q_latent_norm_g"] = gain((N_B_LAYERS, Q_LORA))
    inp["w_uq"] = nrm((N_B_LAYERS, Q_LORA, MLA_HEADS * (NOPE_DIM + ROPE_DIM)), Q_LORA ** -0.5)
    inp["q_nope_norm_g"] = gain((N_B_LAYERS, NOPE_DIM))
    inp["q_rope_norm_g"] = gain((N_B_LAYERS, ROPE_DIM))
    return inp


def reference(x_prompt, x_sample, cache_mla_latent, cache_mla_krope, cache_mem_k, cache_mem_v,
              state_ssm_re, state_ssm_im, state_conv, mem_prompt,
              norm_mix_g, w_mix_in, w_mix_out, norm_ffn_g, w_ffn_in, ffn_conv_w, ffn_conv_b, w_ffn_out,
              mem_norm_g, w_mem_kv, mem_q_norm_g, mem_k_norm_g,
              ssm_a_re, ssm_a_im, ssm_log_dt, ssm_b_re, ssm_b_im, ssm_c_re, ssm_c_im, ssm_d, w_glu, b_glu,
              kv_norm_g, w_dkv, latent_norm_g, krope_norm_g, w_uk, w_uv, k_nope_norm_g,
              q_latent_norm_g, w_uq, q_nope_norm_g, q_rope_norm_g):
    weights = (norm_mix_g, w_mix_in, w_mix_out, norm_ffn_g, w_ffn_in, ffn_conv_w, ffn_conv_b, w_ffn_out,
               mem_q_norm_g,
               ssm_a_re, ssm_a_im, ssm_log_dt, ssm_b_re, ssm_b_im, ssm_c_re, ssm_c_im, ssm_d, w_glu, b_glu,
               kv_norm_g, w_dkv, latent_norm_g, krope_norm_g, w_uk, w_uv, k_nope_norm_g,
               q_latent_norm_g, w_uq, q_nope_norm_g, q_rope_norm_g)
    mem_k_p, mem_v_p = memory_kv(mem_prompt, mem_norm_g, w_mem_kv, mem_k_norm_g)
    y_prompt, lat_p, krope_p, ssm_re_p, ssm_im_p, conv_p = trunk(
        x_prompt, mem_k_p, mem_v_p, None, None, None, None, None, *weights)
    y_sample, lat_s, krope_s, ssm_re_s, ssm_im_s, conv_s = trunk(
        x_sample, cache_mem_k, cache_mem_v, state_ssm_re, state_ssm_im, state_conv,
        cache_mla_latent, cache_mla_krope, *weights)
    return (y_prompt, y_sample, mem_k_p, mem_v_p, lat_p, krope_p, ssm_re_p, ssm_im_p, conv_p,
            lat_s, krope_s, ssm_re_s, ssm_im_s, conv_s)
```

```python
import functools
import math

import jax
import jax.numpy as jnp
from jax import lax
from jax.experimental import pallas as pl
from jax.experimental.pallas import tpu as pltpu

F32 = jnp.float32
BF16 = jnp.bfloat16

D_MODEL = 1024
DEPTH = 4
CHUNK = 64
N_A_LAYERS = DEPTH // 2
N_B_LAYERS = DEPTH - N_A_LAYERS
MIX_IN = 768
MEM_HEADS = 4
MEM_HEAD_DIM = 64
MEM_WIDTH = MEM_HEADS * MEM_HEAD_DIM
N_MEM = 256
SSM_GROUP = 16
SSM_GROUPS = MIX_IN // SSM_GROUP
SSM_STATE = 64
SSM_COLS = SSM_GROUPS * SSM_STATE
MLA_HEADS = 12
NOPE_DIM = 64
ROPE_DIM = 32
ROPE_HALF = ROPE_DIM // 2
V_DIM = 64
KV_LORA = 256
ROPE_BASE = 10000.0
MLA_SCALE = (NOPE_DIM + ROPE_DIM) ** -0.5
MEM_SCALE = MEM_HEAD_DIM ** -0.5
D_FF = 2816
CONV_W = 3
EPS = 1e-6
NEG_INF = -1e30

V7X_VMEM_LIMIT_BYTES = 56 * 1024 * 1024
LANES = 128
SUBLANES = 8

S5_BLOCKS = 3
S5_BLOCK_CH = MIX_IN // S5_BLOCKS
S5_BLOCK_ST = SSM_COLS // S5_BLOCKS
FFN_TF = 256
N_FF_TILES = D_FF // FFN_TF


def _params(n_axes):
    return pltpu.CompilerParams(
        dimension_semantics=("arbitrary",) * n_axes,
        vmem_limit_bytes=V7X_VMEM_LIMIT_BYTES,
    )


def _dot(a, b):
    return jnp.dot(a, b, preferred_element_type=F32)


def _dot_nt(a, b):
    return lax.dot_general(a, b, (((1,), (1,)), ((), ())), preferred_element_type=F32)


def _rms_rows(x, g):
    ms = jnp.mean(x * x, axis=-1, keepdims=True)
    return x * lax.rsqrt(ms + EPS) * g


def _seg_rms(x, seg_mat, g):
    ms = _dot((x * x).astype(BF16), seg_mat)
    return x * lax.rsqrt(ms + EPS) * g


def _seg_matrix(width, seg):
    idx = jnp.arange(width) // seg
    return jnp.where(idx[:, None] == idx[None, :], 1.0 / seg, 0.0).astype(BF16)


def _full(shape):
    nd = len(shape)
    return pl.BlockSpec(shape, lambda *_: (0,) * nd)


def _rope_table_kernel(pos_ref, inv_ref, cos_ref, sin_ref):
    ang = pos_ref[...] * inv_ref[...]
    cos_ref[...] = jnp.cos(ang)
    sin_ref[...] = jnp.sin(ang)


def _rope_tables(past, length):
    pos = (past + jnp.arange(length, dtype=jnp.int32)).astype(F32)[:, None]
    inv_freq = (1.0 / (ROPE_BASE ** (jnp.arange(0, ROPE_DIM, 2, dtype=F32) / ROPE_DIM)))[None, :]
    cos, sin = pl.pallas_call(
        _rope_table_kernel,
        out_shape=[jax.ShapeDtypeStruct((length, ROPE_HALF), F32)] * 2,
        name="rope_table",
    )(pos, inv_freq)
    cos_t = jnp.concatenate([cos, cos], axis=1)
    sin_t = jnp.concatenate([-sin, sin], axis=1)
    return cos_t, sin_t


def _memkv_kernel(mem_ref, g_ref, w_ref, kg_ref, seg_ref, k_ref, v_ref, kb_ref, vb_ref):
    xn = _rms_rows(mem_ref[0], g_ref[0]).astype(BF16)
    kv = _dot(xn, w_ref[0])
    k = _seg_rms(kv[:, :MEM_WIDTH], seg_ref[...], kg_ref[0])
    v = kv[:, MEM_WIDTH:]
    k_ref[0, 0] = k
    v_ref[0, 0] = v
    kb_ref[0, 0] = k.astype(BF16)
    vb_ref[0, 0] = v.astype(BF16)


def _memory_kv(mem, mem_norm_g, w_mem_kv_bf, mem_k_norm_g, seg_mem):
    bsz = mem.shape[0]
    kg = jnp.tile(mem_k_norm_g, (1, MEM_HEADS)).reshape(DEPTH, 1, MEM_WIDTH)
    out4 = lambda dt: jax.ShapeDtypeStruct((DEPTH, bsz, N_MEM, MEM_WIDTH), dt)
    spec4 = pl.BlockSpec((1, 1, N_MEM, MEM_WIDTH), lambda l, b: (l, b, 0, 0))
    return pl.pallas_call(
        _memkv_kernel,
        grid=(DEPTH, bsz),
        in_specs=[
            pl.BlockSpec((1, N_MEM, D_MODEL), lambda l, b: (b, 0, 0)),
            pl.BlockSpec((1, 1, D_MODEL), lambda l, b: (l, 0, 0)),
            pl.BlockSpec((1, D_MODEL, 2 * MEM_WIDTH), lambda l, b: (l, 0, 0)),
            pl.BlockSpec((1, 1, MEM_WIDTH), lambda l, b: (l, 0, 0)),
            _full((MEM_WIDTH, MEM_WIDTH)),
        ],
        out_specs=[spec4, spec4, spec4, spec4],
        out_shape=[out4(F32), out4(F32), out4(BF16), out4(BF16)],
        compiler_params=_params(2),
        name="memory_kv",
    )(mem, mem_norm_g.reshape(DEPTH, 1, D_MODEL), w_mem_kv_bf, kg, seg_mem)


def _mixin_kernel(h_ref, g_ref, w_ref, qg_ref, seg_ref, k_ref, v_ref, zmix_ref, mem_ref):
    xn = _rms_rows(h_ref[0], g_ref[...]).astype(BF16)
    z = _dot(xn, w_ref[...])
    zmix_ref[0] = z[:, :MIX_IN]
    mq = _seg_rms(z[:, MIX_IN:], seg_ref[...], qg_ref[...]).astype(BF16)
    k = k_ref[0]
    v = v_ref[0]
    lane = lax.broadcasted_iota(jnp.int32, (1, MEM_WIDTH), 1)
    out = jnp.zeros(mq.shape, F32)
    for head in range(MEM_HEADS):
        in_head = (lane // MEM_HEAD_DIM) == head
        qh = jnp.where(in_head, mq, jnp.zeros_like(mq))
        s = _dot_nt(qh, k) * MEM_SCALE
        p = jnp.exp(s - jnp.max(s, axis=-1, keepdims=True))
        p = p * (1.0 / jnp.sum(p, axis=-1, keepdims=True))
        o = _dot(p.astype(BF16), v)
        out = jnp.where(in_head, o, out)
    mem_ref[0] = out.astype(BF16)


def _mixin(h, g, w_bf, qg_tiled, seg_mem, k_bf, v_bf, tm):
    bsz, length, _ = h.shape
    return pl.pallas_call(
        _mixin_kernel,
        grid=(bsz, length // tm),
        in_specs=[
            pl.BlockSpec((1, tm, D_MODEL), lambda b, i: (b, i, 0)),
            _full((1, D_MODEL)),
            _full((D_MODEL, MIX_IN + MEM_WIDTH)),
            _full((1, MEM_WIDTH)),
            _full((MEM_WIDTH, MEM_WIDTH)),
            pl.BlockSpec((1, N_MEM, MEM_WIDTH), lambda b, i: (b, 0, 0)),
            pl.BlockSpec((1, N_MEM, MEM_WIDTH), lambda b, i: (b, 0, 0)),
        ],
        out_specs=[
            pl.BlockSpec((1, tm, MIX_IN), lambda b, i: (b, i, 0)),
            pl.BlockSpec((1, tm, MEM_WIDTH), lambda b, i: (b, i, 0)),
        ],
        out_shape=[
            jax.ShapeDtypeStruct((bsz, length, MIX_IN), F32),
            jax.ShapeDtypeStruct((bsz, length, MEM_WIDTH), BF16),
        ],
        compiler_params=_params(2),
        name="mix_in_mem_attn",
    )(h, g, w_bf, qg_tiled, seg_mem, k_bf, v_bf)


def _s5_prep_kernel(are_ref, aim_ref, ldt_ref, bre_ref, bim_ref, lre_ref, lim_ref, bbre_ref, bbim_ref):
    a_re, a_im = are_ref[0], aim_ref[0]
    dt = jnp.exp(ldt_ref[0])
    mag = jnp.exp(a_re * dt)
    lam_re = mag * jnp.cos(a_im * dt)
    lam_im = mag * jnp.sin(a_im * dt)
    den = a_re * a_re + a_im * a_im
    x_re = lam_re - 1.0
    f_re = (x_re * a_re + lam_im * a_im) / den
    f_im = (lam_im * a_re - x_re * a_im) / den
    b_re, b_im = bre_ref[0], bim_ref[0]
    lre_ref[0] = lam_re
    lim_ref[0] = lam_im
    bbre_ref[0] = f_re * b_re - f_im * b_im
    bbim_ref[0] = f_re * b_im + f_im * b_re


def _s5_prepare(a_re, a_im, log_dt, b_re, b_im, c_re, c_im):
    n = a_re.shape[0]
    col = lambda t: t.reshape(n, SSM_COLS, 1)
    ldt = jnp.broadcast_to(log_dt[:, :, None], (n, SSM_GROUPS, SSM_STATE))
    col_spec = pl.BlockSpec((1, SSM_COLS, 1), lambda l: (l, 0, 0))
    b_spec = pl.BlockSpec((1, SSM_COLS, SSM_GROUP), lambda l: (l, 0, 0))
    lam_re, lam_im, bb_re, bb_im = pl.pallas_call(
        _s5_prep_kernel,
        grid=(n,),
        in_specs=[col_spec, col_spec, col_spec, b_spec, b_spec],
        out_specs=[col_spec, col_spec, b_spec, b_spec],
        out_shape=[jax.ShapeDtypeStruct((n, SSM_COLS, 1), F32)] * 2
        + [jax.ShapeDtypeStruct((n, SSM_COLS, SSM_GROUP), F32)] * 2,
        compiler_params=_params(1),
        name="s5_discretise",
    )(col(a_re), col(a_im), col(ldt),
      b_re.reshape(n, SSM_COLS, SSM_GROUP), b_im.reshape(n, SSM_COLS, SSM_GROUP))
    gpb = SSM_GROUPS // S5_BLOCKS
    eye = jnp.eye(gpb, dtype=F32)

    def in_blocks(bb):
        t = bb.reshape(n, S5_BLOCKS, gpb, SSM_STATE, SSM_GROUP)
        return jnp.einsum("ljgnc,gh->ljgchn", t, eye).reshape(n, S5_BLOCKS, S5_BLOCK_CH, S5_BLOCK_ST)

    def out_blocks(c):
        t = c.reshape(n, S5_BLOCKS, gpb, SSM_GROUP, SSM_STATE)
        return jnp.einsum("ljgcn,gh->ljgnhc", t, eye).reshape(n, S5_BLOCKS, S5_BLOCK_ST, S5_BLOCK_CH)

    w_in = jnp.concatenate([in_blocks(bb_re), in_blocks(bb_im)], axis=-1).astype(BF16)
    w_out = jnp.concatenate([out_blocks(c_re), -out_blocks(c_im)], axis=-2).astype(BF16)
    return lam_re.reshape(n, 1, SSM_COLS), lam_im.reshape(n, 1, SSM_COLS), w_in, w_out


def _s5_kernel(u_ref, h0re_ref, h0im_ref, lre_ref, lim_ref, win_ref, wout_ref, d_ref, wglu_ref, bglu_ref,
               out_ref, sre_out_ref, sim_out_ref, hb_ref, sre_ref, sim_ref, *, bsz, t_chunk, col_block):
    c = pl.program_id(0)
    rows = bsz * t_chunk

    @pl.when(c == 0)
    def _():
        sre_ref[...] = h0re_ref[...]
        sim_ref[...] = h0im_ref[...]

    u = u_ref[...]
    ut = pltpu.einshape("btd->tbd", u).reshape(rows, MIX_IN).astype(BF16)
    for j in range(S5_BLOCKS):
        hb_ref[:, j * 2 * S5_BLOCK_ST:(j + 1) * 2 * S5_BLOCK_ST] = _dot(
            ut[:, j * S5_BLOCK_CH:(j + 1) * S5_BLOCK_CH], win_ref[j])

    for j in range(S5_BLOCKS):
        for sub in range(S5_BLOCK_ST // col_block):
            nat = j * S5_BLOCK_ST + sub * col_block
            cre = j * 2 * S5_BLOCK_ST + sub * col_block
            cim = cre + S5_BLOCK_ST
            lam_r = jnp.broadcast_to(lre_ref[:, nat:nat + col_block], (bsz, col_block))
            lam_i = jnp.broadcast_to(lim_ref[:, nat:nat + col_block], (bsz, col_block))

            def step(t, carry, cre=cre, cim=cim, lam_r=lam_r, lam_i=lam_i):
                s_r, s_i = carry
                r0 = pl.multiple_of(t * bsz, bsz)
                n_r = lam_r * s_r - lam_i * s_i + hb_ref[pl.ds(r0, bsz), cre:cre + col_block]
                n_i = lam_r * s_i + lam_i * s_r + hb_ref[pl.ds(r0, bsz), cim:cim + col_block]
                hb_ref[pl.ds(r0, bsz), cre:cre + col_block] = n_r
                hb_ref[pl.ds(r0, bsz), cim:cim + col_block] = n_i
                return n_r, n_i

            s_r, s_i = lax.fori_loop(
                0, t_chunk, step,
                (sre_ref[:, nat:nat + col_block], sim_ref[:, nat:nat + col_block]), unroll=8)
            sre_ref[:, nat:nat + col_block] = s_r
            sim_ref[:, nat:nat + col_block] = s_i

    ys = [_dot(hb_ref[:, j * 2 * S5_BLOCK_ST:(j + 1) * 2 * S5_BLOCK_ST].astype(BF16), wout_ref[j])
          for j in range(S5_BLOCKS)]
    yt = jnp.concatenate(ys, axis=1).reshape(t_chunk, bsz, MIX_IN)
    y = pltpu.einshape("tbd->btd", yt) + d_ref[...] * u
    y = jax.nn.gelu(y).reshape(rows, MIX_IN)
    gate = _dot(y.astype(BF16), wglu_ref[...]) + bglu_ref[...]
    out_ref[...] = (y * jax.nn.sigmoid(gate)).reshape(bsz, t_chunk, MIX_IN).astype(BF16)

    @pl.when(c == pl.num_programs(0) - 1)
    def _():
        sre_out_ref[...] = sre_ref[...]
        sim_out_ref[...] = sim_ref[...]


def _s5_mixer(u, h0_re, h0_im, lam_re, lam_im, w_in, w_out, d_skip, w_glu_bf, b_glu, t_chunk):
    bsz, length, _ = u.shape
    rows = bsz * t_chunk
    col_block = (SUBLANES * 512) // bsz
    kern = functools.partial(_s5_kernel, bsz=bsz, t_chunk=t_chunk, col_block=col_block)
    state = jax.ShapeDtypeStruct((bsz, SSM_COLS), F32)
    return pl.pallas_call(
        kern,
        grid=(length // t_chunk,),
        in_specs=[
            pl.BlockSpec((bsz, t_chunk, MIX_IN), lambda c: (0, c, 0)),
            _full((bsz, SSM_COLS)), _full((bsz, SSM_COLS)),
            _full((1, SSM_COLS)), _full((1, SSM_COLS)),
            _full((S5_BLOCKS, S5_BLOCK_CH, 2 * S5_BLOCK_ST)),
            _full((S5_BLOCKS, 2 * S5_BLOCK_ST, S5_BLOCK_CH)),
            _full((1, MIX_IN)), _full((MIX_IN, MIX_IN)), _full((1, MIX_IN)),
        ],
        out_specs=[
            pl.BlockSpec((bsz, t_chunk, MIX_IN), lambda c: (0, c, 0)),
            _full((bsz, SSM_COLS)), _full((bsz, SSM_COLS)),
        ],
        out_shape=[jax.ShapeDtypeStruct((bsz, length, MIX_IN), BF16), state, state],
        scratch_shapes=[
            pltpu.VMEM((rows, 2 * SSM_COLS), F32),
            pltpu.VMEM((bsz, SSM_COLS), F32),
            pltpu.VMEM((bsz, SSM_COLS), F32),
        ],
        compiler_params=_params(1),
        name="s5_mixer",
    )(u, h0_re, h0_im, lam_re, lam_im, w_in, w_out, d_skip, w_glu_bf, b_glu)


def _ffn_kernel(h_ref, mix_ref, mem_ref, wo_ref, g_ref, wa_ref, wg_ref, cwa_ref, cwg_ref, cba_ref, cbg_ref,
                w2_ref, ctxa_ref, ctxg_ref, out_ref, newa_ref, newg_ref,
                xn_ref, pa_ref, pg_ref, carry_ref, *, seqs, rows_per_seq, tiles_per_seq):
    m = pl.program_id(0)
    f = pl.program_id(1)
    tm = seqs * rows_per_seq

    @pl.when(f == 0)
    def _():
        h1 = h_ref[...] + _dot(mix_ref[...], wo_ref[:MIX_IN, :]) + _dot(mem_ref[...], wo_ref[MIX_IN:, :])
        out_ref[...] = h1
        xn_ref[...] = _rms_rows(h1, g_ref[...]).astype(BF16)

    if tiles_per_seq > 1:
        @pl.when(m == 0)
        def _():
            carry_ref[f] = jnp.zeros(carry_ref.shape[1:], F32)

    xn = xn_ref[...]

    def causal_conv(u, pad_ref, ctx_ref, cw_ref, cb_ref, new_ref, slot):
        ctx = ctx_ref[...]
        if tiles_per_seq > 1:
            prev = carry_ref[f, slot]
            ctx = jnp.where(m % tiles_per_seq == 0, ctx, prev[None])
        pad_ref[:, 6:8, :] = ctx
        pad_ref[:, 8:, :] = u.reshape(seqs, rows_per_seq, FFN_TF)
        cw = cw_ref[...]
        y = (cb_ref[...] + pad_ref[:, 6:6 + rows_per_seq, :] * cw[0:1]
             + pad_ref[:, 7:7 + rows_per_seq, :] * cw[1:2]
             + pad_ref[:, 8:8 + rows_per_seq, :] * cw[2:3])
        last2 = pad_ref[:, rows_per_seq + 6:rows_per_seq + 8, :]
        new_ref[...] = last2
        if tiles_per_seq > 1:
            carry_ref[f, slot] = last2[0]
        return y.reshape(tm, FFN_TF)

    ya = causal_conv(_dot(xn, wa_ref[...]), pa_ref, ctxa_ref, cwa_ref, cba_ref, newa_ref, 0)
    yg = causal_conv(_dot(xn, wg_ref[...]), pg_ref, ctxg_ref, cwg_ref, cbg_ref, newg_ref, 1)
    act = (jax.nn.silu(yg) * ya).astype(BF16)
    out_ref[...] += _dot(act, w2_ref[...])


def _mixout_ffn(h, mix, mem, w_mix_out_bf, norm_g, w_in_bf, conv_w, conv_b, w_out_bf, ctx, rows_per_seq_tile):
    nseq, length, _ = h.shape
    m_rows = nseq * length
    if rows_per_seq_tile >= length:
        seqs, rps, tps = nseq, length, 1
    else:
        seqs, rps, tps = 1, rows_per_seq_tile, length // rows_per_seq_tile
    tm = seqs * rps
    n_m = m_rows // tm
    nf = N_FF_TILES
    kern = functools.partial(_ffn_kernel, seqs=seqs, rows_per_seq=rps, tiles_per_seq=tps)
    row = lambda width: pl.BlockSpec((tm, width), lambda m, f: (m, 0))
    ctx_a = pl.BlockSpec((seqs, CONV_W - 1, FFN_TF), lambda m, f: (m // tps, 0, f))
    ctx_g = pl.BlockSpec((seqs, CONV_W - 1, FFN_TF), lambda m, f: (m // tps, 0, nf + f))
    new_spec = pl.BlockSpec((seqs, CONV_W - 1, FFN_TF), lambda m, f: (m, 0, f))
    out, new_a, new_g = pl.pallas_call(
        kern,
        grid=(n_m, nf),
        in_specs=[
            row(D_MODEL), row(MIX_IN), row(MEM_WIDTH),
            _full((MIX_IN + MEM_WIDTH, D_MODEL)),
            _full((1, D_MODEL)),
            pl.BlockSpec((D_MODEL, FFN_TF), lambda m, f: (0, f)),
            pl.BlockSpec((D_MODEL, FFN_TF), lambda m, f: (0, nf + f)),
            pl.BlockSpec((CONV_W, FFN_TF), lambda m, f: (0, f)),
            pl.BlockSpec((CONV_W, FFN_TF), lambda m, f: (0, nf + f)),
            pl.BlockSpec((1, FFN_TF), lambda m, f: (0, f)),
            pl.BlockSpec((1, FFN_TF), lambda m, f: (0, nf + f)),
            pl.BlockSpec((FFN_TF, D_MODEL), lambda m, f: (f, 0)),
            ctx_a, ctx_g,
        ],
        out_specs=[row(D_MODEL), new_spec, new_spec],
        out_shape=[
            jax.ShapeDtypeStruct((m_rows, D_MODEL), F32),
            jax.ShapeDtypeStruct((n_m * seqs, CONV_W - 1, D_FF), F32),
            jax.ShapeDtypeStruct((n_m * seqs, CONV_W - 1, D_FF), F32),
        ],
        scratch_shapes=[
            pltpu.VMEM((tm, D_MODEL), BF16),
            pltpu.VMEM((seqs, rps + SUBLANES, FFN_TF), F32),
            pltpu.VMEM((seqs, rps + SUBLANES, FFN_TF), F32),
            pltpu.VMEM((nf, 2, CONV_W - 1, FFN_TF), F32),
        ],
        compiler_params=_params(2),
        name="mix_out_conv_ffn",
    )(h.reshape(m_rows, D_MODEL), mix.reshape(m_rows, MIX_IN), mem.reshape(m_rows, MEM_WIDTH),
      w_mix_out_bf, norm_g, w_in_bf, w_in_bf, conv_w, conv_w, conv_b, conv_b, w_out_bf, ctx, ctx)
    new_ctx = jnp.concatenate([new_a, new_g], axis=-1).reshape(nseq, tps, CONV_W - 1, 2 * D_FF)[:, -1]
    return out.reshape(nseq, length, D_MODEL), new_ctx


def _dkv_kernel(h_ref, g_ref, wl_ref, wr_ref, wrr_ref, lg_ref, kg_ref, kgr_ref, cos_ref, sin_ref,
                lat_ref, kr_ref):
    xn = _rms_rows(h_ref[...], g_ref[...]).astype(BF16)
    lat_ref[...] = _rms_rows(_dot(xn, wl_ref[...]), lg_ref[...])
    kr = _dot(xn, wr_ref[...])
    kr_rot = _dot(xn, wrr_ref[...])
    r = lax.rsqrt(jnp.mean(kr * kr, axis=-1, keepdims=True) + EPS)
    kr_ref[...] = (kr * r * kg_ref[...]) * cos_ref[...] + (kr_rot * r * kgr_ref[...]) * sin_ref[...]


def _swap_halves(t, axis=-1):
    a, b = jnp.split(t, 2, axis=axis)
    return jnp.concatenate([b, a], axis=axis)


def _shared_kv_down(h, kv_norm_g, w_dkv_bf, latent_norm_g, krope_norm_g, cos_rows, sin_rows, tm):
    nseq, length, _ = h.shape
    m_rows = nseq * length
    tab_blocks = cos_rows.shape[0] // tm
    w_l = w_dkv_bf[:, :KV_LORA]
    w_r = w_dkv_bf[:, KV_LORA:]
    kg = krope_norm_g.reshape(1, ROPE_DIM)
    lat, kr = pl.pallas_call(
        _dkv_kernel,
        grid=(m_rows // tm,),
        in_specs=[
            pl.BlockSpec((tm, D_MODEL), lambda m: (m, 0)),
            _full((1, D_MODEL)),
            _full((D_MODEL, KV_LORA)), _full((D_MODEL, ROPE_DIM)), _full((D_MODEL, ROPE_DIM)),
            _full((1, KV_LORA)), _full((1, ROPE_DIM)), _full((1, ROPE_DIM)),
            pl.BlockSpec((tm, ROPE_DIM), lambda m: (m % tab_blocks, 0)),
            pl.BlockSpec((tm, ROPE_DIM), lambda m: (m % tab_blocks, 0)),
        ],
        out_specs=[pl.BlockSpec((tm, KV_LORA), lambda m: (m, 0)),
                   pl.BlockSpec((tm, ROPE_DIM), lambda m: (m, 0))],
        out_shape=[jax.ShapeDtypeStruct((m_rows, KV_LORA), F32),
                   jax.ShapeDtypeStruct((m_rows, ROPE_DIM), F32)],
        compiler_params=_params(1),
        name="shared_kv_down",
    )(h.reshape(m_rows, D_MODEL), kv_norm_g.reshape(1, D_MODEL), w_l, w_r, _swap_halves(w_r),
      latent_norm_g.reshape(1, KV_LORA), kg, _swap_halves(kg), cos_rows, sin_rows)
    return lat.reshape(nseq, length, KV_LORA), kr.reshape(nseq, length, ROPE_DIM)


def _kv_up_kernel(lat_ref, wk_ref, wv_ref, seg_ref, g_ref, k_ref, v_ref):
    lat = lat_ref[...].astype(BF16)
    k_ref[...] = _seg_rms(_dot(lat, wk_ref[...]), seg_ref[...], g_ref[...]).astype(BF16)
    v_ref[...] = _dot(lat, wv_ref[...]).astype(BF16)


def _kv_up(latent_rows, w_uk_bf, w_uv_bf, seg_nope, k_g_tiled, tr):
    rows = latent_rows.shape[0]
    width = MLA_HEADS * NOPE_DIM
    return pl.pallas_call(
        _kv_up_kernel,
        grid=(rows // tr,),
        in_specs=[
            pl.BlockSpec((tr, KV_LORA), lambda r: (r, 0)),
            _full((KV_LORA, width)), _full((KV_LORA, width)),
            _full((width, width)), _full((1, width)),
        ],
        out_specs=[pl.BlockSpec((tr, width), lambda r: (r, 0))] * 2,
        out_shape=[jax.ShapeDtypeStruct((rows, width), BF16)] * 2,
        compiler_params=_params(1),
        name="kv_up",
    )(latent_rows, w_uk_bf, w_uv_bf, seg_nope, k_g_tiled)


def _q_kernel(z_ref, g_ref, wn_ref, wa_ref, wb_ref, segn_ref, segr_ref, gn_ref, ga_ref, gb_ref,
              cos_ref, sin_ref, qn_ref, qr_ref):
    xn = _rms_rows(z_ref[...], g_ref[...]).astype(BF16)
    qn_ref[...] = _seg_rms(_dot(xn, wn_ref[...]), segn_ref[...], gn_ref[...]).astype(BF16)
    a = _dot(xn, wa_ref[...])
    b = _dot(xn, wb_ref[...])
    r = lax.rsqrt(_dot((a * a).astype(BF16), segr_ref[...]) + EPS)
    qr_ref[...] = ((a * r * ga_ref[...]) * cos_ref[...] + (b * r * gb_ref[...]) * sin_ref[...]).astype(BF16)


def _q_side(z_mix, q_latent_g, w_uq_bf, q_nope_g, q_rope_g, seg_nope, seg_rope, cos_rows, sin_rows, tm):
    nseq, length, _ = z_mix.shape
    m_rows = nseq * length
    tab_blocks = cos_rows.shape[0] // tm
    wn_width = MLA_HEADS * NOPE_DIM
    wr_width = MLA_HEADS * ROPE_DIM
    w3 = w_uq_bf.reshape(MIX_IN, MLA_HEADS, NOPE_DIM + ROPE_DIM)
    w_n = w3[:, :, :NOPE_DIM].reshape(MIX_IN, wn_width)
    w_a = w3[:, :, NOPE_DIM:]
    w_b = _swap_halves(w_a).reshape(MIX_IN, wr_width)
    w_a = w_a.reshape(MIX_IN, wr_width)
    g_n = jnp.tile(q_nope_g.reshape(1, NOPE_DIM), (1, MLA_HEADS))
    g_a = jnp.tile(q_rope_g.reshape(1, ROPE_DIM), (1, MLA_HEADS))
    g_b = jnp.tile(_swap_halves(q_rope_g.reshape(1, ROPE_DIM)), (1, MLA_HEADS))
    cos_q = jnp.tile(cos_rows, (1, MLA_HEADS))
    sin_q = jnp.tile(sin_rows, (1, MLA_HEADS))
    qn, qr = pl.pallas_call(
        _q_kernel,
        grid=(m_rows // tm,),
        in_specs=[
            pl.BlockSpec((tm, MIX_IN), lambda m: (m, 0)),
            _full((1, MIX_IN)),
            _full((MIX_IN, wn_width)), _full((MIX_IN, wr_width)), _full((MIX_IN, wr_width)),
            _full((wn_width, wn_width)), _full((wr_width, wr_width)),
            _full((1, wn_width)), _full((1, wr_width)), _full((1, wr_width)),
            pl.BlockSpec((tm, wr_width), lambda m: (m % tab_blocks, 0)),
            pl.BlockSpec((tm, wr_width), lambda m: (m % tab_blocks, 0)),
        ],
        out_specs=[pl.BlockSpec((tm, wn_width), lambda m: (m, 0)),
                   pl.BlockSpec((tm, wr_width), lambda m: (m, 0))],
        out_shape=[jax.ShapeDtypeStruct((m_rows, wn_width), BF16),
                   jax.ShapeDtypeStruct((m_rows, wr_width), BF16)],
        compiler_params=_params(1),
        name="mla_query",
    )(z_mix.reshape(m_rows, MIX_IN), q_latent_g.reshape(1, MIX_IN), w_n, w_a, w_b, seg_nope, seg_rope,
      g_n, g_a, g_b, cos_q, sin_q)
    return qn.reshape(nseq, length, wn_width), qr.reshape(nseq, length, wr_width)


def _attn_kernel(qn_ref, qr_ref, kn_ref, kr_ref, v_ref, o_ref, *, tq, tk, q_off, lk_valid, lk_pad):
    pair = pl.program_id(1)
    qi = pl.program_id(2)
    qn = qn_ref[0]
    qr = qr_ref[0]
    lane = lax.broadcasted_iota(jnp.int32, (1, LANES), 1)
    q_first = q_off + qi * tq
    q_chunk = (q_first + lax.broadcasted_iota(jnp.int32, (tq, 1), 0)) // CHUNK
    visible = jnp.minimum(lk_valid, ((q_first + tq - 1) // CHUNK + 1) * CHUNK)
    n_blocks = (visible + tk - 1) // tk
    out = jnp.zeros((tq, LANES), F32)
    for j in range(2):
        nope_lanes = (lane // NOPE_DIM) == j
        rope_lanes = (lane // ROPE_DIM) == 2 * (pair % 2) + j
        qcat = jnp.concatenate([jnp.where(nope_lanes, qn, jnp.zeros_like(qn)),
                                jnp.where(rope_lanes, qr, jnp.zeros_like(qr))], axis=1)

        def block(kb, carry, qcat=qcat):
            m_i, l_i, acc = carry
            k0 = pl.multiple_of(kb * tk, tk)
            kcat = jnp.concatenate([kn_ref[0, pl.ds(k0, tk), :], kr_ref[0, pl.ds(k0, tk), :]], axis=1)
            s = _dot_nt(qcat, kcat) * MLA_SCALE
            k_pos = k0 + lax.broadcasted_iota(jnp.int32, (1, tk), 1)
            s = jnp.where(k_pos // CHUNK <= q_chunk, s, NEG_INF)
            if lk_valid < lk_pad:
                s = jnp.where(k_pos < lk_valid, s, NEG_INF)
            m_new = jnp.maximum(m_i, jnp.max(s, axis=-1, keepdims=True))
            alpha = jnp.exp(m_i - m_new)
            p = jnp.exp(s - m_new)
            l_new = alpha * l_i + jnp.sum(p, axis=-1, keepdims=True)
            acc = alpha * acc + _dot(p.astype(BF16), v_ref[0, pl.ds(k0, tk), :])
            return m_new, l_new, acc

        _, l_f, acc = lax.fori_loop(
            0, n_blocks, block,
            (jnp.full((tq, 1), NEG_INF, F32), jnp.zeros((tq, 1), F32), jnp.zeros((tq, LANES), F32)))
        out = jnp.where(nope_lanes, acc * (1.0 / l_f), out)
    o_ref[0] = out.astype(BF16)


def _mla_attention(qn, qr, kn, kr4, v, q_off, lk_valid, tq, tk):
    nseq, length, _ = qn.shape
    lk_pad = kn.shape[1]
    kern = functools.partial(_attn_kernel, tq=tq, tk=tk, q_off=q_off, lk_valid=lk_valid, lk_pad=lk_pad)
    return pl.pallas_call(
        kern,
        grid=(nseq, MLA_HEADS // 2, length // tq),
        in_specs=[
            pl.BlockSpec((1, tq, LANES), lambda b, p, i: (b, i, p)),
            pl.BlockSpec((1, tq, LANES), lambda b, p, i: (b, i, p // 2)),
            pl.BlockSpec((1, lk_pad, LANES), lambda b, p, i: (b, 0, p)),
            pl.BlockSpec((1, lk_pad, LANES), lambda b, p, i: (b, 0, 0)),
            pl.BlockSpec((1, lk_pad, LANES), lambda b, p, i: (b, 0, p)),
        ],
        out_specs=pl.BlockSpec((1, tq, LANES), lambda b, p, i: (b, i, p)),
        out_shape=jax.ShapeDtypeStruct((nseq, length, MLA_HEADS * V_DIM), BF16),
        compiler_params=_params(3),
        name="mla_attention",
    )(qn, qr, kn, kr4, v)


def _trunk(x, mem_k_bf, mem_v_bf, ssm_h0_re, ssm_h0_im, conv_ctx, past_latent, past_krope, w, cfg):
    nseq, length, _ = x.shape
    past = 0 if past_latent is None else past_latent.shape[1]
    cos_t, sin_t = _rope_tables(past, length)
    reps = cfg["rope_rows"] // length
    cos_rows = jnp.tile(cos_t, (reps, 1))
    sin_rows = jnp.tile(sin_t, (reps, 1))
    if conv_ctx is None:
        conv_ctx = jnp.zeros((DEPTH, nseq, CONV_W - 1, 2 * D_FF), F32)
    if ssm_h0_re is None:
        ssm_h0_re = jnp.zeros((N_A_LAYERS, nseq, SSM_GROUPS, SSM_STATE), F32)
        ssm_h0_im = ssm_h0_re
    h = x
    ssm_re_out, ssm_im_out, conv_out = [], [], []
    for layer in range(DEPTH):
        z_mix, mem_out = _mixin(h, w["norm_mix_g"][layer][None], w["w_mix_in"][layer], w["mem_q_g"][layer],
                                w["seg_mem"], mem_k_bf[layer], mem_v_bf[layer], cfg["tm_mixin"])
        if layer < N_A_LAYERS:
            i = layer
            mix_out, s_re, s_im = _s5_mixer(
                z_mix, ssm_h0_re[i].reshape(nseq, SSM_COLS), ssm_h0_im[i].reshape(nseq, SSM_COLS),
                w["lam_re"][i], w["lam_im"][i], w["s5_in"][i], w["s5_out"][i],
                w["ssm_d"][i][None], w["w_glu"][i], w["b_glu"][i][None], cfg["t_chunk"])
            ssm_re_out.append(s_re.reshape(nseq, SSM_GROUPS, SSM_STATE))
            ssm_im_out.append(s_im.reshape(nseq, SSM_GROUPS, SSM_STATE))
        else:
            if layer == N_A_LAYERS:
                new_latent, new_krope = _shared_kv_down(
                    h, w["kv_norm_g"], w["w_dkv"], w["latent_norm_g"], w["krope_norm_g"],
                    cos_rows, sin_rows, cfg["tm_rows"])
                if past_latent is None:
                    latent_all, krope_all = new_latent, new_krope
                else:
                    latent_all = jnp.concatenate([past_latent, new_latent], axis=1)
                    krope_all = jnp.concatenate([past_krope, new_krope], axis=1)
                lk_valid = latent_all.shape[1]
                lk_pad = -(-lk_valid // cfg["tk"]) * cfg["tk"]
                if lk_pad != lk_valid:
                    zpad = ((0, 0), (0, lk_pad - lk_valid), (0, 0))
                    latent_all = jnp.pad(latent_all, zpad)
                    krope_all = jnp.pad(krope_all, zpad)
                kn, v_all = _kv_up(latent_all.reshape(nseq * lk_pad, KV_LORA), w["w_uk"], w["w_uv"],
                                   w["seg_nope"], w["k_nope_g"], cfg["tr_kv"])
                kn = kn.reshape(nseq, lk_pad, MLA_HEADS * NOPE_DIM)
                v_all = v_all.reshape(nseq, lk_pad, MLA_HEADS * V_DIM)
                kr4 = jnp.tile(krope_all.astype(BF16), (1, 1, LANES // ROPE_DIM))
            j = layer - N_A_LAYERS
            qn, qr = _q_side(z_mix, w["q_latent_norm_g"][j], w["w_uq"][j], w["q_nope_norm_g"][j],
                             w["q_rope_norm_g"][j], w["seg_nope"], w["seg_rope"], cos_rows, sin_rows,
                             cfg["tm_rows"])
            mix_out = _mla_attention(qn, qr, kn, kr4, v_all, past, lk_valid, cfg["tq"], cfg["tk"])
        h, ctx = _mixout_ffn(h, mix_out, mem_out, w["w_mix_out"][layer], w["norm_ffn_g"][layer][None],
                             w["w_ffn_in"][layer], w["ffn_conv_w"][layer], w["ffn_conv_b"][layer][None],
                             w["w_ffn_out"][layer], conv_ctx[layer], cfg["ffn_rows"])
        conv_out.append(ctx)
    return h, new_latent, new_krope, jnp.stack(ssm_re_out), jnp.stack(ssm_im_out), jnp.stack(conv_out)


PROMPT_CFG = dict(tm_mixin=512, t_chunk=64, ffn_rows=1024, tm_rows=1024, rope_rows=2048, tr_kv=1024,
                  tq=256, tk=512)
SAMPLE_CFG = dict(tm_mixin=32, t_chunk=32, ffn_rows=32, tm_rows=512, rope_rows=512, tr_kv=1088,
                  tq=32, tk=256)


def kernel(x_prompt, x_sample, cache_mla_latent, cache_mla_krope, cache_mem_k, cache_mem_v, state_ssm_re, state_ssm_im, state_conv, mem_prompt, norm_mix_g, w_mix_in, w_mix_out, norm_ffn_g, w_ffn_in, ffn_conv_w, ffn_conv_b, w_ffn_out, mem_norm_g, w_mem_kv, mem_q_norm_g, mem_k_norm_g, ssm_a_re, ssm_a_im, ssm_log_dt, ssm_b_re, ssm_b_im, ssm_c_re, ssm_c_im, ssm_d, w_glu, b_glu, kv_norm_g, w_dkv, latent_norm_g, krope_norm_g, w_uk, w_uv, k_nope_norm_g, q_latent_norm_g, w_uq, q_nope_norm_g, q_rope_norm_g):
    bf = lambda t: t.astype(BF16)
    seg_mem = _seg_matrix(MEM_WIDTH, MEM_HEAD_DIM)
    lam_re, lam_im, s5_in, s5_out = _s5_prepare(ssm_a_re, ssm_a_im, ssm_log_dt, ssm_b_re, ssm_b_im,
                                                ssm_c_re, ssm_c_im)
    w = dict(
        norm_mix_g=norm_mix_g, w_mix_in=bf(w_mix_in), w_mix_out=bf(w_mix_out), norm_ffn_g=norm_ffn_g,
        w_ffn_in=bf(w_ffn_in), ffn_conv_w=ffn_conv_w, ffn_conv_b=ffn_conv_b, w_ffn_out=bf(w_ffn_out),
        mem_q_g=jnp.tile(mem_q_norm_g, (1, MEM_HEADS)).reshape(DEPTH, 1, MEM_WIDTH),
        seg_mem=seg_mem,
        seg_nope=_seg_matrix(MLA_HEADS * NOPE_DIM, NOPE_DIM),
        seg_rope=_seg_matrix(MLA_HEADS * ROPE_DIM, ROPE_DIM),
        lam_re=lam_re, lam_im=lam_im, s5_in=s5_in, s5_out=s5_out,
        ssm_d=ssm_d, w_glu=bf(w_glu), b_glu=b_glu,
        kv_norm_g=kv_norm_g, w_dkv=bf(w_dkv), latent_norm_g=latent_norm_g, krope_norm_g=krope_norm_g,
        w_uk=bf(w_uk), w_uv=bf(w_uv),
        k_nope_g=jnp.tile(k_nope_norm_g.reshape(1, NOPE_DIM), (1, MLA_HEADS)),
        q_latent_norm_g=q_latent_norm_g, w_uq=bf(w_uq), q_nope_norm_g=q_nope_norm_g,
        q_rope_norm_g=q_rope_norm_g,
    )
    bsz = mem_prompt.shape[0]
    mem_k_p, mem_v_p, mem_k_bf, mem_v_bf = _memory_kv(mem_prompt, mem_norm_g, bf(w_mem_kv), mem_k_norm_g,
                                                      seg_mem)
    y_prompt, lat_p, krope_p, ssm_re_p, ssm_im_p, conv_p = _trunk(
        x_prompt, mem_k_bf, mem_v_bf, None, None, None, None, None, w, PROMPT_CFG)
    dec = cache_mem_k.shape[1]
    y_sample, lat_s, krope_s, ssm_re_s, ssm_im_s, conv_s = _trunk(
        x_sample, bf(cache_mem_k).reshape(DEPTH, dec, N_MEM, MEM_WIDTH),
        bf(cache_mem_v).reshape(DEPTH, dec, N_MEM, MEM_WIDTH),
        state_ssm_re, state_ssm_im, state_conv, cache_mla_latent, cache_mla_krope, w, SAMPLE_CFG)
    shape5 = (DEPTH, bsz, N_MEM, MEM_HEADS, MEM_HEAD_DIM)
    return (y_prompt, y_sample, mem_k_p.reshape(shape5), mem_v_p.reshape(shape5), lat_p, krope_p,
            ssm_re_p, ssm_im_p, conv_p, lat_s, krope_s, ssm_re_s, ssm_im_s, conv_s)
```

```python
import functools
import math

import jax
import jax.numpy as jnp
from jax import lax
from jax.experimental import pallas as pl
from jax.experimental.pallas import tpu as pltpu

F32 = jnp.float32
BF16 = jnp.bfloat16

D_MODEL = 1024
DEPTH = 4
CHUNK = 64
N_A_LAYERS = DEPTH // 2
N_B_LAYERS = DEPTH - N_A_LAYERS
MIX_IN = 768
MEM_HEADS = 4
MEM_HEAD_DIM = 64
MEM_WIDTH = MEM_HEADS * MEM_HEAD_DIM
N_MEM = 256
SSM_GROUP = 16
SSM_GROUPS = MIX_IN // SSM_GROUP
SSM_STATE = 64
SSM_COLS = SSM_GROUPS * SSM_STATE
MLA_HEADS = 12
NOPE_DIM = 64
ROPE_DIM = 32
ROPE_HALF = ROPE_DIM // 2
V_DIM = 64
KV_LORA = 256
ROPE_BASE = 10000.0
MLA_SCALE = (NOPE_DIM + ROPE_DIM) ** -0.5
MEM_SCALE = MEM_HEAD_DIM ** -0.5
D_FF = 2816
CONV_W = 3
EPS = 1e-6
NEG_INF = -1e30

V7X_VMEM_LIMIT_BYTES = 56 * 1024 * 1024
LANES = 128
SUBLANES = 8

S5_BLOCKS = 3
S5_BLOCK_CH = MIX_IN // S5_BLOCKS
S5_BLOCK_ST = SSM_COLS // S5_BLOCKS
FFN_TF = 256
N_FF_TILES = D_FF // FFN_TF


def _params(n_axes):
    return pltpu.CompilerParams(
        dimension_semantics=("arbitrary",) * n_axes,
        vmem_limit_bytes=V7X_VMEM_LIMIT_BYTES,
    )


def _dot(a, b):
    return jnp.dot(a, b, preferred_element_type=F32)


def _dot_nt(a, b):
    return lax.dot_general(a, b, (((1,), (1,)), ((), ())), preferred_element_type=F32)


def _rms_rows(x, g):
    ms = jnp.mean(x * x, axis=-1, keepdims=True)
    return x * lax.rsqrt(ms + EPS) * g


def _seg_rms(x, seg_mat, g):
    ms = _dot((x * x).astype(BF16), seg_mat)
    return x * lax.rsqrt(ms + EPS) * g


def _seg_matrix(width, seg):
    idx = jnp.arange(width) // seg
    return jnp.where(idx[:, None] == idx[None, :], 1.0 / seg, 0.0).astype(BF16)


def _full(shape):
    nd = len(shape)
    return pl.BlockSpec(shape, lambda *_: (0,) * nd)


def _rope_table_kernel(pos_ref, inv_ref, cos_ref, sin_ref):
    ang = pos_ref[...] * inv_ref[...]
    cos_ref[...] = jnp.cos(ang)
    sin_ref[...] = jnp.sin(ang)


def _rope_tables(past, length):
    pos = (past + jnp.arange(length, dtype=jnp.int32)).astype(F32)[:, None]
    inv_freq = (1.0 / (ROPE_BASE ** (jnp.arange(0, ROPE_DIM, 2, dtype=F32) / ROPE_DIM)))[None, :]
    cos, sin = pl.pallas_call(
        _rope_table_kernel,
        out_shape=[jax.ShapeDtypeStruct((length, ROPE_HALF), F32)] * 2,
        name="rope_table",
    )(pos, inv_freq)
    cos_t = jnp.concatenate([cos, cos], axis=1)
    sin_t = jnp.concatenate([-sin, sin], axis=1)
    return cos_t, sin_t


def _memkv_kernel(mem_ref, g_ref, w_ref, kg_ref, seg_ref, k_ref, v_ref, kb_ref, vb_ref):
    xn = _rms_rows(mem_ref[0], g_ref[0]).astype(BF16)
    kv = _dot(xn, w_ref[0])
    k = _seg_rms(kv[:, :MEM_WIDTH], seg_ref[...], kg_ref[0])
    v = kv[:, MEM_WIDTH:]
    k_ref[0, 0] = k
    v_ref[0, 0] = v
    kb_ref[0, 0] = k.astype(BF16)
    vb_ref[0, 0] = v.astype(BF16)


def _memory_kv(mem, mem_norm_g, w_mem_kv_bf, mem_k_norm_g, seg_mem):
    bsz = mem.shape[0]
    kg = jnp.tile(mem_k_norm_g, (1, MEM_HEADS)).reshape(DEPTH, 1, MEM_WIDTH)
    out4 = lambda dt: jax.ShapeDtypeStruct((DEPTH, bsz, N_MEM, MEM_WIDTH), dt)
    spec4 = pl.BlockSpec((1, 1, N_MEM, MEM_WIDTH), lambda l, b: (l, b, 0, 0))
    return pl.pallas_call(
        _memkv_kernel,
        grid=(DEPTH, bsz),
        in_specs=[
            pl.BlockSpec((1, N_MEM, D_MODEL), lambda l, b: (b, 0, 0)),
            pl.BlockSpec((1, 1, D_MODEL), lambda l, b: (l, 0, 0)),
            pl.BlockSpec((1, D_MODEL, 2 * MEM_WIDTH), lambda l, b: (l, 0, 0)),
            pl.BlockSpec((1, 1, MEM_WIDTH), lambda l, b: (l, 0, 0)),
            _full((MEM_WIDTH, MEM_WIDTH)),
        ],
        out_specs=[spec4, spec4, spec4, spec4],
        out_shape=[out4(F32), out4(F32), out4(BF16), out4(BF16)],
        compiler_params=_params(2),
        name="memory_kv",
    )(mem, mem_norm_g.reshape(DEPTH, 1, D_MODEL), w_mem_kv_bf, kg, seg_mem)


def _mixin_kernel(h_ref, g_ref, w_ref, qg_ref, seg_ref, k_ref, v_ref, zmix_ref, mem_ref):
    xn = _rms_rows(h_ref[0], g_ref[...]).astype(BF16)
    z = _dot(xn, w_ref[...])
    zmix_ref[0] = z[:, :MIX_IN]
    mq = _seg_rms(z[:, MIX_IN:], seg_ref[...], qg_ref[...]).astype(BF16)
    k = k_ref[0]
    v = v_ref[0]
    lane = lax.broadcasted_iota(jnp.int32, (1, MEM_WIDTH), 1)
    out = jnp.zeros(mq.shape, F32)
    for head in range(MEM_HEADS):
        in_head = (lane // MEM_HEAD_DIM) == head
        qh = jnp.where(in_head, mq, jnp.zeros_like(mq))
        s = _dot_nt(qh, k) * MEM_SCALE
        p = jnp.exp(s - jnp.max(s, axis=-1, keepdims=True))
        p = p * (1.0 / jnp.sum(p, axis=-1, keepdims=True))
        o = _dot(p.astype(BF16), v)
        out = jnp.where(in_head, o, out)
    mem_ref[0] = out.astype(BF16)


def _mixin(h, g, w_bf, qg_tiled, seg_mem, k_bf, v_bf, tm):
    bsz, length, _ = h.shape
    return pl.pallas_call(
        _mixin_kernel,
        grid=(bsz, length // tm),
        in_specs=[
            pl.BlockSpec((1, tm, D_MODEL), lambda b, i: (b, i, 0)),
            _full((1, D_MODEL)),
            _full((D_MODEL, MIX_IN + MEM_WIDTH)),
            _full((1, MEM_WIDTH)),
            _full((MEM_WIDTH, MEM_WIDTH)),
            pl.BlockSpec((1, N_MEM, MEM_WIDTH), lambda b, i: (b, 0, 0)),
            pl.BlockSpec((1, N_MEM, MEM_WIDTH), lambda b, i: (b, 0, 0)),
        ],
        out_specs=[
            pl.BlockSpec((1, tm, MIX_IN), lambda b, i: (b, i, 0)),
            pl.BlockSpec((1, tm, MEM_WIDTH), lambda b, i: (b, i, 0)),
        ],
        out_shape=[
            jax.ShapeDtypeStruct((bsz, length, MIX_IN), F32),
            jax.ShapeDtypeStruct((bsz, length, MEM_WIDTH), BF16),
        ],
        compiler_params=_params(2),
        name="mix_in_mem_attn",
    )(h, g, w_bf, qg_tiled, seg_mem, k_bf, v_bf)


def _s5_prep_kernel(are_ref, aim_ref, ldt_ref, bre_ref, bim_ref, lre_ref, lim_ref, bbre_ref, bbim_ref):
    a_re, a_im = are_ref[0], aim_ref[0]
    dt = jnp.exp(ldt_ref[0])
    mag = jnp.exp(a_re * dt)
    lam_re = mag * jnp.cos(a_im * dt)
    lam_im = mag * jnp.sin(a_im * dt)
    den = a_re * a_re + a_im * a_im
    x_re = lam_re - 1.0
    f_re = (x_re * a_re + lam_im * a_im) / den
    f_im = (lam_im * a_re - x_re * a_im) / den
    b_re, b_im = bre_ref[0], bim_ref[0]
    lre_ref[0] = lam_re
    lim_ref[0] = lam_im
    bbre_ref[0] = f_re * b_re - f_im * b_im
    bbim_ref[0] = f_re * b_im + f_im * b_re


def _s5_prepare(a_re, a_im, log_dt, b_re, b_im, c_re, c_im):
    n = a_re.shape[0]
    col = lambda t: t.reshape(n, SSM_COLS, 1)
    ldt = jnp.broadcast_to(log_dt[:, :, None], (n, SSM_GROUPS, SSM_STATE))
    col_spec = pl.BlockSpec((1, SSM_COLS, 1), lambda l: (l, 0, 0))
    b_spec = pl.BlockSpec((1, SSM_COLS, SSM_GROUP), lambda l: (l, 0, 0))
    lam_re, lam_im, bb_re, bb_im = pl.pallas_call(
        _s5_prep_kernel,
        grid=(n,),
        in_specs=[col_spec, col_spec, col_spec, b_spec, b_spec],
        out_specs=[col_spec, col_spec, b_spec, b_spec],
        out_shape=[jax.ShapeDtypeStruct((n, SSM_COLS, 1), F32)] * 2
        + [jax.ShapeDtypeStruct((n, SSM_COLS, SSM_GROUP), F32)] * 2,
        compiler_params=_params(1),
        name="s5_discretise",
    )(col(a_re), col(a_im), col(ldt),
      b_re.reshape(n, SSM_COLS, SSM_GROUP), b_im.reshape(n, SSM_COLS, SSM_GROUP))
    gpb = SSM_GROUPS // S5_BLOCKS
    eye = jnp.eye(gpb, dtype=F32)

    def in_blocks(bb):
        t = bb.reshape(n, S5_BLOCKS, gpb, SSM_STATE, SSM_GROUP)
        return jnp.einsum("ljgnc,gh->ljgchn", t, eye).reshape(n, S5_BLOCKS, S5_BLOCK_CH, S5_BLOCK_ST)

    def out_blocks(c):
        t = c.reshape(n, S5_BLOCKS, gpb, SSM_GROUP, SSM_STATE)
        return jnp.einsum("ljgcn,gh->ljgnhc", t, eye).reshape(n, S5_BLOCKS, S5_BLOCK_ST, S5_BLOCK_CH)

    w_in = jnp.concatenate([in_blocks(bb_re), in_blocks(bb_im)], axis=-1).astype(BF16)
    w_out = jnp.concatenate([out_blocks(c_re), -out_blocks(c_im)], axis=-2).astype(BF16)
    return lam_re.reshape(n, 1, SSM_COLS), lam_im.reshape(n, 1, SSM_COLS), w_in, w_out


def _s5_kernel(u_ref, h0re_ref, h0im_ref, lre_ref, lim_ref, win_ref, wout_ref, d_ref, wglu_ref, bglu_ref,
               out_ref, sre_out_ref, sim_out_ref, hb_ref, sre_ref, sim_ref, *, bsz, t_chunk, col_block):
    c = pl.program_id(0)
    rows = bsz * t_chunk

    @pl.when(c == 0)
    def _():
        sre_ref[...] = h0re_ref[...]
        sim_ref[...] = h0im_ref[...]

    u = u_ref[...]
    ut = jnp.swapaxes(u, 0, 1).reshape(rows, MIX_IN).astype(BF16)
    for j in range(S5_BLOCKS):
        hb_ref[:, j * 2 * S5_BLOCK_ST:(j + 1) * 2 * S5_BLOCK_ST] = _dot(
            ut[:, j * S5_BLOCK_CH:(j + 1) * S5_BLOCK_CH], win_ref[j])

    for j in range(S5_BLOCKS):
        for sub in range(S5_BLOCK_ST // col_block):
            nat = j * S5_BLOCK_ST + sub * col_block
            cre = j * 2 * S5_BLOCK_ST + sub * col_block
            cim = cre + S5_BLOCK_ST
            lam_r = jnp.broadcast_to(lre_ref[:, nat:nat + col_block], (bsz, col_block))
            lam_i = jnp.broadcast_to(lim_ref[:, nat:nat + col_block], (bsz, col_block))

            def step(t, carry, cre=cre, cim=cim, lam_r=lam_r, lam_i=lam_i):
                s_r, s_i = carry
                r0 = pl.multiple_of(t * bsz, bsz)
                n_r = lam_r * s_r - lam_i * s_i + hb_ref[pl.ds(r0, bsz), cre:cre + col_block]
                n_i = lam_r * s_i + lam_i * s_r + hb_ref[pl.ds(r0, bsz), cim:cim + col_block]
                hb_ref[pl.ds(r0, bsz), cre:cre + col_block] = n_r
                hb_ref[pl.ds(r0, bsz), cim:cim + col_block] = n_i
                return n_r, n_i

            s_r, s_i = lax.fori_loop(
                0, t_chunk, step,
                (sre_ref[:, nat:nat + col_block], sim_ref[:, nat:nat + col_block]), unroll=8)
            sre_ref[:, nat:nat + col_block] = s_r
            sim_ref[:, nat:nat + col_block] = s_i

    ys = [_dot(hb_ref[:, j * 2 * S5_BLOCK_ST:(j + 1) * 2 * S5_BLOCK_ST].astype(BF16), wout_ref[j])
          for j in range(S5_BLOCKS)]
    yt = jnp.concatenate(ys, axis=1).reshape(t_chunk, bsz, MIX_IN)
    y = jnp.swapaxes(yt, 0, 1) + d_ref[...] * u
    y = jax.nn.gelu(y).reshape(rows, MIX_IN)
    gate = _dot(y.astype(BF16), wglu_ref[...]) + bglu_ref[...]
    out_ref[...] = (y * jax.nn.sigmoid(gate)).reshape(bsz, t_chunk, MIX_IN).astype(BF16)

    @pl.when(c == pl.num_programs(0) - 1)
    def _():
        sre_out_ref[...] = sre_ref[...]
        sim_out_ref[...] = sim_ref[...]


def _s5_mixer(u, h0_re, h0_im, lam_re, lam_im, w_in, w_out, d_skip, w_glu_bf, b_glu, t_chunk):
    bsz, length, _ = u.shape
    rows = bsz * t_chunk
    col_block = (SUBLANES * 512) // bsz
    kern = functools.partial(_s5_kernel, bsz=bsz, t_chunk=t_chunk, col_block=col_block)
    state = jax.ShapeDtypeStruct((bsz, SSM_COLS), F32)
    return pl.pallas_call(
        kern,
        grid=(length // t_chunk,),
        in_specs=[
            pl.BlockSpec((bsz, t_chunk, MIX_IN), lambda c: (0, c, 0)),
            _full((bsz, SSM_COLS)), _full((bsz, SSM_COLS)),
            _full((1, SSM_COLS)), _full((1, SSM_COLS)),
            _full((S5_BLOCKS, S5_BLOCK_CH, 2 * S5_BLOCK_ST)),
            _full((S5_BLOCKS, 2 * S5_BLOCK_ST, S5_BLOCK_CH)),
            _full((1, MIX_IN)), _full((MIX_IN, MIX_IN)), _full((1, MIX_IN)),
        ],
        out_specs=[
            pl.BlockSpec((bsz, t_chunk, MIX_IN), lambda c: (0, c, 0)),
            _full((bsz, SSM_COLS)), _full((bsz, SSM_COLS)),
        ],
        out_shape=[jax.ShapeDtypeStruct((bsz, length, MIX_IN), BF16), state, state],
        scratch_shapes=[
            pltpu.VMEM((rows, 2 * SSM_COLS), F32),
            pltpu.VMEM((bsz, SSM_COLS), F32),
            pltpu.VMEM((bsz, SSM_COLS), F32),
        ],
        compiler_params=_params(1),
        name="s5_mixer",
    )(u, h0_re, h0_im, lam_re, lam_im, w_in, w_out, d_skip, w_glu_bf, b_glu)


def _ffn_kernel(h_ref, mix_ref, mem_ref, wo_ref, g_ref, wa_ref, wg_ref, cwa_ref, cwg_ref, cba_ref, cbg_ref,
                w2_ref, ctxa_ref, ctxg_ref, out_ref, newa_ref, newg_ref,
                xn_ref, pa_ref, pg_ref, carry_ref, *, seqs, rows_per_seq, tiles_per_seq):
    m = pl.program_id(0)
    f = pl.program_id(1)
    tm = seqs * rows_per_seq

    @pl.when(f == 0)
    def _():
        h1 = h_ref[...] + _dot(mix_ref[...], wo_ref[:MIX_IN, :]) + _dot(mem_ref[...], wo_ref[MIX_IN:, :])
        out_ref[...] = h1
        xn_ref[...] = _rms_rows(h1, g_ref[...]).astype(BF16)

    if tiles_per_seq > 1:
        @pl.when(m == 0)
        def _():
            carry_ref[f] = jnp.zeros(carry_ref.shape[1:], F32)

    xn = xn_ref[...]

    def causal_conv(u, pad_ref, ctx_ref, cw_ref, cb_ref, new_ref, slot):
        ctx = ctx_ref[...]
        if tiles_per_seq > 1:
            prev = carry_ref[f, slot]
            ctx = jnp.where(m % tiles_per_seq == 0, ctx, prev[None])
        pad_ref[:, 6:8, :] = ctx
        pad_ref[:, 8:, :] = u.reshape(seqs, rows_per_seq, FFN_TF)
        cw = cw_ref[...]
        y = (cb_ref[...] + pad_ref[:, 6:6 + rows_per_seq, :] * cw[0:1]
             + pad_ref[:, 7:7 + rows_per_seq, :] * cw[1:2]
             + pad_ref[:, 8:8 + rows_per_seq, :] * cw[2:3])
        last2 = pad_ref[:, rows_per_seq + 6:rows_per_seq + 8, :]
        new_ref[...] = last2
        if tiles_per_seq > 1:
            carry_ref[f, slot] = last2[0]
        return y.reshape(tm, FFN_TF)

    ya = causal_conv(_dot(xn, wa_ref[...]), pa_ref, ctxa_ref, cwa_ref, cba_ref, newa_ref, 0)
    yg = causal_conv(_dot(xn, wg_ref[...]), pg_ref, ctxg_ref, cwg_ref, cbg_ref, newg_ref, 1)
    act = (jax.nn.silu(yg) * ya).astype(BF16)
    out_ref[...] += _dot(act, w2_ref[...])


def _mixout_ffn(h, mix, mem, w_mix_out_bf, norm_g, w_in_bf, conv_w, conv_b, w_out_bf, ctx, rows_per_seq_tile):
    nseq, length, _ = h.shape
    m_rows = nseq * length
    if rows_per_seq_tile >= length:
        seqs, rps, tps = nseq, length, 1
    else:
        seqs, rps, tps = 1, rows_per_seq_tile, length // rows_per_seq_tile
    tm = seqs * rps
    n_m = m_rows // tm
    nf = N_FF_TILES
    kern = functools.partial(_ffn_kernel, seqs=seqs, rows_per_seq=rps, tiles_per_seq=tps)
    row = lambda width: pl.BlockSpec((tm, width), lambda m, f: (m, 0))
    ctx_a = pl.BlockSpec((seqs, CONV_W - 1, FFN_TF), lambda m, f: (m // tps, 0, f))
    ctx_g = pl.BlockSpec((seqs, CONV_W - 1, FFN_TF), lambda m, f: (m // tps, 0, nf + f))
    new_spec = pl.BlockSpec((seqs, CONV_W - 1, FFN_TF), lambda m, f: (m, 0, f))
    out, new_a, new_g = pl.pallas_call(
        kern,
        grid=(n_m, nf),
        in_specs=[
            row(D_MODEL), row(MIX_IN), row(MEM_WIDTH),
            _full((MIX_IN + MEM_WIDTH, D_MODEL)),
            _full((1, D_MODEL)),
            pl.BlockSpec((D_MODEL, FFN_TF), lambda m, f: (0, f)),
            pl.BlockSpec((D_MODEL, FFN_TF), lambda m, f: (0, nf + f)),
            pl.BlockSpec((CONV_W, FFN_TF), lambda m, f: (0, f)),
            pl.BlockSpec((CONV_W, FFN_TF), lambda m, f: (0, nf + f)),
            pl.BlockSpec((1, FFN_TF), lambda m, f: (0, f)),
            pl.BlockSpec((1, FFN_TF), lambda m, f: (0, nf + f)),
            pl.BlockSpec((FFN_TF, D_MODEL), lambda m, f: (f, 0)),
            ctx_a, ctx_g,
        ],
        out_specs=[row(D_MODEL), new_spec, new_spec],
        out_shape=[
            jax.ShapeDtypeStruct((m_rows, D_MODEL), F32),
            jax.ShapeDtypeStruct((n_m * seqs, CONV_W - 1, D_FF), F32),
            jax.ShapeDtypeStruct((n_m * seqs, CONV_W - 1, D_FF), F32),
        ],
        scratch_shapes=[
            pltpu.VMEM((tm, D_MODEL), BF16),
            pltpu.VMEM((seqs, rps + SUBLANES, FFN_TF), F32),
            pltpu.VMEM((seqs, rps + SUBLANES, FFN_TF), F32),
            pltpu.VMEM((nf, 2, CONV_W - 1, FFN_TF), F32),
        ],
        compiler_params=_params(2),
        name="mix_out_conv_ffn",
    )(h.reshape(m_rows, D_MODEL), mix.reshape(m_rows, MIX_IN), mem.reshape(m_rows, MEM_WIDTH),
      w_mix_out_bf, norm_g, w_in_bf, w_in_bf, conv_w, conv_w, conv_b, conv_b, w_out_bf, ctx, ctx)
    new_ctx = jnp.concatenate([new_a, new_g], axis=-1).reshape(nseq, tps, CONV_W - 1, 2 * D_FF)[:, -1]
    return out.reshape(nseq, length, D_MODEL), new_ctx


def _dkv_kernel(h_ref, g_ref, wl_ref, wr_ref, wrr_ref, lg_ref, kg_ref, kgr_ref, cos_ref, sin_ref,
                lat_ref, kr_ref):
    xn = _rms_rows(h_ref[...], g_ref[...]).astype(BF16)
    lat_ref[...] = _rms_rows(_dot(xn, wl_ref[...]), lg_ref[...])
    kr = _dot(xn, wr_ref[...])
    kr_rot = _dot(xn, wrr_ref[...])
    r = lax.rsqrt(jnp.mean(kr * kr, axis=-1, keepdims=True) + EPS)
    kr_ref[...] = (kr * r * kg_ref[...]) * cos_ref[...] + (kr_rot * r * kgr_ref[...]) * sin_ref[...]


def _swap_halves(t, axis=-1):
    a, b = jnp.split(t, 2, axis=axis)
    return jnp.concatenate([b, a], axis=axis)


def _shared_kv_down(h, kv_norm_g, w_dkv_bf, latent_norm_g, krope_norm_g, cos_rows, sin_rows, tm):
    nseq, length, _ = h.shape
    m_rows = nseq * length
    tab_blocks = cos_rows.shape[0] // tm
    w_l = w_dkv_bf[:, :KV_LORA]
    w_r = w_dkv_bf[:, KV_LORA:]
    kg = krope_norm_g.reshape(1, ROPE_DIM)
    lat, kr = pl.pallas_call(
        _dkv_kernel,
        grid=(m_rows // tm,),
        in_specs=[
            pl.BlockSpec((tm, D_MODEL), lambda m: (m, 0)),
            _full((1, D_MODEL)),
            _full((D_MODEL, KV_LORA)), _full((D_MODEL, ROPE_DIM)), _full((D_MODEL, ROPE_DIM)),
            _full((1, KV_LORA)), _full((1, ROPE_DIM)), _full((1, ROPE_DIM)),
            pl.BlockSpec((tm, ROPE_DIM), lambda m: (m % tab_blocks, 0)),
            pl.BlockSpec((tm, ROPE_DIM), lambda m: (m % tab_blocks, 0)),
        ],
        out_specs=[pl.BlockSpec((tm, KV_LORA), lambda m: (m, 0)),
                   pl.BlockSpec((tm, ROPE_DIM), lambda m: (m, 0))],
        out_shape=[jax.ShapeDtypeStruct((m_rows, KV_LORA), F32),
                   jax.ShapeDtypeStruct((m_rows, ROPE_DIM), F32)],
        compiler_params=_params(1),
        name="shared_kv_down",
    )(h.reshape(m_rows, D_MODEL), kv_norm_g.reshape(1, D_MODEL), w_l, w_r, _swap_halves(w_r),
      latent_norm_g.reshape(1, KV_LORA), kg, _swap_halves(kg), cos_rows, sin_rows)
    return lat.reshape(nseq, length, KV_LORA), kr.reshape(nseq, length, ROPE_DIM)


def _kv_up_kernel(lat_ref, wk_ref, wv_ref, seg_ref, g_ref, k_ref, v_ref):
    lat = lat_ref[...].astype(BF16)
    k_ref[...] = _seg_rms(_dot(lat, wk_ref[...]), seg_ref[...], g_ref[...]).astype(BF16)
    v_ref[...] = _dot(lat, wv_ref[...]).astype(BF16)


def _kv_up(latent_rows, w_uk_bf, w_uv_bf, seg_nope, k_g_tiled, tr):
    rows = latent_rows.shape[0]
    width = MLA_HEADS * NOPE_DIM
    return pl.pallas_call(
        _kv_up_kernel,
        grid=(rows // tr,),
        in_specs=[
            pl.BlockSpec((tr, KV_LORA), lambda r: (r, 0)),
            _full((KV_LORA, width)), _full((KV_LORA, width)),
            _full((width, width)), _full((1, width)),
        ],
        out_specs=[pl.BlockSpec((tr, width), lambda r: (r, 0))] * 2,
        out_shape=[jax.ShapeDtypeStruct((rows, width), BF16)] * 2,
        compiler_params=_params(1),
        name="kv_up",
    )(latent_rows, w_uk_bf, w_uv_bf, seg_nope, k_g_tiled)


Q_PRESCALE = MLA_SCALE * math.log2(math.e)


def _q_kernel(z_ref, g_ref, wn_ref, wa_ref, wb_ref, segn_ref, segr_ref, gn_ref, ga_ref, gb_ref,
              cos_ref, sin_ref, kg_ref, wuk_ref, qn_ref, qr_ref, *maybe_qp_ref):
    xn = _rms_rows(z_ref[...], g_ref[...]).astype(BF16)
    qn = _seg_rms(_dot(xn, wn_ref[...]), segn_ref[...], gn_ref[...]) * Q_PRESCALE
    qn_ref[...] = qn.astype(BF16)
    a = _dot(xn, wa_ref[...])
    b = _dot(xn, wb_ref[...])
    r = lax.rsqrt(_dot((a * a).astype(BF16), segr_ref[...]) + EPS)
    rot = (a * r * ga_ref[...]) * cos_ref[...] + (b * r * gb_ref[...]) * sin_ref[...]
    qr_ref[...] = (rot * Q_PRESCALE).astype(BF16)
    if maybe_qp_ref:
        qp_ref, = maybe_qp_ref
        lane = lax.broadcasted_iota(jnp.int32, (1, MLA_HEADS * NOPE_DIM), 1)
        qg = qn * kg_ref[...]
        for head in range(MLA_HEADS):
            qh = jnp.where(lane // NOPE_DIM == head, qg, 0.0).astype(BF16)
            qp_ref[:, head * KV_LORA:(head + 1) * KV_LORA] = _dot_nt(qh, wuk_ref[...]).astype(BF16)


def _q_side(z_mix, q_latent_g, w_uq_bf, q_nope_g, q_rope_g, seg_nope, seg_rope, cos_rows, sin_rows,
            k_g_tiled, w_uk_bf, tm, absorbed):
    nseq, length, _ = z_mix.shape
    m_rows = nseq * length
    tab_blocks = cos_rows.shape[0] // tm
    wn_width = MLA_HEADS * NOPE_DIM
    wr_width = MLA_HEADS * ROPE_DIM
    wp_width = MLA_HEADS * KV_LORA
    w3 = w_uq_bf.reshape(MIX_IN, MLA_HEADS, NOPE_DIM + ROPE_DIM)
    w_n = w3[:, :, :NOPE_DIM].reshape(MIX_IN, wn_width)
    w_a = w3[:, :, NOPE_DIM:]
    w_b = _swap_halves(w_a).reshape(MIX_IN, wr_width)
    w_a = w_a.reshape(MIX_IN, wr_width)
    g_n = jnp.tile(q_nope_g.reshape(1, NOPE_DIM), (1, MLA_HEADS))
    g_a = jnp.tile(q_rope_g.reshape(1, ROPE_DIM), (1, MLA_HEADS))
    g_b = jnp.tile(_swap_halves(q_rope_g.reshape(1, ROPE_DIM)), (1, MLA_HEADS))
    cos_q = jnp.tile(cos_rows, (1, MLA_HEADS))
    sin_q = jnp.tile(sin_rows, (1, MLA_HEADS))
    widths = [wn_width, wr_width] + ([wp_width] if absorbed else [])
    outs = pl.pallas_call(
        _q_kernel,
        grid=(m_rows // tm,),
        in_specs=[
            pl.BlockSpec((tm, MIX_IN), lambda m: (m, 0)),
            _full((1, MIX_IN)),
            _full((MIX_IN, wn_width)), _full((MIX_IN, wr_width)), _full((MIX_IN, wr_width)),
            _full((wn_width, wn_width)), _full((wr_width, wr_width)),
            _full((1, wn_width)), _full((1, wr_width)), _full((1, wr_width)),
            pl.BlockSpec((tm, wr_width), lambda m: (m % tab_blocks, 0)),
            pl.BlockSpec((tm, wr_width), lambda m: (m % tab_blocks, 0)),
            _full((1, wn_width)), _full((KV_LORA, wn_width)),
        ],
        out_specs=[pl.BlockSpec((tm, wd), lambda m: (m, 0)) for wd in widths],
        out_shape=[jax.ShapeDtypeStruct((m_rows, wd), BF16) for wd in widths],
        compiler_params=_params(1),
        name="mla_query",
    )(z_mix.reshape(m_rows, MIX_IN), q_latent_g.reshape(1, MIX_IN), w_n, w_a, w_b, seg_nope, seg_rope,
      g_n, g_a, g_b, cos_q, sin_q, k_g_tiled, w_uk_bf)
    return [o.reshape(nseq, length, o.shape[-1]) for o in outs]


def _attn_kernel(qn_ref, qr_ref, kn_ref, kr_ref, v_ref, o_ref, *, tile):
    pair = pl.program_id(1)
    qi = pl.program_id(2)
    qn = qn_ref[0]
    qr = qr_ref[0]
    lane = lax.broadcasted_iota(jnp.int32, (1, LANES), 1)
    head_lanes = [(lane // NOPE_DIM) == j for j in range(2)]
    qcat = [jnp.concatenate(
        [jnp.where(head_lanes[j], qn, jnp.zeros_like(qn)),
         jnp.where((lane // ROPE_DIM) == 2 * (pair % 2) + j, qr, jnp.zeros_like(qr))], axis=1)
        for j in range(2)]
    row_chunk = lax.broadcasted_iota(jnp.int32, (tile, 1), 0) // CHUNK
    col_chunk = lax.broadcasted_iota(jnp.int32, (1, tile), 1) // CHUNK
    diag_visible = col_chunk <= row_chunk

    def one_block(kb, carry, masked):
        k0 = pl.multiple_of(kb * tile, tile)
        kcat = jnp.concatenate([kn_ref[0, pl.ds(k0, tile), :], kr_ref[0, pl.ds(k0, tile), :]], axis=1)
        vb = v_ref[0, pl.ds(k0, tile), :]
        new = []
        for j in range(2):
            m_i, acc = carry[j]
            s = _dot_nt(qcat[j], kcat)
            if masked:
                s = jnp.where(diag_visible, s, NEG_INF)
            m_new = jnp.maximum(m_i, jnp.max(s, axis=-1, keepdims=True))
            alpha = jnp.exp2(m_i - m_new)
            p = jnp.exp2(s - m_new).astype(BF16)
            v_j = jnp.where(head_lanes[j], vb, jnp.ones_like(vb))
            new.append((m_new, alpha * acc + _dot(p, v_j)))
        return tuple(new)

    init = tuple((jnp.full((tile, 1), NEG_INF, F32), jnp.zeros((tile, LANES), F32)) for _ in range(2))
    carry = lax.fori_loop(0, qi, lambda kb, c: one_block(kb, c, False), init)
    carry = one_block(qi, carry, True)
    out = jnp.zeros((tile, LANES), F32)
    for j in range(2):
        acc = carry[j][1]
        row_sum = pltpu.roll(acc, NOPE_DIM, axis=1)
        out = jnp.where(head_lanes[j], acc * (1.0 / row_sum), out)
    o_ref[0] = out.astype(BF16)


def _mla_attention(qn, qr, kn, kr4, v, tile):
    nseq, length, _ = qn.shape
    kern = functools.partial(_attn_kernel, tile=tile)
    return pl.pallas_call(
        kern,
        grid=(nseq, MLA_HEADS // 2, length // tile),
        in_specs=[
            pl.BlockSpec((1, tile, LANES), lambda b, p, i: (b, i, p)),
            pl.BlockSpec((1, tile, LANES), lambda b, p, i: (b, i, p // 2)),
            pl.BlockSpec((1, length, LANES), lambda b, p, i: (b, 0, p)),
            pl.BlockSpec((1, length, LANES), lambda b, p, i: (b, 0, 0)),
            pl.BlockSpec((1, length, LANES), lambda b, p, i: (b, 0, p)),
        ],
        out_specs=pl.BlockSpec((1, tile, LANES), lambda b, p, i: (b, i, p)),
        out_shape=jax.ShapeDtypeStruct((nseq, length, MLA_HEADS * V_DIM), BF16),
        compiler_params=_params(3),
        name="mla_attention",
    )(qn, qr, kn, kr4, v)


KNORM_ROWS = 16


def _key_norm_kernel(lat_ref, wukt_ref, rt_ref, *, n_chunks, chunk):
    def body(c, carry):
        k0 = pl.multiple_of(c * chunk, chunk)
        kt = _dot_nt(wukt_ref[...], lat_ref[0, pl.ds(k0, chunk), :])
        ss = jnp.sum((kt * kt).reshape(MLA_HEADS, NOPE_DIM, chunk), axis=1)
        r = lax.rsqrt(ss * (1.0 / NOPE_DIM) + EPS)
        rt_ref[0, :, pl.ds(k0, chunk)] = jnp.concatenate(
            [r, jnp.ones((KNORM_ROWS - MLA_HEADS, chunk), F32)], axis=0)
        return carry
    lax.fori_loop(0, n_chunks, body, 0)


def _key_norms(lat_bf, w_ukt_bf, chunk):
    nseq, lk_pad, _ = lat_bf.shape
    kern = functools.partial(_key_norm_kernel, n_chunks=lk_pad // chunk, chunk=chunk)
    return pl.pallas_call(
        kern,
        grid=(nseq,),
        in_specs=[pl.BlockSpec((1, lk_pad, KV_LORA), lambda b: (b, 0, 0)),
                  _full((MLA_HEADS * NOPE_DIM, KV_LORA))],
        out_specs=pl.BlockSpec((1, KNORM_ROWS, lk_pad), lambda b: (b, 0, 0)),
        out_shape=jax.ShapeDtypeStruct((nseq, KNORM_ROWS, lk_pad), F32),
        compiler_params=_params(1),
        name="mla_key_norms",
    )(lat_bf, w_ukt_bf)


def _attn_absorbed_kernel(qp_ref, qr_ref, lat_ref, kr_ref, rt_ref, wuv_ref, o_ref, op_ref,
                          *, tq, tk, q_off, lk_valid, n_blocks):
    b = pl.program_id(0)
    nseq = pl.num_programs(0)
    rows = MLA_HEADS * tq
    qp = qp_ref[0]
    qr = qr_ref[0]
    lane = lax.broadcasted_iota(jnp.int32, (1, LANES), 1)
    per_block = LANES // ROPE_DIM
    qs = jnp.concatenate([qp[:, h * KV_LORA:(h + 1) * KV_LORA] for h in range(MLA_HEADS)], axis=0)
    qrs = jnp.concatenate(
        [jnp.where(lane // ROPE_DIM == h % per_block,
                   qr[:, (h // per_block) * LANES:(h // per_block + 1) * LANES], 0.0).astype(BF16)
         for h in range(MLA_HEADS)], axis=0)
    q_chunk = (q_off + lax.broadcasted_iota(jnp.int32, (rows, 1), 0) % tq) // CHUNK

    def block(kb, carry):
        m_i, l_i, acc = carry
        k0 = pl.multiple_of(kb * tk, tk)
        lat = lat_ref[0, pl.ds(k0, tk), :]
        rt = rt_ref[0, :, pl.ds(k0, tk)]
        knorm = jnp.concatenate([jnp.broadcast_to(rt[h:h + 1, :], (tq, tk)) for h in range(MLA_HEADS)],
                                axis=0)
        s = _dot_nt(qs, lat) * knorm + _dot_nt(qrs, kr_ref[0, pl.ds(k0, tk), :])
        k_pos = k0 + lax.broadcasted_iota(jnp.int32, (1, tk), 1)
        s = jnp.where(k_pos // CHUNK <= q_chunk, s, NEG_INF)
        s = jnp.where(k_pos < lk_valid, s, NEG_INF)
        m_new = jnp.maximum(m_i, jnp.max(s, axis=-1, keepdims=True))
        alpha = jnp.exp2(m_i - m_new)
        p = jnp.exp2(s - m_new)
        l_new = alpha * l_i + jnp.sum(p, axis=-1, keepdims=True)
        return m_new, l_new, alpha * acc + _dot(p.astype(BF16), lat)

    _, l_f, acc = lax.fori_loop(
        0, n_blocks, block,
        (jnp.full((rows, 1), NEG_INF, F32), jnp.zeros((rows, 1), F32), jnp.zeros((rows, KV_LORA), F32)))
    op_ref[b] = (acc * (1.0 / l_f)).astype(BF16)

    @pl.when(b == nseq - 1)
    def _():
        n_all = op_ref.shape[0]
        lane_o = lax.broadcasted_iota(jnp.int32, (1, MLA_HEADS * V_DIM), 1)
        out = jnp.zeros((n_all * tq, MLA_HEADS * V_DIM), F32)
        for h in range(MLA_HEADS):
            x = op_ref[:, h * tq:(h + 1) * tq, :].reshape(n_all * tq, KV_LORA)
            out = jnp.where(lane_o // V_DIM == h, _dot(x, wuv_ref[...]), out)
        o_ref[...] = out.astype(BF16)


def _mla_attention_absorbed(qp, qr, lat_bf, kr4, r_t, w_uv_bf, q_off, lk_valid, tk):
    nseq, tq, _ = qp.shape
    lk_pad = lat_bf.shape[1]
    visible = min(lk_valid, ((q_off + tq - 1) // CHUNK + 1) * CHUNK)
    kern = functools.partial(_attn_absorbed_kernel, tq=tq, tk=tk, q_off=q_off, lk_valid=lk_valid,
                             n_blocks=-(-visible // tk))
    out = pl.pallas_call(
        kern,
        grid=(nseq,),
        in_specs=[
            pl.BlockSpec((1, tq, MLA_HEADS * KV_LORA), lambda b: (b, 0, 0)),
            pl.BlockSpec((1, tq, MLA_HEADS * ROPE_DIM), lambda b: (b, 0, 0)),
            pl.BlockSpec((1, lk_pad, KV_LORA), lambda b: (b, 0, 0)),
            pl.BlockSpec((1, lk_pad, LANES), lambda b: (b, 0, 0)),
            pl.BlockSpec((1, KNORM_ROWS, lk_pad), lambda b: (b, 0, 0)),
            _full((KV_LORA, MLA_HEADS * V_DIM)),
        ],
        out_specs=_full((nseq * tq, MLA_HEADS * V_DIM)),
        out_shape=jax.ShapeDtypeStruct((nseq * tq, MLA_HEADS * V_DIM), BF16),
        scratch_shapes=[pltpu.VMEM((nseq, MLA_HEADS * tq, KV_LORA), BF16)],
        compiler_params=_params(1),
        name="mla_attention_absorbed",
    )(qp, qr, lat_bf, kr4, r_t, w_uv_bf)
    return out.reshape(nseq, tq, MLA_HEADS * V_DIM)


def _trunk(x, mem_k_bf, mem_v_bf, ssm_h0_re, ssm_h0_im, conv_ctx, past_latent, past_krope, w, cfg):
    nseq, length, _ = x.shape
    past = 0 if past_latent is None else past_latent.shape[1]
    cos_t, sin_t = _rope_tables(past, length)
    reps = cfg["rope_rows"] // length
    cos_rows = jnp.tile(cos_t, (reps, 1))
    sin_rows = jnp.tile(sin_t, (reps, 1))
    if conv_ctx is None:
        conv_ctx = jnp.zeros((DEPTH, nseq, CONV_W - 1, 2 * D_FF), F32)
    if ssm_h0_re is None:
        ssm_h0_re = jnp.zeros((N_A_LAYERS, nseq, SSM_GROUPS, SSM_STATE), F32)
        ssm_h0_im = ssm_h0_re
    h = x
    ssm_re_out, ssm_im_out, conv_out = [], [], []
    for layer in range(DEPTH):
        z_mix, mem_out = _mixin(h, w["norm_mix_g"][layer][None], w["w_mix_in"][layer], w["mem_q_g"][layer],
                                w["seg_mem"], mem_k_bf[layer], mem_v_bf[layer], cfg["tm_mixin"])
        if layer < N_A_LAYERS:
            i = layer
            mix_out, s_re, s_im = _s5_mixer(
                z_mix, ssm_h0_re[i].reshape(nseq, SSM_COLS), ssm_h0_im[i].reshape(nseq, SSM_COLS),
                w["lam_re"][i], w["lam_im"][i], w["s5_in"][i], w["s5_out"][i],
                w["ssm_d"][i][None], w["w_glu"][i], w["b_glu"][i][None], cfg["t_chunk"])
            ssm_re_out.append(s_re.reshape(nseq, SSM_GROUPS, SSM_STATE))
            ssm_im_out.append(s_im.reshape(nseq, SSM_GROUPS, SSM_STATE))
        else:
            if layer == N_A_LAYERS:
                new_latent, new_krope = _shared_kv_down(
                    h, w["kv_norm_g"], w["w_dkv"], w["latent_norm_g"], w["krope_norm_g"],
                    cos_rows, sin_rows, cfg["tm_rows"])
                if cfg["absorbed"]:
                    lk_valid = past + length
                    lk_pad = -(-lk_valid // cfg["tk"]) * cfg["tk"]
                    tail = lambda width: jnp.zeros((nseq, lk_pad - lk_valid, width), BF16)
                    lat_bf = jnp.concatenate(
                        [past_latent.astype(BF16), new_latent.astype(BF16), tail(KV_LORA)], axis=1)
                    krope_bf = jnp.concatenate(
                        [past_krope.astype(BF16), new_krope.astype(BF16), tail(ROPE_DIM)], axis=1)
                    r_t = _key_norms(lat_bf, w["w_uk_t"], cfg["tk"])
                else:
                    assert past_latent is None
                    kn, v_all = _kv_up(new_latent.reshape(nseq * length, KV_LORA), w["w_uk"], w["w_uv"],
                                       w["seg_nope"], w["k_nope_g"], cfg["tr_kv"])
                    kn = kn.reshape(nseq, length, MLA_HEADS * NOPE_DIM)
                    v_all = v_all.reshape(nseq, length, MLA_HEADS * V_DIM)
                    krope_bf = new_krope.astype(BF16)
                kr4 = jnp.tile(krope_bf, (1, 1, LANES // ROPE_DIM))
            j = layer - N_A_LAYERS
            q_out = _q_side(z_mix, w["q_latent_norm_g"][j], w["w_uq"][j], w["q_nope_norm_g"][j],
                            w["q_rope_norm_g"][j], w["seg_nope"], w["seg_rope"], cos_rows, sin_rows,
                            w["k_nope_g"], w["w_uk"], cfg["tm_rows"], cfg["absorbed"])
            if cfg["absorbed"]:
                mix_out = _mla_attention_absorbed(q_out[2], q_out[1], lat_bf, kr4, r_t, w["w_uv"],
                                                  past, lk_valid, cfg["tk"])
            else:
                mix_out = _mla_attention(q_out[0], q_out[1], kn, kr4, v_all, cfg["tq"])
        h, ctx = _mixout_ffn(h, mix_out, mem_out, w["w_mix_out"][layer], w["norm_ffn_g"][layer][None],
                             w["w_ffn_in"][layer], w["ffn_conv_w"][layer], w["ffn_conv_b"][layer][None],
                             w["w_ffn_out"][layer], conv_ctx[layer], cfg["ffn_rows"])
        conv_out.append(ctx)
    return h, new_latent, new_krope, jnp.stack(ssm_re_out), jnp.stack(ssm_im_out), jnp.stack(conv_out)


PROMPT_CFG = dict(tm_mixin=512, t_chunk=64, ffn_rows=1024, tm_rows=1024, rope_rows=2048, tr_kv=1024,
                  tq=256, absorbed=False)
SAMPLE_CFG = dict(tm_mixin=32, t_chunk=32, ffn_rows=32, tm_rows=512, rope_rows=512, tk=256, absorbed=True)


def kernel(x_prompt, x_sample, cache_mla_latent, cache_mla_krope, cache_mem_k, cache_mem_v, state_ssm_re, state_ssm_im, state_conv, mem_prompt, norm_mix_g, w_mix_in, w_mix_out, norm_ffn_g, w_ffn_in, ffn_conv_w, ffn_conv_b, w_ffn_out, mem_norm_g, w_mem_kv, mem_q_norm_g, mem_k_norm_g, ssm_a_re, ssm_a_im, ssm_log_dt, ssm_b_re, ssm_b_im, ssm_c_re, ssm_c_im, ssm_d, w_glu, b_glu, kv_norm_g, w_dkv, latent_norm_g, krope_norm_g, w_uk, w_uv, k_nope_norm_g, q_latent_norm_g, w_uq, q_nope_norm_g, q_rope_norm_g):
    bf = lambda t: t.astype(BF16)
    seg_mem = _seg_matrix(MEM_WIDTH, MEM_HEAD_DIM)
    lam_re, lam_im, s5_in, s5_out = _s5_prepare(ssm_a_re, ssm_a_im, ssm_log_dt, ssm_b_re, ssm_b_im,
                                                ssm_c_re, ssm_c_im)
    w = dict(
        norm_mix_g=norm_mix_g, w_mix_in=bf(w_mix_in), w_mix_out=bf(w_mix_out), norm_ffn_g=norm_ffn_g,
        w_ffn_in=bf(w_ffn_in), ffn_conv_w=ffn_conv_w, ffn_conv_b=ffn_conv_b, w_ffn_out=bf(w_ffn_out),
        mem_q_g=jnp.tile(mem_q_norm_g, (1, MEM_HEADS)).reshape(DEPTH, 1, MEM_WIDTH),
        seg_mem=seg_mem,
        seg_nope=_seg_matrix(MLA_HEADS * NOPE_DIM, NOPE_DIM),
        seg_rope=_seg_matrix(MLA_HEADS * ROPE_DIM, ROPE_DIM),
        lam_re=lam_re, lam_im=lam_im, s5_in=s5_in, s5_out=s5_out,
        ssm_d=ssm_d, w_glu=bf(w_glu), b_glu=b_glu,
        kv_norm_g=kv_norm_g, w_dkv=bf(w_dkv), latent_norm_g=latent_norm_g, krope_norm_g=krope_norm_g,
        w_uk=bf(w_uk), w_uk_t=bf(w_uk).T, w_uv=bf(w_uv),
        k_nope_g=jnp.tile(k_nope_norm_g.reshape(1, NOPE_DIM), (1, MLA_HEADS)),
        q_latent_norm_g=q_latent_norm_g, w_uq=bf(w_uq), q_nope_norm_g=q_nope_norm_g,
        q_rope_norm_g=q_rope_norm_g,
    )
    bsz = mem_prompt.shape[0]
    mem_k_p, mem_v_p, mem_k_bf, mem_v_bf = _memory_kv(mem_prompt, mem_norm_g, bf(w_mem_kv), mem_k_norm_g,
                                                      seg_mem)
    y_prompt, lat_p, krope_p, ssm_re_p, ssm_im_p, conv_p = _trunk(
        x_prompt, mem_k_bf, mem_v_bf, None, None, None, None, None, w, PROMPT_CFG)
    dec = cache_mem_k.shape[1]
    y_sample, lat_s, krope_s, ssm_re_s, ssm_im_s, conv_s = _trunk(
        x_sample, bf(cache_mem_k).reshape(DEPTH, dec, N_MEM, MEM_WIDTH),
        bf(cache_mem_v).reshape(DEPTH, dec, N_MEM, MEM_WIDTH),
        state_ssm_re, state_ssm_im, state_conv, cache_mla_latent, cache_mla_krope, w, SAMPLE_CFG)
    shape5 = (DEPTH, bsz, N_MEM, MEM_HEADS, MEM_HEAD_DIM)
    return (y_prompt, y_sample, mem_k_p.reshape(shape5), mem_v_p.reshape(shape5), lat_p, krope_p,
            ssm_re_p, ssm_im_p, conv_p, lat_s, krope_s, ssm_re_s, ssm_im_s, conv_s)
```

```python
import functools
import math

import jax
import jax.numpy as jnp
from jax import lax
from jax.experimental import pallas as pl
from jax.experimental.pallas import tpu as pltpu

F32 = jnp.float32
BF16 = jnp.bfloat16

D_MODEL = 1024
DEPTH = 4
CHUNK = 64
N_A_LAYERS = DEPTH // 2
N_B_LAYERS = DEPTH - N_A_LAYERS
MIX_IN = 768
MEM_HEADS = 4
MEM_HEAD_DIM = 64
MEM_WIDTH = MEM_HEADS * MEM_HEAD_DIM
N_MEM = 256
SSM_GROUP = 16
SSM_GROUPS = MIX_IN // SSM_GROUP
SSM_STATE = 64
SSM_COLS = SSM_GROUPS * SSM_STATE
MLA_HEADS = 12
NOPE_DIM = 64
ROPE_DIM = 32
ROPE_HALF = ROPE_DIM // 2
V_DIM = 64
KV_LORA = 256
ROPE_BASE = 10000.0
MLA_SCALE = (NOPE_DIM + ROPE_DIM) ** -0.5
MEM_SCALE = MEM_HEAD_DIM ** -0.5
D_FF = 2816
CONV_W = 3
EPS = 1e-6
NEG_INF = -1e30

V7X_VMEM_LIMIT_BYTES = 56 * 1024 * 1024
LANES = 128
SUBLANES = 8

S5_BLOCKS = 3
S5_BLOCK_CH = MIX_IN // S5_BLOCKS
S5_BLOCK_ST = SSM_COLS // S5_BLOCKS
FFN_TF = 256
N_FF_TILES = D_FF // FFN_TF
FFN_PAD_SLOTS = 4


def _params(n_axes):
    return pltpu.CompilerParams(
        dimension_semantics=("arbitrary",) * n_axes,
        vmem_limit_bytes=V7X_VMEM_LIMIT_BYTES,
    )


def _dot(a, b):
    return jnp.dot(a, b, preferred_element_type=F32)


def _dot_nt(a, b):
    return lax.dot_general(a, b, (((1,), (1,)), ((), ())), preferred_element_type=F32)


def _rms_rows(x, g):
    ms = jnp.mean(x * x, axis=-1, keepdims=True)
    return x * lax.rsqrt(ms + EPS) * g


def _seg_rms(x, seg_mat, g):
    ms = _dot((x * x).astype(BF16), seg_mat)
    return x * lax.rsqrt(ms + EPS) * g


def _seg_matrix(width, seg):
    idx = jnp.arange(width) // seg
    return jnp.where(idx[:, None] == idx[None, :], 1.0 / seg, 0.0).astype(BF16)


def _full(shape):
    nd = len(shape)
    return pl.BlockSpec(shape, lambda *_: (0,) * nd)


def _rope_table_kernel(pos_ref, inv_ref, cos_ref, sin_ref):
    ang = pos_ref[...] * inv_ref[...]
    cos_ref[...] = jnp.cos(ang)
    sin_ref[...] = jnp.sin(ang)


def _rope_tables(past, length):
    pos = (past + jnp.arange(length, dtype=jnp.int32)).astype(F32)[:, None]
    inv_freq = (1.0 / (ROPE_BASE ** (jnp.arange(0, ROPE_DIM, 2, dtype=F32) / ROPE_DIM)))[None, :]
    cos, sin = pl.pallas_call(
        _rope_table_kernel,
        out_shape=[jax.ShapeDtypeStruct((length, ROPE_HALF), F32)] * 2,
        name="rope_table",
    )(pos, inv_freq)
    cos_t = jnp.concatenate([cos, cos], axis=1)
    sin_t = jnp.concatenate([-sin, sin], axis=1)
    return cos_t, sin_t


def _memkv_kernel(mem_ref, g_ref, w_ref, kg_ref, seg_ref, k_ref, v_ref, kb_ref, vb_ref):
    xn = _rms_rows(mem_ref[0], g_ref[0]).astype(BF16)
    kv = _dot(xn, w_ref[0])
    k = _seg_rms(kv[:, :MEM_WIDTH], seg_ref[...], kg_ref[0])
    v = kv[:, MEM_WIDTH:]
    k_ref[0, 0] = k
    v_ref[0, 0] = v
    kb_ref[0, 0] = k.astype(BF16)
    vb_ref[0, 0] = v.astype(BF16)


def _memory_kv(mem, mem_norm_g, w_mem_kv_bf, mem_k_norm_g, seg_mem):
    bsz = mem.shape[0]
    kg = jnp.tile(mem_k_norm_g, (1, MEM_HEADS)).reshape(DEPTH, 1, MEM_WIDTH)
    out4 = lambda dt: jax.ShapeDtypeStruct((DEPTH, bsz, N_MEM, MEM_WIDTH), dt)
    spec4 = pl.BlockSpec((1, 1, N_MEM, MEM_WIDTH), lambda l, b: (l, b, 0, 0))
    return pl.pallas_call(
        _memkv_kernel,
        grid=(DEPTH, bsz),
        in_specs=[
            pl.BlockSpec((1, N_MEM, D_MODEL), lambda l, b: (b, 0, 0)),
            pl.BlockSpec((1, 1, D_MODEL), lambda l, b: (l, 0, 0)),
            pl.BlockSpec((1, D_MODEL, 2 * MEM_WIDTH), lambda l, b: (l, 0, 0)),
            pl.BlockSpec((1, 1, MEM_WIDTH), lambda l, b: (l, 0, 0)),
            _full((MEM_WIDTH, MEM_WIDTH)),
        ],
        out_specs=[spec4, spec4, spec4, spec4],
        out_shape=[out4(F32), out4(F32), out4(BF16), out4(BF16)],
        compiler_params=_params(2),
        name="memory_kv",
    )(mem, mem_norm_g.reshape(DEPTH, 1, D_MODEL), w_mem_kv_bf, kg, seg_mem)


def _mixin_kernel(h_ref, g_ref, w_ref, qg_ref, seg_ref, k_ref, v_ref, zmix_ref, mem_ref):
    xn = _rms_rows(h_ref[0], g_ref[...]).astype(BF16)
    z = _dot(xn, w_ref[...])
    zmix_ref[0] = z[:, :MIX_IN]
    mq = _seg_rms(z[:, MIX_IN:], seg_ref[...], qg_ref[...]).astype(BF16)
    k = k_ref[0]
    v = v_ref[0]
    lane = lax.broadcasted_iota(jnp.int32, (1, MEM_WIDTH), 1)
    out = jnp.zeros(mq.shape, F32)
    for head in range(MEM_HEADS):
        in_head = (lane // MEM_HEAD_DIM) == head
        qh = jnp.where(in_head, mq, jnp.zeros_like(mq))
        s = _dot_nt(qh, k) * MEM_SCALE
        p = jnp.exp(s - jnp.max(s, axis=-1, keepdims=True))
        p = p * (1.0 / jnp.sum(p, axis=-1, keepdims=True))
        o = _dot(p.astype(BF16), v)
        out = jnp.where(in_head, o, out)
    mem_ref[0] = out.astype(BF16)


def _mixin(h, g, w_bf, qg_tiled, seg_mem, k_bf, v_bf, tm):
    bsz, length, _ = h.shape
    return pl.pallas_call(
        _mixin_kernel,
        grid=(bsz, length // tm),
        in_specs=[
            pl.BlockSpec((1, tm, D_MODEL), lambda b, i: (b, i, 0)),
            _full((1, D_MODEL)),
            _full((D_MODEL, MIX_IN + MEM_WIDTH)),
            _full((1, MEM_WIDTH)),
            _full((MEM_WIDTH, MEM_WIDTH)),
            pl.BlockSpec((1, N_MEM, MEM_WIDTH), lambda b, i: (b, 0, 0)),
            pl.BlockSpec((1, N_MEM, MEM_WIDTH), lambda b, i: (b, 0, 0)),
        ],
        out_specs=[
            pl.BlockSpec((1, tm, MIX_IN), lambda b, i: (b, i, 0)),
            pl.BlockSpec((1, tm, MEM_WIDTH), lambda b, i: (b, i, 0)),
        ],
        out_shape=[
            jax.ShapeDtypeStruct((bsz, length, MIX_IN), F32),
            jax.ShapeDtypeStruct((bsz, length, MEM_WIDTH), BF16),
        ],
        compiler_params=_params(2),
        name="mix_in_mem_attn",
    )(h, g, w_bf, qg_tiled, seg_mem, k_bf, v_bf)


def _s5_prep_kernel(are_ref, aim_ref, ldt_ref, bre_ref, bim_ref, lre_ref, lim_ref, bbre_ref, bbim_ref):
    a_re, a_im = are_ref[0], aim_ref[0]
    dt = jnp.exp(ldt_ref[0])
    mag = jnp.exp(a_re * dt)
    lam_re = mag * jnp.cos(a_im * dt)
    lam_im = mag * jnp.sin(a_im * dt)
    den = a_re * a_re + a_im * a_im
    x_re = lam_re - 1.0
    f_re = (x_re * a_re + lam_im * a_im) / den
    f_im = (lam_im * a_re - x_re * a_im) / den
    b_re, b_im = bre_ref[0], bim_ref[0]
    lre_ref[0] = lam_re
    lim_ref[0] = lam_im
    bbre_ref[0] = f_re * b_re - f_im * b_im
    bbim_ref[0] = f_re * b_im + f_im * b_re


def _s5_prepare(a_re, a_im, log_dt, b_re, b_im, c_re, c_im):
    n = a_re.shape[0]
    col = lambda t: t.reshape(n, SSM_COLS, 1)
    ldt = jnp.broadcast_to(log_dt[:, :, None], (n, SSM_GROUPS, SSM_STATE))
    col_spec = pl.BlockSpec((1, SSM_COLS, 1), lambda l: (l, 0, 0))
    b_spec = pl.BlockSpec((1, SSM_COLS, SSM_GROUP), lambda l: (l, 0, 0))
    lam_re, lam_im, bb_re, bb_im = pl.pallas_call(
        _s5_prep_kernel,
        grid=(n,),
        in_specs=[col_spec, col_spec, col_spec, b_spec, b_spec],
        out_specs=[col_spec, col_spec, b_spec, b_spec],
        out_shape=[jax.ShapeDtypeStruct((n, SSM_COLS, 1), F32)] * 2
        + [jax.ShapeDtypeStruct((n, SSM_COLS, SSM_GROUP), F32)] * 2,
        compiler_params=_params(1),
        name="s5_discretise",
    )(col(a_re), col(a_im), col(ldt),
      b_re.reshape(n, SSM_COLS, SSM_GROUP), b_im.reshape(n, SSM_COLS, SSM_GROUP))
    gpb = SSM_GROUPS // S5_BLOCKS
    eye = jnp.eye(gpb, dtype=F32)

    def in_blocks(bb):
        t = bb.reshape(n, S5_BLOCKS, gpb, SSM_STATE, SSM_GROUP)
        return jnp.einsum("ljgnc,gh->ljgchn", t, eye).reshape(n, S5_BLOCKS, S5_BLOCK_CH, S5_BLOCK_ST)

    def out_blocks(c):
        t = c.reshape(n, S5_BLOCKS, gpb, SSM_GROUP, SSM_STATE)
        return jnp.einsum("ljgcn,gh->ljgnhc", t, eye).reshape(n, S5_BLOCKS, S5_BLOCK_ST, S5_BLOCK_CH)

    w_in = jnp.concatenate([in_blocks(bb_re), in_blocks(bb_im)], axis=-1).astype(BF16)
    w_out = jnp.concatenate([out_blocks(c_re), -out_blocks(c_im)], axis=-2).astype(BF16)
    return lam_re.reshape(n, 1, SSM_COLS), lam_im.reshape(n, 1, SSM_COLS), w_in, w_out


def _s5_kernel(u_ref, h0re_ref, h0im_ref, lre_ref, lim_ref, win_ref, wout_ref, d_ref, wglu_ref, bglu_ref,
               out_ref, sre_out_ref, sim_out_ref, hb_ref, sre_ref, sim_ref, *, bsz, t_chunk, col_block):
    c = pl.program_id(0)
    rows = bsz * t_chunk

    @pl.when(c == 0)
    def _():
        sre_ref[...] = h0re_ref[...]
        sim_ref[...] = h0im_ref[...]

    u = u_ref[...]
    ut = jnp.swapaxes(u, 0, 1).reshape(rows, MIX_IN).astype(BF16)
    ys = []
    for j in range(S5_BLOCKS):
        cols = slice(j * 2 * S5_BLOCK_ST, (j + 1) * 2 * S5_BLOCK_ST)
        hb_ref[:, cols] = _dot(ut[:, j * S5_BLOCK_CH:(j + 1) * S5_BLOCK_CH], win_ref[j])
        for sub in range(S5_BLOCK_ST // col_block):
            nat = j * S5_BLOCK_ST + sub * col_block
            cre = j * 2 * S5_BLOCK_ST + sub * col_block
            cim = cre + S5_BLOCK_ST
            lam_r = jnp.broadcast_to(lre_ref[:, nat:nat + col_block], (bsz, col_block))
            lam_i = jnp.broadcast_to(lim_ref[:, nat:nat + col_block], (bsz, col_block))
            s_r = sre_ref[:, nat:nat + col_block]
            s_i = sim_ref[:, nat:nat + col_block]
            for t in range(t_chunk):
                rows_t = slice(t * bsz, (t + 1) * bsz)
                n_r = lam_r * s_r - lam_i * s_i + hb_ref[rows_t, cre:cre + col_block]
                n_i = lam_r * s_i + lam_i * s_r + hb_ref[rows_t, cim:cim + col_block]
                hb_ref[rows_t, cre:cre + col_block] = n_r
                hb_ref[rows_t, cim:cim + col_block] = n_i
                s_r, s_i = n_r, n_i
            sre_ref[:, nat:nat + col_block] = s_r
            sim_ref[:, nat:nat + col_block] = s_i
        ys.append(_dot(hb_ref[:, cols].astype(BF16), wout_ref[j]))
    yt = jnp.concatenate(ys, axis=1).reshape(t_chunk, bsz, MIX_IN)
    y = jnp.swapaxes(yt, 0, 1) + d_ref[...] * u
    y = jax.nn.gelu(y).reshape(rows, MIX_IN)
    gate = _dot(y.astype(BF16), wglu_ref[...]) + bglu_ref[...]
    out_ref[...] = (y * jax.nn.sigmoid(gate)).reshape(bsz, t_chunk, MIX_IN).astype(BF16)

    @pl.when(c == pl.num_programs(0) - 1)
    def _():
        sre_out_ref[...] = sre_ref[...]
        sim_out_ref[...] = sim_ref[...]


def _s5_mixer(u, h0_re, h0_im, lam_re, lam_im, w_in, w_out, d_skip, w_glu_bf, b_glu, t_chunk):
    bsz, length, _ = u.shape
    rows = bsz * t_chunk
    col_block = (SUBLANES * 512) // bsz
    kern = functools.partial(_s5_kernel, bsz=bsz, t_chunk=t_chunk, col_block=col_block)
    state = jax.ShapeDtypeStruct((bsz, SSM_COLS), F32)
    return pl.pallas_call(
        kern,
        grid=(length // t_chunk,),
        in_specs=[
            pl.BlockSpec((bsz, t_chunk, MIX_IN), lambda c: (0, c, 0)),
            _full((bsz, SSM_COLS)), _full((bsz, SSM_COLS)),
            _full((1, SSM_COLS)), _full((1, SSM_COLS)),
            _full((S5_BLOCKS, S5_BLOCK_CH, 2 * S5_BLOCK_ST)),
            _full((S5_BLOCKS, 2 * S5_BLOCK_ST, S5_BLOCK_CH)),
            _full((1, MIX_IN)), _full((MIX_IN, MIX_IN)), _full((1, MIX_IN)),
        ],
        out_specs=[
            pl.BlockSpec((bsz, t_chunk, MIX_IN), lambda c: (0, c, 0)),
            _full((bsz, SSM_COLS)), _full((bsz, SSM_COLS)),
        ],
        out_shape=[jax.ShapeDtypeStruct((bsz, length, MIX_IN), BF16), state, state],
        scratch_shapes=[
            pltpu.VMEM((rows, 2 * SSM_COLS), F32),
            pltpu.VMEM((bsz, SSM_COLS), F32),
            pltpu.VMEM((bsz, SSM_COLS), F32),
        ],
        compiler_params=_params(1),
        name="s5_mixer",
    )(u, h0_re, h0_im, lam_re, lam_im, w_in, w_out, d_skip, w_glu_bf, b_glu)


def _ffn_kernel(h_ref, mix_ref, mem_ref, wo_ref, g_ref, win_ref, cw_ref, cb_ref, w2_ref, ctx_ref,
                out_ref, new_ref, act_ref, pad_ref, carry_ref, *, seqs, rows_per_seq, tiles_per_seq):
    m = pl.program_id(0)
    tm = seqs * rows_per_seq
    h1 = h_ref[...] + _dot(mix_ref[...], wo_ref[:MIX_IN, :]) + _dot(mem_ref[...], wo_ref[MIX_IN:, :])
    out_ref[...] = h1
    xn = _rms_rows(h1, g_ref[...]).astype(BF16)

    if tiles_per_seq > 1:
        @pl.when(m == 0)
        def _():
            carry_ref[...] = jnp.zeros(carry_ref.shape, F32)

    def causal_conv(col, slot):
        cols = slice(col, col + FFN_TF)
        u = _dot(xn, win_ref[:, cols])
        ctx = ctx_ref[:, :, cols]
        if tiles_per_seq > 1:
            ctx = jnp.where(m % tiles_per_seq == 0, ctx, carry_ref[:, cols][None])
        pad = pad_ref.at[slot]
        pad[:, 6:8, :] = ctx
        pad[:, 8:, :] = u.reshape(seqs, rows_per_seq, FFN_TF)
        cw = cw_ref[:, cols]
        y = (cb_ref[:, cols] + pad[:, 6:6 + rows_per_seq, :] * cw[0:1]
             + pad[:, 7:7 + rows_per_seq, :] * cw[1:2]
             + pad[:, 8:8 + rows_per_seq, :] * cw[2:3])
        last2 = pad[:, rows_per_seq + 6:rows_per_seq + 8, :]
        new_ref[:, :, cols] = last2
        if tiles_per_seq > 1:
            carry_ref[:, cols] = last2[0]
        return y.reshape(tm, FFN_TF)

    for f in range(N_FF_TILES):
        ya = causal_conv(f * FFN_TF, (2 * f) % FFN_PAD_SLOTS)
        yg = causal_conv(D_FF + f * FFN_TF, (2 * f + 1) % FFN_PAD_SLOTS)
        act_ref[:, f * FFN_TF:(f + 1) * FFN_TF] = (jax.nn.silu(yg) * ya).astype(BF16)
    out_ref[...] += _dot(act_ref[...], w2_ref[...])


def _mixout_ffn(h, mix, mem, w, layer, ctx, rows_per_seq_tile):
    nseq, length, _ = h.shape
    m_rows = nseq * length
    if rows_per_seq_tile >= length:
        seqs, rps, tps = nseq, length, 1
    else:
        seqs, rps, tps = 1, rows_per_seq_tile, length // rows_per_seq_tile
    tm = seqs * rps
    n_m = m_rows // tm
    kern = functools.partial(_ffn_kernel, seqs=seqs, rows_per_seq=rps, tiles_per_seq=tps)
    row = lambda width: pl.BlockSpec((tm, width), lambda m: (m, 0))
    resident = lambda *shape: pl.BlockSpec((None,) + shape, lambda m: (layer,) + (0,) * len(shape),
                                           pipeline_mode=pl.Buffered(1))
    out, new = pl.pallas_call(
        kern,
        grid=(n_m,),
        in_specs=[
            row(D_MODEL), row(MIX_IN), row(MEM_WIDTH),
            resident(MIX_IN + MEM_WIDTH, D_MODEL),
            resident(1, D_MODEL),
            resident(D_MODEL, 2 * D_FF),
            resident(CONV_W, 2 * D_FF),
            resident(1, 2 * D_FF),
            resident(D_FF, D_MODEL),
            pl.BlockSpec((seqs, CONV_W - 1, 2 * D_FF), lambda m: (m // tps, 0, 0)),
        ],
        out_specs=[row(D_MODEL), pl.BlockSpec((seqs, CONV_W - 1, 2 * D_FF), lambda m: (m, 0, 0))],
        out_shape=[
            jax.ShapeDtypeStruct((m_rows, D_MODEL), F32),
            jax.ShapeDtypeStruct((n_m * seqs, CONV_W - 1, 2 * D_FF), F32),
        ],
        scratch_shapes=[
            pltpu.VMEM((tm, D_FF), BF16),
            pltpu.VMEM((FFN_PAD_SLOTS, seqs, rps + SUBLANES, FFN_TF), F32),
            pltpu.VMEM((CONV_W - 1, 2 * D_FF), F32),
        ],
        compiler_params=_params(1),
        name="mix_out_conv_ffn",
    )(h.reshape(m_rows, D_MODEL), mix.reshape(m_rows, MIX_IN), mem.reshape(m_rows, MEM_WIDTH),
      w["w_mix_out"], w["norm_ffn_g"], w["w_ffn_in"], w["ffn_conv_w"], w["ffn_conv_b"], w["w_ffn_out"], ctx)
    new_ctx = new.reshape(nseq, tps, CONV_W - 1, 2 * D_FF)[:, -1]
    return out.reshape(nseq, length, D_MODEL), new_ctx


def _dkv_kernel(h_ref, g_ref, wl_ref, wr_ref, wrr_ref, lg_ref, kg_ref, kgr_ref, cos_ref, sin_ref,
                lat_ref, kr_ref):
    xn = _rms_rows(h_ref[...], g_ref[...]).astype(BF16)
    lat_ref[...] = _rms_rows(_dot(xn, wl_ref[...]), lg_ref[...])
    kr = _dot(xn, wr_ref[...])
    kr_rot = _dot(xn, wrr_ref[...])
    r = lax.rsqrt(jnp.mean(kr * kr, axis=-1, keepdims=True) + EPS)
    kr_ref[...] = (kr * r * kg_ref[...]) * cos_ref[...] + (kr_rot * r * kgr_ref[...]) * sin_ref[...]


def _swap_halves(t, axis=-1):
    a, b = jnp.split(t, 2, axis=axis)
    return jnp.concatenate([b, a], axis=axis)


def _shared_kv_down(h, kv_norm_g, w_dkv_bf, latent_norm_g, krope_norm_g, cos_rows, sin_rows, tm):
    nseq, length, _ = h.shape
    m_rows = nseq * length
    tab_blocks = cos_rows.shape[0] // tm
    w_l = w_dkv_bf[:, :KV_LORA]
    w_r = w_dkv_bf[:, KV_LORA:]
    kg = krope_norm_g.reshape(1, ROPE_DIM)
    lat, kr = pl.pallas_call(
        _dkv_kernel,
        grid=(m_rows // tm,),
        in_specs=[
            pl.BlockSpec((tm, D_MODEL), lambda m: (m, 0)),
            _full((1, D_MODEL)),
            _full((D_MODEL, KV_LORA)), _full((D_MODEL, ROPE_DIM)), _full((D_MODEL, ROPE_DIM)),
            _full((1, KV_LORA)), _full((1, ROPE_DIM)), _full((1, ROPE_DIM)),
            pl.BlockSpec((tm, ROPE_DIM), lambda m: (m % tab_blocks, 0)),
            pl.BlockSpec((tm, ROPE_DIM), lambda m: (m % tab_blocks, 0)),
        ],
        out_specs=[pl.BlockSpec((tm, KV_LORA), lambda m: (m, 0)),
                   pl.BlockSpec((tm, ROPE_DIM), lambda m: (m, 0))],
        out_shape=[jax.ShapeDtypeStruct((m_rows, KV_LORA), F32),
                   jax.ShapeDtypeStruct((m_rows, ROPE_DIM), F32)],
        compiler_params=_params(1),
        name="shared_kv_down",
    )(h.reshape(m_rows, D_MODEL), kv_norm_g.reshape(1, D_MODEL), w_l, w_r, _swap_halves(w_r),
      latent_norm_g.reshape(1, KV_LORA), kg, _swap_halves(kg), cos_rows, sin_rows)
    return lat.reshape(nseq, length, KV_LORA), kr.reshape(nseq, length, ROPE_DIM)


def _kv_up_kernel(lat_ref, wk_ref, wv_ref, seg_ref, g_ref, k_ref, v_ref):
    lat = lat_ref[...].astype(BF16)
    k_ref[...] = _seg_rms(_dot(lat, wk_ref[...]), seg_ref[...], g_ref[...]).astype(BF16)
    v_ref[...] = _dot(lat, wv_ref[...]).astype(BF16)


def _kv_up(latent_rows, w_uk_bf, w_uv_bf, seg_nope, k_g_tiled, tr):
    rows = latent_rows.shape[0]
    width = MLA_HEADS * NOPE_DIM
    return pl.pallas_call(
        _kv_up_kernel,
        grid=(rows // tr,),
        in_specs=[
            pl.BlockSpec((tr, KV_LORA), lambda r: (r, 0)),
            _full((KV_LORA, width)), _full((KV_LORA, width)),
            _full((width, width)), _full((1, width)),
        ],
        out_specs=[pl.BlockSpec((tr, width), lambda r: (r, 0))] * 2,
        out_shape=[jax.ShapeDtypeStruct((rows, width), BF16)] * 2,
        compiler_params=_params(1),
        name="kv_up",
    )(latent_rows, w_uk_bf, w_uv_bf, seg_nope, k_g_tiled)


Q_PRESCALE = MLA_SCALE * math.log2(math.e)


def _q_kernel(z_ref, g_ref, wn_ref, wa_ref, wb_ref, segn_ref, segr_ref, gn_ref, ga_ref, gb_ref,
              cos_ref, sin_ref, kg_ref, wuk_ref, qn_ref, qr_ref, *maybe_qp_ref):
    xn = _rms_rows(z_ref[...], g_ref[...]).astype(BF16)
    qn = _seg_rms(_dot(xn, wn_ref[...]), segn_ref[...], gn_ref[...]) * Q_PRESCALE
    qn_ref[...] = qn.astype(BF16)
    a = _dot(xn, wa_ref[...])
    b = _dot(xn, wb_ref[...])
    r = lax.rsqrt(_dot((a * a).astype(BF16), segr_ref[...]) + EPS)
    rot = (a * r * ga_ref[...]) * cos_ref[...] + (b * r * gb_ref[...]) * sin_ref[...]
    qr_ref[...] = (rot * Q_PRESCALE).astype(BF16)
    if maybe_qp_ref:
        qp_ref, = maybe_qp_ref
        lane = lax.broadcasted_iota(jnp.int32, (1, MLA_HEADS * NOPE_DIM), 1)
        qg = qn * kg_ref[...]
        for head in range(MLA_HEADS):
            qh = jnp.where(lane // NOPE_DIM == head, qg, 0.0).astype(BF16)
            qp_ref[:, head * KV_LORA:(head + 1) * KV_LORA] = _dot_nt(qh, wuk_ref[...]).astype(BF16)


def _q_side(z_mix, q_latent_g, w_uq_bf, q_nope_g, q_rope_g, seg_nope, seg_rope, cos_rows, sin_rows,
            k_g_tiled, w_uk_bf, tm, absorbed):
    nseq, length, _ = z_mix.shape
    m_rows = nseq * length
    tab_blocks = cos_rows.shape[0] // tm
    wn_width = MLA_HEADS * NOPE_DIM
    wr_width = MLA_HEADS * ROPE_DIM
    wp_width = MLA_HEADS * KV_LORA
    w3 = w_uq_bf.reshape(MIX_IN, MLA_HEADS, NOPE_DIM + ROPE_DIM)
    w_n = w3[:, :, :NOPE_DIM].reshape(MIX_IN, wn_width)
    w_a = w3[:, :, NOPE_DIM:]
    w_b = _swap_halves(w_a).reshape(MIX_IN, wr_width)
    w_a = w_a.reshape(MIX_IN, wr_width)
    g_n = jnp.tile(q_nope_g.reshape(1, NOPE_DIM), (1, MLA_HEADS))
    g_a = jnp.tile(q_rope_g.reshape(1, ROPE_DIM), (1, MLA_HEADS))
    g_b = jnp.tile(_swap_halves(q_rope_g.reshape(1, ROPE_DIM)), (1, MLA_HEADS))
    cos_q = jnp.tile(cos_rows, (1, MLA_HEADS))
    sin_q = jnp.tile(sin_rows, (1, MLA_HEADS))
    widths = [wn_width, wr_width] + ([wp_width] if absorbed else [])
    outs = pl.pallas_call(
        _q_kernel,
        grid=(m_rows // tm,),
        in_specs=[
            pl.BlockSpec((tm, MIX_IN), lambda m: (m, 0)),
            _full((1, MIX_IN)),
            _full((MIX_IN, wn_width)), _full((MIX_IN, wr_width)), _full((MIX_IN, wr_width)),
            _full((wn_width, wn_width)), _full((wr_width, wr_width)),
            _full((1, wn_width)), _full((1, wr_width)), _full((1, wr_width)),
            pl.BlockSpec((tm, wr_width), lambda m: (m % tab_blocks, 0)),
            pl.BlockSpec((tm, wr_width), lambda m: (m % tab_blocks, 0)),
            _full((1, wn_width)), _full((KV_LORA, wn_width)),
        ],
        out_specs=[pl.BlockSpec((tm, wd), lambda m: (m, 0)) for wd in widths],
        out_shape=[jax.ShapeDtypeStruct((m_rows, wd), BF16) for wd in widths],
        compiler_params=_params(1),
        name="mla_query",
    )(z_mix.reshape(m_rows, MIX_IN), q_latent_g.reshape(1, MIX_IN), w_n, w_a, w_b, seg_nope, seg_rope,
      g_n, g_a, g_b, cos_q, sin_q, k_g_tiled, w_uk_bf)
    return [o.reshape(nseq, length, o.shape[-1]) for o in outs]


def _attn_kernel(qn_ref, qr_ref, kn_ref, kr_ref, v_ref, o_ref, *, tile):
    pair = pl.program_id(1)
    qi = pl.program_id(2)
    qn = qn_ref[0]
    qr = qr_ref[0]
    lane = lax.broadcasted_iota(jnp.int32, (1, LANES), 1)
    head_lanes = [(lane // NOPE_DIM) == j for j in range(2)]
    qcat = [jnp.concatenate(
        [jnp.where(head_lanes[j], qn, jnp.zeros_like(qn)),
         jnp.where((lane // ROPE_DIM) == 2 * (pair % 2) + j, qr, jnp.zeros_like(qr))], axis=1)
        for j in range(2)]
    row_chunk = lax.broadcasted_iota(jnp.int32, (tile, 1), 0) // CHUNK
    col_chunk = lax.broadcasted_iota(jnp.int32, (1, tile), 1) // CHUNK
    diag_visible = col_chunk <= row_chunk

    def one_block(kb, carry, masked):
        k0 = pl.multiple_of(kb * tile, tile)
        kcat = jnp.concatenate([kn_ref[0, pl.ds(k0, tile), :], kr_ref[0, pl.ds(k0, tile), :]], axis=1)
        vb = v_ref[0, pl.ds(k0, tile), :]
        new = []
        for j in range(2):
            m_i, acc = carry[j]
            s = _dot_nt(qcat[j], kcat)
            if masked:
                s = jnp.where(diag_visible, s, NEG_INF)
            m_new = jnp.maximum(m_i, jnp.max(s, axis=-1, keepdims=True))
            alpha = jnp.exp2(m_i - m_new)
            p = jnp.exp2(s - m_new).astype(BF16)
            v_j = jnp.where(head_lanes[j], vb, jnp.ones_like(vb))
            new.append((m_new, alpha * acc + _dot(p, v_j)))
        return tuple(new)

    init = tuple((jnp.full((tile, 1), NEG_INF, F32), jnp.zeros((tile, LANES), F32)) for _ in range(2))
    carry = lax.fori_loop(0, qi, lambda kb, c: one_block(kb, c, False), init)
    carry = one_block(qi, carry, True)
    out = jnp.zeros((tile, LANES), F32)
    for j in range(2):
        acc = carry[j][1]
        row_sum = pltpu.roll(acc, NOPE_DIM, axis=1)
        out = jnp.where(head_lanes[j], acc * (1.0 / row_sum), out)
    o_ref[0] = out.astype(BF16)


def _mla_attention(qn, qr, kn, kr4, v, tile):
    nseq, length, _ = qn.shape
    kern = functools.partial(_attn_kernel, tile=tile)
    return pl.pallas_call(
        kern,
        grid=(nseq, MLA_HEADS // 2, length // tile),
        in_specs=[
            pl.BlockSpec((1, tile, LANES), lambda b, p, i: (b, i, p)),
            pl.BlockSpec((1, tile, LANES), lambda b, p, i: (b, i, p // 2)),
            pl.BlockSpec((1, length, LANES), lambda b, p, i: (b, 0, p)),
            pl.BlockSpec((1, length, LANES), lambda b, p, i: (b, 0, 0)),
            pl.BlockSpec((1, length, LANES), lambda b, p, i: (b, 0, p)),
        ],
        out_specs=pl.BlockSpec((1, tile, LANES), lambda b, p, i: (b, i, p)),
        out_shape=jax.ShapeDtypeStruct((nseq, length, MLA_HEADS * V_DIM), BF16),
        compiler_params=_params(3),
        name="mla_attention",
    )(qn, qr, kn, kr4, v)


KNORM_ROWS = 16


def _key_norm_kernel(lat_ref, wukt_ref, rt_ref, *, n_chunks, chunk):
    def body(c, carry):
        k0 = pl.multiple_of(c * chunk, chunk)
        kt = _dot_nt(wukt_ref[...], lat_ref[0, pl.ds(k0, chunk), :])
        ss = jnp.sum((kt * kt).reshape(MLA_HEADS, NOPE_DIM, chunk), axis=1)
        r = lax.rsqrt(ss * (1.0 / NOPE_DIM) + EPS)
        rt_ref[0, :, pl.ds(k0, chunk)] = jnp.concatenate(
            [r, jnp.ones((KNORM_ROWS - MLA_HEADS, chunk), F32)], axis=0)
        return carry
    lax.fori_loop(0, n_chunks, body, 0)


def _key_norms(lat_bf, w_ukt_bf, chunk):
    nseq, lk_pad, _ = lat_bf.shape
    kern = functools.partial(_key_norm_kernel, n_chunks=lk_pad // chunk, chunk=chunk)
    return pl.pallas_call(
        kern,
        grid=(nseq,),
        in_specs=[pl.BlockSpec((1, lk_pad, KV_LORA), lambda b: (b, 0, 0)),
                  _full((MLA_HEADS * NOPE_DIM, KV_LORA))],
        out_specs=pl.BlockSpec((1, KNORM_ROWS, lk_pad), lambda b: (b, 0, 0)),
        out_shape=jax.ShapeDtypeStruct((nseq, KNORM_ROWS, lk_pad), F32),
        compiler_params=_params(1),
        name="mla_key_norms",
    )(lat_bf, w_ukt_bf)


def _attn_absorbed_kernel(qp_ref, qr_ref, lat_ref, kr_ref, rt_ref, wuv_ref, o_ref, op_ref,
                          *, tq, tk, q_off, lk_valid, n_blocks):
    b = pl.program_id(0)
    nseq = pl.num_programs(0)
    rows = MLA_HEADS * tq
    qp = qp_ref[0]
    qr = qr_ref[0]
    lane = lax.broadcasted_iota(jnp.int32, (1, LANES), 1)
    per_block = LANES // ROPE_DIM
    qs = jnp.concatenate([qp[:, h * KV_LORA:(h + 1) * KV_LORA] for h in range(MLA_HEADS)], axis=0)
    qrs = jnp.concatenate(
        [jnp.where(lane // ROPE_DIM == h % per_block,
                   qr[:, (h // per_block) * LANES:(h // per_block + 1) * LANES], 0.0).astype(BF16)
         for h in range(MLA_HEADS)], axis=0)
    q_chunk = (q_off + lax.broadcasted_iota(jnp.int32, (rows, 1), 0) % tq) // CHUNK

    def block(kb, carry):
        m_i, l_i, acc = carry
        k0 = pl.multiple_of(kb * tk, tk)
        lat = lat_ref[0, pl.ds(k0, tk), :]
        rt = rt_ref[0, :, pl.ds(k0, tk)]
        knorm = jnp.concatenate([jnp.broadcast_to(rt[h:h + 1, :], (tq, tk)) for h in range(MLA_HEADS)],
                                axis=0)
        s = _dot_nt(qs, lat) * knorm + _dot_nt(qrs, kr_ref[0, pl.ds(k0, tk), :])
        k_pos = k0 + lax.broadcasted_iota(jnp.int32, (1, tk), 1)
        s = jnp.where(k_pos // CHUNK <= q_chunk, s, NEG_INF)
        s = jnp.where(k_pos < lk_valid, s, NEG_INF)
        m_new = jnp.maximum(m_i, jnp.max(s, axis=-1, keepdims=True))
        alpha = jnp.exp2(m_i - m_new)
        p = jnp.exp2(s - m_new)
        l_new = alpha * l_i + jnp.sum(p, axis=-1, keepdims=True)
        return m_new, l_new, alpha * acc + _dot(p.astype(BF16), lat)

    _, l_f, acc = lax.fori_loop(
        0, n_blocks, block,
        (jnp.full((rows, 1), NEG_INF, F32), jnp.zeros((rows, 1), F32), jnp.zeros((rows, KV_LORA), F32)))
    op_ref[b] = (acc * (1.0 / l_f)).astype(BF16)

    @pl.when(b == nseq - 1)
    def _():
        n_all = op_ref.shape[0]
        lane_o = lax.broadcasted_iota(jnp.int32, (1, MLA_HEADS * V_DIM), 1)
        out = jnp.zeros((n_all * tq, MLA_HEADS * V_DIM), F32)
        for h in range(MLA_HEADS):
            x = op_ref[:, h * tq:(h + 1) * tq, :].reshape(n_all * tq, KV_LORA)
            out = jnp.where(lane_o // V_DIM == h, _dot(x, wuv_ref[...]), out)
        o_ref[...] = out.astype(BF16)


def _mla_attention_absorbed(qp, qr, lat_bf, kr4, r_t, w_uv_bf, q_off, lk_valid, tk):
    nseq, tq, _ = qp.shape
    lk_pad = lat_bf.shape[1]
    visible = min(lk_valid, ((q_off + tq - 1) // CHUNK + 1) * CHUNK)
    kern = functools.partial(_attn_absorbed_kernel, tq=tq, tk=tk, q_off=q_off, lk_valid=lk_valid,
                             n_blocks=-(-visible // tk))
    out = pl.pallas_call(
        kern,
        grid=(nseq,),
        in_specs=[
            pl.BlockSpec((1, tq, MLA_HEADS * KV_LORA), lambda b: (b, 0, 0)),
            pl.BlockSpec((1, tq, MLA_HEADS * ROPE_DIM), lambda b: (b, 0, 0)),
            pl.BlockSpec((1, lk_pad, KV_LORA), lambda b: (b, 0, 0)),
            pl.BlockSpec((1, lk_pad, LANES), lambda b: (b, 0, 0)),
            pl.BlockSpec((1, KNORM_ROWS, lk_pad), lambda b: (b, 0, 0)),
            _full((KV_LORA, MLA_HEADS * V_DIM)),
        ],
        out_specs=_full((nseq * tq, MLA_HEADS * V_DIM)),
        out_shape=jax.ShapeDtypeStruct((nseq * tq, MLA_HEADS * V_DIM), BF16),
        scratch_shapes=[pltpu.VMEM((nseq, MLA_HEADS * tq, KV_LORA), BF16)],
        compiler_params=_params(1),
        name="mla_attention_absorbed",
    )(qp, qr, lat_bf, kr4, r_t, w_uv_bf)
    return out.reshape(nseq, tq, MLA_HEADS * V_DIM)


def _trunk(x, mem_k_bf, mem_v_bf, ssm_h0_re, ssm_h0_im, conv_ctx, past_latent, past_krope, w, cfg):
    nseq, length, _ = x.shape
    past = 0 if past_latent is None else past_latent.shape[1]
    cos_t, sin_t = _rope_tables(past, length)
    reps = cfg["rope_rows"] // length
    cos_rows = jnp.tile(cos_t, (reps, 1))
    sin_rows = jnp.tile(sin_t, (reps, 1))
    if conv_ctx is None:
        conv_ctx = jnp.zeros((DEPTH, nseq, CONV_W - 1, 2 * D_FF), F32)
    if ssm_h0_re is None:
        ssm_h0_re = jnp.zeros((N_A_LAYERS, nseq, SSM_GROUPS, SSM_STATE), F32)
        ssm_h0_im = ssm_h0_re
    h = x
    ssm_re_out, ssm_im_out, conv_out = [], [], []
    for layer in range(DEPTH):
        z_mix, mem_out = _mixin(h, w["norm_mix_g"][layer][None], w["w_mix_in"][layer], w["mem_q_g"][layer],
                                w["seg_mem"], mem_k_bf[layer], mem_v_bf[layer], cfg["tm_mixin"])
        if layer < N_A_LAYERS:
            i = layer
            mix_out, s_re, s_im = _s5_mixer(
                z_mix, ssm_h0_re[i].reshape(nseq, SSM_COLS), ssm_h0_im[i].reshape(nseq, SSM_COLS),
                w["lam_re"][i], w["lam_im"][i], w["s5_in"][i], w["s5_out"][i],
                w["ssm_d"][i][None], w["w_glu"][i], w["b_glu"][i][None], cfg["t_chunk"])
            ssm_re_out.append(s_re.reshape(nseq, SSM_GROUPS, SSM_STATE))
            ssm_im_out.append(s_im.reshape(nseq, SSM_GROUPS, SSM_STATE))
        else:
            if layer == N_A_LAYERS:
                new_latent, new_krope = _shared_kv_down(
                    h, w["kv_norm_g"], w["w_dkv"], w["latent_norm_g"], w["krope_norm_g"],
                    cos_rows, sin_rows, cfg["tm_rows"])
                if cfg["absorbed"]:
                    lk_valid = past + length
                    lk_pad = -(-lk_valid // LANES) * LANES
                    tail = lambda width: jnp.zeros((nseq, lk_pad - lk_valid, width), BF16)
                    lat_bf = jnp.concatenate(
                        [past_latent.astype(BF16), new_latent.astype(BF16), tail(KV_LORA)], axis=1)
                    krope_bf = jnp.concatenate(
                        [past_krope.astype(BF16), new_krope.astype(BF16), tail(ROPE_DIM)], axis=1)
                    r_t = _key_norms(lat_bf, w["w_uk_t"], cfg["knorm_chunk"])
                else:
                    assert past_latent is None
                    kn, v_all = _kv_up(new_latent.reshape(nseq * length, KV_LORA), w["w_uk"], w["w_uv"],
                                       w["seg_nope"], w["k_nope_g"], cfg["tr_kv"])
                    kn = kn.reshape(nseq, length, MLA_HEADS * NOPE_DIM)
                    v_all = v_all.reshape(nseq, length, MLA_HEADS * V_DIM)
                    krope_bf = new_krope.astype(BF16)
                kr4 = jnp.tile(krope_bf, (1, 1, LANES // ROPE_DIM))
            j = layer - N_A_LAYERS
            q_out = _q_side(z_mix, w["q_latent_norm_g"][j], w["w_uq"][j], w["q_nope_norm_g"][j],
                            w["q_rope_norm_g"][j], w["seg_nope"], w["seg_rope"], cos_rows, sin_rows,
                            w["k_nope_g"], w["w_uk"], cfg["tm_rows"], cfg["absorbed"])
            if cfg["absorbed"]:
                mix_out = _mla_attention_absorbed(q_out[2], q_out[1], lat_bf, kr4, r_t, w["w_uv"],
                                                  past, lk_valid, lk_pad)
            else:
                mix_out = _mla_attention(q_out[0], q_out[1], kn, kr4, v_all, cfg["tq"])
        h, ctx = _mixout_ffn(h, mix_out, mem_out, w, layer, conv_ctx[layer], cfg["ffn_rows"])
        conv_out.append(ctx)
    return h, new_latent, new_krope, jnp.stack(ssm_re_out), jnp.stack(ssm_im_out), jnp.stack(conv_out)


PROMPT_CFG = dict(tm_mixin=512, t_chunk=64, ffn_rows=512, tm_rows=1024, rope_rows=2048, tr_kv=1024,
                  tq=512, absorbed=False)
SAMPLE_CFG = dict(tm_mixin=32, t_chunk=32, ffn_rows=32, tm_rows=512, rope_rows=512, knorm_chunk=1408,
                  absorbed=True)


def kernel(x_prompt, x_sample, cache_mla_latent, cache_mla_krope, cache_mem_k, cache_mem_v, state_ssm_re, state_ssm_im, state_conv, mem_prompt, norm_mix_g, w_mix_in, w_mix_out, norm_ffn_g, w_ffn_in, ffn_conv_w, ffn_conv_b, w_ffn_out, mem_norm_g, w_mem_kv, mem_q_norm_g, mem_k_norm_g, ssm_a_re, ssm_a_im, ssm_log_dt, ssm_b_re, ssm_b_im, ssm_c_re, ssm_c_im, ssm_d, w_glu, b_glu, kv_norm_g, w_dkv, latent_norm_g, krope_norm_g, w_uk, w_uv, k_nope_norm_g, q_latent_norm_g, w_uq, q_nope_norm_g, q_rope_norm_g):
    bf = lambda t: t.astype(BF16)
    seg_mem = _seg_matrix(MEM_WIDTH, MEM_HEAD_DIM)
    lam_re, lam_im, s5_in, s5_out = _s5_prepare(ssm_a_re, ssm_a_im, ssm_log_dt, ssm_b_re, ssm_b_im,
                                                ssm_c_re, ssm_c_im)
    w = dict(
        norm_mix_g=norm_mix_g, w_mix_in=bf(w_mix_in), w_mix_out=bf(w_mix_out),
        norm_ffn_g=norm_ffn_g.reshape(DEPTH, 1, D_MODEL),
        w_ffn_in=bf(w_ffn_in), ffn_conv_w=ffn_conv_w, ffn_conv_b=ffn_conv_b.reshape(DEPTH, 1, 2 * D_FF),
        w_ffn_out=bf(w_ffn_out),
        mem_q_g=jnp.tile(mem_q_norm_g, (1, MEM_HEADS)).reshape(DEPTH, 1, MEM_WIDTH),
        seg_mem=seg_mem,
        seg_nope=_seg_matrix(MLA_HEADS * NOPE_DIM, NOPE_DIM),
        seg_rope=_seg_matrix(MLA_HEADS * ROPE_DIM, ROPE_DIM),
        lam_re=lam_re, lam_im=lam_im, s5_in=s5_in, s5_out=s5_out,
        ssm_d=ssm_d, w_glu=bf(w_glu), b_glu=b_glu,
        kv_norm_g=kv_norm_g, w_dkv=bf(w_dkv), latent_norm_g=latent_norm_g, krope_norm_g=krope_norm_g,
        w_uk=bf(w_uk), w_uk_t=bf(w_uk).T, w_uv=bf(w_uv),
        k_nope_g=jnp.tile(k_nope_norm_g.reshape(1, NOPE_DIM), (1, MLA_HEADS)),
        q_latent_norm_g=q_latent_norm_g, w_uq=bf(w_uq), q_nope_norm_g=q_nope_norm_g,
        q_rope_norm_g=q_rope_norm_g,
    )
    bsz = mem_prompt.shape[0]
    mem_k_p, mem_v_p, mem_k_bf, mem_v_bf = _memory_kv(mem_prompt, mem_norm_g, bf(w_mem_kv), mem_k_norm_g,
                                                      seg_mem)
    y_prompt, lat_p, krope_p, ssm_re_p, ssm_im_p, conv_p = _trunk(
        x_prompt, mem_k_bf, mem_v_bf, None, None, None, None, None, w, PROMPT_CFG)
    dec = cache_mem_k.shape[1]
    y_sample, lat_s, krope_s, ssm_re_s, ssm_im_s, conv_s = _trunk(
        x_sample, bf(cache_mem_k).reshape(DEPTH, dec, N_MEM, MEM_WIDTH),
        bf(cache_mem_v).reshape(DEPTH, dec, N_MEM, MEM_WIDTH),
        state_ssm_re, state_ssm_im, state_conv, cache_mla_latent, cache_mla_krope, w, SAMPLE_CFG)
    shape5 = (DEPTH, bsz, N_MEM, MEM_HEADS, MEM_HEAD_DIM)
    return (y_prompt, y_sample, mem_k_p.reshape(shape5), mem_v_p.reshape(shape5), lat_p, krope_p,
            ssm_re_p, ssm_im_p, conv_p, lat_s, krope_s, ssm_re_s, ssm_im_s, conv_s)
```

```python
import functools
import math

import jax
import jax.numpy as jnp
from jax import lax
from jax.experimental import pallas as pl
from jax.experimental.pallas import tpu as pltpu

F32 = jnp.float32
BF16 = jnp.bfloat16

D_MODEL = 1024
DEPTH = 4
CHUNK = 64
N_A_LAYERS = DEPTH // 2
N_B_LAYERS = DEPTH - N_A_LAYERS
MIX_IN = 768
MEM_HEADS = 4
MEM_HEAD_DIM = 64
MEM_WIDTH = MEM_HEADS * MEM_HEAD_DIM
N_MEM = 256
SSM_GROUP = 16
SSM_GROUPS = MIX_IN // SSM_GROUP
SSM_STATE = 64
SSM_COLS = SSM_GROUPS * SSM_STATE
MLA_HEADS = 12
NOPE_DIM = 64
ROPE_DIM = 32
ROPE_HALF = ROPE_DIM // 2
V_DIM = 64
KV_LORA = 256
ROPE_BASE = 10000.0
MLA_SCALE = (NOPE_DIM + ROPE_DIM) ** -0.5
MEM_SCALE = MEM_HEAD_DIM ** -0.5
D_FF = 2816
CONV_W = 3
EPS = 1e-6
NEG_INF = -1e30

V7X_VMEM_LIMIT_BYTES = 56 * 1024 * 1024
LANES = 128
SUBLANES = 8

S5_BLOCKS = 3
S5_BLOCK_CH = MIX_IN // S5_BLOCKS
S5_BLOCK_ST = SSM_COLS // S5_BLOCKS
FFN_TF = 256
N_FF_TILES = D_FF // FFN_TF
FFN_PAD_SLOTS = 4


def _params(n_axes):
    return pltpu.CompilerParams(
        dimension_semantics=("arbitrary",) * n_axes,
        vmem_limit_bytes=V7X_VMEM_LIMIT_BYTES,
    )


def _dot(a, b):
    return jnp.dot(a, b, preferred_element_type=F32)


def _dot_nt(a, b):
    return lax.dot_general(a, b, (((1,), (1,)), ((), ())), preferred_element_type=F32)


def _rms_rows(x, g):
    ms = jnp.mean(x * x, axis=-1, keepdims=True)
    return x * lax.rsqrt(ms + EPS) * g


def _seg_rms(x, seg_mat, g):
    ms = _dot((x * x).astype(BF16), seg_mat)
    return x * lax.rsqrt(ms + EPS) * g


def _seg_matrix(width, seg):
    idx = jnp.arange(width) // seg
    return jnp.where(idx[:, None] == idx[None, :], 1.0 / seg, 0.0).astype(BF16)


def _full(shape):
    nd = len(shape)
    return pl.BlockSpec(shape, lambda *_: (0,) * nd)


def _rope_table_kernel(pos_ref, inv_ref, cos_ref, sin_ref):
    ang = pos_ref[...] * inv_ref[...]
    cos_ref[...] = jnp.cos(ang)
    sin_ref[...] = jnp.sin(ang)


def _rope_tables(past, length):
    pos = (past + jnp.arange(length, dtype=jnp.int32)).astype(F32)[:, None]
    inv_freq = (1.0 / (ROPE_BASE ** (jnp.arange(0, ROPE_DIM, 2, dtype=F32) / ROPE_DIM)))[None, :]
    cos, sin = pl.pallas_call(
        _rope_table_kernel,
        out_shape=[jax.ShapeDtypeStruct((length, ROPE_HALF), F32)] * 2,
        name="rope_table",
    )(pos, inv_freq)
    cos_t = jnp.concatenate([cos, cos], axis=1)
    sin_t = jnp.concatenate([-sin, sin], axis=1)
    return cos_t, sin_t


def _memkv_kernel(mem_ref, g_ref, w_ref, kg_ref, seg_ref, k_ref, v_ref, kb_ref, vb_ref):
    xn = _rms_rows(mem_ref[0], g_ref[0]).astype(BF16)
    kv = _dot(xn, w_ref[0])
    k = _seg_rms(kv[:, :MEM_WIDTH], seg_ref[...], kg_ref[0])
    v = kv[:, MEM_WIDTH:]
    k_ref[0, 0] = k
    v_ref[0, 0] = v
    kb_ref[0, 0] = k.astype(BF16)
    vb_ref[0, 0] = v.astype(BF16)


def _memory_kv(mem, mem_norm_g, w_mem_kv_bf, mem_k_norm_g, seg_mem):
    bsz = mem.shape[0]
    kg = jnp.tile(mem_k_norm_g, (1, MEM_HEADS)).reshape(DEPTH, 1, MEM_WIDTH)
    out4 = lambda dt: jax.ShapeDtypeStruct((DEPTH, bsz, N_MEM, MEM_WIDTH), dt)
    spec4 = pl.BlockSpec((1, 1, N_MEM, MEM_WIDTH), lambda l, b: (l, b, 0, 0))
    return pl.pallas_call(
        _memkv_kernel,
        grid=(DEPTH, bsz),
        in_specs=[
            pl.BlockSpec((1, N_MEM, D_MODEL), lambda l, b: (b, 0, 0)),
            pl.BlockSpec((1, 1, D_MODEL), lambda l, b: (l, 0, 0)),
            pl.BlockSpec((1, D_MODEL, 2 * MEM_WIDTH), lambda l, b: (l, 0, 0)),
            pl.BlockSpec((1, 1, MEM_WIDTH), lambda l, b: (l, 0, 0)),
            _full((MEM_WIDTH, MEM_WIDTH)),
        ],
        out_specs=[spec4, spec4, spec4, spec4],
        out_shape=[out4(F32), out4(F32), out4(BF16), out4(BF16)],
        compiler_params=_params(2),
        name="memory_kv",
    )(mem, mem_norm_g.reshape(DEPTH, 1, D_MODEL), w_mem_kv_bf, kg, seg_mem)


def _mixin_kernel(h_ref, g_ref, w_ref, qg_ref, seg_ref, k_ref, v_ref, zmix_ref, mem_ref, *, seqs, rows_per_seq):
    xn = _rms_rows(h_ref[...], g_ref[...]).astype(BF16)
    z = _dot(xn, w_ref[...])
    zmix_ref[...] = z[:, :MIX_IN]
    mq = _seg_rms(z[:, MIX_IN:], seg_ref[...], qg_ref[...]).astype(BF16)
    lane = lax.broadcasted_iota(jnp.int32, (1, MEM_WIDTH), 1)
    for b in range(seqs):
        rows = slice(b * rows_per_seq, (b + 1) * rows_per_seq)
        mq_b = mq[rows]
        k = k_ref[b]
        v = v_ref[b]
        out = jnp.zeros(mq_b.shape, F32)
        for head in range(MEM_HEADS):
            in_head = (lane // MEM_HEAD_DIM) == head
            qh = jnp.where(in_head, mq_b, jnp.zeros_like(mq_b))
            s = _dot_nt(qh, k) * MEM_SCALE
            p = jnp.exp(s - jnp.max(s, axis=-1, keepdims=True))
            p = p * (1.0 / jnp.sum(p, axis=-1, keepdims=True))
            o = _dot(p.astype(BF16), v)
            out = jnp.where(in_head, o, out)
        mem_ref[rows, :] = out.astype(BF16)


def _mixin(h, w, layer, k_bf, v_bf, rows_per_seq_tile):
    nseq, length, _ = h.shape
    m_rows = nseq * length
    if rows_per_seq_tile >= length:
        seqs, rps, tps = nseq, length, 1
    else:
        seqs, rps, tps = 1, rows_per_seq_tile, length // rows_per_seq_tile
    tm = seqs * rps
    kern = functools.partial(_mixin_kernel, seqs=seqs, rows_per_seq=rps)
    row = lambda width: pl.BlockSpec((tm, width), lambda m: (m, 0))
    of_layer = lambda *shape: pl.BlockSpec((None,) + shape, lambda m: (layer,) + (0,) * len(shape))
    kv_spec = pl.BlockSpec((None, seqs, N_MEM, MEM_WIDTH), lambda m: (layer, m // tps, 0, 0))
    z_mix, mem_out = pl.pallas_call(
        kern,
        grid=(m_rows // tm,),
        in_specs=[
            row(D_MODEL),
            of_layer(1, D_MODEL),
            of_layer(D_MODEL, MIX_IN + MEM_WIDTH),
            of_layer(1, MEM_WIDTH),
            _full((MEM_WIDTH, MEM_WIDTH)),
            kv_spec, kv_spec,
        ],
        out_specs=[row(MIX_IN), row(MEM_WIDTH)],
        out_shape=[
            jax.ShapeDtypeStruct((m_rows, MIX_IN), F32),
            jax.ShapeDtypeStruct((m_rows, MEM_WIDTH), BF16),
        ],
        compiler_params=_params(1),
        name="mix_in_mem_attn",
    )(h.reshape(m_rows, D_MODEL), w["norm_mix_g"], w["w_mix_in"], w["mem_q_g"], w["seg_mem"], k_bf, v_bf)
    return z_mix.reshape(nseq, length, MIX_IN), mem_out.reshape(nseq, length, MEM_WIDTH)


def _s5_prep_kernel(are_ref, aim_ref, ldt_ref, bre_ref, bim_ref, lre_ref, lim_ref, bbre_ref, bbim_ref):
    a_re, a_im = are_ref[0], aim_ref[0]
    dt = jnp.exp(ldt_ref[0])
    mag = jnp.exp(a_re * dt)
    lam_re = mag * jnp.cos(a_im * dt)
    lam_im = mag * jnp.sin(a_im * dt)
    den = a_re * a_re + a_im * a_im
    x_re = lam_re - 1.0
    f_re = (x_re * a_re + lam_im * a_im) / den
    f_im = (lam_im * a_re - x_re * a_im) / den
    b_re, b_im = bre_ref[0], bim_ref[0]
    lre_ref[0] = lam_re
    lim_ref[0] = lam_im
    bbre_ref[0] = f_re * b_re - f_im * b_im
    bbim_ref[0] = f_re * b_im + f_im * b_re


def _s5_prepare(a_re, a_im, log_dt, b_re, b_im, c_re, c_im):
    n = a_re.shape[0]
    col = lambda t: t.reshape(n, SSM_COLS, 1)
    ldt = jnp.broadcast_to(log_dt[:, :, None], (n, SSM_GROUPS, SSM_STATE))
    col_spec = pl.BlockSpec((1, SSM_COLS, 1), lambda l: (l, 0, 0))
    b_spec = pl.BlockSpec((1, SSM_COLS, SSM_GROUP), lambda l: (l, 0, 0))
    lam_re, lam_im, bb_re, bb_im = pl.pallas_call(
        _s5_prep_kernel,
        grid=(n,),
        in_specs=[col_spec, col_spec, col_spec, b_spec, b_spec],
        out_specs=[col_spec, col_spec, b_spec, b_spec],
        out_shape=[jax.ShapeDtypeStruct((n, SSM_COLS, 1), F32)] * 2
        + [jax.ShapeDtypeStruct((n, SSM_COLS, SSM_GROUP), F32)] * 2,
        compiler_params=_params(1),
        name="s5_discretise",
    )(col(a_re), col(a_im), col(ldt),
      b_re.reshape(n, SSM_COLS, SSM_GROUP), b_im.reshape(n, SSM_COLS, SSM_GROUP))
    gpb = SSM_GROUPS // S5_BLOCKS

    def block_diagonal(t, rows_per_group, cols_per_group):
        rows = gpb * rows_per_group
        same = (jnp.arange(rows) // rows_per_group)[:, None] == jnp.arange(gpb)[None, :]
        wide = jnp.broadcast_to(t[:, :, :, None, :], (n, S5_BLOCKS, rows, gpb, cols_per_group))
        return jnp.where(same[None, None, :, :, None], wide, 0.0).reshape(
            n, S5_BLOCKS, rows, gpb * cols_per_group)

    def in_blocks(bb):
        t = bb.reshape(n, S5_BLOCKS, gpb, SSM_STATE, SSM_GROUP).transpose(0, 1, 2, 4, 3)
        return block_diagonal(t.reshape(n, S5_BLOCKS, S5_BLOCK_CH, SSM_STATE), SSM_GROUP, SSM_STATE)

    def out_blocks(c):
        t = c.reshape(n, S5_BLOCKS, gpb, SSM_GROUP, SSM_STATE).transpose(0, 1, 2, 4, 3)
        return block_diagonal(t.reshape(n, S5_BLOCKS, S5_BLOCK_ST, SSM_GROUP), SSM_STATE, SSM_GROUP)

    w_in = jnp.concatenate([in_blocks(bb_re), in_blocks(bb_im)], axis=-1).astype(BF16)
    w_out = jnp.concatenate([out_blocks(c_re), -out_blocks(c_im)], axis=-2).astype(BF16)
    return lam_re.reshape(n, 1, SSM_COLS), lam_im.reshape(n, 1, SSM_COLS), w_in, w_out


def _s5_kernel(u_ref, h0re_ref, h0im_ref, lre_ref, lim_ref, win_ref, wout_ref, d_ref, wglu_ref, bglu_ref,
               out_ref, sre_out_ref, sim_out_ref, hb_ref, sre_ref, sim_ref, *, bsz, t_chunk, col_block):
    c = pl.program_id(0)
    rows = bsz * t_chunk

    @pl.when(c == 0)
    def _():
        sre_ref[...] = h0re_ref[...]
        sim_ref[...] = h0im_ref[...]

    u = u_ref[...]
    ut = jnp.swapaxes(u, 0, 1).reshape(rows, MIX_IN).astype(BF16)
    ys = []
    for j in range(S5_BLOCKS):
        cols = slice(j * 2 * S5_BLOCK_ST, (j + 1) * 2 * S5_BLOCK_ST)
        hb_ref[:, cols] = _dot(ut[:, j * S5_BLOCK_CH:(j + 1) * S5_BLOCK_CH], win_ref[j])
        for sub in range(S5_BLOCK_ST // col_block):
            nat = j * S5_BLOCK_ST + sub * col_block
            cre = j * 2 * S5_BLOCK_ST + sub * col_block
            cim = cre + S5_BLOCK_ST
            lam_r = jnp.broadcast_to(lre_ref[:, nat:nat + col_block], (bsz, col_block))
            lam_i = jnp.broadcast_to(lim_ref[:, nat:nat + col_block], (bsz, col_block))
            s_r = sre_ref[:, nat:nat + col_block]
            s_i = sim_ref[:, nat:nat + col_block]
            for t in range(t_chunk):
                rows_t = slice(t * bsz, (t + 1) * bsz)
                n_r = lam_r * s_r - lam_i * s_i + hb_ref[rows_t, cre:cre + col_block]
                n_i = lam_r * s_i + lam_i * s_r + hb_ref[rows_t, cim:cim + col_block]
                hb_ref[rows_t, cre:cre + col_block] = n_r
                hb_ref[rows_t, cim:cim + col_block] = n_i
                s_r, s_i = n_r, n_i
            sre_ref[:, nat:nat + col_block] = s_r
            sim_ref[:, nat:nat + col_block] = s_i
        ys.append(_dot(hb_ref[:, cols].astype(BF16), wout_ref[j]))
    yt = jnp.concatenate(ys, axis=1).reshape(t_chunk, bsz, MIX_IN)
    y = jnp.swapaxes(yt, 0, 1) + d_ref[...] * u
    y = jax.nn.gelu(y).reshape(rows, MIX_IN)
    gate = _dot(y.astype(BF16), wglu_ref[...]) + bglu_ref[...]
    out_ref[...] = (y * jax.nn.sigmoid(gate)).reshape(bsz, t_chunk, MIX_IN).astype(BF16)

    @pl.when(c == pl.num_programs(0) - 1)
    def _():
        sre_out_ref[...] = sre_ref[...]
        sim_out_ref[...] = sim_ref[...]


def _s5_mixer(u, h0_re, h0_im, lam_re, lam_im, w_in, w_out, d_skip, w_glu_bf, b_glu, t_chunk):
    bsz, length, _ = u.shape
    rows = bsz * t_chunk
    col_block = (SUBLANES * 512) // bsz
    kern = functools.partial(_s5_kernel, bsz=bsz, t_chunk=t_chunk, col_block=col_block)
    state = jax.ShapeDtypeStruct((bsz, SSM_COLS), F32)
    return pl.pallas_call(
        kern,
        grid=(length // t_chunk,),
        in_specs=[
            pl.BlockSpec((bsz, t_chunk, MIX_IN), lambda c: (0, c, 0)),
            _full((bsz, SSM_COLS)), _full((bsz, SSM_COLS)),
            _full((1, SSM_COLS)), _full((1, SSM_COLS)),
            _full((S5_BLOCKS, S5_BLOCK_CH, 2 * S5_BLOCK_ST)),
            _full((S5_BLOCKS, 2 * S5_BLOCK_ST, S5_BLOCK_CH)),
            _full((1, MIX_IN)), _full((MIX_IN, MIX_IN)), _full((1, MIX_IN)),
        ],
        out_specs=[
            pl.BlockSpec((bsz, t_chunk, MIX_IN), lambda c: (0, c, 0)),
            _full((bsz, SSM_COLS)), _full((bsz, SSM_COLS)),
        ],
        out_shape=[jax.ShapeDtypeStruct((bsz, length, MIX_IN), BF16), state, state],
        scratch_shapes=[
            pltpu.VMEM((rows, 2 * SSM_COLS), F32),
            pltpu.VMEM((bsz, SSM_COLS), F32),
            pltpu.VMEM((bsz, SSM_COLS), F32),
        ],
        compiler_params=_params(1),
        name="s5_mixer",
    )(u, h0_re, h0_im, lam_re, lam_im, w_in, w_out, d_skip, w_glu_bf, b_glu)


def _ffn_kernel(h_ref, mix_ref, mem_ref, wo_ref, g_ref, win_ref, cw_ref, cb_ref, w2_ref, ctx_ref,
                out_ref, new_ref, act_ref, pad_ref, carry_ref, *, seqs, rows_per_seq, tiles_per_seq):
    m = pl.program_id(0)
    tm = seqs * rows_per_seq
    h1 = h_ref[...] + _dot(mix_ref[...], wo_ref[:MIX_IN, :]) + _dot(mem_ref[...], wo_ref[MIX_IN:, :])
    out_ref[...] = h1
    xn = _rms_rows(h1, g_ref[...]).astype(BF16)

    if tiles_per_seq > 1:
        @pl.when(m == 0)
        def _():
            carry_ref[...] = jnp.zeros(carry_ref.shape, F32)

    def causal_conv(col, slot):
        cols = slice(col, col + FFN_TF)
        u = _dot(xn, win_ref[:, cols])
        ctx = ctx_ref[:, :, cols]
        if tiles_per_seq > 1:
            ctx = jnp.where(m % tiles_per_seq == 0, ctx, carry_ref[:, cols][None])
        pad = pad_ref.at[slot]
        pad[:, 6:8, :] = ctx
        pad[:, 8:, :] = u.reshape(seqs, rows_per_seq, FFN_TF)
        cw = cw_ref[:, cols]
        y = (cb_ref[:, cols] + pad[:, 6:6 + rows_per_seq, :] * cw[0:1]
             + pad[:, 7:7 + rows_per_seq, :] * cw[1:2]
             + pad[:, 8:8 + rows_per_seq, :] * cw[2:3])
        last2 = pad[:, rows_per_seq + 6:rows_per_seq + 8, :]
        new_ref[:, :, cols] = last2
        if tiles_per_seq > 1:
            carry_ref[:, cols] = last2[0]
        return y.reshape(tm, FFN_TF)

    for f in range(N_FF_TILES):
        ya = causal_conv(f * FFN_TF, (2 * f) % FFN_PAD_SLOTS)
        yg = causal_conv(D_FF + f * FFN_TF, (2 * f + 1) % FFN_PAD_SLOTS)
        act_ref[:, f * FFN_TF:(f + 1) * FFN_TF] = (jax.nn.silu(yg) * ya).astype(BF16)
    out_ref[...] += _dot(act_ref[...], w2_ref[...])


def _mixout_ffn(h, mix, mem, w, layer, ctx, rows_per_seq_tile):
    nseq, length, _ = h.shape
    m_rows = nseq * length
    if rows_per_seq_tile >= length:
        seqs, rps, tps = nseq, length, 1
    else:
        seqs, rps, tps = 1, rows_per_seq_tile, length // rows_per_seq_tile
    tm = seqs * rps
    n_m = m_rows // tm
    kern = functools.partial(_ffn_kernel, seqs=seqs, rows_per_seq=rps, tiles_per_seq=tps)
    row = lambda width: pl.BlockSpec((tm, width), lambda m: (m, 0))
    resident = lambda *shape: pl.BlockSpec((None,) + shape, lambda m: (layer,) + (0,) * len(shape),
                                           pipeline_mode=pl.Buffered(1))
    out, new = pl.pallas_call(
        kern,
        grid=(n_m,),
        in_specs=[
            row(D_MODEL), row(MIX_IN), row(MEM_WIDTH),
            resident(MIX_IN + MEM_WIDTH, D_MODEL),
            resident(1, D_MODEL),
            resident(D_MODEL, 2 * D_FF),
            resident(CONV_W, 2 * D_FF),
            resident(1, 2 * D_FF),
            resident(D_FF, D_MODEL),
            pl.BlockSpec((seqs, CONV_W - 1, 2 * D_FF), lambda m: (m // tps, 0, 0)),
        ],
        out_specs=[row(D_MODEL), pl.BlockSpec((seqs, CONV_W - 1, 2 * D_FF), lambda m: (m, 0, 0))],
        out_shape=[
            jax.ShapeDtypeStruct((m_rows, D_MODEL), F32),
            jax.ShapeDtypeStruct((n_m * seqs, CONV_W - 1, 2 * D_FF), F32),
        ],
        scratch_shapes=[
            pltpu.VMEM((tm, D_FF), BF16),
            pltpu.VMEM((FFN_PAD_SLOTS, seqs, rps + SUBLANES, FFN_TF), F32),
            pltpu.VMEM((CONV_W - 1, 2 * D_FF), F32),
        ],
        compiler_params=_params(1),
        name="mix_out_conv_ffn",
    )(h.reshape(m_rows, D_MODEL), mix.reshape(m_rows, MIX_IN), mem.reshape(m_rows, MEM_WIDTH),
      w["w_mix_out"], w["norm_ffn_g"], w["w_ffn_in"], w["ffn_conv_w"], w["ffn_conv_b"], w["w_ffn_out"], ctx)
    new_ctx = new.reshape(nseq, tps, CONV_W - 1, 2 * D_FF)[:, -1]
    return out.reshape(nseq, length, D_MODEL), new_ctx


def _dkv_kernel(h_ref, g_ref, wl_ref, wr_ref, wrr_ref, lg_ref, kg_ref, kgr_ref, cos_ref, sin_ref,
                lat_ref, kr_ref):
    xn = _rms_rows(h_ref[...], g_ref[...]).astype(BF16)
    lat_ref[...] = _rms_rows(_dot(xn, wl_ref[...]), lg_ref[...])
    kr = _dot(xn, wr_ref[...])
    kr_rot = _dot(xn, wrr_ref[...])
    r = lax.rsqrt(jnp.mean(kr * kr, axis=-1, keepdims=True) + EPS)
    kr_ref[...] = (kr * r * kg_ref[...]) * cos_ref[...] + (kr_rot * r * kgr_ref[...]) * sin_ref[...]


def _swap_halves(t, axis=-1):
    a, b = jnp.split(t, 2, axis=axis)
    return jnp.concatenate([b, a], axis=axis)


def _shared_kv_down(h, kv_norm_g, w_dkv_bf, latent_norm_g, krope_norm_g, cos_rows, sin_rows, tm):
    nseq, length, _ = h.shape
    m_rows = nseq * length
    tab_blocks = cos_rows.shape[0] // tm
    w_l = w_dkv_bf[:, :KV_LORA]
    w_r = w_dkv_bf[:, KV_LORA:]
    kg = krope_norm_g.reshape(1, ROPE_DIM)
    lat, kr = pl.pallas_call(
        _dkv_kernel,
        grid=(m_rows // tm,),
        in_specs=[
            pl.BlockSpec((tm, D_MODEL), lambda m: (m, 0)),
            _full((1, D_MODEL)),
            _full((D_MODEL, KV_LORA)), _full((D_MODEL, ROPE_DIM)), _full((D_MODEL, ROPE_DIM)),
            _full((1, KV_LORA)), _full((1, ROPE_DIM)), _full((1, ROPE_DIM)),
            pl.BlockSpec((tm, ROPE_DIM), lambda m: (m % tab_blocks, 0)),
            pl.BlockSpec((tm, ROPE_DIM), lambda m: (m % tab_blocks, 0)),
        ],
        out_specs=[pl.BlockSpec((tm, KV_LORA), lambda m: (m, 0)),
                   pl.BlockSpec((tm, ROPE_DIM), lambda m: (m, 0))],
        out_shape=[jax.ShapeDtypeStruct((m_rows, KV_LORA), F32),
                   jax.ShapeDtypeStruct((m_rows, ROPE_DIM), F32)],
        compiler_params=_params(1),
        name="shared_kv_down",
    )(h.reshape(m_rows, D_MODEL), kv_norm_g.reshape(1, D_MODEL), w_l, w_r, _swap_halves(w_r),
      latent_norm_g.reshape(1, KV_LORA), kg, _swap_halves(kg), cos_rows, sin_rows)
    return lat.reshape(nseq, length, KV_LORA), kr.reshape(nseq, length, ROPE_DIM)


def _kv_up_kernel(lat_ref, wk_ref, wv_ref, seg_ref, g_ref, k_ref, v_ref):
    lat = lat_ref[...].astype(BF16)
    k_ref[...] = _seg_rms(_dot(lat, wk_ref[...]), seg_ref[...], g_ref[...]).astype(BF16)
    v_ref[...] = _dot(lat, wv_ref[...]).astype(BF16)


def _kv_up(latent_rows, w_uk_bf, w_uv_bf, seg_nope, k_g_tiled, tr):
    rows = latent_rows.shape[0]
    width = MLA_HEADS * NOPE_DIM
    return pl.pallas_call(
        _kv_up_kernel,
        grid=(rows // tr,),
        in_specs=[
            pl.BlockSpec((tr, KV_LORA), lambda r: (r, 0)),
            _full((KV_LORA, width)), _full((KV_LORA, width)),
            _full((width, width)), _full((1, width)),
        ],
        out_specs=[pl.BlockSpec((tr, width), lambda r: (r, 0))] * 2,
        out_shape=[jax.ShapeDtypeStruct((rows, width), BF16)] * 2,
        compiler_params=_params(1),
        name="kv_up",
    )(latent_rows, w_uk_bf, w_uv_bf, seg_nope, k_g_tiled)


Q_PRESCALE = MLA_SCALE * math.log2(math.e)


def _q_kernel(z_ref, g_ref, wn_ref, wa_ref, wb_ref, segn_ref, segr_ref, gn_ref, ga_ref, gb_ref,
              cos_ref, sin_ref, kg_ref, wuk_ref, qn_ref, qr_ref, *maybe_qp_ref):
    xn = _rms_rows(z_ref[...], g_ref[...]).astype(BF16)
    qn = _seg_rms(_dot(xn, wn_ref[...]), segn_ref[...], gn_ref[...]) * Q_PRESCALE
    qn_ref[...] = qn.astype(BF16)
    a = _dot(xn, wa_ref[...])
    b = _dot(xn, wb_ref[...])
    r = lax.rsqrt(_dot((a * a).astype(BF16), segr_ref[...]) + EPS)
    rot = (a * r * ga_ref[...]) * cos_ref[...] + (b * r * gb_ref[...]) * sin_ref[...]
    qr_ref[...] = (rot * Q_PRESCALE).astype(BF16)
    if maybe_qp_ref:
        qp_ref, = maybe_qp_ref
        lane = lax.broadcasted_iota(jnp.int32, (1, MLA_HEADS * NOPE_DIM), 1)
        qg = qn * kg_ref[...]
        for head in range(MLA_HEADS):
            qh = jnp.where(lane // NOPE_DIM == head, qg, 0.0).astype(BF16)
            qp_ref[:, head * KV_LORA:(head + 1) * KV_LORA] = _dot_nt(qh, wuk_ref[...]).astype(BF16)


def _q_side(z_mix, q_latent_g, w_uq_bf, q_nope_g, q_rope_g, seg_nope, seg_rope, cos_rows, sin_rows,
            k_g_tiled, w_uk_bf, tm, absorbed):
    nseq, length, _ = z_mix.shape
    m_rows = nseq * length
    tab_blocks = cos_rows.shape[0] // tm
    wn_width = MLA_HEADS * NOPE_DIM
    wr_width = MLA_HEADS * ROPE_DIM
    wp_width = MLA_HEADS * KV_LORA
    w3 = w_uq_bf.reshape(MIX_IN, MLA_HEADS, NOPE_DIM + ROPE_DIM)
    w_n = w3[:, :, :NOPE_DIM].reshape(MIX_IN, wn_width)
    w_a = w3[:, :, NOPE_DIM:]
    w_b = _swap_halves(w_a).reshape(MIX_IN, wr_width)
    w_a = w_a.reshape(MIX_IN, wr_width)
    g_n = jnp.tile(q_nope_g.reshape(1, NOPE_DIM), (1, MLA_HEADS))
    g_a = jnp.tile(q_rope_g.reshape(1, ROPE_DIM), (1, MLA_HEADS))
    g_b = jnp.tile(_swap_halves(q_rope_g.reshape(1, ROPE_DIM)), (1, MLA_HEADS))
    cos_q = jnp.tile(cos_rows, (1, MLA_HEADS))
    sin_q = jnp.tile(sin_rows, (1, MLA_HEADS))
    widths = [wn_width, wr_width] + ([wp_width] if absorbed else [])
    outs = pl.pallas_call(
        _q_kernel,
        grid=(m_rows // tm,),
        in_specs=[
            pl.BlockSpec((tm, MIX_IN), lambda m: (m, 0)),
            _full((1, MIX_IN)),
            _full((MIX_IN, wn_width)), _full((MIX_IN, wr_width)), _full((MIX_IN, wr_width)),
            _full((wn_width, wn_width)), _full((wr_width, wr_width)),
            _full((1, wn_width)), _full((1, wr_width)), _full((1, wr_width)),
            pl.BlockSpec((tm, wr_width), lambda m: (m % tab_blocks, 0)),
            pl.BlockSpec((tm, wr_width), lambda m: (m % tab_blocks, 0)),
            _full((1, wn_width)), _full((KV_LORA, wn_width)),
        ],
        out_specs=[pl.BlockSpec((tm, wd), lambda m: (m, 0)) for wd in widths],
        out_shape=[jax.ShapeDtypeStruct((m_rows, wd), BF16) for wd in widths],
        compiler_params=_params(1),
        name="mla_query",
    )(z_mix.reshape(m_rows, MIX_IN), q_latent_g.reshape(1, MIX_IN), w_n, w_a, w_b, seg_nope, seg_rope,
      g_n, g_a, g_b, cos_q, sin_q, k_g_tiled, w_uk_bf)
    return [o.reshape(nseq, length, o.shape[-1]) for o in outs]


def _attn_kernel(qn_ref, qr_ref, kn_ref, kr_ref, v_ref, o_ref, *, tile, n_tiles):
    pair = pl.program_id(1)
    qi = pl.program_id(2)
    qn = qn_ref[0]
    qr = qr_ref[0]
    lane = lax.broadcasted_iota(jnp.int32, (1, LANES), 1)
    head_lanes = [(lane // NOPE_DIM) == j for j in range(2)]
    qcat = [jnp.concatenate(
        [jnp.where(head_lanes[j], qn, jnp.zeros_like(qn)),
         jnp.where((lane // ROPE_DIM) == 2 * (pair % 2) + j, qr, jnp.zeros_like(qr))], axis=1)
        for j in range(2)]
    row_chunk = lax.broadcasted_iota(jnp.int32, (tile, 1), 0) // CHUNK
    col_chunk = lax.broadcasted_iota(jnp.int32, (1, tile), 1) // CHUNK
    diag_visible = col_chunk <= row_chunk

    def one_block(kb, carry, masked):
        rows_k = slice(kb * tile, (kb + 1) * tile)
        kcat = jnp.concatenate([kn_ref[0, rows_k, :], kr_ref[0, rows_k, :]], axis=1)
        vb = v_ref[0, rows_k, :]
        new = []
        for j in range(2):
            m_i, acc = carry[j]
            s = _dot_nt(qcat[j], kcat)
            if masked:
                s = jnp.where(diag_visible, s, NEG_INF)
            m_new = jnp.maximum(m_i, jnp.max(s, axis=-1, keepdims=True))
            alpha = jnp.exp2(m_i - m_new)
            p = jnp.exp2(s - m_new).astype(BF16)
            v_j = jnp.where(head_lanes[j], vb, jnp.ones_like(vb))
            new.append((m_new, alpha * acc + _dot(p, v_j)))
        return tuple(new)

    def query_tile(n_full):
        carry = tuple((jnp.full((tile, 1), NEG_INF, F32), jnp.zeros((tile, LANES), F32)) for _ in range(2))
        for kb in range(n_full):
            carry = one_block(kb, carry, False)
        carry = one_block(n_full, carry, True)
        out = jnp.zeros((tile, LANES), F32)
        for j in range(2):
            acc = carry[j][1]
            row_sum = pltpu.roll(acc, NOPE_DIM, axis=1)
            out = jnp.where(head_lanes[j], acc * (1.0 / row_sum), out)
        o_ref[0] = out.astype(BF16)

    for c in range(n_tiles):
        pl.when(qi == c)(functools.partial(query_tile, c))


def _mla_attention(qn, qr, kn, kr4, v, tile):
    nseq, length, _ = qn.shape
    kern = functools.partial(_attn_kernel, tile=tile, n_tiles=length // tile)
    return pl.pallas_call(
        kern,
        grid=(nseq, MLA_HEADS // 2, length // tile),
        in_specs=[
            pl.BlockSpec((1, tile, LANES), lambda b, p, i: (b, i, p)),
            pl.BlockSpec((1, tile, LANES), lambda b, p, i: (b, i, p // 2)),
            pl.BlockSpec((1, length, LANES), lambda b, p, i: (b, 0, p)),
            pl.BlockSpec((1, length, LANES), lambda b, p, i: (b, 0, 0)),
            pl.BlockSpec((1, length, LANES), lambda b, p, i: (b, 0, p)),
        ],
        out_specs=pl.BlockSpec((1, tile, LANES), lambda b, p, i: (b, i, p)),
        out_shape=jax.ShapeDtypeStruct((nseq, length, MLA_HEADS * V_DIM), BF16),
        compiler_params=_params(3),
        name="mla_attention",
    )(qn, qr, kn, kr4, v)


KNORM_ROWS = 16


def _key_norm_kernel(past_ref, new_ref, wukt_ref, rt_ref, *, n_chunks, chunk):
    def norms(lat):
        keys = lat.shape[0]
        kt = _dot_nt(wukt_ref[...], lat)
        ss = jnp.sum((kt * kt).reshape(MLA_HEADS, NOPE_DIM, keys), axis=1)
        r = lax.rsqrt(ss * (1.0 / NOPE_DIM) + EPS)
        return jnp.concatenate([r, jnp.ones((KNORM_ROWS - MLA_HEADS, keys), F32)], axis=0)

    for c in range(n_chunks):
        rt_ref[0, :, c * chunk:(c + 1) * chunk] = norms(past_ref[0, c * chunk:(c + 1) * chunk, :].astype(BF16))
    rt_ref[0, :, n_chunks * chunk:] = norms(new_ref[0])


def _key_norms(past_latent, new_lat_pad, w_ukt_bf, chunk):
    nseq, past, _ = past_latent.shape
    new_pad = new_lat_pad.shape[1]
    lk_pad = past + new_pad
    kern = functools.partial(_key_norm_kernel, n_chunks=past // chunk, chunk=chunk)
    return pl.pallas_call(
        kern,
        grid=(nseq,),
        in_specs=[pl.BlockSpec((1, past, KV_LORA), lambda b: (b, 0, 0)),
                  pl.BlockSpec((1, new_pad, KV_LORA), lambda b: (b, 0, 0)),
                  _full((MLA_HEADS * NOPE_DIM, KV_LORA))],
        out_specs=pl.BlockSpec((1, KNORM_ROWS, lk_pad), lambda b: (b, 0, 0)),
        out_shape=jax.ShapeDtypeStruct((nseq, KNORM_ROWS, lk_pad), F32),
        compiler_params=_params(1),
        name="mla_key_norms",
    )(past_latent, new_lat_pad, w_ukt_bf)


def _attn_absorbed_kernel(qs_ref, qrs_ref, plat_ref, nlat_ref, pkr_ref, nkr_ref, rt_ref, wuv_ref, o_ref, op_ref,
                          *, tq, q_off, lk_valid):
    b = pl.program_id(0)
    nseq = pl.num_programs(0)
    rows = MLA_HEADS * tq
    lat = jnp.concatenate([plat_ref[0].astype(BF16), nlat_ref[0]], axis=0)
    kr = jnp.concatenate([pkr_ref[0].astype(BF16), nkr_ref[0]], axis=0)
    lk_pad = lat.shape[0]
    rt = rt_ref[0]
    knorm = jnp.concatenate([jnp.broadcast_to(rt[h:h + 1, :], (tq, lk_pad)) for h in range(MLA_HEADS)], axis=0)
    s = _dot_nt(qs_ref[0], lat) * knorm + _dot_nt(qrs_ref[0], kr)
    q_chunk = (q_off + lax.broadcasted_iota(jnp.int32, (rows, 1), 0) % tq) // CHUNK
    k_pos = lax.broadcasted_iota(jnp.int32, (1, lk_pad), 1)
    s = jnp.where(k_pos // CHUNK <= q_chunk, s, NEG_INF)
    s = jnp.where(k_pos < lk_valid, s, NEG_INF)
    p = jnp.exp2(s - jnp.max(s, axis=-1, keepdims=True))
    l = jnp.sum(p, axis=-1, keepdims=True)
    op_ref[b] = (_dot(p.astype(BF16), lat) * (1.0 / l)).astype(BF16)

    @pl.when(b == nseq - 1)
    def _():
        n_all = op_ref.shape[0]
        lane_o = lax.broadcasted_iota(jnp.int32, (1, MLA_HEADS * V_DIM), 1)
        out = jnp.zeros((n_all * tq, MLA_HEADS * V_DIM), F32)
        for h in range(MLA_HEADS):
            x = op_ref[:, h * tq:(h + 1) * tq, :].reshape(n_all * tq, KV_LORA)
            out = jnp.where(lane_o // V_DIM == h, _dot(x, wuv_ref[...]), out)
        o_ref[...] = out.astype(BF16)


def _mla_attention_absorbed(qp, qr, past_latent, new_lat_pad, past_krope, new_kr_pad, r_t, w_uv_bf, lk_valid):
    nseq, tq, _ = qp.shape
    past = past_latent.shape[1]
    new_pad = new_lat_pad.shape[1]
    rows = MLA_HEADS * tq
    qs = qp.reshape(nseq, tq, MLA_HEADS, KV_LORA).transpose(0, 2, 1, 3).reshape(nseq, rows, KV_LORA)
    qrs = qr.reshape(nseq, tq, MLA_HEADS, ROPE_DIM).transpose(0, 2, 1, 3).reshape(nseq, rows, ROPE_DIM)
    kern = functools.partial(_attn_absorbed_kernel, tq=tq, q_off=past, lk_valid=lk_valid)
    per_seq = lambda n, width: pl.BlockSpec((1, n, width), lambda b: (b, 0, 0))
    out = pl.pallas_call(
        kern,
        grid=(nseq,),
        in_specs=[
            per_seq(rows, KV_LORA), per_seq(rows, ROPE_DIM),
            per_seq(past, KV_LORA), per_seq(new_pad, KV_LORA),
            per_seq(past, ROPE_DIM), per_seq(new_pad, ROPE_DIM),
            pl.BlockSpec((1, KNORM_ROWS, past + new_pad), lambda b: (b, 0, 0)),
            _full((KV_LORA, MLA_HEADS * V_DIM)),
        ],
        out_specs=_full((nseq * tq, MLA_HEADS * V_DIM)),
        out_shape=jax.ShapeDtypeStruct((nseq * tq, MLA_HEADS * V_DIM), BF16),
        scratch_shapes=[pltpu.VMEM((nseq, rows, KV_LORA), BF16)],
        compiler_params=_params(1),
        name="mla_attention_absorbed",
    )(qs, qrs, past_latent, new_lat_pad, past_krope, new_kr_pad, r_t, w_uv_bf)
    return out.reshape(nseq, tq, MLA_HEADS * V_DIM)


def _trunk(x, mem_k_bf, mem_v_bf, ssm_h0_re, ssm_h0_im, conv_ctx, past_latent, past_krope, w, cfg):
    nseq, length, _ = x.shape
    past = 0 if past_latent is None else past_latent.shape[1]
    cos_t, sin_t = _rope_tables(past, length)
    reps = cfg["rope_rows"] // length
    cos_rows = jnp.tile(cos_t, (reps, 1))
    sin_rows = jnp.tile(sin_t, (reps, 1))
    if conv_ctx is None:
        conv_ctx = jnp.zeros((DEPTH, nseq, CONV_W - 1, 2 * D_FF), F32)
    if ssm_h0_re is None:
        ssm_h0_re = jnp.zeros((N_A_LAYERS, nseq, SSM_GROUPS, SSM_STATE), F32)
        ssm_h0_im = ssm_h0_re
    h = x
    ssm_re_out, ssm_im_out, conv_out = [], [], []
    for layer in range(DEPTH):
        z_mix, mem_out = _mixin(h, w, layer, mem_k_bf, mem_v_bf, cfg["tm_mixin"])
        if layer < N_A_LAYERS:
            i = layer
            mix_out, s_re, s_im = _s5_mixer(
                z_mix, ssm_h0_re[i].reshape(nseq, SSM_COLS), ssm_h0_im[i].reshape(nseq, SSM_COLS),
                w["lam_re"][i], w["lam_im"][i], w["s5_in"][i], w["s5_out"][i],
                w["ssm_d"][i][None], w["w_glu"][i], w["b_glu"][i][None], cfg["t_chunk"])
            ssm_re_out.append(s_re.reshape(nseq, SSM_GROUPS, SSM_STATE))
            ssm_im_out.append(s_im.reshape(nseq, SSM_GROUPS, SSM_STATE))
        else:
            if layer == N_A_LAYERS:
                new_latent, new_krope = _shared_kv_down(
                    h, w["kv_norm_g"], w["w_dkv"], w["latent_norm_g"], w["krope_norm_g"],
                    cos_rows, sin_rows, cfg["tm_rows"])
                if cfg["absorbed"]:
                    assert past % LANES == 0
                    lk_valid = past + length
                    new_pad = -(-length // LANES) * LANES
                    pad_rows = lambda t: jnp.pad(t.astype(BF16), ((0, 0), (0, new_pad - length), (0, 0)))
                    new_lat_pad, new_kr_pad = pad_rows(new_latent), pad_rows(new_krope)
                    r_t = _key_norms(past_latent, new_lat_pad, w["w_uk_t"], cfg["knorm_chunk"])
                else:
                    assert past_latent is None
                    kn, v_all = _kv_up(new_latent.reshape(nseq * length, KV_LORA), w["w_uk"], w["w_uv"],
                                       w["seg_nope"], w["k_nope_g"], cfg["tr_kv"])
                    kn = kn.reshape(nseq, length, MLA_HEADS * NOPE_DIM)
                    v_all = v_all.reshape(nseq, length, MLA_HEADS * V_DIM)
                    kr4 = jnp.tile(new_krope.astype(BF16), (1, 1, LANES // ROPE_DIM))
            j = layer - N_A_LAYERS
            q_out = _q_side(z_mix, w["q_latent_norm_g"][j], w["w_uq"][j], w["q_nope_norm_g"][j],
                            w["q_rope_norm_g"][j], w["seg_nope"], w["seg_rope"], cos_rows, sin_rows,
                            w["k_nope_g"], w["w_uk"], cfg["tm_rows"], cfg["absorbed"])
            if cfg["absorbed"]:
                mix_out = _mla_attention_absorbed(q_out[2], q_out[1], past_latent, new_lat_pad, past_krope,
                                                  new_kr_pad, r_t, w["w_uv"], lk_valid)
            else:
                mix_out = _mla_attention(q_out[0], q_out[1], kn, kr4, v_all, cfg["tq"])
        h, ctx = _mixout_ffn(h, mix_out, mem_out, w, layer, conv_ctx[layer], cfg["ffn_rows"])
        conv_out.append(ctx)
    return h, new_latent, new_krope, jnp.stack(ssm_re_out), jnp.stack(ssm_im_out), jnp.stack(conv_out)


PROMPT_CFG = dict(tm_mixin=512, t_chunk=64, ffn_rows=512, tm_rows=1024, rope_rows=2048, tr_kv=1024,
                  tq=512, absorbed=False)
SAMPLE_CFG = dict(tm_mixin=32, t_chunk=32, ffn_rows=32, tm_rows=512, rope_rows=512, knorm_chunk=1024,
                  absorbed=True)


def kernel(x_prompt, x_sample, cache_mla_latent, cache_mla_krope, cache_mem_k, cache_mem_v, state_ssm_re, state_ssm_im, state_conv, mem_prompt, norm_mix_g, w_mix_in, w_mix_out, norm_ffn_g, w_ffn_in, ffn_conv_w, ffn_conv_b, w_ffn_out, mem_norm_g, w_mem_kv, mem_q_norm_g, mem_k_norm_g, ssm_a_re, ssm_a_im, ssm_log_dt, ssm_b_re, ssm_b_im, ssm_c_re, ssm_c_im, ssm_d, w_glu, b_glu, kv_norm_g, w_dkv, latent_norm_g, krope_norm_g, w_uk, w_uv, k_nope_norm_g, q_latent_norm_g, w_uq, q_nope_norm_g, q_rope_norm_g):
    bf = lambda t: t.astype(BF16)
    seg_mem = _seg_matrix(MEM_WIDTH, MEM_HEAD_DIM)
    lam_re, lam_im, s5_in, s5_out = _s5_prepare(ssm_a_re, ssm_a_im, ssm_log_dt, ssm_b_re, ssm_b_im,
                                                ssm_c_re, ssm_c_im)
    w = dict(
        norm_mix_g=norm_mix_g.reshape(DEPTH, 1, D_MODEL), w_mix_in=bf(w_mix_in), w_mix_out=bf(w_mix_out),
        norm_ffn_g=norm_ffn_g.reshape(DEPTH, 1, D_MODEL),
        w_ffn_in=bf(w_ffn_in), ffn_conv_w=ffn_conv_w, ffn_conv_b=ffn_conv_b.reshape(DEPTH, 1, 2 * D_FF),
        w_ffn_out=bf(w_ffn_out),
        mem_q_g=jnp.tile(mem_q_norm_g, (1, MEM_HEADS)).reshape(DEPTH, 1, MEM_WIDTH),
        seg_mem=seg_mem,
        seg_nope=_seg_matrix(MLA_HEADS * NOPE_DIM, NOPE_DIM),
        seg_rope=_seg_matrix(MLA_HEADS * ROPE_DIM, ROPE_DIM),
        lam_re=lam_re, lam_im=lam_im, s5_in=s5_in, s5_out=s5_out,
        ssm_d=ssm_d, w_glu=bf(w_glu), b_glu=b_glu,
        kv_norm_g=kv_norm_g, w_dkv=bf(w_dkv), latent_norm_g=latent_norm_g, krope_norm_g=krope_norm_g,
        w_uk=bf(w_uk), w_uk_t=bf(w_uk).T, w_uv=bf(w_uv),
        k_nope_g=jnp.tile(k_nope_norm_g.reshape(1, NOPE_DIM), (1, MLA_HEADS)),
        q_latent_norm_g=q_latent_norm_g, w_uq=bf(w_uq), q_nope_norm_g=q_nope_norm_g,
        q_rope_norm_g=q_rope_norm_g,
    )
    bsz = mem_prompt.shape[0]
    mem_k_p, mem_v_p, mem_k_bf, mem_v_bf = _memory_kv(mem_prompt, mem_norm_g, bf(w_mem_kv), mem_k_norm_g,
                                                      seg_mem)
    y_prompt, lat_p, krope_p, ssm_re_p, ssm_im_p, conv_p = _trunk(
        x_prompt, mem_k_bf, mem_v_bf, None, None, None, None, None, w, PROMPT_CFG)
    dec = cache_mem_k.shape[1]
    y_sample, lat_s, krope_s, ssm_re_s, ssm_im_s, conv_s = _trunk(
        x_sample, bf(cache_mem_k).reshape(DEPTH, dec, N_MEM, MEM_WIDTH),
        bf(cache_mem_v).reshape(DEPTH, dec, N_MEM, MEM_WIDTH),
        state_ssm_re, state_ssm_im, state_conv, cache_mla_latent, cache_mla_krope, w, SAMPLE_CFG)
    shape5 = (DEPTH, bsz, N_MEM, MEM_HEADS, MEM_HEAD_DIM)
    return (y_prompt, y_sample, mem_k_p.reshape(shape5), mem_v_p.reshape(shape5), lat_p, krope_p,
            ssm_re_p, ssm_im_p, conv_p, lat_s, krope_s, ssm_re_s, ssm_im_s, conv_s)
```

```python
import functools
import math

import jax
import jax.numpy as jnp
from jax import lax
from jax.experimental import pallas as pl
from jax.experimental.pallas import tpu as pltpu

F32 = jnp.float32
BF16 = jnp.bfloat16

D_MODEL = 1024
DEPTH = 4
CHUNK = 64
N_A_LAYERS = DEPTH // 2
N_B_LAYERS = DEPTH - N_A_LAYERS
MIX_IN = 768
MEM_HEADS = 4
MEM_HEAD_DIM = 64
MEM_WIDTH = MEM_HEADS * MEM_HEAD_DIM
N_MEM = 256
SSM_GROUP = 16
SSM_GROUPS = MIX_IN // SSM_GROUP
SSM_STATE = 64
SSM_COLS = SSM_GROUPS * SSM_STATE
MLA_HEADS = 12
NOPE_DIM = 64
ROPE_DIM = 32
ROPE_HALF = ROPE_DIM // 2
V_DIM = 64
KV_LORA = 256
ROPE_BASE = 10000.0
MLA_SCALE = (NOPE_DIM + ROPE_DIM) ** -0.5
MEM_SCALE = MEM_HEAD_DIM ** -0.5
D_FF = 2816
CONV_W = 3
EPS = 1e-6
NEG_INF = -1e30

V7X_VMEM_LIMIT_BYTES = 56 * 1024 * 1024
LANES = 128
SUBLANES = 8

S5_BLOCKS = 3
S5_BLOCK_CH = MIX_IN // S5_BLOCKS
S5_BLOCK_ST = SSM_COLS // S5_BLOCKS
FFN_TF = 256
N_FF_TILES = D_FF // FFN_TF
FFN_PAD_SLOTS = 4


def _params(n_axes):
    return pltpu.CompilerParams(
        dimension_semantics=("arbitrary",) * n_axes,
        vmem_limit_bytes=V7X_VMEM_LIMIT_BYTES,
    )


def _dot(a, b):
    return jnp.dot(a, b, preferred_element_type=F32)


def _dot_nt(a, b):
    return lax.dot_general(a, b, (((1,), (1,)), ((), ())), preferred_element_type=F32)


def _rms_rows(x, g):
    ms = jnp.mean(x * x, axis=-1, keepdims=True)
    return x * lax.rsqrt(ms + EPS) * g


def _seg_rms(x, seg_mat, g):
    ms = _dot((x * x).astype(BF16), seg_mat)
    return x * lax.rsqrt(ms + EPS) * g


def _seg_matrix(width, seg):
    idx = jnp.arange(width) // seg
    return jnp.where(idx[:, None] == idx[None, :], 1.0 / seg, 0.0).astype(BF16)


def _full(shape):
    nd = len(shape)
    return pl.BlockSpec(shape, lambda *_: (0,) * nd)


def _rope_table_kernel(pos_ref, inv_ref, cos_ref, sin_ref):
    ang = pos_ref[...] * inv_ref[...]
    cos_ref[...] = jnp.cos(ang)
    sin_ref[...] = jnp.sin(ang)


def _rope_tables(past, length):
    pos = (past + jnp.arange(length, dtype=jnp.int32)).astype(F32)[:, None]
    inv_freq = (1.0 / (ROPE_BASE ** (jnp.arange(0, ROPE_DIM, 2, dtype=F32) / ROPE_DIM)))[None, :]
    cos, sin = pl.pallas_call(
        _rope_table_kernel,
        out_shape=[jax.ShapeDtypeStruct((length, ROPE_HALF), F32)] * 2,
        name="rope_table",
    )(pos, inv_freq)
    cos_t = jnp.concatenate([cos, cos], axis=1)
    sin_t = jnp.concatenate([-sin, sin], axis=1)
    return cos_t, sin_t


def _memkv_kernel(mem_ref, g_ref, w_ref, kg_ref, seg_ref, k_ref, v_ref, kb_ref, vb_ref):
    xn = _rms_rows(mem_ref[0], g_ref[0]).astype(BF16)
    kv = _dot(xn, w_ref[0])
    k = _seg_rms(kv[:, :MEM_WIDTH], seg_ref[...], kg_ref[0])
    v = kv[:, MEM_WIDTH:]
    k_ref[0, 0] = k
    v_ref[0, 0] = v
    kb_ref[0, 0] = k.astype(BF16)
    vb_ref[0, 0] = v.astype(BF16)


def _memory_kv(mem, mem_norm_g, w_mem_kv_bf, mem_k_norm_g, seg_mem):
    bsz = mem.shape[0]
    kg = jnp.tile(mem_k_norm_g, (1, MEM_HEADS)).reshape(DEPTH, 1, MEM_WIDTH)
    out4 = lambda dt: jax.ShapeDtypeStruct((DEPTH, bsz, N_MEM, MEM_WIDTH), dt)
    spec4 = pl.BlockSpec((1, 1, N_MEM, MEM_WIDTH), lambda l, b: (l, b, 0, 0))
    return pl.pallas_call(
        _memkv_kernel,
        grid=(DEPTH, bsz),
        in_specs=[
            pl.BlockSpec((1, N_MEM, D_MODEL), lambda l, b: (b, 0, 0)),
            pl.BlockSpec((1, 1, D_MODEL), lambda l, b: (l, 0, 0)),
            pl.BlockSpec((1, D_MODEL, 2 * MEM_WIDTH), lambda l, b: (l, 0, 0)),
            pl.BlockSpec((1, 1, MEM_WIDTH), lambda l, b: (l, 0, 0)),
            _full((MEM_WIDTH, MEM_WIDTH)),
        ],
        out_specs=[spec4, spec4, spec4, spec4],
        out_shape=[out4(F32), out4(F32), out4(BF16), out4(BF16)],
        compiler_params=_params(2),
        name="memory_kv",
    )(mem, mem_norm_g.reshape(DEPTH, 1, D_MODEL), w_mem_kv_bf, kg, seg_mem)


def _mixin_kernel(h_ref, g_ref, w_ref, qg_ref, seg_ref, k_ref, v_ref, zmix_ref, mem_ref, *, seqs, rows_per_seq):
    xn = _rms_rows(h_ref[...], g_ref[...]).astype(BF16)
    z = _dot(xn, w_ref[...])
    zmix_ref[...] = z[:, :MIX_IN]
    mq = _seg_rms(z[:, MIX_IN:], seg_ref[...], qg_ref[...]).astype(BF16)
    lane = lax.broadcasted_iota(jnp.int32, (1, MEM_WIDTH), 1)
    for b in range(seqs):
        rows = slice(b * rows_per_seq, (b + 1) * rows_per_seq)
        mq_b = mq[rows]
        k = k_ref[b]
        v = v_ref[b]
        out = jnp.zeros(mq_b.shape, F32)
        for head in range(MEM_HEADS):
            in_head = (lane // MEM_HEAD_DIM) == head
            qh = jnp.where(in_head, mq_b, jnp.zeros_like(mq_b))
            s = _dot_nt(qh, k) * MEM_SCALE
            p = jnp.exp(s - jnp.max(s, axis=-1, keepdims=True))
            p = p * (1.0 / jnp.sum(p, axis=-1, keepdims=True))
            o = _dot(p.astype(BF16), v)
            out = jnp.where(in_head, o, out)
        mem_ref[rows, :] = out.astype(BF16)


def _mixin(h, w, layer, k_bf, v_bf, rows_per_seq_tile):
    nseq, length, _ = h.shape
    m_rows = nseq * length
    if rows_per_seq_tile >= length:
        seqs, rps, tps = nseq, length, 1
    else:
        seqs, rps, tps = 1, rows_per_seq_tile, length // rows_per_seq_tile
    tm = seqs * rps
    kern = functools.partial(_mixin_kernel, seqs=seqs, rows_per_seq=rps)
    row = lambda width: pl.BlockSpec((tm, width), lambda m: (m, 0))
    of_layer = lambda *shape: pl.BlockSpec((None,) + shape, lambda m: (layer,) + (0,) * len(shape))
    kv_spec = pl.BlockSpec((None, seqs, N_MEM, MEM_WIDTH), lambda m: (layer, m // tps, 0, 0))
    z_mix, mem_out = pl.pallas_call(
        kern,
        grid=(m_rows // tm,),
        in_specs=[
            row(D_MODEL),
            of_layer(1, D_MODEL),
            of_layer(D_MODEL, MIX_IN + MEM_WIDTH),
            of_layer(1, MEM_WIDTH),
            _full((MEM_WIDTH, MEM_WIDTH)),
            kv_spec, kv_spec,
        ],
        out_specs=[row(MIX_IN), row(MEM_WIDTH)],
        out_shape=[
            jax.ShapeDtypeStruct((m_rows, MIX_IN), F32),
            jax.ShapeDtypeStruct((m_rows, MEM_WIDTH), BF16),
        ],
        compiler_params=_params(1),
        name="mix_in_mem_attn",
    )(h.reshape(m_rows, D_MODEL), w["norm_mix_g"], w["w_mix_in"], w["mem_q_g"], w["seg_mem"], k_bf, v_bf)
    return z_mix.reshape(nseq, length, MIX_IN), mem_out.reshape(nseq, length, MEM_WIDTH)


def _s5_prep_kernel(are_ref, aim_ref, ldt_ref, bre_ref, bim_ref, lre_ref, lim_ref, bbre_ref, bbim_ref):
    a_re, a_im = are_ref[0], aim_ref[0]
    dt = jnp.exp(ldt_ref[0])
    mag = jnp.exp(a_re * dt)
    lam_re = mag * jnp.cos(a_im * dt)
    lam_im = mag * jnp.sin(a_im * dt)
    den = a_re * a_re + a_im * a_im
    x_re = lam_re - 1.0
    f_re = (x_re * a_re + lam_im * a_im) / den
    f_im = (lam_im * a_re - x_re * a_im) / den
    b_re, b_im = bre_ref[0], bim_ref[0]
    lre_ref[0] = lam_re
    lim_ref[0] = lam_im
    bbre_ref[0] = f_re * b_re - f_im * b_im
    bbim_ref[0] = f_re * b_im + f_im * b_re


def _s5_prepare(a_re, a_im, log_dt, b_re, b_im, c_re, c_im):
    n = a_re.shape[0]
    col = lambda t: t.reshape(n, SSM_COLS, 1)
    ldt = jnp.broadcast_to(log_dt[:, :, None], (n, SSM_GROUPS, SSM_STATE))
    col_spec = pl.BlockSpec((1, SSM_COLS, 1), lambda l: (l, 0, 0))
    b_spec = pl.BlockSpec((1, SSM_COLS, SSM_GROUP), lambda l: (l, 0, 0))
    lam_re, lam_im, bb_re, bb_im = pl.pallas_call(
        _s5_prep_kernel,
        grid=(n,),
        in_specs=[col_spec, col_spec, col_spec, b_spec, b_spec],
        out_specs=[col_spec, col_spec, b_spec, b_spec],
        out_shape=[jax.ShapeDtypeStruct((n, SSM_COLS, 1), F32)] * 2
        + [jax.ShapeDtypeStruct((n, SSM_COLS, SSM_GROUP), F32)] * 2,
        compiler_params=_params(1),
        name="s5_discretise",
    )(col(a_re), col(a_im), col(ldt),
      b_re.reshape(n, SSM_COLS, SSM_GROUP), b_im.reshape(n, SSM_COLS, SSM_GROUP))
    gpb = SSM_GROUPS // S5_BLOCKS

    def block_diagonal(t, rows_per_group, cols_per_group):
        rows = gpb * rows_per_group
        same = (jnp.arange(rows) // rows_per_group)[:, None] == jnp.arange(gpb)[None, :]
        wide = jnp.broadcast_to(t[:, :, :, None, :], (n, S5_BLOCKS, rows, gpb, cols_per_group))
        return jnp.where(same[None, None, :, :, None], wide, 0.0).reshape(
            n, S5_BLOCKS, rows, gpb * cols_per_group)

    def in_blocks(bb):
        t = bb.reshape(n, S5_BLOCKS, gpb, SSM_STATE, SSM_GROUP).transpose(0, 1, 2, 4, 3)
        return block_diagonal(t.reshape(n, S5_BLOCKS, S5_BLOCK_CH, SSM_STATE), SSM_GROUP, SSM_STATE)

    def out_blocks(c):
        t = c.reshape(n, S5_BLOCKS, gpb, SSM_GROUP, SSM_STATE).transpose(0, 1, 2, 4, 3)
        return block_diagonal(t.reshape(n, S5_BLOCKS, S5_BLOCK_ST, SSM_GROUP), SSM_STATE, SSM_GROUP)

    w_in = jnp.concatenate([in_blocks(bb_re), in_blocks(bb_im)], axis=-1).astype(BF16)
    w_out = jnp.concatenate([out_blocks(c_re), -out_blocks(c_im)], axis=-2).astype(BF16)
    return lam_re.reshape(n, 1, SSM_COLS), lam_im.reshape(n, 1, SSM_COLS), w_in, w_out


def _s5_kernel(u_ref, h0re_ref, h0im_ref, lre_ref, lim_ref, win_ref, wout_ref, d_ref, wglu_ref, bglu_ref,
               out_ref, sre_out_ref, sim_out_ref, hb_ref, sre_ref, sim_ref, *, bsz, t_chunk, col_block):
    c = pl.program_id(0)
    rows = bsz * t_chunk

    @pl.when(c == 0)
    def _():
        sre_ref[...] = h0re_ref[...]
        sim_ref[...] = h0im_ref[...]

    u = u_ref[...]
    ut = jnp.swapaxes(u, 0, 1).reshape(rows, MIX_IN).astype(BF16)
    ys = []
    for j in range(S5_BLOCKS):
        cols = slice(j * 2 * S5_BLOCK_ST, (j + 1) * 2 * S5_BLOCK_ST)
        hb_ref[:, cols] = _dot(ut[:, j * S5_BLOCK_CH:(j + 1) * S5_BLOCK_CH], win_ref[j])
        for sub in range(S5_BLOCK_ST // col_block):
            nat = j * S5_BLOCK_ST + sub * col_block
            cre = j * 2 * S5_BLOCK_ST + sub * col_block
            cim = cre + S5_BLOCK_ST
            lam_r = jnp.broadcast_to(lre_ref[:, nat:nat + col_block], (bsz, col_block))
            lam_i = jnp.broadcast_to(lim_ref[:, nat:nat + col_block], (bsz, col_block))
            s_r = sre_ref[:, nat:nat + col_block]
            s_i = sim_ref[:, nat:nat + col_block]
            for t in range(t_chunk):
                rows_t = slice(t * bsz, (t + 1) * bsz)
                n_r = lam_r * s_r - lam_i * s_i + hb_ref[rows_t, cre:cre + col_block]
                n_i = lam_r * s_i + lam_i * s_r + hb_ref[rows_t, cim:cim + col_block]
                hb_ref[rows_t, cre:cre + col_block] = n_r
                hb_ref[rows_t, cim:cim + col_block] = n_i
                s_r, s_i = n_r, n_i
            sre_ref[:, nat:nat + col_block] = s_r
            sim_ref[:, nat:nat + col_block] = s_i
        ys.append(_dot(hb_ref[:, cols].astype(BF16), wout_ref[j]))
    yt = jnp.concatenate(ys, axis=1).reshape(t_chunk, bsz, MIX_IN)
    y = jnp.swapaxes(yt, 0, 1) + d_ref[...] * u
    y = jax.nn.gelu(y).reshape(rows, MIX_IN)
    gate = _dot(y.astype(BF16), wglu_ref[...]) + bglu_ref[...]
    out_ref[...] = (y * jax.nn.sigmoid(gate)).reshape(bsz, t_chunk, MIX_IN).astype(BF16)

    @pl.when(c == pl.num_programs(0) - 1)
    def _():
        sre_out_ref[...] = sre_ref[...]
        sim_out_ref[...] = sim_ref[...]


def _s5_mixer(u, h0_re, h0_im, lam_re, lam_im, w_in, w_out, d_skip, w_glu_bf, b_glu, t_chunk):
    bsz, length, _ = u.shape
    rows = bsz * t_chunk
    col_block = (SUBLANES * 512) // bsz
    kern = functools.partial(_s5_kernel, bsz=bsz, t_chunk=t_chunk, col_block=col_block)
    state = jax.ShapeDtypeStruct((bsz, SSM_COLS), F32)
    return pl.pallas_call(
        kern,
        grid=(length // t_chunk,),
        in_specs=[
            pl.BlockSpec((bsz, t_chunk, MIX_IN), lambda c: (0, c, 0)),
            _full((bsz, SSM_COLS)), _full((bsz, SSM_COLS)),
            _full((1, SSM_COLS)), _full((1, SSM_COLS)),
            _full((S5_BLOCKS, S5_BLOCK_CH, 2 * S5_BLOCK_ST)),
            _full((S5_BLOCKS, 2 * S5_BLOCK_ST, S5_BLOCK_CH)),
            _full((1, MIX_IN)), _full((MIX_IN, MIX_IN)), _full((1, MIX_IN)),
        ],
        out_specs=[
            pl.BlockSpec((bsz, t_chunk, MIX_IN), lambda c: (0, c, 0)),
            _full((bsz, SSM_COLS)), _full((bsz, SSM_COLS)),
        ],
        out_shape=[jax.ShapeDtypeStruct((bsz, length, MIX_IN), BF16), state, state],
        scratch_shapes=[
            pltpu.VMEM((rows, 2 * SSM_COLS), F32),
            pltpu.VMEM((bsz, SSM_COLS), F32),
            pltpu.VMEM((bsz, SSM_COLS), F32),
        ],
        compiler_params=_params(1),
        name="s5_mixer",
    )(u, h0_re, h0_im, lam_re, lam_im, w_in, w_out, d_skip, w_glu_bf, b_glu)


def _ffn_kernel(h_ref, mix_ref, mem_ref, wo_ref, g_ref, win_ref, cw_ref, cb_ref, w2_ref, ctx_ref,
                out_ref, new_ref, act_ref, pad_ref, carry_ref, *, seqs, rows_per_seq, tiles_per_seq):
    m = pl.program_id(0)
    tm = seqs * rows_per_seq
    h1 = h_ref[...] + _dot(mix_ref[...], wo_ref[:MIX_IN, :]) + _dot(mem_ref[...], wo_ref[MIX_IN:, :])
    out_ref[...] = h1
    xn = _rms_rows(h1, g_ref[...]).astype(BF16)

    if tiles_per_seq > 1:
        @pl.when(m == 0)
        def _():
            carry_ref[...] = jnp.zeros(carry_ref.shape, F32)

    def causal_conv(col, slot):
        cols = slice(col, col + FFN_TF)
        u = _dot(xn, win_ref[:, cols])
        ctx = ctx_ref[:, :, cols]
        if tiles_per_seq > 1:
            ctx = jnp.where(m % tiles_per_seq == 0, ctx, carry_ref[:, cols][None])
        pad = pad_ref.at[slot]
        pad[:, 6:8, :] = ctx
        pad[:, 8:, :] = u.reshape(seqs, rows_per_seq, FFN_TF)
        cw = cw_ref[:, cols]
        y = (cb_ref[:, cols] + pad[:, 6:6 + rows_per_seq, :] * cw[0:1]
             + pad[:, 7:7 + rows_per_seq, :] * cw[1:2]
             + pad[:, 8:8 + rows_per_seq, :] * cw[2:3])
        last2 = pad[:, rows_per_seq + 6:rows_per_seq + 8, :]
        new_ref[:, :, cols] = last2
        if tiles_per_seq > 1:
            carry_ref[:, cols] = last2[0]
        return y.reshape(tm, FFN_TF)

    for f in range(N_FF_TILES):
        ya = causal_conv(f * FFN_TF, (2 * f) % FFN_PAD_SLOTS)
        yg = causal_conv(D_FF + f * FFN_TF, (2 * f + 1) % FFN_PAD_SLOTS)
        act_ref[:, f * FFN_TF:(f + 1) * FFN_TF] = (jax.nn.silu(yg) * ya).astype(BF16)
    out_ref[...] += _dot(act_ref[...], w2_ref[...])


def _mixout_ffn(h, mix, mem, w, layer, ctx, rows_per_seq_tile):
    nseq, length, _ = h.shape
    m_rows = nseq * length
    if rows_per_seq_tile >= length:
        seqs, rps, tps = nseq, length, 1
    else:
        seqs, rps, tps = 1, rows_per_seq_tile, length // rows_per_seq_tile
    tm = seqs * rps
    n_m = m_rows // tm
    kern = functools.partial(_ffn_kernel, seqs=seqs, rows_per_seq=rps, tiles_per_seq=tps)
    row = lambda width: pl.BlockSpec((tm, width), lambda m: (m, 0))
    resident = lambda *shape: pl.BlockSpec((None,) + shape, lambda m: (layer,) + (0,) * len(shape),
                                           pipeline_mode=pl.Buffered(1))
    out, new = pl.pallas_call(
        kern,
        grid=(n_m,),
        in_specs=[
            row(D_MODEL), row(MIX_IN), row(MEM_WIDTH),
            resident(MIX_IN + MEM_WIDTH, D_MODEL),
            resident(1, D_MODEL),
            resident(D_MODEL, 2 * D_FF),
            resident(CONV_W, 2 * D_FF),
            resident(1, 2 * D_FF),
            resident(D_FF, D_MODEL),
            pl.BlockSpec((seqs, CONV_W - 1, 2 * D_FF), lambda m: (m // tps, 0, 0)),
        ],
        out_specs=[row(D_MODEL), pl.BlockSpec((seqs, CONV_W - 1, 2 * D_FF), lambda m: (m, 0, 0))],
        out_shape=[
            jax.ShapeDtypeStruct((m_rows, D_MODEL), F32),
            jax.ShapeDtypeStruct((n_m * seqs, CONV_W - 1, 2 * D_FF), F32),
        ],
        scratch_shapes=[
            pltpu.VMEM((tm, D_FF), BF16),
            pltpu.VMEM((FFN_PAD_SLOTS, seqs, rps + SUBLANES, FFN_TF), F32),
            pltpu.VMEM((CONV_W - 1, 2 * D_FF), F32),
        ],
        compiler_params=_params(1),
        name="mix_out_conv_ffn",
    )(h.reshape(m_rows, D_MODEL), mix.reshape(m_rows, MIX_IN), mem.reshape(m_rows, MEM_WIDTH),
      w["w_mix_out"], w["norm_ffn_g"], w["w_ffn_in"], w["ffn_conv_w"], w["ffn_conv_b"], w["w_ffn_out"], ctx)
    new_ctx = new.reshape(nseq, tps, CONV_W - 1, 2 * D_FF)[:, -1]
    return out.reshape(nseq, length, D_MODEL), new_ctx


def _dkv_kernel(h_ref, g_ref, wl_ref, wr_ref, wrr_ref, lg_ref, kg_ref, kgr_ref, cos_ref, sin_ref,
                lat_ref, kr_ref):
    xn = _rms_rows(h_ref[...], g_ref[...]).astype(BF16)
    lat_ref[...] = _rms_rows(_dot(xn, wl_ref[...]), lg_ref[...])
    kr = _dot(xn, wr_ref[...])
    kr_rot = _dot(xn, wrr_ref[...])
    r = lax.rsqrt(jnp.mean(kr * kr, axis=-1, keepdims=True) + EPS)
    kr_ref[...] = (kr * r * kg_ref[...]) * cos_ref[...] + (kr_rot * r * kgr_ref[...]) * sin_ref[...]


def _swap_halves(t, axis=-1):
    a, b = jnp.split(t, 2, axis=axis)
    return jnp.concatenate([b, a], axis=axis)


def _shared_kv_down(h, kv_norm_g, w_dkv_bf, latent_norm_g, krope_norm_g, cos_rows, sin_rows, tm):
    nseq, length, _ = h.shape
    m_rows = nseq * length
    tab_blocks = cos_rows.shape[0] // tm
    w_l = w_dkv_bf[:, :KV_LORA]
    w_r = w_dkv_bf[:, KV_LORA:]
    kg = krope_norm_g.reshape(1, ROPE_DIM)
    lat, kr = pl.pallas_call(
        _dkv_kernel,
        grid=(m_rows // tm,),
        in_specs=[
            pl.BlockSpec((tm, D_MODEL), lambda m: (m, 0)),
            _full((1, D_MODEL)),
            _full((D_MODEL, KV_LORA)), _full((D_MODEL, ROPE_DIM)), _full((D_MODEL, ROPE_DIM)),
            _full((1, KV_LORA)), _full((1, ROPE_DIM)), _full((1, ROPE_DIM)),
            pl.BlockSpec((tm, ROPE_DIM), lambda m: (m % tab_blocks, 0)),
            pl.BlockSpec((tm, ROPE_DIM), lambda m: (m % tab_blocks, 0)),
        ],
        out_specs=[pl.BlockSpec((tm, KV_LORA), lambda m: (m, 0)),
                   pl.BlockSpec((tm, ROPE_DIM), lambda m: (m, 0))],
        out_shape=[jax.ShapeDtypeStruct((m_rows, KV_LORA), F32),
                   jax.ShapeDtypeStruct((m_rows, ROPE_DIM), F32)],
        compiler_params=_params(1),
        name="shared_kv_down",
    )(h.reshape(m_rows, D_MODEL), kv_norm_g.reshape(1, D_MODEL), w_l, w_r, _swap_halves(w_r),
      latent_norm_g.reshape(1, KV_LORA), kg, _swap_halves(kg), cos_rows, sin_rows)
    return lat.reshape(nseq, length, KV_LORA), kr.reshape(nseq, length, ROPE_DIM)


def _kv_up_kernel(lat_ref, wk_ref, wv_ref, seg_ref, g_ref, k_ref, v_ref):
    lat = lat_ref[...].astype(BF16)
    k_ref[...] = _seg_rms(_dot(lat, wk_ref[...]), seg_ref[...], g_ref[...]).astype(BF16)
    v_ref[...] = _dot(lat, wv_ref[...]).astype(BF16)


def _kv_up(latent_rows, w_uk_bf, w_uv_bf, seg_nope, k_g_tiled, tr):
    rows = latent_rows.shape[0]
    width = MLA_HEADS * NOPE_DIM
    return pl.pallas_call(
        _kv_up_kernel,
        grid=(rows // tr,),
        in_specs=[
            pl.BlockSpec((tr, KV_LORA), lambda r: (r, 0)),
            _full((KV_LORA, width)), _full((KV_LORA, width)),
            _full((width, width)), _full((1, width)),
        ],
        out_specs=[pl.BlockSpec((tr, width), lambda r: (r, 0))] * 2,
        out_shape=[jax.ShapeDtypeStruct((rows, width), BF16)] * 2,
        compiler_params=_params(1),
        name="kv_up",
    )(latent_rows, w_uk_bf, w_uv_bf, seg_nope, k_g_tiled)


Q_PRESCALE = MLA_SCALE * math.log2(math.e)


def _q_kernel(z_ref, g_ref, wn_ref, wa_ref, wb_ref, segn_ref, segr_ref, gn_ref, ga_ref, gb_ref,
              cos_ref, sin_ref, kg_ref, wuk_ref, qn_ref, qr_ref, *maybe_absorbed_refs):
    xn = _rms_rows(z_ref[...], g_ref[...]).astype(BF16)
    qn = _seg_rms(_dot(xn, wn_ref[...]), segn_ref[...], gn_ref[...]) * Q_PRESCALE
    qn_ref[...] = qn.astype(BF16)
    a = _dot(xn, wa_ref[...])
    b = _dot(xn, wb_ref[...])
    r = lax.rsqrt(_dot((a * a).astype(BF16), segr_ref[...]) + EPS)
    rot = (a * r * ga_ref[...]) * cos_ref[...] + (b * r * gb_ref[...]) * sin_ref[...]
    qr_ref[...] = (rot * Q_PRESCALE).astype(BF16)
    if maybe_absorbed_refs:
        qp_ref, qrh_ref = maybe_absorbed_refs
        lane = lax.broadcasted_iota(jnp.int32, (1, MLA_HEADS * NOPE_DIM), 1)
        qg = qn * kg_ref[...]
        rot_bf = (rot * Q_PRESCALE).astype(BF16)
        for head in range(MLA_HEADS):
            qh = jnp.where(lane // NOPE_DIM == head, qg, 0.0).astype(BF16)
            qp_ref[head] = _dot_nt(qh, wuk_ref[...]).astype(BF16)
            qrh_ref[head] = rot_bf[:, head * ROPE_DIM:(head + 1) * ROPE_DIM]


def _q_side(z_mix, q_latent_g, w_uq_bf, q_nope_g, q_rope_g, seg_nope, seg_rope, cos_rows, sin_rows,
            k_g_tiled, w_uk_bf, tm, absorbed):
    nseq, length, _ = z_mix.shape
    m_rows = nseq * length
    tab_blocks = cos_rows.shape[0] // tm
    wn_width = MLA_HEADS * NOPE_DIM
    wr_width = MLA_HEADS * ROPE_DIM
    w3 =w_uq_bf.reshape(MIX_IN, MLA_HEADS, NOPE_DIM + ROPE_DIM)
    w_n = w3[:, :, :NOPE_DIM].reshape(MIX_IN, wn_width)
    w_a = w3[:, :, NOPE_DIM:]
    w_b = _swap_halves(w_a).reshape(MIX_IN, wr_width)
    w_a = w_a.reshape(MIX_IN, wr_width)
    g_n = jnp.tile(q_nope_g.reshape(1, NOPE_DIM), (1, MLA_HEADS))
    g_a = jnp.tile(q_rope_g.reshape(1, ROPE_DIM), (1, MLA_HEADS))
    g_b = jnp.tile(_swap_halves(q_rope_g.reshape(1, ROPE_DIM)), (1, MLA_HEADS))
    cos_q = jnp.tile(cos_rows, (1, MLA_HEADS))
    sin_q = jnp.tile(sin_rows, (1, MLA_HEADS))
    out_specs = [pl.BlockSpec((tm, wd), lambda m: (m, 0)) for wd in (wn_width, wr_width)]
    out_shape = [jax.ShapeDtypeStruct((m_rows, wd), BF16) for wd in (wn_width, wr_width)]
    if absorbed:
        for wd in (KV_LORA, ROPE_DIM):
            out_specs.append(pl.BlockSpec((MLA_HEADS, tm, wd), lambda m: (0, m, 0)))
            out_shape.append(jax.ShapeDtypeStruct((MLA_HEADS, m_rows, wd), BF16))
    outs = pl.pallas_call(
        _q_kernel,
        grid=(m_rows // tm,),
        in_specs=[
            pl.BlockSpec((tm, MIX_IN), lambda m: (m, 0)),
            _full((1, MIX_IN)),
            _full((MIX_IN, wn_width)), _full((MIX_IN, wr_width)), _full((MIX_IN, wr_width)),
            _full((wn_width, wn_width)), _full((wr_width, wr_width)),
            _full((1, wn_width)), _full((1, wr_width)), _full((1, wr_width)),
            pl.BlockSpec((tm, wr_width), lambda m: (m % tab_blocks, 0)),
            pl.BlockSpec((tm, wr_width), lambda m: (m % tab_blocks, 0)),
            _full((1, wn_width)), _full((KV_LORA, wn_width)),
        ],
        out_specs=out_specs,
        out_shape=out_shape,
        compiler_params=_params(1),
        name="mla_query",
    )(z_mix.reshape(m_rows, MIX_IN), q_latent_g.reshape(1, MIX_IN), w_n, w_a, w_b, seg_nope, seg_rope,
      g_n, g_a, g_b, cos_q, sin_q, k_g_tiled, w_uk_bf)
    return outs


ATTN_HEADS_PER_STEP = 4


def _attn_kernel(qn_ref, qr_ref, kn_ref, kr_ref, v_ref, o_ref, *, tile, n_tiles):
    qi = pl.program_id(2)
    pairs = ATTN_HEADS_PER_STEP // 2
    lane = lax.broadcasted_iota(jnp.int32, (1, LANES), 1)
    head_lanes = [(lane // NOPE_DIM) == j for j in range(2)]
    qr = qr_ref[0]
    qcat = []
    for h in range(ATTN_HEADS_PER_STEP):
        qn = qn_ref[0, :, (h // 2) * LANES:(h // 2 + 1) * LANES]
        qcat.append(jnp.concatenate(
            [jnp.where(head_lanes[h % 2], qn, jnp.zeros_like(qn)),
             jnp.where((lane // ROPE_DIM) == h, qr, jnp.zeros_like(qr))], axis=1))
    row_chunk = lax.broadcasted_iota(jnp.int32, (tile, 1), 0) // CHUNK
    col_chunk = lax.broadcasted_iota(jnp.int32, (1, tile), 1) // CHUNK
    diag_visible = col_chunk <= row_chunk

    def one_block(kb, carry, masked):
        rows_k = slice(kb * tile, (kb + 1) * tile)
        kr = kr_ref[0, rows_k, :]
        new = []
        for h in range(ATTN_HEADS_PER_STEP):
            lanes_p = slice((h // 2) * LANES, (h // 2 + 1) * LANES)
            m_i, acc = carry[h]
            kcat = jnp.concatenate([kn_ref[0, rows_k, lanes_p], kr], axis=1)
            s = _dot_nt(qcat[h], kcat)
            if masked:
                s = jnp.where(diag_visible, s, NEG_INF)
            m_new = jnp.maximum(m_i, jnp.max(s, axis=-1, keepdims=True))
            alpha = jnp.exp2(m_i - m_new)
            p = jnp.exp2(s - m_new).astype(BF16)
            vb = v_ref[0, rows_k, lanes_p]
            v_h = jnp.where(head_lanes[h % 2], vb, jnp.ones_like(vb))
            new.append((m_new, alpha * acc + _dot(p, v_h)))
        return tuple(new)

    def query_tile(n_full):
        carry = tuple((jnp.full((tile, 1), NEG_INF, F32), jnp.zeros((tile, LANES), F32))
                      for _ in range(ATTN_HEADS_PER_STEP))
        for kb in range(n_full):
            carry = one_block(kb, carry, False)
        carry = one_block(n_full, carry, True)
        for p in range(pairs):
            out = jnp.zeros((tile, LANES), F32)
            for j in range(2):
                acc = carry[2 * p + j][1]
                row_sum = pltpu.roll(acc, NOPE_DIM, axis=1)
                out = jnp.where(head_lanes[j], acc * (1.0 / row_sum), out)
            o_ref[0, :, p * LANES:(p + 1) * LANES] = out.astype(BF16)

    for c in range(n_tiles):
        pl.when(qi == c)(functools.partial(query_tile, c))


def _mla_attention(qn, qr, kn, kr4, v, tile):
    nseq, length, _ = qn.shape
    kern = functools.partial(_attn_kernel, tile=tile, n_tiles=length // tile)
    width = (ATTN_HEADS_PER_STEP // 2) * LANES
    return pl.pallas_call(
        kern,
        grid=(nseq, MLA_HEADS // ATTN_HEADS_PER_STEP, length // tile),
        in_specs=[
            pl.BlockSpec((1, tile, width), lambda b, g, i: (b, i, g)),
            pl.BlockSpec((1, tile, LANES), lambda b, g, i: (b, i, g)),
            pl.BlockSpec((1, length, width), lambda b, g, i: (b, 0, g)),
            pl.BlockSpec((1, length, LANES), lambda b, g, i: (b, 0, 0)),
            pl.BlockSpec((1, length, width), lambda b, g, i: (b, 0, g)),
        ],
        out_specs=pl.BlockSpec((1, tile, width), lambda b, g, i: (b, i, g)),
        out_shape=jax.ShapeDtypeStruct((nseq, length, MLA_HEADS * V_DIM), BF16),
        compiler_params=_params(3),
        name="mla_attention",
    )(qn, qr, kn, kr4, v)


KNORM_ROWS = 16


def _key_norm_kernel(past_ref, new_ref, wukt_ref, rt_ref, *, n_chunks, chunk):
    def norms(lat):
        keys = lat.shape[0]
        kt = _dot_nt(wukt_ref[...], lat)
        ss = jnp.sum((kt * kt).reshape(MLA_HEADS, NOPE_DIM, keys), axis=1)
        r = lax.rsqrt(ss * (1.0 / NOPE_DIM) + EPS)
        return jnp.concatenate([r, jnp.ones((KNORM_ROWS - MLA_HEADS, keys), F32)], axis=0)

    for c in range(n_chunks):
        rt_ref[0, :, c * chunk:(c + 1) * chunk] = norms(past_ref[0, c * chunk:(c + 1) * chunk, :].astype(BF16))
    rt_ref[0, :, n_chunks * chunk:] = norms(new_ref[0])


def _key_norms(past_latent, new_lat_pad, w_ukt_bf, chunk):
    nseq, past, _ = past_latent.shape
    new_pad = new_lat_pad.shape[1]
    lk_pad = past + new_pad
    kern = functools.partial(_key_norm_kernel, n_chunks=past // chunk, chunk=chunk)
    return pl.pallas_call(
        kern,
        grid=(nseq,),
        in_specs=[pl.BlockSpec((1, past, KV_LORA), lambda b: (b, 0, 0)),
                  pl.BlockSpec((1, new_pad, KV_LORA), lambda b: (b, 0, 0)),
                  _full((MLA_HEADS * NOPE_DIM, KV_LORA))],
        out_specs=pl.BlockSpec((1, KNORM_ROWS, lk_pad), lambda b: (b, 0, 0)),
        out_shape=jax.ShapeDtypeStruct((nseq, KNORM_ROWS, lk_pad), F32),
        compiler_params=_params(1),
        name="mla_key_norms",
    )(past_latent, new_lat_pad, w_ukt_bf)


def _attn_absorbed_kernel(qs_ref, qrs_ref, plat_ref, nlat_ref, pkrt_ref, nkrt_ref, rt_ref, wuv_ref, o_ref, op_ref,
                          *, tq, q_off, lk_valid):
    b = pl.program_id(0)
    nseq = pl.num_programs(0)
    rows = MLA_HEADS * tq
    lat = jnp.concatenate([plat_ref[0].astype(BF16), nlat_ref[0]], axis=0)
    kr_t = jnp.concatenate([pkrt_ref[0].astype(BF16), nkrt_ref[0]], axis=1)
    lk_pad = lat.shape[0]
    rt = rt_ref[0]
    knorm = jnp.concatenate([jnp.broadcast_to(rt[h:h + 1, :], (tq, lk_pad)) for h in range(MLA_HEADS)], axis=0)
    qs = qs_ref[...].reshape(rows, KV_LORA)
    qrs = qrs_ref[...].reshape(rows, ROPE_DIM)
    s = _dot_nt(qs, lat) * knorm + _dot(qrs, kr_t)
    q_chunk = (q_off + lax.broadcasted_iota(jnp.int32, (rows, 1), 0) % tq) // CHUNK
    k_pos = lax.broadcasted_iota(jnp.int32, (1, lk_pad), 1)
    s = jnp.where(k_pos // CHUNK <= q_chunk, s, NEG_INF)
    s = jnp.where(k_pos < lk_valid, s, NEG_INF)
    p = jnp.exp2(s - jnp.max(s, axis=-1, keepdims=True))
    l = jnp.sum(p, axis=-1, keepdims=True)
    op_ref[b] = (_dot(p.astype(BF16), lat) * (1.0 / l)).astype(BF16)

    @pl.when(b == nseq - 1)
    def _():
        n_all = op_ref.shape[0]
        lane_o = lax.broadcasted_iota(jnp.int32, (1, MLA_HEADS * V_DIM), 1)
        out = jnp.zeros((n_all * tq, MLA_HEADS * V_DIM), F32)
        for h in range(MLA_HEADS):
            x = op_ref[:, h * tq:(h + 1) * tq, :].reshape(n_all * tq, KV_LORA)
            out = jnp.where(lane_o // V_DIM == h, _dot(x, wuv_ref[...]), out)
        o_ref[...] = out.astype(BF16)


def _mla_attention_absorbed(qp, qrh, past_latent, new_lat_pad, past_krope_t, new_kr_pad_t, r_t, w_uv_bf,
                            tq, lk_valid):
    nseq, past, _ = past_latent.shape
    new_pad = new_lat_pad.shape[1]
    kern = functools.partial(_attn_absorbed_kernel, tq=tq, q_off=past, lk_valid=lk_valid)
    per_seq = lambda n, width: pl.BlockSpec((1, n, width), lambda b: (b, 0, 0))
    heads_of_seq = lambda width: pl.BlockSpec((MLA_HEADS, tq, width), lambda b: (0, b, 0))
    out = pl.pallas_call(
        kern,
        grid=(nseq,),
        in_specs=[
            heads_of_seq(KV_LORA), heads_of_seq(ROPE_DIM),
            per_seq(past, KV_LORA), per_seq(new_pad, KV_LORA),
            per_seq(ROPE_DIM, past), per_seq(ROPE_DIM, new_pad),
            pl.BlockSpec((1, KNORM_ROWS, past + new_pad), lambda b: (b, 0, 0)),
            _full((KV_LORA, MLA_HEADS * V_DIM)),
        ],
        out_specs=_full((nseq * tq, MLA_HEADS * V_DIM)),
        out_shape=jax.ShapeDtypeStruct((nseq * tq, MLA_HEADS * V_DIM), BF16),
        scratch_shapes=[pltpu.VMEM((nseq, MLA_HEADS * tq, KV_LORA), BF16)],
        compiler_params=_params(1),
        name="mla_attention_absorbed",
    )(qp, qrh, past_latent, new_lat_pad, past_krope_t, new_kr_pad_t, r_t, w_uv_bf)
    return out.reshape(nseq, tq, MLA_HEADS * V_DIM)


def _trunk(x, mem_k_bf, mem_v_bf, ssm_h0_re, ssm_h0_im, conv_ctx, past_latent, past_krope, w, cfg):
    nseq, length, _ = x.shape
    past = 0 if past_latent is None else past_latent.shape[1]
    cos_t, sin_t = _rope_tables(past, length)
    reps = cfg["rope_rows"] // length
    cos_rows = jnp.tile(cos_t, (reps, 1))
    sin_rows = jnp.tile(sin_t, (reps, 1))
    if conv_ctx is None:
        conv_ctx = jnp.zeros((DEPTH, nseq, CONV_W - 1, 2 * D_FF), F32)
    if ssm_h0_re is None:
        ssm_h0_re = jnp.zeros((N_A_LAYERS, nseq, SSM_GROUPS, SSM_STATE), F32)
        ssm_h0_im = ssm_h0_re
    h = x
    ssm_re_out, ssm_im_out, conv_out = [], [], []
    for layer in range(DEPTH):
        z_mix, mem_out = _mixin(h, w, layer, mem_k_bf, mem_v_bf, cfg["tm_mixin"])
        if layer < N_A_LAYERS:
            i = layer
            mix_out, s_re, s_im = _s5_mixer(
                z_mix, ssm_h0_re[i].reshape(nseq, SSM_COLS), ssm_h0_im[i].reshape(nseq, SSM_COLS),
                w["lam_re"][i], w["lam_im"][i], w["s5_in"][i], w["s5_out"][i],
                w["ssm_d"][i][None], w["w_glu"][i], w["b_glu"][i][None], cfg["t_chunk"])
            ssm_re_out.append(s_re.reshape(nseq, SSM_GROUPS, SSM_STATE))
            ssm_im_out.append(s_im.reshape(nseq, SSM_GROUPS, SSM_STATE))
        else:
            if layer == N_A_LAYERS:
                new_latent, new_krope = _shared_kv_down(
                    h, w["kv_norm_g"], w["w_dkv"], w["latent_norm_g"], w["krope_norm_g"],
                    cos_rows, sin_rows, cfg["tm_rows"])
                if cfg["absorbed"]:
                    assert past % LANES == 0
                    lk_valid = past + length
                    new_pad = -(-length // LANES) * LANES
                    pad_rows = lambda t: jnp.pad(t.astype(BF16), ((0, 0), (0, new_pad - length), (0, 0)))
                    new_lat_pad = pad_rows(new_latent)
                    new_kr_pad_t = jnp.swapaxes(pad_rows(new_krope), 1, 2)
                    past_krope_t = jnp.swapaxes(past_krope, 1, 2)
                    r_t = _key_norms(past_latent, new_lat_pad, w["w_uk_t"], cfg["knorm_chunk"])
                else:
                    assert past_latent is None
                    kn, v_all = _kv_up(new_latent.reshape(nseq * length, KV_LORA), w["w_uk"], w["w_uv"],
                                       w["seg_nope"], w["k_nope_g"], cfg["tr_kv"])
                    kn = kn.reshape(nseq, length, MLA_HEADS * NOPE_DIM)
                    v_all = v_all.reshape(nseq, length, MLA_HEADS * V_DIM)
                    kr4 = jnp.tile(new_krope.astype(BF16), (1, 1, LANES // ROPE_DIM))
            j = layer - N_A_LAYERS
            q_out = _q_side(z_mix, w["q_latent_norm_g"][j], w["w_uq"][j], w["q_nope_norm_g"][j],
                            w["q_rope_norm_g"][j], w["seg_nope"], w["seg_rope"], cos_rows, sin_rows,
                            w["k_nope_g"], w["w_uk"], cfg["tm_rows"], cfg["absorbed"])
            if cfg["absorbed"]:
                mix_out = _mla_attention_absorbed(q_out[2], q_out[3], past_latent, new_lat_pad, past_krope_t,
                                                  new_kr_pad_t, r_t, w["w_uv"], length, lk_valid)
            else:
                qn = q_out[0].reshape(nseq, length, MLA_HEADS * NOPE_DIM)
                qr = q_out[1].reshape(nseq, length, MLA_HEADS * ROPE_DIM)
                mix_out = _mla_attention(qn, qr, kn, kr4, v_all, cfg["tq"])
        h, ctx = _mixout_ffn(h, mix_out, mem_out, w, layer, conv_ctx[layer], cfg["ffn_rows"])
        conv_out.append(ctx)
    return h, new_latent, new_krope, jnp.stack(ssm_re_out), jnp.stack(ssm_im_out), jnp.stack(conv_out)


PROMPT_CFG = dict(tm_mixin=1024, t_chunk=64, ffn_rows=512, tm_rows=1024, rope_rows=2048, tr_kv=1024,
                  tq=512, absorbed=False)
SAMPLE_CFG = dict(tm_mixin=32, t_chunk=32, ffn_rows=32, tm_rows=512, rope_rows=512, knorm_chunk=1024,
                  absorbed=True)


def kernel(x_prompt, x_sample, cache_mla_latent, cache_mla_krope, cache_mem_k, cache_mem_v, state_ssm_re, state_ssm_im, state_conv, mem_prompt, norm_mix_g, w_mix_in, w_mix_out, norm_ffn_g, w_ffn_in, ffn_conv_w, ffn_conv_b, w_ffn_out, mem_norm_g, w_mem_kv, mem_q_norm_g, mem_k_norm_g, ssm_a_re, ssm_a_im, ssm_log_dt, ssm_b_re, ssm_b_im, ssm_c_re, ssm_c_im, ssm_d, w_glu, b_glu, kv_norm_g, w_dkv, latent_norm_g, krope_norm_g, w_uk, w_uv, k_nope_norm_g, q_latent_norm_g, w_uq, q_nope_norm_g, q_rope_norm_g):
    bf = lambda t: t.astype(BF16)
    seg_mem = _seg_matrix(MEM_WIDTH, MEM_HEAD_DIM)
    lam_re, lam_im, s5_in, s5_out = _s5_prepare(ssm_a_re, ssm_a_im, ssm_log_dt, ssm_b_re, ssm_b_im,
                                                ssm_c_re, ssm_c_im)
    w = dict(
        norm_mix_g=norm_mix_g.reshape(DEPTH, 1, D_MODEL), w_mix_in=bf(w_mix_in), w_mix_out=bf(w_mix_out),
        norm_ffn_g=norm_ffn_g.reshape(DEPTH, 1, D_MODEL),
        w_ffn_in=bf(w_ffn_in), ffn_conv_w=ffn_conv_w, ffn_conv_b=ffn_conv_b.reshape(DEPTH, 1, 2 * D_FF),
        w_ffn_out=bf(w_ffn_out),
        mem_q_g=jnp.tile(mem_q_norm_g, (1, MEM_HEADS)).reshape(DEPTH, 1, MEM_WIDTH),
        seg_mem=seg_mem,
        seg_nope=_seg_matrix(MLA_HEADS * NOPE_DIM, NOPE_DIM),
        seg_rope=_seg_matrix(MLA_HEADS * ROPE_DIM, ROPE_DIM),
        lam_re=lam_re, lam_im=lam_im, s5_in=s5_in, s5_out=s5_out,
        ssm_d=ssm_d, w_glu=bf(w_glu), b_glu=b_glu,
        kv_norm_g=kv_norm_g, w_dkv=bf(w_dkv), latent_norm_g=latent_norm_g, krope_norm_g=krope_norm_g,
        w_uk=bf(w_uk), w_uk_t=bf(w_uk).T, w_uv=bf(w_uv),
        k_nope_g=jnp.tile(k_nope_norm_g.reshape(1, NOPE_DIM), (1, MLA_HEADS)),
        q_latent_norm_g=q_latent_norm_g, w_uq=bf(w_uq), q_nope_norm_g=q_nope_norm_g,
        q_rope_norm_g=q_rope_norm_g,
    )
    bsz = mem_prompt.shape[0]
    mem_k_p, mem_v_p, mem_k_bf, mem_v_bf = _memory_kv(mem_prompt, mem_norm_g, bf(w_mem_kv), mem_k_norm_g,
                                                      seg_mem)
    y_prompt, lat_p, krope_p, ssm_re_p, ssm_im_p, conv_p = _trunk(
        x_prompt, mem_k_bf, mem_v_bf, None, None, None, None, None, w, PROMPT_CFG)
    dec = cache_mem_k.shape[1]
    y_sample, lat_s, krope_s, ssm_re_s, ssm_im_s, conv_s = _trunk(
        x_sample, bf(cache_mem_k).reshape(DEPTH, dec, N_MEM, MEM_WIDTH),
        bf(cache_mem_v).reshape(DEPTH, dec, N_MEM, MEM_WIDTH),
        state_ssm_re, state_ssm_im, state_conv, cache_mla_latent, cache_mla_krope, w, SAMPLE_CFG)
    shape5 = (DEPTH, bsz, N_MEM, MEM_HEADS, MEM_HEAD_DIM)
    return (y_prompt, y_sample, mem_k_p.reshape(shape5), mem_v_p.reshape(shape5), lat_p, krope_p,
            ssm_re_p, ssm_im_p, conv_p, lat_s, krope_s, ssm_re_s, ssm_im_s, conv_s)
```

```python
import functools
import math

import jax
import jax.numpy as jnp
from jax import lax
from jax.experimental import pallas as pl
from jax.experimental.pallas import tpu as pltpu

F32 = jnp.float32
BF16 = jnp.bfloat16

D_MODEL = 1024
DEPTH = 4
CHUNK = 64
N_A_LAYERS = DEPTH // 2
N_B_LAYERS = DEPTH - N_A_LAYERS
MIX_IN = 768
MEM_HEADS = 4
MEM_HEAD_DIM = 64
MEM_WIDTH = MEM_HEADS * MEM_HEAD_DIM
N_MEM = 256
SSM_GROUP = 16
SSM_GROUPS = MIX_IN // SSM_GROUP
SSM_STATE = 64
SSM_COLS = SSM_GROUPS * SSM_STATE
MLA_HEADS = 12
NOPE_DIM = 64
ROPE_DIM = 32
ROPE_HALF = ROPE_DIM // 2
V_DIM = 64
KV_LORA = 256
ROPE_BASE = 10000.0
MLA_SCALE = (NOPE_DIM + ROPE_DIM) ** -0.5
MEM_SCALE = MEM_HEAD_DIM ** -0.5
D_FF = 2816
CONV_W = 3
EPS = 1e-6
NEG_INF = -1e30

V7X_VMEM_LIMIT_BYTES = 56 * 1024 * 1024
LANES = 128
SUBLANES = 8

S5_BLOCKS = 3
S5_BLOCK_CH = MIX_IN // S5_BLOCKS
S5_BLOCK_ST = SSM_COLS // S5_BLOCKS
FFN_TF = 256
N_FF_TILES = D_FF // FFN_TF
FFN_PAD_SLOTS = 4


def _params(n_axes):
    return pltpu.CompilerParams(
        dimension_semantics=("arbitrary",) * n_axes,
        vmem_limit_bytes=V7X_VMEM_LIMIT_BYTES,
    )


def _dot(a, b):
    return jnp.dot(a, b, preferred_element_type=F32)


def _dot_nt(a, b):
    return lax.dot_general(a, b, (((1,), (1,)), ((), ())), preferred_element_type=F32)


def _rms_rows(x, g):
    ms = jnp.mean(x * x, axis=-1, keepdims=True)
    return x * lax.rsqrt(ms + EPS) * g


def _seg_rms(x, seg_mat, g):
    ms = _dot((x * x).astype(BF16), seg_mat)
    return x * lax.rsqrt(ms + EPS) * g


def _seg_matrix(width, seg):
    idx = jnp.arange(width) // seg
    return jnp.where(idx[:, None] == idx[None, :], 1.0 / seg, 0.0).astype(BF16)


def _full(shape):
    nd = len(shape)
    return pl.BlockSpec(shape, lambda *_: (0,) * nd)


def _rope_table_kernel(pos_ref, inv_ref, cos_ref, sin_ref):
    ang = pos_ref[...] * inv_ref[...]
    cos_ref[...] = jnp.cos(ang)
    sin_ref[...] = jnp.sin(ang)


def _rope_tables(past, length):
    pos = (past + jnp.arange(length, dtype=jnp.int32)).astype(F32)[:, None]
    inv_freq = (1.0 / (ROPE_BASE ** (jnp.arange(0, ROPE_DIM, 2, dtype=F32) / ROPE_DIM)))[None, :]
    cos, sin = pl.pallas_call(
        _rope_table_kernel,
        out_shape=[jax.ShapeDtypeStruct((length, ROPE_HALF), F32)] * 2,
        name="rope_table",
    )(pos, inv_freq)
    cos_t = jnp.concatenate([cos, cos], axis=1)
    sin_t = jnp.concatenate([-sin, sin], axis=1)
    return cos_t, sin_t


def _memkv_kernel(mem_ref, g_ref, w_ref, kg_ref, seg_ref, k_ref, v_ref, kb_ref, vb_ref):
    xn = _rms_rows(mem_ref[0], g_ref[0]).astype(BF16)
    kv = _dot(xn, w_ref[0])
    k = _seg_rms(kv[:, :MEM_WIDTH], seg_ref[...], kg_ref[0])
    v = kv[:, MEM_WIDTH:]
    k_ref[0, 0] = k
    v_ref[0, 0] = v
    kb_ref[0, 0] = k.astype(BF16)
    vb_ref[0, 0] = v.astype(BF16)


def _memory_kv(mem, mem_norm_g, w_mem_kv_bf, mem_k_norm_g, seg_mem):
    bsz = mem.shape[0]
    kg = jnp.tile(mem_k_norm_g, (1, MEM_HEADS)).reshape(DEPTH, 1, MEM_WIDTH)
    out4 = lambda dt: jax.ShapeDtypeStruct((DEPTH, bsz, N_MEM, MEM_WIDTH), dt)
    spec4 = pl.BlockSpec((1, 1, N_MEM, MEM_WIDTH), lambda l, b: (l, b, 0, 0))
    return pl.pallas_call(
        _memkv_kernel,
        grid=(DEPTH, bsz),
        in_specs=[
            pl.BlockSpec((1, N_MEM, D_MODEL), lambda l, b: (b, 0, 0)),
            pl.BlockSpec((1, 1, D_MODEL), lambda l, b: (l, 0, 0)),
            pl.BlockSpec((1, D_MODEL, 2 * MEM_WIDTH), lambda l, b: (l, 0, 0)),
            pl.BlockSpec((1, 1, MEM_WIDTH), lambda l, b: (l, 0, 0)),
            _full((MEM_WIDTH, MEM_WIDTH)),
        ],
        out_specs=[spec4, spec4, spec4, spec4],
        out_shape=[out4(F32), out4(F32), out4(BF16), out4(BF16)],
        compiler_params=_params(2),
        name="memory_kv",
    )(mem, mem_norm_g.reshape(DEPTH, 1, D_MODEL), w_mem_kv_bf, kg, seg_mem)


def _mixin_kernel(h_ref, g_ref, w_ref, qg_ref, seg_ref, k_ref, v_ref, zmix_ref, mem_ref, *, seqs, rows_per_seq):
    xn = _rms_rows(h_ref[...], g_ref[...]).astype(BF16)
    z = _dot(xn, w_ref[...])
    zmix_ref[...] = z[:, :MIX_IN]
    mq = _seg_rms(z[:, MIX_IN:], seg_ref[...], qg_ref[...]).astype(BF16)
    lane = lax.broadcasted_iota(jnp.int32, (1, MEM_WIDTH), 1)
    for b in range(seqs):
        rows = slice(b * rows_per_seq, (b + 1) * rows_per_seq)
        mq_b = mq[rows]
        k = k_ref[b]
        v = v_ref[b]
        out = jnp.zeros(mq_b.shape, F32)
        for head in range(MEM_HEADS):
            in_head = (lane // MEM_HEAD_DIM) == head
            qh = jnp.where(in_head, mq_b, jnp.zeros_like(mq_b))
            s = _dot_nt(qh, k) * MEM_SCALE
            p = jnp.exp(s - jnp.max(s, axis=-1, keepdims=True))
            p = p * (1.0 / jnp.sum(p, axis=-1, keepdims=True))
            o = _dot(p.astype(BF16), v)
            out = jnp.where(in_head, o, out)
        mem_ref[rows, :] = out.astype(BF16)


def _mixin(h, w, layer, k_bf, v_bf, rows_per_seq_tile):
    nseq, length, _ = h.shape
    m_rows = nseq * length
    if rows_per_seq_tile >= length:
        seqs, rps, tps = nseq, length, 1
    else:
        seqs, rps, tps = 1, rows_per_seq_tile, length // rows_per_seq_tile
    tm = seqs * rps
    kern = functools.partial(_mixin_kernel, seqs=seqs, rows_per_seq=rps)
    row = lambda width: pl.BlockSpec((tm, width), lambda m: (m, 0))
    of_layer = lambda *shape: pl.BlockSpec((None,) + shape, lambda m: (layer,) + (0,) * len(shape))
    kv_spec = pl.BlockSpec((None, seqs, N_MEM, MEM_WIDTH), lambda m: (layer, m // tps, 0, 0))
    z_mix, mem_out = pl.pallas_call(
        kern,
        grid=(m_rows // tm,),
        in_specs=[
            row(D_MODEL),
            of_layer(1, D_MODEL),
            of_layer(D_MODEL, MIX_IN + MEM_WIDTH),
            of_layer(1, MEM_WIDTH),
            _full((MEM_WIDTH, MEM_WIDTH)),
            kv_spec, kv_spec,
        ],
        out_specs=[row(MIX_IN), row(MEM_WIDTH)],
        out_shape=[
            jax.ShapeDtypeStruct((m_rows, MIX_IN), F32),
            jax.ShapeDtypeStruct((m_rows, MEM_WIDTH), BF16),
        ],
        compiler_params=_params(1),
        name="mix_in_mem_attn",
    )(h.reshape(m_rows, D_MODEL), w["norm_mix_g"], w["w_mix_in"], w["mem_q_g"], w["seg_mem"], k_bf, v_bf)
    return z_mix.reshape(nseq, length, MIX_IN), mem_out.reshape(nseq, length, MEM_WIDTH)


def _s5_prep_kernel(are_ref, aim_ref, ldt_ref, bre_ref, bim_ref, lre_ref, lim_ref, bbre_ref, bbim_ref):
    a_re, a_im = are_ref[0], aim_ref[0]
    dt = jnp.exp(ldt_ref[0])
    mag = jnp.exp(a_re * dt)
    lam_re = mag * jnp.cos(a_im * dt)
    lam_im = mag * jnp.sin(a_im * dt)
    den = a_re * a_re + a_im * a_im
    x_re = lam_re - 1.0
    f_re = (x_re * a_re + lam_im * a_im) / den
    f_im = (lam_im * a_re - x_re * a_im) / den
    b_re, b_im = bre_ref[0], bim_ref[0]
    lre_ref[0] = lam_re
    lim_ref[0] = lam_im
    bbre_ref[0] = f_re * b_re - f_im * b_im
    bbim_ref[0] = f_re * b_im + f_im * b_re


def _s5_prepare(a_re, a_im, log_dt, b_re, b_im, c_re, c_im):
    n = a_re.shape[0]
    col = lambda t: t.reshape(n, SSM_COLS, 1)
    ldt = jnp.broadcast_to(log_dt[:, :, None], (n, SSM_GROUPS, SSM_STATE))
    col_spec = pl.BlockSpec((1, SSM_COLS, 1), lambda l: (l, 0, 0))
    b_spec = pl.BlockSpec((1, SSM_COLS, SSM_GROUP), lambda l: (l, 0, 0))
    lam_re, lam_im, bb_re, bb_im = pl.pallas_call(
        _s5_prep_kernel,
        grid=(n,),
        in_specs=[col_spec, col_spec, col_spec, b_spec, b_spec],
        out_specs=[col_spec, col_spec, b_spec, b_spec],
        out_shape=[jax.ShapeDtypeStruct((n, SSM_COLS, 1), F32)] * 2
        + [jax.ShapeDtypeStruct((n, SSM_COLS, SSM_GROUP), F32)] * 2,
        compiler_params=_params(1),
        name="s5_discretise",
    )(col(a_re), col(a_im), col(ldt),
      b_re.reshape(n, SSM_COLS, SSM_GROUP), b_im.reshape(n, SSM_COLS, SSM_GROUP))
    gpb = SSM_GROUPS // S5_BLOCKS

    def block_diagonal(t, rows_per_group, cols_per_group):
        rows = gpb * rows_per_group
        same = (jnp.arange(rows) // rows_per_group)[:, None] == jnp.arange(gpb)[None, :]
        wide = jnp.broadcast_to(t[:, :, :, None, :], (n, S5_BLOCKS, rows, gpb, cols_per_group))
        return jnp.where(same[None, None, :, :, None], wide, 0.0).reshape(
            n, S5_BLOCKS, rows, gpb * cols_per_group)

    def in_blocks(bb):
        t = bb.reshape(n, S5_BLOCKS, gpb, SSM_STATE, SSM_GROUP).transpose(0, 1, 2, 4, 3)
        return block_diagonal(t.reshape(n, S5_BLOCKS, S5_BLOCK_CH, SSM_STATE), SSM_GROUP, SSM_STATE)

    def out_blocks(c):
        t = c.reshape(n, S5_BLOCKS, gpb, SSM_GROUP, SSM_STATE).transpose(0, 1, 2, 4, 3)
        return block_diagonal(t.reshape(n, S5_BLOCKS, S5_BLOCK_ST, SSM_GROUP), SSM_STATE, SSM_GROUP)

    w_in = jnp.concatenate([in_blocks(bb_re), in_blocks(bb_im)], axis=-1).astype(BF16)
    w_out = jnp.concatenate([out_blocks(c_re), -out_blocks(c_im)], axis=-2).astype(BF16)
    return lam_re.reshape(n, 1, SSM_COLS), lam_im.reshape(n, 1, SSM_COLS), w_in, w_out


def _s5_kernel(u_ref, h0re_ref, h0im_ref, lre_ref, lim_ref, win_ref, wout_ref, d_ref, wglu_ref, bglu_ref,
               out_ref, sre_out_ref, sim_out_ref, hb_ref, sre_ref, sim_ref, *, bsz, t_chunk, col_block):
    c = pl.program_id(0)
    rows = bsz * t_chunk

    @pl.when(c == 0)
    def _():
        sre_ref[...] = h0re_ref[...]
        sim_ref[...] = h0im_ref[...]

    u = u_ref[...]
    ut = jnp.swapaxes(u, 0, 1).reshape(rows, MIX_IN).astype(BF16)
    ys = []
    for j in range(S5_BLOCKS):
        cols = slice(j * 2 * S5_BLOCK_ST, (j + 1) * 2 * S5_BLOCK_ST)
        hb_ref[:, cols] = _dot(ut[:, j * S5_BLOCK_CH:(j + 1) * S5_BLOCK_CH], win_ref[j])
        for sub in range(S5_BLOCK_ST // col_block):
            nat = j * S5_BLOCK_ST + sub * col_block
            cre = j * 2 * S5_BLOCK_ST + sub * col_block
            cim = cre + S5_BLOCK_ST
            lam_r = jnp.broadcast_to(lre_ref[:, nat:nat + col_block], (bsz, col_block))
            lam_i = jnp.broadcast_to(lim_ref[:, nat:nat + col_block], (bsz, col_block))
            s_r = sre_ref[:, nat:nat + col_block]
            s_i = sim_ref[:, nat:nat + col_block]
            for t in range(t_chunk):
                rows_t = slice(t * bsz, (t + 1) * bsz)
                n_r = lam_r * s_r - lam_i * s_i + hb_ref[rows_t, cre:cre + col_block]
                n_i = lam_r * s_i + lam_i * s_r + hb_ref[rows_t, cim:cim + col_block]
                hb_ref[rows_t, cre:cre + col_block] = n_r
                hb_ref[rows_t, cim:cim + col_block] = n_i
                s_r, s_i = n_r, n_i
            sre_ref[:, nat:nat + col_block] = s_r
            sim_ref[:, nat:nat + col_block] = s_i
        ys.append(_dot(hb_ref[:, cols].astype(BF16), wout_ref[j]))
    yt = jnp.concatenate(ys, axis=1).reshape(t_chunk, bsz, MIX_IN)
    y = jnp.swapaxes(yt, 0, 1) + d_ref[...] * u
    y = jax.nn.gelu(y).reshape(rows, MIX_IN)
    gate = _dot(y.astype(BF16), wglu_ref[...]) + bglu_ref[...]
    out_ref[...] = (y * jax.nn.sigmoid(gate)).reshape(bsz, t_chunk, MIX_IN).astype(BF16)

    @pl.when(c == pl.num_programs(0) - 1)
    def _():
        sre_out_ref[...] = sre_ref[...]
        sim_out_ref[...] = sim_ref[...]


def _s5_mixer(u, h0_re, h0_im, lam_re, lam_im, w_in, w_out, d_skip, w_glu_bf, b_glu, t_chunk):
    bsz, length, _ = u.shape
    rows = bsz * t_chunk
    col_block = (SUBLANES * 512) // bsz
    kern = functools.partial(_s5_kernel, bsz=bsz, t_chunk=t_chunk, col_block=col_block)
    state = jax.ShapeDtypeStruct((bsz, SSM_COLS), F32)
    return pl.pallas_call(
        kern,
        grid=(length // t_chunk,),
        in_specs=[
            pl.BlockSpec((bsz, t_chunk, MIX_IN), lambda c: (0, c, 0)),
            _full((bsz, SSM_COLS)), _full((bsz, SSM_COLS)),
            _full((1, SSM_COLS)), _full((1, SSM_COLS)),
            _full((S5_BLOCKS, S5_BLOCK_CH, 2 * S5_BLOCK_ST)),
            _full((S5_BLOCKS, 2 * S5_BLOCK_ST, S5_BLOCK_CH)),
            _full((1, MIX_IN)), _full((MIX_IN, MIX_IN)), _full((1, MIX_IN)),
        ],
        out_specs=[
            pl.BlockSpec((bsz, t_chunk, MIX_IN), lambda c: (0, c, 0)),
            _full((bsz, SSM_COLS)), _full((bsz, SSM_COLS)),
        ],
        out_shape=[jax.ShapeDtypeStruct((bsz, length, MIX_IN), BF16), state, state],
        scratch_shapes=[
            pltpu.VMEM((rows, 2 * SSM_COLS), F32),
            pltpu.VMEM((bsz, SSM_COLS), F32),
            pltpu.VMEM((bsz, SSM_COLS), F32),
        ],
        compiler_params=_params(1),
        name="s5_mixer",
    )(u, h0_re, h0_im, lam_re, lam_im, w_in, w_out, d_skip, w_glu_bf, b_glu)


def _ffn_kernel(h_ref, mix_ref, mem_ref, wo_ref, g_ref, win_ref, cw_ref, cb_ref, w2_ref, ctx_ref,
                out_ref, new_ref, act_ref, pad_ref, carry_ref, *, seqs, rows_per_seq, tiles_per_seq):
    m = pl.program_id(0)
    tm = seqs * rows_per_seq
    h1 = h_ref[...] + _dot(mix_ref[...], wo_ref[:MIX_IN, :]) + _dot(mem_ref[...], wo_ref[MIX_IN:, :])
    out_ref[...] = h1
    xn = _rms_rows(h1, g_ref[...]).astype(BF16)

    if tiles_per_seq > 1:
        @pl.when(m == 0)
        def _():
            carry_ref[...] = jnp.zeros(carry_ref.shape, F32)

    def causal_conv(col, slot):
        cols = slice(col, col + FFN_TF)
        u = _dot(xn, win_ref[:, cols])
        ctx = ctx_ref[:, :, cols]
        if tiles_per_seq > 1:
            ctx = jnp.where(m % tiles_per_seq == 0, ctx, carry_ref[:, cols][None])
        cw = cw_ref[:, cols]
        if seqs == 1:
            row = lax.broadcasted_iota(jnp.int32, (SUBLANES, 1), 0)
            back1 = pltpu.roll(u, 1, axis=0)
            back2 = pltpu.roll(u, 2, axis=0)
            head1 = jnp.where(row == 0, ctx[0, 1:2], back1[:SUBLANES])
            head2 = jnp.where(row == 0, ctx[0, 0:1], jnp.where(row == 1, ctx[0, 1:2], back2[:SUBLANES]))
            back1 = jnp.concatenate([head1, back1[SUBLANES:]], axis=0)
            back2 = jnp.concatenate([head2, back2[SUBLANES:]], axis=0)
            y = cb_ref[:, cols] + back2 * cw[0:1] + back1 * cw[1:2] + u * cw[2:3]
            last2 = u[tm - 2:][None]
        else:
            pad = pad_ref.at[slot]
            pad[:, 6:8, :] = ctx
            pad[:, 8:, :] = u.reshape(seqs, rows_per_seq, FFN_TF)
            y = (cb_ref[:, cols] + pad[:, 6:6 + rows_per_seq, :] * cw[0:1]
                 + pad[:, 7:7 + rows_per_seq, :] * cw[1:2]
                 + pad[:, 8:8 + rows_per_seq, :] * cw[2:3]).reshape(tm, FFN_TF)
            last2 = pad[:, rows_per_seq + 6:rows_per_seq + 8, :]
        new_ref[:, :, cols] = last2
        if tiles_per_seq > 1:
            carry_ref[:, cols] = last2[0]
        return y

    for f in range(N_FF_TILES):
        ya = causal_conv(f * FFN_TF, (2 * f) % FFN_PAD_SLOTS)
        yg = causal_conv(D_FF + f * FFN_TF, (2 * f + 1) % FFN_PAD_SLOTS)
        act_ref[:, f * FFN_TF:(f + 1) * FFN_TF] = (jax.nn.silu(yg) * ya).astype(BF16)
    out_ref[...] += _dot(act_ref[...], w2_ref[...])


def _mixout_ffn(h, mix, mem, w, layer, ctx, rows_per_seq_tile):
    nseq, length, _ = h.shape
    m_rows = nseq * length
    if rows_per_seq_tile >= length:
        seqs, rps, tps = nseq, length, 1
    else:
        seqs, rps, tps = 1, rows_per_seq_tile, length // rows_per_seq_tile
    tm = seqs * rps
    n_m = m_rows // tm
    kern = functools.partial(_ffn_kernel, seqs=seqs, rows_per_seq=rps, tiles_per_seq=tps)
    row = lambda width: pl.BlockSpec((tm, width), lambda m: (m, 0))
    resident = lambda *shape: pl.BlockSpec((None,) + shape, lambda m: (layer,) + (0,) * len(shape),
                                           pipeline_mode=pl.Buffered(1))
    out, new = pl.pallas_call(
        kern,
        grid=(n_m,),
        in_specs=[
            row(D_MODEL), row(MIX_IN), row(MEM_WIDTH),
            resident(MIX_IN + MEM_WIDTH, D_MODEL),
            resident(1, D_MODEL),
            resident(D_MODEL, 2 * D_FF),
            resident(CONV_W, 2 * D_FF),
            resident(1, 2 * D_FF),
            resident(D_FF, D_MODEL),
            pl.BlockSpec((seqs, CONV_W - 1, 2 * D_FF), lambda m: (m // tps, 0, 0)),
        ],
        out_specs=[row(D_MODEL), pl.BlockSpec((seqs, CONV_W - 1, 2 * D_FF), lambda m: (m, 0, 0))],
        out_shape=[
            jax.ShapeDtypeStruct((m_rows, D_MODEL), F32),
            jax.ShapeDtypeStruct((n_m * seqs, CONV_W - 1, 2 * D_FF), F32),
        ],
        scratch_shapes=[
            pltpu.VMEM((tm, D_FF), BF16),
            pltpu.VMEM((FFN_PAD_SLOTS, seqs, rps + SUBLANES, FFN_TF) if seqs > 1 else (1, 1, SUBLANES, LANES), F32),
            pltpu.VMEM((CONV_W - 1, 2 * D_FF), F32),
        ],
        compiler_params=_params(1),
        name="mix_out_conv_ffn",
    )(h.reshape(m_rows, D_MODEL), mix.reshape(m_rows, MIX_IN), mem.reshape(m_rows, MEM_WIDTH),
      w["w_mix_out"], w["norm_ffn_g"], w["w_ffn_in"], w["ffn_conv_w"], w["ffn_conv_b"], w["w_ffn_out"], ctx)
    new_ctx = new.reshape(nseq, tps, CONV_W - 1, 2 * D_FF)[:, -1]
    return out.reshape(nseq, length, D_MODEL), new_ctx


def _dkv_kernel(h_ref, g_ref, wl_ref, wr_ref, wrr_ref, lg_ref, kg_ref, kgr_ref, cos_ref, sin_ref,
                lat_ref, kr_ref):
    xn = _rms_rows(h_ref[...], g_ref[...]).astype(BF16)
    lat_ref[...] = _rms_rows(_dot(xn, wl_ref[...]), lg_ref[...])
    kr = _dot(xn, wr_ref[...])
    kr_rot = _dot(xn, wrr_ref[...])
    r = lax.rsqrt(jnp.mean(kr * kr, axis=-1, keepdims=True) + EPS)
    kr_ref[...] = (kr * r * kg_ref[...]) * cos_ref[...] + (kr_rot * r * kgr_ref[...]) * sin_ref[...]


def _swap_halves(t, axis=-1):
    a, b = jnp.split(t, 2, axis=axis)
    return jnp.concatenate([b, a], axis=axis)


def _shared_kv_down(h, kv_norm_g, w_dkv_bf, latent_norm_g, krope_norm_g, cos_rows, sin_rows, tm):
    nseq, length, _ = h.shape
    m_rows = nseq * length
    tab_blocks = cos_rows.shape[0] // tm
    w_l = w_dkv_bf[:, :KV_LORA]
    w_r = w_dkv_bf[:, KV_LORA:]
    kg = krope_norm_g.reshape(1, ROPE_DIM)
    lat, kr = pl.pallas_call(
        _dkv_kernel,
        grid=(m_rows // tm,),
        in_specs=[
            pl.BlockSpec((tm, D_MODEL), lambda m: (m, 0)),
            _full((1, D_MODEL)),
            _full((D_MODEL, KV_LORA)), _full((D_MODEL, ROPE_DIM)), _full((D_MODEL, ROPE_DIM)),
            _full((1, KV_LORA)), _full((1, ROPE_DIM)), _full((1, ROPE_DIM)),
            pl.BlockSpec((tm, ROPE_DIM), lambda m: (m % tab_blocks, 0)),
            pl.BlockSpec((tm, ROPE_DIM), lambda m: (m % tab_blocks, 0)),
        ],
        out_specs=[pl.BlockSpec((tm, KV_LORA), lambda m: (m, 0)),
                   pl.BlockSpec((tm, ROPE_DIM), lambda m: (m, 0))],
        out_shape=[jax.ShapeDtypeStruct((m_rows, KV_LORA), F32),
                   jax.ShapeDtypeStruct((m_rows, ROPE_DIM), F32)],
        compiler_params=_params(1),
        name="shared_kv_down",
    )(h.reshape(m_rows, D_MODEL), kv_norm_g.reshape(1, D_MODEL), w_l, w_r, _swap_halves(w_r),
      latent_norm_g.reshape(1, KV_LORA), kg, _swap_halves(kg), cos_rows, sin_rows)
    return lat.reshape(nseq, length, KV_LORA), kr.reshape(nseq, length, ROPE_DIM)


def _kv_up_kernel(lat_ref, wk_ref, wv_ref, seg_ref, g_ref, k_ref, v_ref):
    lat = lat_ref[...].astype(BF16)
    k_ref[...] = _seg_rms(_dot(lat, wk_ref[...]), seg_ref[...], g_ref[...]).astype(BF16)
    v_ref[...] = _dot(lat, wv_ref[...]).astype(BF16)


def _kv_up(latent_rows, w_uk_bf, w_uv_bf, seg_nope, k_g_tiled, tr):
    rows = latent_rows.shape[0]
    width = MLA_HEADS * NOPE_DIM
    return pl.pallas_call(
        _kv_up_kernel,
        grid=(rows // tr,),
        in_specs=[
            pl.BlockSpec((tr, KV_LORA), lambda r: (r, 0)),
            _full((KV_LORA, width)), _full((KV_LORA, width)),
            _full((width, width)), _full((1, width)),
        ],
        out_specs=[pl.BlockSpec((tr, width), lambda r: (r, 0))] * 2,
        out_shape=[jax.ShapeDtypeStruct((rows, width), BF16)] * 2,
        compiler_params=_params(1),
        name="kv_up",
    )(latent_rows, w_uk_bf, w_uv_bf, seg_nope, k_g_tiled)


Q_PRESCALE = MLA_SCALE * math.log2(math.e)


def _q_kernel(z_ref, g_ref, wn_ref, wa_ref, wb_ref, segn_ref, segr_ref, gn_ref, ga_ref, gb_ref,
              cos_ref, sin_ref, kg_ref, wuk_ref, qn_ref, qr_ref, *maybe_absorbed_refs):
    xn = _rms_rows(z_ref[...], g_ref[...]).astype(BF16)
    qn = _seg_rms(_dot(xn, wn_ref[...]), segn_ref[...], gn_ref[...]) * Q_PRESCALE
    qn_ref[...] = qn.astype(BF16)
    a = _dot(xn, wa_ref[...])
    b = _dot(xn, wb_ref[...])
    r = lax.rsqrt(_dot((a * a).astype(BF16), segr_ref[...]) + EPS)
    rot = (a * r * ga_ref[...]) * cos_ref[...] + (b * r * gb_ref[...]) * sin_ref[...]
    qr_ref[...] = (rot * Q_PRESCALE).astype(BF16)
    if maybe_absorbed_refs:
        qp_ref, qrh_ref = maybe_absorbed_refs
        lane = lax.broadcasted_iota(jnp.int32, (1, MLA_HEADS * NOPE_DIM), 1)
        qg = qn * kg_ref[...]
        rot_bf = (rot * Q_PRESCALE).astype(BF16)
        for head in range(MLA_HEADS):
            qh = jnp.where(lane // NOPE_DIM == head, qg, 0.0).astype(BF16)
            qp_ref[head] = _dot_nt(qh, wuk_ref[...]).astype(BF16)
            qrh_ref[head] = rot_bf[:, head * ROPE_DIM:(head + 1) * ROPE_DIM]


def _q_side(z_mix, q_latent_g, w_uq_bf, q_nope_g, q_rope_g, seg_nope, seg_rope, cos_rows, sin_rows,
            k_g_tiled, w_uk_bf, tm, absorbed):
    nseq, length, _ = z_mix.shape
    m_rows = nseq * length
    tab_blocks = cos_rows.shape[0] // tm
    wn_width = MLA_HEADS * NOPE_DIM
    wr_width = MLA_HEADS * ROPE_DIM
    w3 = w_uq_bf.reshape(MIX_IN, MLA_HEADS, NOPE_DIM + ROPE_DIM)
    w_n = w3[:, :, :NOPE_DIM].reshape(MIX_IN, wn_width)
    w_a = w3[:, :, NOPE_DIM:]
    w_b = _swap_halves(w_a).reshape(MIX_IN, wr_width)
    w_a = w_a.reshape(MIX_IN, wr_width)
    g_n = jnp.tile(q_nope_g.reshape(1, NOPE_DIM), (1, MLA_HEADS))
    g_a = jnp.tile(q_rope_g.reshape(1, ROPE_DIM), (1, MLA_HEADS))
    g_b = jnp.tile(_swap_halves(q_rope_g.reshape(1, ROPE_DIM)), (1, MLA_HEADS))
    cos_q = jnp.tile(cos_rows, (1, MLA_HEADS))
    sin_q = jnp.tile(sin_rows, (1, MLA_HEADS))
    out_specs = [pl.BlockSpec((tm, wd), lambda m: (m, 0)) for wd in (wn_width, wr_width)]
    out_shape = [jax.ShapeDtypeStruct((m_rows, wd), BF16) for wd in (wn_width, wr_width)]
    if absorbed:
        for wd in (KV_LORA, ROPE_DIM):
            out_specs.append(pl.BlockSpec((MLA_HEADS, tm, wd), lambda m: (0, m, 0)))
            out_shape.append(jax.ShapeDtypeStruct((MLA_HEADS, m_rows, wd), BF16))
    outs = pl.pallas_call(
        _q_kernel,
        grid=(m_rows // tm,),
        in_specs=[
            pl.BlockSpec((tm, MIX_IN), lambda m: (m, 0)),
            _full((1, MIX_IN)),
            _full((MIX_IN, wn_width)), _full((MIX_IN, wr_width)), _full((MIX_IN, wr_width)),
            _full((wn_width, wn_width)), _full((wr_width, wr_width)),
            _full((1, wn_width)), _full((1, wr_width)), _full((1, wr_width)),
            pl.BlockSpec((tm, wr_width), lambda m: (m % tab_blocks, 0)),
            pl.BlockSpec((tm, wr_width), lambda m: (m % tab_blocks, 0)),
            _full((1, wn_width)), _full((KV_LORA, wn_width)),
        ],
        out_specs=out_specs,
        out_shape=out_shape,
        compiler_params=_params(1),
        name="mla_query",
    )(z_mix.reshape(m_rows, MIX_IN), q_latent_g.reshape(1, MIX_IN), w_n, w_a, w_b, seg_nope, seg_rope,
      g_n, g_a, g_b, cos_q, sin_q, k_g_tiled, w_uk_bf)
    return outs


ATTN_HEADS_PER_STEP = 4


def _attn_kernel(qn_ref, qr_ref, kn_ref, kr_ref, v_ref, o_ref, *, tile, n_tiles):
    qi = pl.program_id(2)
    pairs = ATTN_HEADS_PER_STEP // 2
    lane = lax.broadcasted_iota(jnp.int32, (1, LANES), 1)
    head_lanes = [(lane // NOPE_DIM) == j for j in range(2)]
    qr = qr_ref[0]
    qcat = []
    for h in range(ATTN_HEADS_PER_STEP):
        qn = qn_ref[0, :, (h // 2) * LANES:(h // 2 + 1) * LANES]
        qcat.append(jnp.concatenate(
            [jnp.where(head_lanes[h % 2], qn, jnp.zeros_like(qn)),
             jnp.where((lane // ROPE_DIM) == h, qr, jnp.zeros_like(qr))], axis=1))
    row_chunk = lax.broadcasted_iota(jnp.int32, (tile, 1), 0) // CHUNK
    col_chunk = lax.broadcasted_iota(jnp.int32, (1, tile), 1) // CHUNK
    diag_visible = col_chunk <= row_chunk

    def one_block(kb, carry, masked):
        rows_k = slice(kb * tile, (kb + 1) * tile)
        kr = kr_ref[0, rows_k, :]
        new = []
        for h in range(ATTN_HEADS_PER_STEP):
            lanes_p = slice((h // 2) * LANES, (h // 2 + 1) * LANES)
            m_i, acc = carry[h]
            kcat = jnp.concatenate([kn_ref[0, rows_k, lanes_p], kr], axis=1)
            s = _dot_nt(qcat[h], kcat)
            if masked:
                s = jnp.where(diag_visible, s, NEG_INF)
            m_new = jnp.maximum(m_i, jnp.max(s, axis=-1, keepdims=True))
            alpha = jnp.exp2(m_i - m_new)
            p = jnp.exp2(s - m_new).astype(BF16)
            vb = v_ref[0, rows_k, lanes_p]
            v_h = jnp.where(head_lanes[h % 2], vb, jnp.ones_like(vb))
            new.append((m_new, alpha * acc + _dot(p, v_h)))
        return tuple(new)

    def query_tile(n_full):
        carry = tuple((jnp.full((tile, 1), NEG_INF, F32), jnp.zeros((tile, LANES), F32))
                      for _ in range(ATTN_HEADS_PER_STEP))
        for kb in range(n_full):
            carry = one_block(kb, carry, False)
        carry = one_block(n_full, carry, True)
        for p in range(pairs):
            out = jnp.zeros((tile, LANES), F32)
            for j in range(2):
                acc = carry[2 * p + j][1]
                row_sum = pltpu.roll(acc, NOPE_DIM, axis=1)
                out = jnp.where(head_lanes[j], acc * (1.0 / row_sum), out)
            o_ref[0, :, p * LANES:(p + 1) * LANES] = out.astype(BF16)

    for c in range(n_tiles):
        pl.when(qi == c)(functools.partial(query_tile, c))


def _mla_attention(qn, qr, kn, kr4, v, tile):
    nseq, length, _ = qn.shape
    kern = functools.partial(_attn_kernel, tile=tile, n_tiles=length // tile)
    width = (ATTN_HEADS_PER_STEP // 2) * LANES
    return pl.pallas_call(
        kern,
        grid=(nseq, MLA_HEADS // ATTN_HEADS_PER_STEP, length // tile),
        in_specs=[
            pl.BlockSpec((1, tile, width), lambda b, g, i: (b, i, g)),
            pl.BlockSpec((1, tile, LANES), lambda b, g, i: (b, i, g)),
            pl.BlockSpec((1, length, width), lambda b, g, i: (b, 0, g)),
            pl.BlockSpec((1, length, LANES), lambda b, g, i: (b, 0, 0)),
            pl.BlockSpec((1, length, width), lambda b, g, i: (b, 0, g)),
        ],
        out_specs=pl.BlockSpec((1, tile, width), lambda b, g, i: (b, i, g)),
        out_shape=jax.ShapeDtypeStruct((nseq, length, MLA_HEADS * V_DIM), BF16),
        compiler_params=_params(3),
        name="mla_attention",
    )(qn, qr, kn, kr4, v)


KNORM_ROWS = 16


def _key_norm_kernel(past_ref, new_ref, wukt_ref, rt_ref, *, n_chunks, chunk):
    def norms(lat):
        keys = lat.shape[0]
        kt = _dot_nt(wukt_ref[...], lat)
        ss = jnp.sum((kt * kt).reshape(MLA_HEADS, NOPE_DIM, keys), axis=1)
        r = lax.rsqrt(ss * (1.0 / NOPE_DIM) + EPS)
        return jnp.concatenate([r, jnp.ones((KNORM_ROWS - MLA_HEADS, keys), F32)], axis=0)

    for c in range(n_chunks):
        rt_ref[0, :, c * chunk:(c + 1) * chunk] = norms(past_ref[0, c * chunk:(c + 1) * chunk, :].astype(BF16))
    rt_ref[0, :, n_chunks * chunk:] = norms(new_ref[0])


def _key_norms(past_latent, new_lat_pad, w_ukt_bf, chunk):
    nseq, past, _ = past_latent.shape
    new_pad = new_lat_pad.shape[1]
    lk_pad = past + new_pad
    kern = functools.partial(_key_norm_kernel, n_chunks=past // chunk, chunk=chunk)
    return pl.pallas_call(
        kern,
        grid=(nseq,),
        in_specs=[pl.BlockSpec((1, past, KV_LORA), lambda b: (b, 0, 0)),
                  pl.BlockSpec((1, new_pad, KV_LORA), lambda b: (b, 0, 0)),
                  _full((MLA_HEADS * NOPE_DIM, KV_LORA))],
        out_specs=pl.BlockSpec((1, KNORM_ROWS, lk_pad), lambda b: (b, 0, 0)),
        out_shape=jax.ShapeDtypeStruct((nseq, KNORM_ROWS, lk_pad), F32),
        compiler_params=_params(1),
        name="mla_key_norms",
    )(past_latent, new_lat_pad, w_ukt_bf)


def _attn_absorbed_kernel(qs_ref, qrs_ref, plat_ref, nlat_ref, pkrt_ref, nkrt_ref, rt_ref, wuv_ref, o_ref, op_ref,
                          *, tq, q_off, lk_valid):
    b = pl.program_id(0)
    nseq = pl.num_programs(0)
    rows = MLA_HEADS * tq
    lat = jnp.concatenate([plat_ref[0].astype(BF16), nlat_ref[0]], axis=0)
    kr_t = jnp.concatenate([pkrt_ref[0].astype(BF16), nkrt_ref[0]], axis=1)
    lk_pad = lat.shape[0]
    rt = rt_ref[0]
    knorm = jnp.concatenate([jnp.broadcast_to(rt[h:h + 1, :], (tq, lk_pad)) for h in range(MLA_HEADS)], axis=0)
    qs = qs_ref[...].reshape(rows, KV_LORA)
    qrs = qrs_ref[...].reshape(rows, ROPE_DIM)
    s = _dot_nt(qs, lat) * knorm + _dot(qrs, kr_t)
    k_pos = lax.broadcasted_iota(jnp.int32, (1, lk_pad), 1)
    if (lk_valid - 1) // CHUNK > q_off // CHUNK:
        q_chunk = (q_off + lax.broadcasted_iota(jnp.int32, (rows, 1), 0) % tq) // CHUNK
        s = jnp.where(k_pos // CHUNK <= q_chunk, s, NEG_INF)
    s = jnp.where(k_pos < lk_valid, s, NEG_INF)
    p = jnp.exp2(s - jnp.max(s, axis=-1, keepdims=True))
    l = jnp.sum(p, axis=-1, keepdims=True)
    op_ref[b] = (_dot(p.astype(BF16), lat) * (1.0 / l)).astype(BF16)

    @pl.when(b == nseq - 1)
    def _():
        n_all = op_ref.shape[0]
        lane_o = lax.broadcasted_iota(jnp.int32, (1, MLA_HEADS * V_DIM), 1)
        out = jnp.zeros((n_all * tq, MLA_HEADS * V_DIM), F32)
        for h in range(MLA_HEADS):
            x = op_ref[:, h * tq:(h + 1) * tq, :].reshape(n_all * tq, KV_LORA)
            out = jnp.where(lane_o // V_DIM == h, _dot(x, wuv_ref[...]), out)
        o_ref[...] = out.astype(BF16)


def _mla_attention_absorbed(qp, qrh, past_latent, new_lat_pad, past_krope_t, new_kr_pad_t, r_t, w_uv_bf,
                            tq, lk_valid):
    nseq, past, _ = past_latent.shape
    new_pad = new_lat_pad.shape[1]
    kern = functools.partial(_attn_absorbed_kernel, tq=tq, q_off=past, lk_valid=lk_valid)
    per_seq = lambda n, width: pl.BlockSpec((1, n, width), lambda b: (b, 0, 0))
    heads_of_seq = lambda width: pl.BlockSpec((MLA_HEADS, tq, width), lambda b: (0, b, 0))
    out = pl.pallas_call(
        kern,
        grid=(nseq,),
        in_specs=[
            heads_of_seq(KV_LORA), heads_of_seq(ROPE_DIM),
            per_seq(past, KV_LORA), per_seq(new_pad, KV_LORA),
            per_seq(ROPE_DIM, past), per_seq(ROPE_DIM, new_pad),
            pl.BlockSpec((1, KNORM_ROWS, past + new_pad), lambda b: (b, 0, 0)),
            _full((KV_LORA, MLA_HEADS * V_DIM)),
        ],
        out_specs=_full((nseq * tq, MLA_HEADS * V_DIM)),
        out_shape=jax.ShapeDtypeStruct((nseq * tq, MLA_HEADS * V_DIM), BF16),
        scratch_shapes=[pltpu.VMEM((nseq, MLA_HEADS * tq, KV_LORA), BF16)],
        compiler_params=_params(1),
        name="mla_attention_absorbed",
    )(qp, qrh, past_latent, new_lat_pad, past_krope_t, new_kr_pad_t, r_t, w_uv_bf)
    return out.reshape(nseq, tq, MLA_HEADS * V_DIM)


def _trunk(x, mem_k_bf, mem_v_bf, ssm_h0_re, ssm_h0_im, conv_ctx, past_latent, past_krope, w, cfg):
    nseq, length, _ = x.shape
    past = 0 if past_latent is None else past_latent.shape[1]
    cos_t, sin_t = _rope_tables(past, length)
    reps = cfg["rope_rows"] // length
    cos_rows = jnp.tile(cos_t, (reps, 1))
    sin_rows = jnp.tile(sin_t, (reps, 1))
    if conv_ctx is None:
        conv_ctx = jnp.zeros((DEPTH, nseq, CONV_W - 1, 2 * D_FF), F32)
    if ssm_h0_re is None:
        ssm_h0_re = jnp.zeros((N_A_LAYERS, nseq, SSM_GROUPS, SSM_STATE), F32)
        ssm_h0_im = ssm_h0_re
    h = x
    ssm_re_out, ssm_im_out, conv_out = [], [], []
    for layer in range(DEPTH):
        z_mix, mem_out = _mixin(h, w, layer, mem_k_bf, mem_v_bf, cfg["tm_mixin"])
        if layer < N_A_LAYERS:
            i = layer
            mix_out, s_re, s_im = _s5_mixer(
                z_mix, ssm_h0_re[i].reshape(nseq, SSM_COLS), ssm_h0_im[i].reshape(nseq, SSM_COLS),
                w["lam_re"][i], w["lam_im"][i], w["s5_in"][i], w["s5_out"][i],
                w["ssm_d"][i][None], w["w_glu"][i], w["b_glu"][i][None], cfg["t_chunk"])
            ssm_re_out.append(s_re.reshape(nseq, SSM_GROUPS, SSM_STATE))
            ssm_im_out.append(s_im.reshape(nseq, SSM_GROUPS, SSM_STATE))
        else:
            if layer == N_A_LAYERS:
                new_latent, new_krope = _shared_kv_down(
                    h, w["kv_norm_g"], w["w_dkv"], w["latent_norm_g"], w["krope_norm_g"],
                    cos_rows, sin_rows, cfg["tm_rows"])
                if cfg["absorbed"]:
                    assert past % LANES == 0
                    lk_valid = past + length
                    new_pad = -(-length // LANES) * LANES
                    pad_rows = lambda t: jnp.pad(t.astype(BF16), ((0, 0), (0, new_pad - length), (0, 0)))
                    new_lat_pad = pad_rows(new_latent)
                    new_kr_pad_t = jnp.swapaxes(pad_rows(new_krope), 1, 2)
                    past_krope_t = jnp.swapaxes(past_krope, 1, 2)
                    r_t = _key_norms(past_latent, new_lat_pad, w["w_uk_t"], cfg["knorm_chunk"])
                else:
                    assert past_latent is None
                    kn, v_all = _kv_up(new_latent.reshape(nseq * length, KV_LORA), w["w_uk"], w["w_uv"],
                                       w["seg_nope"], w["k_nope_g"], cfg["tr_kv"])
                    kn = kn.reshape(nseq, length, MLA_HEADS * NOPE_DIM)
                    v_all = v_all.reshape(nseq, length, MLA_HEADS * V_DIM)
                    kr4 = jnp.tile(new_krope.astype(BF16), (1, 1, LANES // ROPE_DIM))
            j = layer - N_A_LAYERS
            q_out = _q_side(z_mix, w["q_latent_norm_g"][j], w["w_uq"][j], w["q_nope_norm_g"][j],
                            w["q_rope_norm_g"][j], w["seg_nope"], w["seg_rope"], cos_rows, sin_rows,
                            w["k_nope_g"], w["w_uk"], cfg["tm_rows"], cfg["absorbed"])
            if cfg["absorbed"]:
                mix_out = _mla_attention_absorbed(q_out[2], q_out[3], past_latent, new_lat_pad, past_krope_t,
                                                  new_kr_pad_t, r_t, w["w_uv"], length, lk_valid)
            else:
                qn = q_out[0].reshape(nseq, length, MLA_HEADS * NOPE_DIM)
                qr = q_out[1].reshape(nseq, length, MLA_HEADS * ROPE_DIM)
                mix_out = _mla_attention(qn, qr, kn, kr4, v_all, cfg["tq"])
        h, ctx = _mixout_ffn(h, mix_out, mem_out, w, layer, conv_ctx[layer], cfg["ffn_rows"])
        conv_out.append(ctx)
    return h, new_latent, new_krope, jnp.stack(ssm_re_out), jnp.stack(ssm_im_out), jnp.stack(conv_out)


PROMPT_CFG = dict(tm_mixin=1024, t_chunk=64, ffn_rows=512, tm_rows=1024, rope_rows=2048, tr_kv=1024,
                  tq=512, absorbed=False)
SAMPLE_CFG = dict(tm_mixin=32, t_chunk=32, ffn_rows=32, tm_rows=512, rope_rows=512, knorm_chunk=1024,
                  absorbed=True)


def kernel(x_prompt, x_sample, cache_mla_latent, cache_mla_krope, cache_mem_k, cache_mem_v, state_ssm_re, state_ssm_im, state_conv, mem_prompt, norm_mix_g, w_mix_in, w_mix_out, norm_ffn_g, w_ffn_in, ffn_conv_w, ffn_conv_b, w_ffn_out, mem_norm_g, w_mem_kv, mem_q_norm_g, mem_k_norm_g, ssm_a_re, ssm_a_im, ssm_log_dt, ssm_b_re, ssm_b_im, ssm_c_re, ssm_c_im, ssm_d, w_glu, b_glu, kv_norm_g, w_dkv, latent_norm_g, krope_norm_g, w_uk, w_uv, k_nope_norm_g, q_latent_norm_g, w_uq, q_nope_norm_g, q_rope_norm_g):
    bf = lambda t: t.astype(BF16)
    seg_mem = _seg_matrix(MEM_WIDTH, MEM_HEAD_DIM)
    lam_re, lam_im, s5_in, s5_out = _s5_prepare(ssm_a_re, ssm_a_im, ssm_log_dt, ssm_b_re, ssm_b_im,
                                                ssm_c_re, ssm_c_im)
    w = dict(
        norm_mix_g=norm_mix_g.reshape(DEPTH, 1, D_MODEL), w_mix_in=bf(w_mix_in), w_mix_out=bf(w_mix_out),
        norm_ffn_g=norm_ffn_g.reshape(DEPTH, 1, D_MODEL),
        w_ffn_in=bf(w_ffn_in), ffn_conv_w=ffn_conv_w, ffn_conv_b=ffn_conv_b.reshape(DEPTH, 1, 2 * D_FF),
        w_ffn_out=bf(w_ffn_out),
        mem_q_g=jnp.tile(mem_q_norm_g, (1, MEM_HEADS)).reshape(DEPTH, 1, MEM_WIDTH),
        seg_mem=seg_mem,
        seg_nope=_seg_matrix(MLA_HEADS * NOPE_DIM, NOPE_DIM),
        seg_rope=_seg_matrix(MLA_HEADS * ROPE_DIM, ROPE_DIM),
        lam_re=lam_re, lam_im=lam_im, s5_in=s5_in, s5_out=s5_out,
        ssm_d=ssm_d, w_glu=bf(w_glu), b_glu=b_glu,
        kv_norm_g=kv_norm_g, w_dkv=bf(w_dkv), latent_norm_g=latent_norm_g, krope_norm_g=krope_norm_g,
        w_uk=bf(w_uk), w_uk_t=bf(w_uk).T, w_uv=bf(w_uv),
        k_nope_g=jnp.tile(k_nope_norm_g.reshape(1, NOPE_DIM), (1, MLA_HEADS)),
        q_latent_norm_g=q_latent_norm_g, w_uq=bf(w_uq), q_nope_norm_g=q_nope_norm_g,
        q_rope_norm_g=q_rope_norm_g,
    )
    bsz = mem_prompt.shape[0]
    mem_k_p, mem_v_p, mem_k_bf, mem_v_bf = _memory_kv(mem_prompt, mem_norm_g, bf(w_mem_kv), mem_k_norm_g,
                                                      seg_mem)
    y_prompt, lat_p, krope_p, ssm_re_p, ssm_im_p, conv_p = _trunk(
        x_prompt, mem_k_bf, mem_v_bf, None, None, None, None, None, w, PROMPT_CFG)
    dec = cache_mem_k.shape[1]
    y_sample, lat_s, krope_s, ssm_re_s, ssm_im_s, conv_s = _trunk(
        x_sample, bf(cache_mem_k).reshape(DEPTH, dec, N_MEM, MEM_WIDTH),
        bf(cache_mem_v).reshape(DEPTH, dec, N_MEM, MEM_WIDTH),
        state_ssm_re, state_ssm_im, state_conv, cache_mla_latent, cache_mla_krope, w, SAMPLE_CFG)
    shape5 = (DEPTH, bsz, N_MEM, MEM_HEADS, MEM_HEAD_DIM)
    return (y_prompt, y_sample, mem_k_p.reshape(shape5), mem_v_p.reshape(shape5), lat_p, krope_p,
            ssm_re_p, ssm_im_p, conv_p, lat_s, krope_s, ssm_re_s, ssm_im_s, conv_s)
```

```python
import functools
import math

import jax
import jax.numpy as jnp
from jax import lax
from jax.experimental import pallas as pl
from jax.experimental.pallas import tpu as pltpu

F32 = jnp.float32
BF16 = jnp.bfloat16

D_MODEL = 1024
DEPTH = 4
CHUNK = 64
N_A_LAYERS = DEPTH // 2
N_B_LAYERS = DEPTH - N_A_LAYERS
MIX_IN = 768
MEM_HEADS = 4
MEM_HEAD_DIM = 64
MEM_WIDTH = MEM_HEADS * MEM_HEAD_DIM
N_MEM = 256
SSM_GROUP = 16
SSM_GROUPS = MIX_IN // SSM_GROUP
SSM_STATE = 64
SSM_COLS = SSM_GROUPS * SSM_STATE
MLA_HEADS = 12
NOPE_DIM = 64
ROPE_DIM = 32
ROPE_HALF = ROPE_DIM // 2
V_DIM = 64
KV_LORA = 256
ROPE_BASE = 10000.0
MLA_SCALE = (NOPE_DIM + ROPE_DIM) ** -0.5
MEM_SCALE = MEM_HEAD_DIM ** -0.5
D_FF = 2816
CONV_W = 3
EPS = 1e-6
NEG_INF = -1e30

V7X_VMEM_LIMIT_BYTES = 56 * 1024 * 1024
LANES = 128
SUBLANES = 8

S5_BLOCKS = 3
S5_BLOCK_CH = MIX_IN // S5_BLOCKS
S5_BLOCK_ST = SSM_COLS // S5_BLOCKS
FFN_TF = 256
N_FF_TILES = D_FF // FFN_TF
FFN_PAD_SLOTS = 4


def _params(n_axes):
    return pltpu.CompilerParams(
        dimension_semantics=("arbitrary",) * n_axes,
        vmem_limit_bytes=V7X_VMEM_LIMIT_BYTES,
    )


def _dot(a, b):
    return jnp.dot(a, b, preferred_element_type=F32)


def _dot_nt(a, b):
    return lax.dot_general(a, b, (((1,), (1,)), ((), ())), preferred_element_type=F32)


def _rms_rows(x, g):
    ms = jnp.mean(x * x, axis=-1, keepdims=True)
    return x * lax.rsqrt(ms + EPS) * g


def _seg_rms(x, seg_mat, g):
    ms = _dot((x * x).astype(BF16), seg_mat)
    return x * lax.rsqrt(ms + EPS) * g


def _seg_matrix(width, seg):
    idx = jnp.arange(width) // seg
    return jnp.where(idx[:, None] == idx[None, :], 1.0 / seg, 0.0).astype(BF16)


def _full(shape):
    nd = len(shape)
    return pl.BlockSpec(shape, lambda *_: (0,) * nd)


def _rope_table_kernel(pos_ref, inv_ref, cos_ref, sin_ref):
    ang = pos_ref[...] * inv_ref[...]
    cos_ref[...] = jnp.cos(ang)
    sin_ref[...] = jnp.sin(ang)


def _rope_tables(past, length):
    pos = (past + jnp.arange(length, dtype=jnp.int32)).astype(F32)[:, None]
    inv_freq = (1.0 / (ROPE_BASE ** (jnp.arange(0, ROPE_DIM, 2, dtype=F32) / ROPE_DIM)))[None, :]
    cos, sin = pl.pallas_call(
        _rope_table_kernel,
        out_shape=[jax.ShapeDtypeStruct((length, ROPE_HALF), F32)] * 2,
        name="rope_table",
    )(pos, inv_freq)
    cos_t = jnp.concatenate([cos, cos], axis=1)
    sin_t = jnp.concatenate([-sin, sin], axis=1)
    return cos_t, sin_t


def _memkv_kernel(mem_ref, g_ref, w_ref, kg_ref, seg_ref, k_ref, v_ref, kb_ref, vb_ref):
    xn = _rms_rows(mem_ref[0], g_ref[0]).astype(BF16)
    kv = _dot(xn, w_ref[0])
    k = _seg_rms(kv[:, :MEM_WIDTH], seg_ref[...], kg_ref[0])
    v = kv[:, MEM_WIDTH:]
    k_ref[0, 0] = k
    v_ref[0, 0] = v
    kb_ref[0, 0] = k.astype(BF16)
    vb_ref[0, 0] = v.astype(BF16)


def _memory_kv(mem, mem_norm_g, w_mem_kv_bf, mem_k_norm_g, seg_mem):
    bsz = mem.shape[0]
    kg = jnp.tile(mem_k_norm_g, (1, MEM_HEADS)).reshape(DEPTH, 1, MEM_WIDTH)
    out4 = lambda dt: jax.ShapeDtypeStruct((DEPTH, bsz, N_MEM, MEM_WIDTH), dt)
    spec4 = pl.BlockSpec((1, 1, N_MEM, MEM_WIDTH), lambda l, b: (l, b, 0, 0))
    return pl.pallas_call(
        _memkv_kernel,
        grid=(DEPTH, bsz),
        in_specs=[
            pl.BlockSpec((1, N_MEM, D_MODEL), lambda l, b: (b, 0, 0)),
            pl.BlockSpec((1, 1, D_MODEL), lambda l, b: (l, 0, 0)),
            pl.BlockSpec((1, D_MODEL, 2 * MEM_WIDTH), lambda l, b: (l, 0, 0)),
            pl.BlockSpec((1, 1, MEM_WIDTH), lambda l, b: (l, 0, 0)),
            _full((MEM_WIDTH, MEM_WIDTH)),
        ],
        out_specs=[spec4, spec4, spec4, spec4],
        out_shape=[out4(F32), out4(F32), out4(BF16), out4(BF16)],
        compiler_params=_params(2),
        name="memory_kv",
    )(mem, mem_norm_g.reshape(DEPTH, 1, D_MODEL), w_mem_kv_bf, kg, seg_mem)


def _mixin_kernel(h_ref, g_ref, w_ref, qg_ref, seg_ref, k_ref, v_ref, *rest, seqs, rows_per_seq, n_cast):
    cast_in, (zmix_ref, mem_ref), cast_out = rest[:n_cast], rest[n_cast:n_cast + 2], rest[n_cast + 2:]
    for src, dst in zip(cast_in, cast_out):
        dst[...] = src[...].astype(BF16)
    xn = _rms_rows(h_ref[...], g_ref[...]).astype(BF16)
    z = _dot(xn, w_ref[...])
    zmix_ref[...] = z[:, :MIX_IN]
    mq = _seg_rms(z[:, MIX_IN:], seg_ref[...], qg_ref[...]).astype(BF16)
    lane = lax.broadcasted_iota(jnp.int32, (1, MEM_WIDTH), 1)
    for b in range(seqs):
        rows = slice(b * rows_per_seq, (b + 1) * rows_per_seq)
        mq_b = mq[rows]
        k = k_ref[b]
        v = v_ref[b]
        out = jnp.zeros(mq_b.shape, F32)
        for head in range(MEM_HEADS):
            in_head = (lane // MEM_HEAD_DIM) == head
            qh = jnp.where(in_head, mq_b, jnp.zeros_like(mq_b))
            s = _dot_nt(qh, k) * MEM_SCALE
            p = jnp.exp(s - jnp.max(s, axis=-1, keepdims=True))
            p = p * (1.0 / jnp.sum(p, axis=-1, keepdims=True))
            o = _dot(p.astype(BF16), v)
            out = jnp.where(in_head, o, out)
        mem_ref[rows, :] = out.astype(BF16)


def _mixin(h, w, layer, k_bf, v_bf, rows_per_seq_tile, to_cast=()):
    nseq, length, _ = h.shape
    m_rows = nseq * length
    if rows_per_seq_tile >= length:
        seqs, rps, tps = nseq, length, 1
    else:
        seqs, rps, tps = 1, rows_per_seq_tile, length // rows_per_seq_tile
    tm = seqs * rps
    n_steps = m_rows // tm
    kern = functools.partial(_mixin_kernel, seqs=seqs, rows_per_seq=rps, n_cast=len(to_cast))
    row = lambda width: pl.BlockSpec((tm, width), lambda m: (m, 0))
    of_layer = lambda *shape: pl.BlockSpec((None,) + shape, lambda m: (layer,) + (0,) * len(shape))
    kv_spec = pl.BlockSpec((None, seqs, N_MEM, MEM_WIDTH), lambda m: (layer, m // tps, 0, 0))
    slab = lambda t: t.shape[1] // n_steps
    cast_in = [pl.BlockSpec((None, slab(t), t.shape[2]), lambda m: (layer, m, 0)) for t in to_cast]
    cast_out = [pl.BlockSpec((slab(t), t.shape[2]), lambda m: (m, 0)) for t in to_cast]
    outs = pl.pallas_call(
        kern,
        grid=(n_steps,),
        in_specs=[
            row(D_MODEL),
            of_layer(1, D_MODEL),
            of_layer(D_MODEL, MIX_IN + MEM_WIDTH),
            of_layer(1, MEM_WIDTH),
            _full((MEM_WIDTH, MEM_WIDTH)),
            kv_spec, kv_spec,
        ] + cast_in,
        out_specs=[row(MIX_IN), row(MEM_WIDTH)] + cast_out,
        out_shape=[
            jax.ShapeDtypeStruct((m_rows, MIX_IN), F32),
            jax.ShapeDtypeStruct((m_rows, MEM_WIDTH), BF16),
        ] + [jax.ShapeDtypeStruct(t.shape[1:], BF16) for t in to_cast],
        compiler_params=_params(1),
        name="mix_in_mem_attn",
    )(h.reshape(m_rows, D_MODEL), w["norm_mix_g"], w["w_mix_in"], w["mem_q_g"], w["seg_mem"], k_bf, v_bf,
      *to_cast)
    return outs[0].reshape(nseq, length, MIX_IN), outs[1].reshape(nseq, length, MEM_WIDTH), outs[2:]


def _s5_prep_kernel(are_ref, aim_ref, ldt_ref, bre_ref, bim_ref, lre_ref, lim_ref, bbre_ref, bbim_ref):
    a_re, a_im = are_ref[0], aim_ref[0]
    dt = jnp.exp(ldt_ref[0])
    mag = jnp.exp(a_re * dt)
    lam_re = mag * jnp.cos(a_im * dt)
    lam_im = mag * jnp.sin(a_im * dt)
    den = a_re * a_re + a_im * a_im
    x_re = lam_re - 1.0
    f_re = (x_re * a_re + lam_im * a_im) / den
    f_im = (lam_im * a_re - x_re * a_im) / den
    b_re, b_im = bre_ref[0], bim_ref[0]
    lre_ref[0] = lam_re
    lim_ref[0] = lam_im
    bbre_ref[0] = f_re * b_re - f_im * b_im
    bbim_ref[0] = f_re * b_im + f_im * b_re


def _s5_prepare(a_re, a_im, log_dt, b_re, b_im, c_re, c_im):
    n = a_re.shape[0]
    col = lambda t: t.reshape(n, SSM_COLS, 1)
    ldt = jnp.broadcast_to(log_dt[:, :, None], (n, SSM_GROUPS, SSM_STATE))
    col_spec = pl.BlockSpec((1, SSM_COLS, 1), lambda l: (l, 0, 0))
    b_spec = pl.BlockSpec((1, SSM_COLS, SSM_GROUP), lambda l: (l, 0, 0))
    lam_re, lam_im, bb_re, bb_im = pl.pallas_call(
        _s5_prep_kernel,
        grid=(n,),
        in_specs=[col_spec, col_spec, col_spec, b_spec, b_spec],
        out_specs=[col_spec, col_spec, b_spec, b_spec],
        out_shape=[jax.ShapeDtypeStruct((n, SSM_COLS, 1), F32)] * 2
        + [jax.ShapeDtypeStruct((n, SSM_COLS, SSM_GROUP), F32)] * 2,
        compiler_params=_params(1),
        name="s5_discretise",
    )(col(a_re), col(a_im), col(ldt),
      b_re.reshape(n, SSM_COLS, SSM_GROUP), b_im.reshape(n, SSM_COLS, SSM_GROUP))
    gpb = SSM_GROUPS // S5_BLOCKS

    def block_diagonal(t, rows_per_group, cols_per_group):
        rows = gpb * rows_per_group
        same = (jnp.arange(rows) // rows_per_group)[:, None] == jnp.arange(gpb)[None, :]
        wide = jnp.broadcast_to(t[:, :, :, None, :], (n, S5_BLOCKS, rows, gpb, cols_per_group))
        return jnp.where(same[None, None, :, :, None], wide, 0.0).reshape(
            n, S5_BLOCKS, rows, gpb * cols_per_group)

    def in_blocks(bb):
        t = bb.reshape(n, S5_BLOCKS, gpb, SSM_STATE, SSM_GROUP).transpose(0, 1, 2, 4, 3)
        return block_diagonal(t.reshape(n, S5_BLOCKS, S5_BLOCK_CH, SSM_STATE), SSM_GROUP, SSM_STATE)

    def out_blocks(c):
        t = c.reshape(n, S5_BLOCKS, gpb, SSM_GROUP, SSM_STATE).transpose(0, 1, 2, 4, 3)
        return block_diagonal(t.reshape(n, S5_BLOCKS, S5_BLOCK_ST, SSM_GROUP), SSM_STATE, SSM_GROUP)

    w_in = jnp.concatenate([in_blocks(bb_re), in_blocks(bb_im)], axis=-1).astype(BF16)
    w_out = jnp.concatenate([out_blocks(c_re), -out_blocks(c_im)], axis=-2).astype(BF16)
    return lam_re.reshape(n, 1, SSM_COLS), lam_im.reshape(n, 1, SSM_COLS), w_in, w_out


def _s5_kernel(u_ref, h0re_ref, h0im_ref, lre_ref, lim_ref, win_ref, wout_ref, d_ref, wglu_ref, bglu_ref,
               out_ref, sre_out_ref, sim_out_ref, hb_ref, sre_ref, sim_ref, *, bsz, t_chunk, col_block):
    c = pl.program_id(0)
    rows = bsz * t_chunk

    @pl.when(c == 0)
    def _():
        sre_ref[...] = h0re_ref[...]
        sim_ref[...] = h0im_ref[...]

    u = u_ref[...]
    ut = jnp.swapaxes(u, 0, 1).reshape(rows, MIX_IN).astype(BF16)
    ys = []
    for j in range(S5_BLOCKS):
        cols = slice(j * 2 * S5_BLOCK_ST, (j + 1) * 2 * S5_BLOCK_ST)
        hb_ref[:, cols] = _dot(ut[:, j * S5_BLOCK_CH:(j + 1) * S5_BLOCK_CH], win_ref[j])
        for sub in range(S5_BLOCK_ST // col_block):
            nat = j * S5_BLOCK_ST + sub * col_block
            cre = j * 2 * S5_BLOCK_ST + sub * col_block
            cim = cre + S5_BLOCK_ST
            lam_r = jnp.broadcast_to(lre_ref[:, nat:nat + col_block], (bsz, col_block))
            lam_i = jnp.broadcast_to(lim_ref[:, nat:nat + col_block], (bsz, col_block))
            s_r = sre_ref[:, nat:nat + col_block]
            s_i = sim_ref[:, nat:nat + col_block]
            for t in range(t_chunk):
                rows_t = slice(t * bsz, (t + 1) * bsz)
                n_r = lam_r * s_r - lam_i * s_i + hb_ref[rows_t, cre:cre + col_block]
                n_i = lam_r * s_i + lam_i * s_r + hb_ref[rows_t, cim:cim + col_block]
                hb_ref[rows_t, cre:cre + col_block] = n_r
                hb_ref[rows_t, cim:cim + col_block] = n_i
                s_r, s_i = n_r, n_i
            sre_ref[:, nat:nat + col_block] = s_r
            sim_ref[:, nat:nat + col_block] = s_i
        ys.append(_dot(hb_ref[:, cols].astype(BF16), wout_ref[j]))
    yt = jnp.concatenate(ys, axis=1).reshape(t_chunk, bsz, MIX_IN)
    y = jnp.swapaxes(yt, 0, 1) + d_ref[...] * u
    y = jax.nn.gelu(y).reshape(rows, MIX_IN)
    gate = _dot(y.astype(BF16), wglu_ref[...]) + bglu_ref[...]
    out_ref[...] = (y * jax.nn.sigmoid(gate)).reshape(bsz, t_chunk, MIX_IN).astype(BF16)

    @pl.when(c == pl.num_programs(0) - 1)
    def _():
        sre_out_ref[...] = sre_ref[...]
        sim_out_ref[...] = sim_ref[...]


def _s5_mixer(u, h0_re, h0_im, lam_re, lam_im, w_in, w_out, d_skip, w_glu_bf, b_glu, t_chunk):
    bsz, length, _ = u.shape
    rows = bsz * t_chunk
    col_block = (SUBLANES * 512) // bsz
    kern = functools.partial(_s5_kernel, bsz=bsz, t_chunk=t_chunk, col_block=col_block)
    state = jax.ShapeDtypeStruct((bsz, SSM_COLS), F32)
    return pl.pallas_call(
        kern,
        grid=(length // t_chunk,),
        in_specs=[
            pl.BlockSpec((bsz, t_chunk, MIX_IN), lambda c: (0, c, 0)),
            _full((bsz, SSM_COLS)), _full((bsz, SSM_COLS)),
            _full((1, SSM_COLS)), _full((1, SSM_COLS)),
            _full((S5_BLOCKS, S5_BLOCK_CH, 2 * S5_BLOCK_ST)),
            _full((S5_BLOCKS, 2 * S5_BLOCK_ST, S5_BLOCK_CH)),
            _full((1, MIX_IN)), _full((MIX_IN, MIX_IN)), _full((1, MIX_IN)),
        ],
        out_specs=[
            pl.BlockSpec((bsz, t_chunk, MIX_IN), lambda c: (0, c, 0)),
            _full((bsz, SSM_COLS)), _full((bsz, SSM_COLS)),
        ],
        out_shape=[jax.ShapeDtypeStruct((bsz, length, MIX_IN), BF16), state, state],
        scratch_shapes=[
            pltpu.VMEM((rows, 2 * SSM_COLS), F32),
            pltpu.VMEM((bsz, SSM_COLS), F32),
            pltpu.VMEM((bsz, SSM_COLS), F32),
        ],
        compiler_params=_params(1),
        name="s5_mixer",
    )(u, h0_re, h0_im, lam_re, lam_im, w_in, w_out, d_skip, w_glu_bf, b_glu)


def _ffn_kernel(h_ref, mix_ref, mem_ref, wo_ref, g_ref, win_ref, cw_ref, cb_ref, w2_ref, ctx_ref,
                out_ref, new_ref, act_ref, pad_ref, carry_ref, *, seqs, rows_per_seq, tiles_per_seq):
    m = pl.program_id(0)
    tm = seqs * rows_per_seq
    h1 = h_ref[...] + _dot(mix_ref[...], wo_ref[:MIX_IN, :]) + _dot(mem_ref[...], wo_ref[MIX_IN:, :])
    out_ref[...] = h1
    xn = _rms_rows(h1, g_ref[...]).astype(BF16)

    if tiles_per_seq > 1:
        @pl.when(m == 0)
        def _():
            carry_ref[...] = jnp.zeros(carry_ref.shape, F32)

    def causal_conv(col, slot):
        cols = slice(col, col + FFN_TF)
        u = _dot(xn, win_ref[:, cols])
        ctx = ctx_ref[:, :, cols]
        if tiles_per_seq > 1:
            ctx = jnp.where(m % tiles_per_seq == 0, ctx, carry_ref[:, cols][None])
        cw = cw_ref[:, cols]
        if seqs == 1:
            row = lax.broadcasted_iota(jnp.int32, (SUBLANES, 1), 0)
            back1 = pltpu.roll(u, 1, axis=0)
            back2 = pltpu.roll(u, 2, axis=0)
            head1 = jnp.where(row == 0, ctx[0, 1:2], back1[:SUBLANES])
            head2 = jnp.where(row == 0, ctx[0, 0:1], jnp.where(row == 1, ctx[0, 1:2], back2[:SUBLANES]))
            back1 = jnp.concatenate([head1, back1[SUBLANES:]], axis=0)
            back2 = jnp.concatenate([head2, back2[SUBLANES:]], axis=0)
            y = cb_ref[:, cols] + back2 * cw[0:1] + back1 * cw[1:2] + u * cw[2:3]
            last2 = u[tm - 2:][None]
        else:
            pad = pad_ref.at[slot]
            pad[:, 6:8, :] = ctx
            pad[:, 8:, :] = u.reshape(seqs, rows_per_seq, FFN_TF)
            y = (cb_ref[:, cols] + pad[:, 6:6 + rows_per_seq, :] * cw[0:1]
                 + pad[:, 7:7 + rows_per_seq, :] * cw[1:2]
                 + pad[:, 8:8 + rows_per_seq, :] * cw[2:3]).reshape(tm, FFN_TF)
            last2 = pad[:, rows_per_seq + 6:rows_per_seq + 8, :]
        new_ref[:, :, cols] = last2
        if tiles_per_seq > 1:
            carry_ref[:, cols] = last2[0]
        return y

    for f in range(N_FF_TILES):
        ya = causal_conv(f * FFN_TF, (2 * f) % FFN_PAD_SLOTS)
        yg = causal_conv(D_FF + f * FFN_TF, (2 * f + 1) % FFN_PAD_SLOTS)
        act_ref[:, f * FFN_TF:(f + 1) * FFN_TF] = (jax.nn.silu(yg) * ya).astype(BF16)
    out_ref[...] += _dot(act_ref[...], w2_ref[...])


def _mixout_ffn(h, mix, mem, w, layer, ffn_w, ctx, rows_per_seq_tile):
    nseq, length, _ = h.shape
    m_rows = nseq * length
    if rows_per_seq_tile >= length:
        seqs, rps, tps = nseq, length, 1
    else:
        seqs, rps, tps = 1, rows_per_seq_tile, length // rows_per_seq_tile
    tm = seqs * rps
    n_m = m_rows // tm
    kern = functools.partial(_ffn_kernel, seqs=seqs, rows_per_seq=rps, tiles_per_seq=tps)
    row = lambda width: pl.BlockSpec((tm, width), lambda m: (m, 0))
    resident = lambda *shape: pl.BlockSpec(shape, lambda m: (0,) * len(shape), pipeline_mode=pl.Buffered(1))
    of_layer = lambda *shape: pl.BlockSpec((None,) + shape, lambda m: (layer,) + (0,) * len(shape),
                                           pipeline_mode=pl.Buffered(1))
    out, new = pl.pallas_call(
        kern,
        grid=(n_m,),
        in_specs=[
            row(D_MODEL), row(MIX_IN), row(MEM_WIDTH),
            resident(MIX_IN + MEM_WIDTH, D_MODEL),
            of_layer(1, D_MODEL),
            resident(D_MODEL, 2 * D_FF),
            of_layer(CONV_W, 2 * D_FF),
            of_layer(1, 2 * D_FF),
            resident(D_FF, D_MODEL),
            pl.BlockSpec((seqs, CONV_W - 1, 2 * D_FF), lambda m: (m // tps, 0, 0)),
        ],
        out_specs=[row(D_MODEL), pl.BlockSpec((seqs, CONV_W - 1, 2 * D_FF), lambda m: (m, 0, 0))],
        out_shape=[
            jax.ShapeDtypeStruct((m_rows, D_MODEL), F32),
            jax.ShapeDtypeStruct((n_m * seqs, CONV_W - 1, 2 * D_FF), F32),
        ],
        scratch_shapes=[
            pltpu.VMEM((tm, D_FF), BF16),
            pltpu.VMEM((FFN_PAD_SLOTS, seqs, rps + SUBLANES, FFN_TF) if seqs > 1 else (1, 1, SUBLANES, LANES), F32),
            pltpu.VMEM((CONV_W - 1, 2 * D_FF), F32),
        ],
        compiler_params=_params(1),
        name="mix_out_conv_ffn",
    )(h.reshape(m_rows, D_MODEL), mix.reshape(m_rows, MIX_IN), mem.reshape(m_rows, MEM_WIDTH),
      ffn_w[0], w["norm_ffn_g"], ffn_w[1], w["ffn_conv_w"], w["ffn_conv_b"], ffn_w[2], ctx)
    new_ctx = new.reshape(nseq, tps, CONV_W - 1, 2 * D_FF)[:, -1]
    return out.reshape(nseq, length, D_MODEL), new_ctx


def _dkv_kernel(h_ref, g_ref, wl_ref, wr_ref, wrr_ref, lg_ref, kg_ref, kgr_ref, cos_ref, sin_ref,
                lat_ref, kr_ref):
    xn = _rms_rows(h_ref[...], g_ref[...]).astype(BF16)
    lat_ref[...] = _rms_rows(_dot(xn, wl_ref[...]), lg_ref[...])
    kr = _dot(xn, wr_ref[...])
    kr_rot = _dot(xn, wrr_ref[...])
    r = lax.rsqrt(jnp.mean(kr * kr, axis=-1, keepdims=True) + EPS)
    kr_ref[...] = (kr * r * kg_ref[...]) * cos_ref[...] + (kr_rot * r * kgr_ref[...]) * sin_ref[...]


def _swap_halves(t, axis=-1):
    a, b = jnp.split(t, 2, axis=axis)
    return jnp.concatenate([b, a], axis=axis)


def _shared_kv_down(h, kv_norm_g, w_dkv_bf, latent_norm_g, krope_norm_g, cos_rows, sin_rows, tm):
    nseq, length, _ = h.shape
    m_rows = nseq * length
    tab_blocks = cos_rows.shape[0] // tm
    w_l = w_dkv_bf[:, :KV_LORA]
    w_r = w_dkv_bf[:, KV_LORA:]
    kg = krope_norm_g.reshape(1, ROPE_DIM)
    lat, kr = pl.pallas_call(
        _dkv_kernel,
        grid=(m_rows // tm,),
        in_specs=[
            pl.BlockSpec((tm, D_MODEL), lambda m: (m, 0)),
            _full((1, D_MODEL)),
            _full((D_MODEL, KV_LORA)), _full((D_MODEL, ROPE_DIM)), _full((D_MODEL, ROPE_DIM)),
            _full((1, KV_LORA)), _full((1, ROPE_DIM)), _full((1, ROPE_DIM)),
            pl.BlockSpec((tm, ROPE_DIM), lambda m: (m % tab_blocks, 0)),
            pl.BlockSpec((tm, ROPE_DIM), lambda m: (m % tab_blocks, 0)),
        ],
        out_specs=[pl.BlockSpec((tm, KV_LORA), lambda m: (m, 0)),
                   pl.BlockSpec((tm, ROPE_DIM), lambda m: (m, 0))],
        out_shape=[jax.ShapeDtypeStruct((m_rows, KV_LORA), F32),
                   jax.ShapeDtypeStruct((m_rows, ROPE_DIM), F32)],
        compiler_params=_params(1),
        name="shared_kv_down",
    )(h.reshape(m_rows, D_MODEL), kv_norm_g.reshape(1, D_MODEL), w_l, w_r, _swap_halves(w_r),
      latent_norm_g.reshape(1, KV_LORA), kg, _swap_halves(kg), cos_rows, sin_rows)
    return lat.reshape(nseq, length, KV_LORA), kr.reshape(nseq, length, ROPE_DIM)


def _kv_up_kernel(lat_ref, wk_ref, wv_ref, seg_ref, g_ref, k_ref, v_ref):
    lat = lat_ref[...].astype(BF16)
    k_ref[...] = _seg_rms(_dot(lat, wk_ref[...]), seg_ref[...], g_ref[...]).astype(BF16)
    v_ref[...] = _dot(lat, wv_ref[...]).astype(BF16)


def _kv_up(latent_rows, w_uk_bf, w_uv_bf, seg_nope, k_g_tiled, tr):
    rows = latent_rows.shape[0]
    width = MLA_HEADS * NOPE_DIM
    return pl.pallas_call(
        _kv_up_kernel,
        grid=(rows // tr,),
        in_specs=[
            pl.BlockSpec((tr, KV_LORA), lambda r: (r, 0)),
            _full((KV_LORA, width)), _full((KV_LORA, width)),
            _full((width, width)), _full((1, width)),
        ],
        out_specs=[pl.BlockSpec((tr, width), lambda r: (r, 0))] * 2,
        out_shape=[jax.ShapeDtypeStruct((rows, width), BF16)] * 2,
        compiler_params=_params(1),
        name="kv_up",
    )(latent_rows, w_uk_bf, w_uv_bf, seg_nope, k_g_tiled)


Q_PRESCALE = MLA_SCALE * math.log2(math.e)


def _q_kernel(z_ref, g_ref, wn_ref, wa_ref, segn_ref, segr_ref, gn_ref, ga_ref,
              cos_ref, sin_ref, kg_ref, wuk_ref, qn_ref, qr_ref, *maybe_absorbed_refs):
    xn = _rms_rows(z_ref[...], g_ref[...]).astype(BF16)
    qn = _seg_rms(_dot(xn, wn_ref[...]), segn_ref[...], gn_ref[...]) * Q_PRESCALE
    qn_ref[...] = qn.astype(BF16)
    a = _dot(xn, wa_ref[...])
    an = a * lax.rsqrt(_dot((a * a).astype(BF16), segr_ref[...]) + EPS) * ga_ref[...]
    width = an.shape[1]
    lane = lax.broadcasted_iota(jnp.int32, (1, width), 1)
    partner = jnp.where(lane % ROPE_DIM < ROPE_HALF,
                        pltpu.roll(an, width - ROPE_HALF, axis=1), pltpu.roll(an, ROPE_HALF, axis=1))
    rot = an * cos_ref[...] + partner * sin_ref[...]
    qr_ref[...] = (rot * Q_PRESCALE).astype(BF16)
    if maybe_absorbed_refs:
        qp_ref, qrh_ref = maybe_absorbed_refs
        lane = lax.broadcasted_iota(jnp.int32, (1, MLA_HEADS * NOPE_DIM), 1)
        qg = qn * kg_ref[...]
        rot_bf = (rot * Q_PRESCALE).astype(BF16)
        for head in range(MLA_HEADS):
            qh = jnp.where(lane // NOPE_DIM == head, qg, 0.0).astype(BF16)
            qp_ref[head] = _dot_nt(qh, wuk_ref[...]).astype(BF16)
            qrh_ref[head] = rot_bf[:, head * ROPE_DIM:(head + 1) * ROPE_DIM]


def _q_side(z_mix, q_latent_g, w_uq_bf, q_nope_g, q_rope_g, seg_nope, seg_rope, cos_rows, sin_rows,
            k_g_tiled, w_uk_bf, tm, absorbed):
    nseq, length, _ = z_mix.shape
    m_rows = nseq * length
    tab_blocks = cos_rows.shape[0] // tm
    wn_width = MLA_HEADS * NOPE_DIM
    wr_width = MLA_HEADS * ROPE_DIM
    w3 = w_uq_bf.reshape(MIX_IN, MLA_HEADS, NOPE_DIM + ROPE_DIM)
    w_n = w3[:, :, :NOPE_DIM].reshape(MIX_IN, wn_width)
    w_a = w3[:, :, NOPE_DIM:].reshape(MIX_IN, wr_width)
    g_n = jnp.tile(q_nope_g.reshape(1, NOPE_DIM), (1, MLA_HEADS))
    g_a = jnp.tile(q_rope_g.reshape(1, ROPE_DIM), (1, MLA_HEADS))
    cos_q = jnp.tile(cos_rows, (1, MLA_HEADS))
    sin_q = jnp.tile(sin_rows, (1, MLA_HEADS))
    out_specs = [pl.BlockSpec((tm, wd), lambda m: (m, 0)) for wd in (wn_width, wr_width)]
    out_shape = [jax.ShapeDtypeStruct((m_rows, wd), BF16) for wd in (wn_width, wr_width)]
    if absorbed:
        for wd in (KV_LORA, ROPE_DIM):
            out_specs.append(pl.BlockSpec((MLA_HEADS, tm, wd), lambda m: (0, m, 0)))
            out_shape.append(jax.ShapeDtypeStruct((MLA_HEADS, m_rows, wd), BF16))
    outs = pl.pallas_call(
        _q_kernel,
        grid=(m_rows // tm,),
        in_specs=[
            pl.BlockSpec((tm, MIX_IN), lambda m: (m, 0)),
            _full((1, MIX_IN)),
            _full((MIX_IN, wn_width)), _full((MIX_IN, wr_width)),
            _full((wn_width, wn_width)), _full((wr_width, wr_width)),
            _full((1, wn_width)), _full((1, wr_width)),
            pl.BlockSpec((tm, wr_width), lambda m: (m % tab_blocks, 0)),
            pl.BlockSpec((tm, wr_width), lambda m: (m % tab_blocks, 0)),
            _full((1, wn_width)), _full((KV_LORA, wn_width)),
        ],
        out_specs=out_specs,
        out_shape=out_shape,
        compiler_params=_params(1),
        name="mla_query",
    )(z_mix.reshape(m_rows, MIX_IN), q_latent_g.reshape(1, MIX_IN), w_n, w_a, seg_nope, seg_rope,
      g_n, g_a, cos_q, sin_q, k_g_tiled, w_uk_bf)
    return outs


ATTN_HEADS_PER_STEP = 4


def _attn_kernel(qn_ref, qr_ref, kn_ref, kr_ref, v_ref, o_ref, *, tile, n_tiles):
    qi = pl.program_id(2)
    pairs = ATTN_HEADS_PER_STEP // 2
    lane = lax.broadcasted_iota(jnp.int32, (1, LANES), 1)
    head_lanes = [(lane // NOPE_DIM) == j for j in range(2)]
    qr = qr_ref[0]
    qcat = []
    for h in range(ATTN_HEADS_PER_STEP):
        qn = qn_ref[0, :, (h // 2) * LANES:(h // 2 + 1) * LANES]
        qcat.append(jnp.concatenate(
            [jnp.where(head_lanes[h % 2], qn, jnp.zeros_like(qn)),
             jnp.where((lane // ROPE_DIM) == h, qr, jnp.zeros_like(qr))], axis=1))
    row_chunk = lax.broadcasted_iota(jnp.int32, (tile, 1), 0) // CHUNK
    col_chunk = lax.broadcasted_iota(jnp.int32, (1, tile), 1) // CHUNK
    diag_visible = col_chunk <= row_chunk

    def one_block(kb, carry, masked):
        rows_k = slice(kb * tile, (kb + 1) * tile)
        kr = kr_ref[0, rows_k, :]
        new = []
        for h in range(ATTN_HEADS_PER_STEP):
            lanes_p = slice((h // 2) * LANES, (h // 2 + 1) * LANES)
            m_i, acc = carry[h]
            kcat = jnp.concatenate([kn_ref[0, rows_k, lanes_p], kr], axis=1)
            s = _dot_nt(qcat[h], kcat)
            if masked:
                s = jnp.where(diag_visible, s, NEG_INF)
            m_new = jnp.maximum(m_i, jnp.max(s, axis=-1, keepdims=True))
            alpha = jnp.exp2(m_i - m_new)
            p = jnp.exp2(s - m_new).astype(BF16)
            vb = v_ref[0, rows_k, lanes_p]
            v_h = jnp.where(head_lanes[h % 2], vb, jnp.ones_like(vb))
            new.append((m_new, alpha * acc + _dot(p, v_h)))
        return tuple(new)

    def query_tile(n_full):
        carry = tuple((jnp.full((tile, 1), NEG_INF, F32), jnp.zeros((tile, LANES), F32))
                      for _ in range(ATTN_HEADS_PER_STEP))
        for kb in range(n_full):
            carry = one_block(kb, carry, False)
        carry = one_block(n_full, carry, True)
        for p in range(pairs):
            out = jnp.zeros((tile, LANES), F32)
            for j in range(2):
                acc = carry[2 * p + j][1]
                row_sum = pltpu.roll(acc, NOPE_DIM, axis=1)
                out = jnp.where(head_lanes[j], acc * (1.0 / row_sum), out)
            o_ref[0, :, p * LANES:(p + 1) * LANES] = out.astype(BF16)

    for c in range(n_tiles):
        pl.when(qi == c)(functools.partial(query_tile, c))


def _mla_attention(qn, qr, kn, kr4, v, tile):
    nseq, length, _ = qn.shape
    kern = functools.partial(_attn_kernel, tile=tile, n_tiles=length // tile)
    width = (ATTN_HEADS_PER_STEP // 2) * LANES
    return pl.pallas_call(
        kern,
        grid=(nseq, MLA_HEADS // ATTN_HEADS_PER_STEP, length // tile),
        in_specs=[
            pl.BlockSpec((1, tile, width), lambda b, g, i: (b, i, g)),
            pl.BlockSpec((1, tile, LANES), lambda b, g, i: (b, i, g)),
            pl.BlockSpec((1, length, width), lambda b, g, i: (b, 0, g)),
            pl.BlockSpec((1, length, LANES), lambda b, g, i: (b, 0, 0)),
            pl.BlockSpec((1, length, width), lambda b, g, i: (b, 0, g)),
        ],
        out_specs=pl.BlockSpec((1, tile, width), lambda b, g, i: (b, i, g)),
        out_shape=jax.ShapeDtypeStruct((nseq, length, MLA_HEADS * V_DIM), BF16),
        compiler_params=_params(3),
        name="mla_attention",
    )(qn, qr, kn, kr4, v)


KNORM_ROWS = 16


def _key_norm_kernel(past_ref, new_ref, wukt_ref, rt_ref, *, n_chunks, chunk):
    def norms(lat):
        keys = lat.shape[0]
        kt = _dot_nt(wukt_ref[...], lat)
        ss = jnp.sum((kt * kt).reshape(MLA_HEADS, NOPE_DIM, keys), axis=1)
        r = lax.rsqrt(ss * (1.0 / NOPE_DIM) + EPS)
        return jnp.concatenate([r, jnp.ones((KNORM_ROWS - MLA_HEADS, keys), F32)], axis=0)

    for c in range(n_chunks):
        rt_ref[0, :, c * chunk:(c + 1) * chunk] = norms(past_ref[0, c * chunk:(c + 1) * chunk, :].astype(BF16))
    rt_ref[0, :, n_chunks * chunk:] = norms(new_ref[0])


def _key_norms(past_latent, new_lat_pad, w_ukt_bf, chunk):
    nseq, past, _ = past_latent.shape
    new_pad = new_lat_pad.shape[1]
    lk_pad = past + new_pad
    kern = functools.partial(_key_norm_kernel, n_chunks=past // chunk, chunk=chunk)
    return pl.pallas_call(
        kern,
        grid=(nseq,),
        in_specs=[pl.BlockSpec((1, past, KV_LORA), lambda b: (b, 0, 0)),
                  pl.BlockSpec((1, new_pad, KV_LORA), lambda b: (b, 0, 0)),
                  _full((MLA_HEADS * NOPE_DIM, KV_LORA))],
        out_specs=pl.BlockSpec((1, KNORM_ROWS, lk_pad), lambda b: (b, 0, 0)),
        out_shape=jax.ShapeDtypeStruct((nseq, KNORM_ROWS, lk_pad), F32),
        compiler_params=_params(1),
        name="mla_key_norms",
    )(past_latent, new_lat_pad, w_ukt_bf)


def _attn_absorbed_kernel(qs_ref, qrs_ref, plat_ref, nlat_ref, pkrt_ref, nkrt_ref, rt_ref, wuv_ref, o_ref, op_ref,
                          *, tq, q_off, lk_valid):
    b = pl.program_id(0)
    nseq = pl.num_programs(0)
    rows = MLA_HEADS * tq
    lat = jnp.concatenate([plat_ref[0].astype(BF16), nlat_ref[0]], axis=0)
    kr_t = jnp.concatenate([pkrt_ref[0].astype(BF16), nkrt_ref[0]], axis=1)
    lk_pad = lat.shape[0]
    rt = rt_ref[0]
    knorm = jnp.concatenate([jnp.broadcast_to(rt[h:h + 1, :], (tq, lk_pad)) for h in range(MLA_HEADS)], axis=0)
    qs = qs_ref[...].reshape(rows, KV_LORA)
    qrs = qrs_ref[...].reshape(rows, ROPE_DIM)
    s = _dot_nt(qs, lat) * knorm + _dot(qrs, kr_t)
    k_pos = lax.broadcasted_iota(jnp.int32, (1, lk_pad), 1)
    if (lk_valid - 1) // CHUNK > q_off // CHUNK:
        q_chunk = (q_off + lax.broadcasted_iota(jnp.int32, (rows, 1), 0) % tq) // CHUNK
        s = jnp.where(k_pos // CHUNK <= q_chunk, s, NEG_INF)
    s = jnp.where(k_pos < lk_valid, s, NEG_INF)
    p = jnp.exp2(s - jnp.max(s, axis=-1, keepdims=True))
    l = jnp.sum(p, axis=-1, keepdims=True)
    op_ref[b] = (_dot(p.astype(BF16), lat) * (1.0 / l)).astype(BF16)

    @pl.when(b == nseq - 1)
    def _():
        n_all = op_ref.shape[0]
        lane_o = lax.broadcasted_iota(jnp.int32, (1, MLA_HEADS * V_DIM), 1)
        out = jnp.zeros((n_all * tq, MLA_HEADS * V_DIM), F32)
        for h in range(MLA_HEADS):
            x = op_ref[:, h * tq:(h + 1) * tq, :].reshape(n_all * tq, KV_LORA)
            out = jnp.where(lane_o // V_DIM == h, _dot(x, wuv_ref[...]), out)
        o_ref[...] = out.astype(BF16)


def _mla_attention_absorbed(qp, qrh, past_latent, new_lat_pad, past_krope_t, new_kr_pad_t, r_t, w_uv_bf,
                            tq, lk_valid):
    nseq, past, _ = past_latent.shape
    new_pad = new_lat_pad.shape[1]
    kern = functools.partial(_attn_absorbed_kernel, tq=tq, q_off=past, lk_valid=lk_valid)
    per_seq = lambda n, width: pl.BlockSpec((1, n, width), lambda b: (b, 0, 0))
    heads_of_seq = lambda width: pl.BlockSpec((MLA_HEADS, tq, width), lambda b: (0, b, 0))
    out = pl.pallas_call(
        kern,
        grid=(nseq,),
        in_specs=[
            heads_of_seq(KV_LORA), heads_of_seq(ROPE_DIM),
            per_seq(past, KV_LORA), per_seq(new_pad, KV_LORA),
            per_seq(ROPE_DIM, past), per_seq(ROPE_DIM, new_pad),
            pl.BlockSpec((1, KNORM_ROWS, past + new_pad), lambda b: (b, 0, 0)),
            _full((KV_LORA, MLA_HEADS * V_DIM)),
        ],
        out_specs=_full((nseq * tq, MLA_HEADS * V_DIM)),
        out_shape=jax.ShapeDtypeStruct((nseq * tq, MLA_HEADS * V_DIM), BF16),
        scratch_shapes=[pltpu.VMEM((nseq, MLA_HEADS * tq, KV_LORA), BF16)],
        compiler_params=_params(1),
        name="mla_attention_absorbed",
    )(qp, qrh, past_latent, new_lat_pad, past_krope_t, new_kr_pad_t, r_t, w_uv_bf)
    return out.reshape(nseq, tq, MLA_HEADS * V_DIM)


def _trunk(x, mem_k_bf, mem_v_bf, ssm_h0_re, ssm_h0_im, conv_ctx, past_latent, past_krope, w, cfg, ffn_bf=None):
    nseq, length, _ = x.shape
    past = 0 if past_latent is None else past_latent.shape[1]
    cos_t, sin_t = _rope_tables(past, length)
    reps = cfg["rope_rows"] // length
    cos_rows = jnp.tile(cos_t, (reps, 1))
    sin_rows = jnp.tile(sin_t, (reps, 1))
    if conv_ctx is None:
        conv_ctx = jnp.zeros((DEPTH, nseq, CONV_W - 1, 2 * D_FF), F32)
    if ssm_h0_re is None:
        ssm_h0_re = jnp.zeros((N_A_LAYERS, nseq, SSM_GROUPS, SSM_STATE), F32)
        ssm_h0_im = ssm_h0_re
    h = x
    ssm_re_out, ssm_im_out, conv_out = [], [], []
    make_ffn_bf = ffn_bf is None
    if make_ffn_bf:
        ffn_bf = []
    for layer in range(DEPTH):
        to_cast = (w["w_mix_out_f32"], w["w_ffn_in_f32"], w["w_ffn_out_f32"]) if make_ffn_bf else ()
        z_mix, mem_out, cast = _mixin(h, w, layer, mem_k_bf, mem_v_bf, cfg["tm_mixin"], to_cast)
        if make_ffn_bf:
            ffn_bf.append(cast)
        if layer < N_A_LAYERS:
            i = layer
            mix_out, s_re, s_im = _s5_mixer(
                z_mix, ssm_h0_re[i].reshape(nseq, SSM_COLS), ssm_h0_im[i].reshape(nseq, SSM_COLS),
                w["lam_re"][i], w["lam_im"][i], w["s5_in"][i], w["s5_out"][i],
                w["ssm_d"][i][None], w["w_glu"][i], w["b_glu"][i][None], cfg["t_chunk"])
            ssm_re_out.append(s_re.reshape(nseq, SSM_GROUPS, SSM_STATE))
            ssm_im_out.append(s_im.reshape(nseq, SSM_GROUPS, SSM_STATE))
        else:
            if layer == N_A_LAYERS:
                new_latent, new_krope = _shared_kv_down(
                    h, w["kv_norm_g"], w["w_dkv"], w["latent_norm_g"], w["krope_norm_g"],
                    cos_rows, sin_rows, cfg["tm_rows"])
                if cfg["absorbed"]:
                    assert past % LANES == 0
                    lk_valid = past + length
                    new_pad = -(-length // LANES) * LANES
                    pad_rows = lambda t: jnp.pad(t.astype(BF16), ((0, 0), (0, new_pad - length), (0, 0)))
                    new_lat_pad = pad_rows(new_latent)
                    new_kr_pad_t = jnp.swapaxes(pad_rows(new_krope), 1, 2)
                    past_krope_t = jnp.swapaxes(past_krope, 1, 2)
                    r_t = _key_norms(past_latent, new_lat_pad, w["w_uk_t"], cfg["knorm_chunk"])
                else:
                    assert past_latent is None
                    kn, v_all = _kv_up(new_latent.reshape(nseq * length, KV_LORA), w["w_uk"], w["w_uv"],
                                       w["seg_nope"], w["k_nope_g"], cfg["tr_kv"])
                    kn = kn.reshape(nseq, length, MLA_HEADS * NOPE_DIM)
                    v_all = v_all.reshape(nseq, length, MLA_HEADS * V_DIM)
                    kr4 = jnp.tile(new_krope.astype(BF16), (1, 1, LANES // ROPE_DIM))
            j = layer - N_A_LAYERS
            q_out = _q_side(z_mix, w["q_latent_norm_g"][j], w["w_uq"][j], w["q_nope_norm_g"][j],
                            w["q_rope_norm_g"][j], w["seg_nope"], w["seg_rope"], cos_rows, sin_rows,
                            w["k_nope_g"], w["w_uk"], cfg["tm_rows"], cfg["absorbed"])
            if cfg["absorbed"]:
                mix_out = _mla_attention_absorbed(q_out[2], q_out[3], past_latent, new_lat_pad, past_krope_t,
                                                  new_kr_pad_t, r_t, w["w_uv"], length, lk_valid)
            else:
                qn = q_out[0].reshape(nseq, length, MLA_HEADS * NOPE_DIM)
                qr = q_out[1].reshape(nseq, length, MLA_HEADS * ROPE_DIM)
                mix_out = _mla_attention(qn, qr, kn, kr4, v_all, cfg["tq"])
        h, ctx = _mixout_ffn(h, mix_out, mem_out, w, layer, ffn_bf[layer], conv_ctx[layer], cfg["ffn_rows"])
        conv_out.append(ctx)
    return (h, new_latent, new_krope, jnp.stack(ssm_re_out), jnp.stack(ssm_im_out), jnp.stack(conv_out)), ffn_bf


PROMPT_CFG = dict(tm_mixin=1024, t_chunk=64, ffn_rows=512, tm_rows=1024, rope_rows=2048, tr_kv=1024,
                  tq=512, absorbed=False)
SAMPLE_CFG = dict(tm_mixin=32, t_chunk=32, ffn_rows=32, tm_rows=512, rope_rows=512, knorm_chunk=1024,
                  absorbed=True)


def kernel(x_prompt, x_sample, cache_mla_latent, cache_mla_krope, cache_mem_k, cache_mem_v, state_ssm_re, state_ssm_im, state_conv, mem_prompt, norm_mix_g, w_mix_in, w_mix_out, norm_ffn_g, w_ffn_in, ffn_conv_w, ffn_conv_b, w_ffn_out, mem_norm_g, w_mem_kv, mem_q_norm_g, mem_k_norm_g, ssm_a_re, ssm_a_im, ssm_log_dt, ssm_b_re, ssm_b_im, ssm_c_re, ssm_c_im, ssm_d, w_glu, b_glu, kv_norm_g, w_dkv, latent_norm_g, krope_norm_g, w_uk, w_uv, k_nope_norm_g, q_latent_norm_g, w_uq, q_nope_norm_g, q_rope_norm_g):
    bf = lambda t: t.astype(BF16)
    seg_mem = _seg_matrix(MEM_WIDTH, MEM_HEAD_DIM)
    lam_re, lam_im, s5_in, s5_out = _s5_prepare(ssm_a_re, ssm_a_im, ssm_log_dt, ssm_b_re, ssm_b_im,
                                                ssm_c_re, ssm_c_im)
    w = dict(
        norm_mix_g=norm_mix_g.reshape(DEPTH, 1, D_MODEL), w_mix_in=bf(w_mix_in),
        norm_ffn_g=norm_ffn_g.reshape(DEPTH, 1, D_MODEL),
        ffn_conv_w=ffn_conv_w, ffn_conv_b=ffn_conv_b.reshape(DEPTH, 1, 2 * D_FF),
        w_mix_out_f32=w_mix_out, w_ffn_in_f32=w_ffn_in, w_ffn_out_f32=w_ffn_out,
        mem_q_g=jnp.tile(mem_q_norm_g, (1, MEM_HEADS)).reshape(DEPTH, 1, MEM_WIDTH),
        seg_mem=seg_mem,
        seg_nope=_seg_matrix(MLA_HEADS * NOPE_DIM, NOPE_DIM),
        seg_rope=_seg_matrix(MLA_HEADS * ROPE_DIM, ROPE_DIM),
        lam_re=lam_re, lam_im=lam_im, s5_in=s5_in, s5_out=s5_out,
        ssm_d=ssm_d, w_glu=bf(w_glu), b_glu=b_glu,
        kv_norm_g=kv_norm_g, w_dkv=bf(w_dkv), latent_norm_g=latent_norm_g, krope_norm_g=krope_norm_g,
        w_uk=bf(w_uk), w_uk_t=bf(w_uk).T, w_uv=bf(w_uv),
        k_nope_g=jnp.tile(k_nope_norm_g.reshape(1, NOPE_DIM), (1, MLA_HEADS)),
        q_latent_norm_g=q_latent_norm_g, w_uq=bf(w_uq), q_nope_norm_g=q_nope_norm_g,
        q_rope_norm_g=q_rope_norm_g,
    )
    bsz = mem_prompt.shape[0]
    mem_k_p, mem_v_p, mem_k_bf, mem_v_bf = _memory_kv(mem_prompt, mem_norm_g, bf(w_mem_kv), mem_k_norm_g,
                                                      seg_mem)
    (y_prompt, lat_p, krope_p, ssm_re_p, ssm_im_p, conv_p), ffn_bf = _trunk(
        x_prompt, mem_k_bf, mem_v_bf, None, None, None, None, None, w, PROMPT_CFG)
    dec = cache_mem_k.shape[1]
    (y_sample, lat_s, krope_s, ssm_re_s, ssm_im_s, conv_s), _ = _trunk(
        x_sample, bf(cache_mem_k).reshape(DEPTH, dec, N_MEM, MEM_WIDTH),
        bf(cache_mem_v).reshape(DEPTH, dec, N_MEM, MEM_WIDTH),
        state_ssm_re, state_ssm_im, state_conv, cache_mla_latent, cache_mla_krope, w, SAMPLE_CFG, ffn_bf)
    shape5 = (DEPTH, bsz, N_MEM, MEM_HEADS, MEM_HEAD_DIM)
    return (y_prompt, y_sample, mem_k_p.reshape(shape5), mem_v_p.reshape(shape5), lat_p, krope_p,
            ssm_re_p, ssm_im_p, conv_p, lat_s, krope_s, ssm_re_s, ssm_im_s, conv_s)
```

```python
import functools
import math

import jax
import jax.numpy as jnp
from jax import lax
from jax.experimental import pallas as pl
from jax.experimental.pallas import tpu as pltpu

F32 = jnp.float32
BF16 = jnp.bfloat16

D_MODEL = 1024
DEPTH = 4
CHUNK = 64
N_A_LAYERS = DEPTH // 2
N_B_LAYERS = DEPTH - N_A_LAYERS
MIX_IN = 768
MEM_HEADS = 4
MEM_HEAD_DIM = 64
MEM_WIDTH = MEM_HEADS * MEM_HEAD_DIM
N_MEM = 256
SSM_GROUP = 16
SSM_GROUPS = MIX_IN // SSM_GROUP
SSM_STATE = 64
SSM_COLS = SSM_GROUPS * SSM_STATE
MLA_HEADS = 12
NOPE_DIM = 64
ROPE_DIM = 32
ROPE_HALF = ROPE_DIM // 2
V_DIM = 64
KV_LORA = 256
ROPE_BASE = 10000.0
MLA_SCALE = (NOPE_DIM + ROPE_DIM) ** -0.5
MEM_SCALE = MEM_HEAD_DIM ** -0.5
D_FF = 2816
CONV_W = 3
EPS = 1e-6
NEG_INF = -1e30

V7X_VMEM_LIMIT_BYTES = 56 * 1024 * 1024
LANES = 128
SUBLANES = 8

S5_BLOCKS = 3
S5_BLOCK_CH = MIX_IN // S5_BLOCKS
S5_BLOCK_ST = SSM_COLS // S5_BLOCKS
FFN_TF = 256
N_FF_TILES = D_FF // FFN_TF
FFN_PAD_SLOTS = 4


def _params(n_axes):
    return pltpu.CompilerParams(
        dimension_semantics=("arbitrary",) * n_axes,
        vmem_limit_bytes=V7X_VMEM_LIMIT_BYTES,
    )


def _dot(a, b):
    return jnp.dot(a, b, preferred_element_type=F32)


def _dot_nt(a, b):
    return lax.dot_general(a, b, (((1,), (1,)), ((), ())), preferred_element_type=F32)


def _rms_rows(x, g):
    ms = jnp.mean(x * x, axis=-1, keepdims=True)
    return x * lax.rsqrt(ms + EPS) * g


def _seg_rms(x, seg_mat, g):
    ms = _dot((x * x).astype(BF16), seg_mat)
    return x * lax.rsqrt(ms + EPS) * g


def _seg_matrix(width, seg):
    idx = jnp.arange(width) // seg
    return jnp.where(idx[:, None] == idx[None, :], 1.0 / seg, 0.0).astype(BF16)


def _full(shape):
    nd = len(shape)
    return pl.BlockSpec(shape, lambda *_: (0,) * nd)


def _rope_table_kernel(pos_ref, inv_ref, cos_ref, sin_ref):
    ang = pos_ref[...] * inv_ref[...]
    cos_ref[...] = jnp.cos(ang)
    sin_ref[...] = jnp.sin(ang)


def _rope_tables(past, length):
    pos = (past + jnp.arange(length, dtype=jnp.int32)).astype(F32)[:, None]
    inv_freq = (1.0 / (ROPE_BASE ** (jnp.arange(0, ROPE_DIM, 2, dtype=F32) / ROPE_DIM)))[None, :]
    cos, sin = pl.pallas_call(
        _rope_table_kernel,
        out_shape=[jax.ShapeDtypeStruct((length, ROPE_HALF), F32)] * 2,
        name="rope_table",
    )(pos, inv_freq)
    cos_t = jnp.concatenate([cos, cos], axis=1)
    sin_t = jnp.concatenate([-sin, sin], axis=1)
    return cos_t, sin_t


def _memkv_kernel(mem_ref, g_ref, w_ref, kg_ref, seg_ref, k_ref, v_ref, kb_ref, vb_ref):
    xn = _rms_rows(mem_ref[0], g_ref[0]).astype(BF16)
    kv = _dot(xn, w_ref[0].astype(BF16))
    k = _seg_rms(kv[:, :MEM_WIDTH], seg_ref[...], kg_ref[0])
    v = kv[:, MEM_WIDTH:]
    k_ref[0, 0] = k
    v_ref[0, 0] = v
    kb_ref[0, 0] = k.astype(BF16)
    vb_ref[0, 0] = v.astype(BF16)


def _memory_kv(mem, mem_norm_g, w_mem_kv, mem_k_norm_g, seg_mem):
    bsz = mem.shape[0]
    kg = jnp.tile(mem_k_norm_g, (1, MEM_HEADS)).reshape(DEPTH, 1, MEM_WIDTH)
    out4 = lambda dt: jax.ShapeDtypeStruct((DEPTH, bsz, N_MEM, MEM_WIDTH), dt)
    spec4 = pl.BlockSpec((1, 1, N_MEM, MEM_WIDTH), lambda l, b: (l, b, 0, 0))
    return pl.pallas_call(
        _memkv_kernel,
        grid=(DEPTH, bsz),
        in_specs=[
            pl.BlockSpec((1, N_MEM, D_MODEL), lambda l, b: (b, 0, 0)),
            pl.BlockSpec((1, 1, D_MODEL), lambda l, b: (l, 0, 0)),
            pl.BlockSpec((1, D_MODEL, 2 * MEM_WIDTH), lambda l, b: (l, 0, 0)),
            pl.BlockSpec((1, 1, MEM_WIDTH), lambda l, b: (l, 0, 0)),
            _full((MEM_WIDTH, MEM_WIDTH)),
        ],
        out_specs=[spec4, spec4, spec4, spec4],
        out_shape=[out4(F32), out4(F32), out4(BF16), out4(BF16)],
        compiler_params=_params(2),
        name="memory_kv",
    )(mem, mem_norm_g.reshape(DEPTH, 1, D_MODEL), w_mem_kv, kg, seg_mem)


def _mixin_kernel(h_ref, g_ref, w_ref, qg_ref, seg_ref, k_ref, v_ref, *rest, seqs, rows_per_seq, n_cast):
    cast_in, (zmix_ref, mem_ref), cast_out = rest[:n_cast], rest[n_cast:n_cast + 2], rest[n_cast + 2:]
    for src, dst in zip(cast_in, cast_out):
        dst[...] = src[...].astype(BF16)
    xn = _rms_rows(h_ref[...], g_ref[...]).astype(BF16)
    z = _dot(xn, w_ref[...].astype(BF16))
    zmix_ref[...] = z[:, :MIX_IN]
    mq = _seg_rms(z[:, MIX_IN:], seg_ref[...], qg_ref[...]).astype(BF16)
    lane = lax.broadcasted_iota(jnp.int32, (1, MEM_WIDTH), 1)
    for b in range(seqs):
        rows = slice(b * rows_per_seq, (b + 1) * rows_per_seq)
        mq_b = mq[rows]
        k = k_ref[b]
        v = v_ref[b]
        out = jnp.zeros(mq_b.shape, F32)
        for head in range(MEM_HEADS):
            in_head = (lane // MEM_HEAD_DIM) == head
            qh = jnp.where(in_head, mq_b, jnp.zeros_like(mq_b))
            s = _dot_nt(qh, k) * MEM_SCALE
            p = jnp.exp(s - jnp.max(s, axis=-1, keepdims=True))
            p = p * (1.0 / jnp.sum(p, axis=-1, keepdims=True))
            o = _dot(p.astype(BF16), v)
            out = jnp.where(in_head, o, out)
        mem_ref[rows, :] = out.astype(BF16)


def _mixin(h, w, layer, k_bf, v_bf, rows_per_seq_tile, to_cast=()):
    nseq, length, _ = h.shape
    m_rows = nseq * length
    if rows_per_seq_tile >= length:
        seqs, rps, tps = nseq, length, 1
    else:
        seqs, rps, tps = 1, rows_per_seq_tile, length // rows_per_seq_tile
    tm = seqs * rps
    n_steps = m_rows // tm
    kern = functools.partial(_mixin_kernel, seqs=seqs, rows_per_seq=rps, n_cast=len(to_cast))
    row = lambda width: pl.BlockSpec((tm, width), lambda m: (m, 0))
    of_layer = lambda *shape: pl.BlockSpec((None,) + shape, lambda m: (layer,) + (0,) * len(shape))
    kv_spec = pl.BlockSpec((None, seqs, N_MEM, MEM_WIDTH), lambda m: (layer, m // tps, 0, 0))
    slab = lambda t: t.shape[1] // n_steps
    cast_in = [pl.BlockSpec((None, slab(t), t.shape[2]), lambda m: (layer, m, 0)) for t in to_cast]
    cast_out = [pl.BlockSpec((slab(t), t.shape[2]), lambda m: (m, 0)) for t in to_cast]
    outs = pl.pallas_call(
        kern,
        grid=(n_steps,),
        in_specs=[
            row(D_MODEL),
            of_layer(1, D_MODEL),
            of_layer(D_MODEL, MIX_IN + MEM_WIDTH),
            of_layer(1, MEM_WIDTH),
            _full((MEM_WIDTH, MEM_WIDTH)),
            kv_spec, kv_spec,
        ] + cast_in,
        out_specs=[row(MIX_IN), row(MEM_WIDTH)] + cast_out,
        out_shape=[
            jax.ShapeDtypeStruct((m_rows, MIX_IN), F32),
            jax.ShapeDtypeStruct((m_rows, MEM_WIDTH), BF16),
        ] + [jax.ShapeDtypeStruct(t.shape[1:], BF16) for t in to_cast],
        compiler_params=_params(1),
        name="mix_in_mem_attn",
    )(h.reshape(m_rows, D_MODEL), w["norm_mix_g"], w["w_mix_in"], w["mem_q_g"], w["seg_mem"], k_bf, v_bf,
      *to_cast)
    return outs[0].reshape(nseq, length, MIX_IN), outs[1].reshape(nseq, length, MEM_WIDTH), outs[2:]


def _s5_discretise_kernel(are_ref, aim_ref, ldt_ref, lre_ref, lim_ref, fre_ref, fim_ref):
    a_re, a_im = are_ref[...], aim_ref[...]
    dt = jnp.exp(ldt_ref[...])
    mag = jnp.exp(a_re * dt)
    lam_re = mag * jnp.cos(a_im * dt)
    lam_im = mag * jnp.sin(a_im * dt)
    den = a_re * a_re + a_im * a_im
    x_re = lam_re - 1.0
    lre_ref[...] = lam_re
    lim_ref[...] = lam_im
    fre_ref[...] = (x_re * a_re + lam_im * a_im) / den
    fim_ref[...] = (lam_im * a_re - x_re * a_im) / den


def _s5_input_scale_kernel(fre_ref, fim_ref, bre_ref, bim_ref, bbre_ref, bbim_ref):
    f_re, f_im = fre_ref[0], fim_ref[0]
    b_re, b_im = bre_ref[0], bim_ref[0]
    bbre_ref[0] = f_re * b_re - f_im * b_im
    bbim_ref[0] = f_re * b_im + f_im * b_re


def _s5_prepare(a_re, a_im, log_dt, b_re, b_im, c_re, c_im):
    n = a_re.shape[0]
    dense = lambda t: t.reshape(n, SSM_COLS // LANES, LANES)
    ldt = jnp.broadcast_to(log_dt[:, :, None], (n, SSM_GROUPS, SSM_STATE))
    lam_re, lam_im, f_re, f_im = pl.pallas_call(
        _s5_discretise_kernel,
        out_shape=[jax.ShapeDtypeStruct((n, SSM_COLS // LANES, LANES), F32)] * 4,
        name="s5_discretise",
    )(dense(a_re), dense(a_im), dense(ldt))
    col = lambda t: t.reshape(n, SSM_COLS, 1)
    col_spec = pl.BlockSpec((1, SSM_COLS, 1), lambda l: (l, 0, 0))
    b_spec = pl.BlockSpec((1, SSM_COLS, SSM_GROUP), lambda l: (l, 0, 0))
    bb_re, bb_im = pl.pallas_call(
        _s5_input_scale_kernel,
        grid=(n,),
        in_specs=[col_spec, col_spec, b_spec, b_spec],
        out_specs=[b_spec, b_spec],
        out_shape=[jax.ShapeDtypeStruct((n, SSM_COLS, SSM_GROUP), F32)] * 2,
        compiler_params=_params(1),
        name="s5_input_scale",
    )(col(f_re), col(f_im), b_re.reshape(n, SSM_COLS, SSM_GROUP), b_im.reshape(n, SSM_COLS, SSM_GROUP))
    gpb = SSM_GROUPS // S5_BLOCKS

    def block_diagonal(t, rows_per_group, cols_per_group):
        rows = gpb * rows_per_group
        same = (jnp.arange(rows) // rows_per_group)[:, None] == jnp.arange(gpb)[None, :]
        wide = jnp.broadcast_to(t[:, :, :, None, :], (n, S5_BLOCKS, rows, gpb, cols_per_group))
        return jnp.where(same[None, None, :, :, None], wide, 0.0).reshape(
            n, S5_BLOCKS, rows, gpb * cols_per_group)

    def in_blocks(bb):
        t = bb.reshape(n, S5_BLOCKS, gpb, SSM_STATE, SSM_GROUP).transpose(0, 1, 2, 4, 3)
        return block_diagonal(t.reshape(n, S5_BLOCKS, S5_BLOCK_CH, SSM_STATE), SSM_GROUP, SSM_STATE)

    def out_blocks(c):
        t = c.reshape(n, S5_BLOCKS, gpb, SSM_GROUP, SSM_STATE).transpose(0, 1, 2, 4, 3)
        return block_diagonal(t.reshape(n, S5_BLOCKS, S5_BLOCK_ST, SSM_GROUP), SSM_STATE, SSM_GROUP)

    w_in = jnp.concatenate([in_blocks(bb_re), in_blocks(bb_im)], axis=-1).astype(BF16)
    w_out = jnp.concatenate([out_blocks(c_re), -out_blocks(c_im)], axis=-2).astype(BF16)
    return lam_re.reshape(n, 1, SSM_COLS), lam_im.reshape(n, 1, SSM_COLS), w_in, w_out


def _s5_kernel(u_ref, h0re_ref, h0im_ref, lre_ref, lim_ref, win_ref, wout_ref, d_ref, wglu_ref, bglu_ref,
               out_ref, sre_out_ref, sim_out_ref, hb_ref, sre_ref, sim_ref, *, bsz, t_chunk, col_block):
    c = pl.program_id(0)
    rows = bsz * t_chunk

    @pl.when(c == 0)
    def _():
        sre_ref[...] = h0re_ref[...]
        sim_ref[...] = h0im_ref[...]

    u = u_ref[...]
    ut = jnp.swapaxes(u, 0, 1).reshape(rows, MIX_IN).astype(BF16)
    ys = []
    for j in range(S5_BLOCKS):
        cols = slice(j * 2 * S5_BLOCK_ST, (j + 1) * 2 * S5_BLOCK_ST)
        hb_ref[:, cols] = _dot(ut[:, j * S5_BLOCK_CH:(j + 1) * S5_BLOCK_CH], win_ref[j])
        for sub in range(S5_BLOCK_ST // col_block):
            nat = j * S5_BLOCK_ST + sub * col_block
            cre = j * 2 * S5_BLOCK_ST + sub * col_block
            cim = cre + S5_BLOCK_ST
            lam_r = jnp.broadcast_to(lre_ref[:, nat:nat + col_block], (bsz, col_block))
            lam_i = jnp.broadcast_to(lim_ref[:, nat:nat + col_block], (bsz, col_block))
            s_r = sre_ref[:, nat:nat + col_block]
            s_i = sim_ref[:, nat:nat + col_block]
            for t in range(t_chunk):
                rows_t = slice(t * bsz, (t + 1) * bsz)
                n_r = lam_r * s_r - lam_i * s_i + hb_ref[rows_t, cre:cre + col_block]
                n_i = lam_r * s_i + lam_i * s_r + hb_ref[rows_t, cim:cim + col_block]
                hb_ref[rows_t, cre:cre + col_block] = n_r
                hb_ref[rows_t, cim:cim + col_block] = n_i
                s_r, s_i = n_r, n_i
            sre_ref[:, nat:nat + col_block] = s_r
            sim_ref[:, nat:nat + col_block] = s_i
        ys.append(_dot(hb_ref[:, cols].astype(BF16), wout_ref[j]))
    yt = jnp.concatenate(ys, axis=1).reshape(t_chunk, bsz, MIX_IN)
    y = jnp.swapaxes(yt, 0, 1) + d_ref[...] * u
    y = jax.nn.gelu(y).reshape(rows, MIX_IN)
    gate = _dot(y.astype(BF16), wglu_ref[...].astype(BF16)) + bglu_ref[...]
    out_ref[...] = (y * jax.nn.sigmoid(gate)).reshape(bsz, t_chunk, MIX_IN).astype(BF16)

    @pl.when(c == pl.num_programs(0) - 1)
    def _():
        sre_out_ref[...] = sre_ref[...]
        sim_out_ref[...] = sim_ref[...]


def _s5_mixer(u, h0_re, h0_im, lam_re, lam_im, w_in, w_out, d_skip, w_glu, b_glu, t_chunk):
    bsz, length, _ = u.shape
    rows = bsz * t_chunk
    col_block = (SUBLANES * 512) // bsz
    kern = functools.partial(_s5_kernel, bsz=bsz, t_chunk=t_chunk, col_block=col_block)
    state = jax.ShapeDtypeStruct((bsz, SSM_COLS), F32)
    return pl.pallas_call(
        kern,
        grid=(length // t_chunk,),
        in_specs=[
            pl.BlockSpec((bsz, t_chunk, MIX_IN), lambda c: (0, c, 0)),
            _full((bsz, SSM_COLS)), _full((bsz, SSM_COLS)),
            _full((1, SSM_COLS)), _full((1, SSM_COLS)),
            _full((S5_BLOCKS, S5_BLOCK_CH, 2 * S5_BLOCK_ST)),
            _full((S5_BLOCKS, 2 * S5_BLOCK_ST, S5_BLOCK_CH)),
            _full((1, MIX_IN)), _full((MIX_IN, MIX_IN)), _full((1, MIX_IN)),
        ],
        out_specs=[
            pl.BlockSpec((bsz, t_chunk, MIX_IN), lambda c: (0, c, 0)),
            _full((bsz, SSM_COLS)), _full((bsz, SSM_COLS)),
        ],
        out_shape=[jax.ShapeDtypeStruct((bsz, length, MIX_IN), BF16), state, state],
        scratch_shapes=[
            pltpu.VMEM((rows, 2 * SSM_COLS), F32),
            pltpu.VMEM((bsz, SSM_COLS), F32),
            pltpu.VMEM((bsz, SSM_COLS), F32),
        ],
        compiler_params=_params(1),
        name="s5_mixer",
    )(u, h0_re, h0_im, lam_re, lam_im, w_in, w_out, d_skip, w_glu, b_glu)


def _ffn_kernel(h_ref, mix_ref, mem_ref, wo_ref, g_ref, win_ref, cw_ref, cb_ref, w2_ref, ctx_ref,
                out_ref, new_ref, act_ref, pad_ref, carry_ref, *, seqs, rows_per_seq, tiles_per_seq):
    m = pl.program_id(0)
    tm = seqs * rows_per_seq
    h1 = h_ref[...] + _dot(mix_ref[...], wo_ref[:MIX_IN, :]) + _dot(mem_ref[...], wo_ref[MIX_IN:, :])
    out_ref[...] = h1
    xn = _rms_rows(h1, g_ref[...]).astype(BF16)

    if tiles_per_seq > 1:
        @pl.when(m == 0)
        def _():
            carry_ref[...] = jnp.zeros(carry_ref.shape, F32)

    def causal_conv(col, slot):
        cols = slice(col, col + FFN_TF)
        u = _dot(xn, win_ref[:, cols])
        ctx = ctx_ref[:, :, cols]
        if tiles_per_seq > 1:
            ctx = jnp.where(m % tiles_per_seq == 0, ctx, carry_ref[:, cols][None])
        cw = cw_ref[:, cols]
        if seqs == 1:
            row = lax.broadcasted_iota(jnp.int32, (SUBLANES, 1), 0)
            back1 = pltpu.roll(u, 1, axis=0)
            back2 = pltpu.roll(u, 2, axis=0)
            head1 = jnp.where(row == 0, ctx[0, 1:2], back1[:SUBLANES])
            head2 = jnp.where(row == 0, ctx[0, 0:1], jnp.where(row == 1, ctx[0, 1:2], back2[:SUBLANES]))
            back1 = jnp.concatenate([head1, back1[SUBLANES:]], axis=0)
            back2 = jnp.concatenate([head2, back2[SUBLANES:]], axis=0)
            y = cb_ref[:, cols] + back2 * cw[0:1] + back1 * cw[1:2] + u * cw[2:3]
            last2 = u[tm - 2:][None]
        else:
            pad = pad_ref.at[slot]
            pad[:, 6:8, :] = ctx
            pad[:, 8:, :] = u.reshape(seqs, rows_per_seq, FFN_TF)
            y = (cb_ref[:, cols] + pad[:, 6:6 + rows_per_seq, :] * cw[0:1]
                 + pad[:, 7:7 + rows_per_seq, :] * cw[1:2]
                 + pad[:, 8:8 + rows_per_seq, :] * cw[2:3]).reshape(tm, FFN_TF)
            last2 = pad[:, rows_per_seq + 6:rows_per_seq + 8, :]
        new_ref[:, :, cols] = last2
        if tiles_per_seq > 1:
            carry_ref[:, cols] = last2[0]
        return y

    for f in range(N_FF_TILES):
        ya = causal_conv(f * FFN_TF, (2 * f) % FFN_PAD_SLOTS)
        yg = causal_conv(D_FF + f * FFN_TF, (2 * f + 1) % FFN_PAD_SLOTS)
        act_ref[:, f * FFN_TF:(f + 1) * FFN_TF] = (jax.nn.silu(yg) * ya).astype(BF16)
    out_ref[...] += _dot(act_ref[...], w2_ref[...])


def _mixout_ffn(h, mix, mem, w, layer, ffn_w, ctx, rows_per_seq_tile):
    nseq, length, _ = h.shape
    m_rows = nseq * length
    if rows_per_seq_tile >= length:
        seqs, rps, tps = nseq, length, 1
    else:
        seqs, rps, tps = 1, rows_per_seq_tile, length // rows_per_seq_tile
    tm = seqs * rps
    n_m = m_rows // tm
    kern = functools.partial(_ffn_kernel, seqs=seqs, rows_per_seq=rps, tiles_per_seq=tps)
    row = lambda width: pl.BlockSpec((tm, width), lambda m: (m, 0))
    resident = lambda *shape: pl.BlockSpec(shape, lambda m: (0,) * len(shape), pipeline_mode=pl.Buffered(1))
    of_layer = lambda *shape: pl.BlockSpec((None,) + shape, lambda m: (layer,) + (0,) * len(shape),
                                           pipeline_mode=pl.Buffered(1))
    out, new = pl.pallas_call(
        kern,
        grid=(n_m,),
        in_specs=[
            row(D_MODEL), row(MIX_IN), row(MEM_WIDTH),
            resident(MIX_IN + MEM_WIDTH, D_MODEL),
            of_layer(1, D_MODEL),
            resident(D_MODEL, 2 * D_FF),
            of_layer(CONV_W, 2 * D_FF),
            of_layer(1, 2 * D_FF),
            resident(D_FF, D_MODEL),
            pl.BlockSpec((seqs, CONV_W - 1, 2 * D_FF), lambda m: (m // tps, 0, 0)),
        ],
        out_specs=[row(D_MODEL), pl.BlockSpec((seqs, CONV_W - 1, 2 * D_FF), lambda m: (m, 0, 0))],
        out_shape=[
            jax.ShapeDtypeStruct((m_rows, D_MODEL), F32),
            jax.ShapeDtypeStruct((n_m * seqs, CONV_W - 1, 2 * D_FF), F32),
        ],
        scratch_shapes=[
            pltpu.VMEM((tm, D_FF), BF16),
            pltpu.VMEM((FFN_PAD_SLOTS, seqs, rps + SUBLANES, FFN_TF) if seqs > 1 else (1, 1, SUBLANES, LANES), F32),
            pltpu.VMEM((CONV_W - 1, 2 * D_FF), F32),
        ],
        compiler_params=_params(1),
        name="mix_out_conv_ffn",
    )(h.reshape(m_rows, D_MODEL), mix.reshape(m_rows, MIX_IN), mem.reshape(m_rows, MEM_WIDTH),
      ffn_w[0], w["norm_ffn_g"], ffn_w[1], w["ffn_conv_w"], w["ffn_conv_b"], ffn_w[2], ctx)
    new_ctx = new.reshape(nseq, tps, CONV_W - 1, 2 * D_FF)[:, -1]
    return out.reshape(nseq, length, D_MODEL), new_ctx


def _dkv_kernel(h_ref, g_ref, wl_ref, wr_ref, wrr_ref, lg_ref, kg_ref, kgr_ref, cos_ref, sin_ref,
                lat_ref, kr_ref):
    xn = _rms_rows(h_ref[...], g_ref[...]).astype(BF16)
    lat_ref[...] = _rms_rows(_dot(xn, wl_ref[...]), lg_ref[...])
    kr = _dot(xn, wr_ref[...])
    kr_rot = _dot(xn, wrr_ref[...])
    r = lax.rsqrt(jnp.mean(kr * kr, axis=-1, keepdims=True) + EPS)
    kr_ref[...] = (kr * r * kg_ref[...]) * cos_ref[...] + (kr_rot * r * kgr_ref[...]) * sin_ref[...]


def _swap_halves(t, axis=-1):
    a, b = jnp.split(t, 2, axis=axis)
    return jnp.concatenate([b, a], axis=axis)


def _shared_kv_down(h, kv_norm_g, w_dkv_bf, latent_norm_g, krope_norm_g, cos_rows, sin_rows, tm):
    nseq, length, _ = h.shape
    m_rows = nseq * length
    tab_blocks = cos_rows.shape[0] // tm
    w_l = w_dkv_bf[:, :KV_LORA]
    w_r = w_dkv_bf[:, KV_LORA:]
    kg = krope_norm_g.reshape(1, ROPE_DIM)
    lat, kr = pl.pallas_call(
        _dkv_kernel,
        grid=(m_rows // tm,),
        in_specs=[
            pl.BlockSpec((tm, D_MODEL), lambda m: (m, 0)),
            _full((1, D_MODEL)),
            _full((D_MODEL, KV_LORA)), _full((D_MODEL, ROPE_DIM)), _full((D_MODEL, ROPE_DIM)),
            _full((1, KV_LORA)), _full((1, ROPE_DIM)), _full((1, ROPE_DIM)),
            pl.BlockSpec((tm, ROPE_DIM), lambda m: (m % tab_blocks, 0)),
            pl.BlockSpec((tm, ROPE_DIM), lambda m: (m % tab_blocks, 0)),
        ],
        out_specs=[pl.BlockSpec((tm, KV_LORA), lambda m: (m, 0)),
                   pl.BlockSpec((tm, ROPE_DIM), lambda m: (m, 0))],
        out_shape=[jax.ShapeDtypeStruct((m_rows, KV_LORA), F32),
                   jax.ShapeDtypeStruct((m_rows, ROPE_DIM), F32)],
        compiler_params=_params(1),
        name="shared_kv_down",
    )(h.reshape(m_rows, D_MODEL), kv_norm_g.reshape(1, D_MODEL), w_l, w_r, _swap_halves(w_r),
      latent_norm_g.reshape(1, KV_LORA), kg, _swap_halves(kg), cos_rows, sin_rows)
    return lat.reshape(nseq, length, KV_LORA), kr.reshape(nseq, length, ROPE_DIM)


def _kv_up_kernel(lat_ref, wk_ref, wv_ref, seg_ref, g_ref, k_ref, v_ref):
    lat = lat_ref[...].astype(BF16)
    k_ref[...] = _seg_rms(_dot(lat, wk_ref[...]), seg_ref[...], g_ref[...]).astype(BF16)
    v_ref[...] = _dot(lat, wv_ref[...]).astype(BF16)


def _kv_up(latent_rows, w_uk_bf, w_uv_bf, seg_nope, k_g_tiled, tr):
    rows = latent_rows.shape[0]
    width = MLA_HEADS * NOPE_DIM
    return pl.pallas_call(
        _kv_up_kernel,
        grid=(rows // tr,),
        in_specs=[
            pl.BlockSpec((tr, KV_LORA), lambda r: (r, 0)),
            _full((KV_LORA, width)), _full((KV_LORA, width)),
            _full((width, width)), _full((1, width)),
        ],
        out_specs=[pl.BlockSpec((tr, width), lambda r: (r, 0))] * 2,
        out_shape=[jax.ShapeDtypeStruct((rows, width), BF16)] * 2,
        compiler_params=_params(1),
        name="kv_up",
    )(latent_rows, w_uk_bf, w_uv_bf, seg_nope, k_g_tiled)


Q_PRESCALE = MLA_SCALE * math.log2(math.e)


def _q_kernel(z_ref, g_ref, wn_ref, wa_ref, segn_ref, segr_ref, gn_ref, ga_ref,
              cos_ref, sin_ref, kg_ref, wuk_ref, qn_ref, qr_ref, *maybe_absorbed_refs):
    xn = _rms_rows(z_ref[...], g_ref[...]).astype(BF16)
    qn = _seg_rms(_dot(xn, wn_ref[...]), segn_ref[...], gn_ref[...]) * Q_PRESCALE
    qn_ref[...] = qn.astype(BF16)
    a = _dot(xn, wa_ref[...])
    an = a * lax.rsqrt(_dot((a * a).astype(BF16), segr_ref[...]) + EPS) * ga_ref[...]
    width = an.shape[1]
    lane = lax.broadcasted_iota(jnp.int32, (1, width), 1)
    partner = jnp.where(lane % ROPE_DIM < ROPE_HALF,
                        pltpu.roll(an, width - ROPE_HALF, axis=1), pltpu.roll(an, ROPE_HALF, axis=1))
    rot = an * cos_ref[...] + partner * sin_ref[...]
    qr_ref[...] = (rot * Q_PRESCALE).astype(BF16)
    if maybe_absorbed_refs:
        qp_ref, qrh_ref = maybe_absorbed_refs
        lane = lax.broadcasted_iota(jnp.int32, (1, MLA_HEADS * NOPE_DIM), 1)
        qg = qn * kg_ref[...]
        rot_bf = (rot * Q_PRESCALE).astype(BF16)
        for head in range(MLA_HEADS):
            qh = jnp.where(lane // NOPE_DIM == head, qg, 0.0).astype(BF16)
            qp_ref[head] = _dot_nt(qh, wuk_ref[...]).astype(BF16)
            qrh_ref[head] = rot_bf[:, head * ROPE_DIM:(head + 1) * ROPE_DIM]


def _q_side(z_mix, q_latent_g, w_uq_bf, q_nope_g, q_rope_g, seg_nope, seg_rope, cos_rows, sin_rows,
            k_g_tiled, w_uk_bf, tm, absorbed):
    nseq, length, _ = z_mix.shape
    m_rows = nseq * length
    tab_blocks = cos_rows.shape[0] // tm
    wn_width = MLA_HEADS * NOPE_DIM
    wr_width = MLA_HEADS * ROPE_DIM
    w3 = w_uq_bf.reshape(MIX_IN, MLA_HEADS, NOPE_DIM + ROPE_DIM)
    w_n = w3[:, :, :NOPE_DIM].reshape(MIX_IN, wn_width)
    w_a = w3[:, :, NOPE_DIM:].reshape(MIX_IN, wr_width)
    g_n = jnp.tile(q_nope_g.reshape(1, NOPE_DIM), (1, MLA_HEADS))
    g_a = jnp.tile(q_rope_g.reshape(1, ROPE_DIM), (1, MLA_HEADS))
    cos_q = jnp.tile(cos_rows, (1, MLA_HEADS))
    sin_q = jnp.tile(sin_rows, (1, MLA_HEADS))
    out_specs = [pl.BlockSpec((tm, wd), lambda m: (m, 0)) for wd in (wn_width, wr_width)]
    out_shape = [jax.ShapeDtypeStruct((m_rows, wd), BF16) for wd in (wn_width, wr_width)]
    if absorbed:
        for wd in (KV_LORA, ROPE_DIM):
            out_specs.append(pl.BlockSpec((MLA_HEADS, tm, wd), lambda m: (0, m, 0)))
            out_shape.append(jax.ShapeDtypeStruct((MLA_HEADS, m_rows, wd), BF16))
    outs = pl.pallas_call(
        _q_kernel,
        grid=(m_rows // tm,),
        in_specs=[
            pl.BlockSpec((tm, MIX_IN), lambda m: (m, 0)),
            _full((1, MIX_IN)),
            _full((MIX_IN, wn_width)), _full((MIX_IN, wr_width)),
            _full((wn_width, wn_width)), _full((wr_width, wr_width)),
            _full((1, wn_width)), _full((1, wr_width)),
            pl.BlockSpec((tm, wr_width), lambda m: (m % tab_blocks, 0)),
            pl.BlockSpec((tm, wr_width), lambda m: (m % tab_blocks, 0)),
            _full((1, wn_width)), _full((KV_LORA, wn_width)),
        ],
        out_specs=out_specs,
        out_shape=out_shape,
        compiler_params=_params(1),
        name="mla_query",
    )(z_mix.reshape(m_rows, MIX_IN), q_latent_g.reshape(1, MIX_IN), w_n, w_a, seg_nope, seg_rope,
      g_n, g_a, cos_q, sin_q, k_g_tiled, w_uk_bf)
    return outs


ATTN_HEADS_PER_STEP = 4


def _attn_kernel(qn_ref, qr_ref, kn_ref, kr_ref, v_ref, o_ref, *, tile, n_tiles):
    qi = pl.program_id(2)
    pairs = ATTN_HEADS_PER_STEP // 2
    lane = lax.broadcasted_iota(jnp.int32, (1, LANES), 1)
    head_lanes = [(lane // NOPE_DIM) == j for j in range(2)]
    qr = qr_ref[0]
    qcat = []
    for h in range(ATTN_HEADS_PER_STEP):
        qn = qn_ref[0, :, (h // 2) * LANES:(h // 2 + 1) * LANES]
        qcat.append(jnp.concatenate(
            [jnp.where(head_lanes[h % 2], qn, jnp.zeros_like(qn)),
             jnp.where((lane // ROPE_DIM) == h, qr, jnp.zeros_like(qr))], axis=1))
    row_chunk = lax.broadcasted_iota(jnp.int32, (tile, 1), 0) // CHUNK
    col_chunk = lax.broadcasted_iota(jnp.int32, (1, tile), 1) // CHUNK
    diag_visible = col_chunk <= row_chunk

    def one_block(kb, carry, masked):
        rows_k = slice(kb * tile, (kb + 1) * tile)
        kr = kr_ref[0, rows_k, :]
        new = []
        for h in range(ATTN_HEADS_PER_STEP):
            lanes_p = slice((h // 2) * LANES, (h // 2 + 1) * LANES)
            m_i, acc = carry[h]
            kcat = jnp.concatenate([kn_ref[0, rows_k, lanes_p], kr], axis=1)
            s = _dot_nt(qcat[h], kcat)
            if masked:
                s = jnp.where(diag_visible, s, NEG_INF)
            m_new = jnp.maximum(m_i, jnp.max(s, axis=-1, keepdims=True))
            alpha = jnp.exp2(m_i - m_new)
            p = jnp.exp2(s - m_new).astype(BF16)
            vb = v_ref[0, rows_k, lanes_p]
            v_h = jnp.where(head_lanes[h % 2], vb, jnp.ones_like(vb))
            new.append((m_new, alpha * acc + _dot(p, v_h)))
        return tuple(new)

    def query_tile(n_full):
        carry = tuple((jnp.full((tile, 1), NEG_INF, F32), jnp.zeros((tile, LANES), F32))
                      for _ in range(ATTN_HEADS_PER_STEP))
        for kb in range(n_full):
            carry = one_block(kb, carry, False)
        carry = one_block(n_full, carry, True)
        for p in range(pairs):
            out = jnp.zeros((tile, LANES), F32)
            for j in range(2):
                acc = carry[2 * p + j][1]
                row_sum = pltpu.roll(acc, NOPE_DIM, axis=1)
                out = jnp.where(head_lanes[j], acc * (1.0 / row_sum), out)
            o_ref[0, :, p * LANES:(p + 1) * LANES] = out.astype(BF16)

    for c in range(n_tiles):
        pl.when(qi == c)(functools.partial(query_tile, c))


def _mla_attention(qn, qr, kn, kr4, v, tile):
    nseq, length, _ = qn.shape
    kern = functools.partial(_attn_kernel, tile=tile, n_tiles=length // tile)
    width = (ATTN_HEADS_PER_STEP // 2) * LANES
    return pl.pallas_call(
        kern,
        grid=(nseq, MLA_HEADS // ATTN_HEADS_PER_STEP, length // tile),
        in_specs=[
            pl.BlockSpec((1, tile, width), lambda b, g, i: (b, i, g)),
            pl.BlockSpec((1, tile, LANES), lambda b, g, i: (b, i, g)),
            pl.BlockSpec((1, length, width), lambda b, g, i: (b, 0, g)),
            pl.BlockSpec((1, length, LANES), lambda b, g, i: (b, 0, 0)),
            pl.BlockSpec((1, length, width), lambda b, g, i: (b, 0, g)),
        ],
        out_specs=pl.BlockSpec((1, tile, width), lambda b, g, i: (b, i, g)),
        out_shape=jax.ShapeDtypeStruct((nseq, length, MLA_HEADS * V_DIM), BF16),
        compiler_params=_params(3),
        name="mla_attention",
    )(qn, qr, kn, kr4, v)


KNORM_ROWS = 16


def _key_norm_kernel(past_ref, new_ref, wukt_ref, rt_ref, *, n_chunks, chunk):
    def norms(lat):
        keys = lat.shape[0]
        kt = _dot_nt(wukt_ref[...], lat)
        ss = jnp.sum((kt * kt).reshape(MLA_HEADS, NOPE_DIM, keys), axis=1)
        r = lax.rsqrt(ss * (1.0 / NOPE_DIM) + EPS)
        return jnp.concatenate([r, jnp.ones((KNORM_ROWS - MLA_HEADS, keys), F32)], axis=0)

    for c in range(n_chunks):
        rt_ref[0, :, c * chunk:(c + 1) * chunk] = norms(past_ref[0, c * chunk:(c + 1) * chunk, :].astype(BF16))
    rt_ref[0, :, n_chunks * chunk:] = norms(new_ref[0])


def _key_norms(past_latent, new_lat_pad, w_ukt_bf, chunk):
    nseq, past, _ = past_latent.shape
    new_pad = new_lat_pad.shape[1]
    lk_pad = past + new_pad
    kern = functools.partial(_key_norm_kernel, n_chunks=past // chunk, chunk=chunk)
    return pl.pallas_call(
        kern,
        grid=(nseq,),
        in_specs=[pl.BlockSpec((1, past, KV_LORA), lambda b: (b, 0, 0)),
                  pl.BlockSpec((1, new_pad, KV_LORA), lambda b: (b, 0, 0)),
                  _full((MLA_HEADS * NOPE_DIM, KV_LORA))],
        out_specs=pl.BlockSpec((1, KNORM_ROWS, lk_pad), lambda b: (b, 0, 0)),
        out_shape=jax.ShapeDtypeStruct((nseq, KNORM_ROWS, lk_pad), F32),
        compiler_params=_params(1),
        name="mla_key_norms",
    )(past_latent, new_lat_pad, w_ukt_bf)


def _attn_absorbed_kernel(qs_ref, qrs_ref, plat_ref, nlat_ref, pkrt_ref, nkrt_ref, rt_ref, wuv_ref, o_ref, op_ref,
                          *, tq, q_off, lk_valid):
    b = pl.program_id(0)
    nseq = pl.num_programs(0)
    rows = MLA_HEADS * tq
    lat = jnp.concatenate([plat_ref[0].astype(BF16), nlat_ref[0]], axis=0)
    kr_t = jnp.concatenate([pkrt_ref[0].astype(BF16), nkrt_ref[0]], axis=1)
    lk_pad = lat.shape[0]
    rt = rt_ref[0]
    knorm = jnp.concatenate([jnp.broadcast_to(rt[h:h + 1, :], (tq, lk_pad)) for h in range(MLA_HEADS)], axis=0)
    qs = qs_ref[...].reshape(rows, KV_LORA)
    qrs = qrs_ref[...].reshape(rows, ROPE_DIM)
    s = _dot_nt(qs, lat) * knorm + _dot(qrs, kr_t)
    k_pos = lax.broadcasted_iota(jnp.int32, (1, lk_pad), 1)
    if (lk_valid - 1) // CHUNK > q_off // CHUNK:
        q_chunk = (q_off + lax.broadcasted_iota(jnp.int32, (rows, 1), 0) % tq) // CHUNK
        s = jnp.where(k_pos // CHUNK <= q_chunk, s, NEG_INF)
    s = jnp.where(k_pos < lk_valid, s, NEG_INF)
    p = jnp.exp2(s - jnp.max(s, axis=-1, keepdims=True))
    l = jnp.sum(p, axis=-1, keepdims=True)
    op_ref[b] = (_dot(p.astype(BF16), lat) * (1.0 / l)).astype(BF16)

    @pl.when(b == nseq - 1)
    def _():
        n_all = op_ref.shape[0]
        lane_o = lax.broadcasted_iota(jnp.int32, (1, MLA_HEADS * V_DIM), 1)
        out = jnp.zeros((n_all * tq, MLA_HEADS * V_DIM), F32)
        for h in range(MLA_HEADS):
            x = op_ref[:, h * tq:(h + 1) * tq, :].reshape(n_all * tq, KV_LORA)
            out = jnp.where(lane_o // V_DIM == h, _dot(x, wuv_ref[...]), out)
        o_ref[...] = out.astype(BF16)


def _mla_attention_absorbed(qp, qrh, past_latent, new_lat_pad, past_krope_t, new_kr_pad_t, r_t, w_uv_bf,
                            tq, lk_valid):
    nseq, past, _ = past_latent.shape
    new_pad = new_lat_pad.shape[1]
    kern = functools.partial(_attn_absorbed_kernel, tq=tq, q_off=past, lk_valid=lk_valid)
    per_seq = lambda n, width: pl.BlockSpec((1, n, width), lambda b: (b, 0, 0))
    heads_of_seq = lambda width: pl.BlockSpec((MLA_HEADS, tq, width), lambda b: (0, b, 0))
    out = pl.pallas_call(
        kern,
        grid=(nseq,),
        in_specs=[
            heads_of_seq(KV_LORA), heads_of_seq(ROPE_DIM),
            per_seq(past, KV_LORA), per_seq(new_pad, KV_LORA),
            per_seq(ROPE_DIM, past), per_seq(ROPE_DIM, new_pad),
            pl.BlockSpec((1, KNORM_ROWS, past + new_pad), lambda b: (b, 0, 0)),
            _full((KV_LORA, MLA_HEADS * V_DIM)),
        ],
        out_specs=_full((nseq * tq, MLA_HEADS * V_DIM)),
        out_shape=jax.ShapeDtypeStruct((nseq * tq, MLA_HEADS * V_DIM), BF16),
        scratch_shapes=[pltpu.VMEM((nseq, MLA_HEADS * tq, KV_LORA), BF16)],
        compiler_params=_params(1),
        name="mla_attention_absorbed",
    )(qp, qrh, past_latent, new_lat_pad, past_krope_t, new_kr_pad_t, r_t, w_uv_bf)
    return out.reshape(nseq, tq, MLA_HEADS * V_DIM)


def _trunk(x, mem_k_bf, mem_v_bf, ssm_h0_re, ssm_h0_im, conv_ctx, past_latent, past_krope, w, cfg, ffn_bf=None):
    nseq, length, _ = x.shape
    past = 0 if past_latent is None else past_latent.shape[1]
    cos_t, sin_t = _rope_tables(past, length)
    reps = cfg["rope_rows"] // length
    cos_rows = jnp.tile(cos_t, (reps, 1))
    sin_rows = jnp.tile(sin_t, (reps, 1))
    if conv_ctx is None:
        conv_ctx = jnp.zeros((DEPTH, nseq, CONV_W - 1, 2 * D_FF), F32)
    if ssm_h0_re is None:
        ssm_h0_re = jnp.zeros((N_A_LAYERS, nseq, SSM_GROUPS, SSM_STATE), F32)
        ssm_h0_im = ssm_h0_re
    h = x
    ssm_re_out, ssm_im_out, conv_out = [], [], []
    make_ffn_bf = ffn_bf is None
    if make_ffn_bf:
        ffn_bf = []
    for layer in range(DEPTH):
        to_cast = (w["w_mix_out_f32"], w["w_ffn_in_f32"], w["w_ffn_out_f32"]) if make_ffn_bf else ()
        z_mix, mem_out, cast = _mixin(h, w, layer, mem_k_bf, mem_v_bf, cfg["tm_mixin"], to_cast)
        if make_ffn_bf:
            ffn_bf.append(cast)
        if layer < N_A_LAYERS:
            i = layer
            mix_out, s_re, s_im = _s5_mixer(
                z_mix, ssm_h0_re[i].reshape(nseq, SSM_COLS), ssm_h0_im[i].reshape(nseq, SSM_COLS),
                w["lam_re"][i], w["lam_im"][i], w["s5_in"][i], w["s5_out"][i],
                w["ssm_d"][i][None], w["w_glu"][i], w["b_glu"][i][None], cfg["t_chunk"])
            ssm_re_out.append(s_re.reshape(nseq, SSM_GROUPS, SSM_STATE))
            ssm_im_out.append(s_im.reshape(nseq, SSM_GROUPS, SSM_STATE))
        else:
            if layer == N_A_LAYERS:
                new_latent, new_krope = _shared_kv_down(
                    h, w["kv_norm_g"], w["w_dkv"], w["latent_norm_g"], w["krope_norm_g"],
                    cos_rows, sin_rows, cfg["tm_rows"])
                if cfg["absorbed"]:
                    assert past % LANES == 0
                    lk_valid = past + length
                    new_pad = -(-length // LANES) * LANES
                    pad_rows = lambda t: jnp.pad(t.astype(BF16), ((0, 0), (0, new_pad - length), (0, 0)))
                    new_lat_pad = pad_rows(new_latent)
                    new_kr_pad_t = jnp.swapaxes(pad_rows(new_krope), 1, 2)
                    past_krope_t = jnp.swapaxes(past_krope, 1, 2)
                    r_t = _key_norms(past_latent, new_lat_pad, w["w_uk_t"], cfg["knorm_chunk"])
                else:
                    assert past_latent is None
                    kn, v_all = _kv_up(new_latent.reshape(nseq * length, KV_LORA), w["w_uk"], w["w_uv"],
                                       w["seg_nope"], w["k_nope_g"], cfg["tr_kv"])
                    kn = kn.reshape(nseq, length, MLA_HEADS * NOPE_DIM)
                    v_all = v_all.reshape(nseq, length, MLA_HEADS * V_DIM)
                    kr4 = jnp.tile(new_krope.astype(BF16), (1, 1, LANES // ROPE_DIM))
            j = layer - N_A_LAYERS
            q_out = _q_side(z_mix, w["q_latent_norm_g"][j], w["w_uq"][j], w["q_nope_norm_g"][j],
                            w["q_rope_norm_g"][j], w["seg_nope"], w["seg_rope"], cos_rows, sin_rows,
                            w["k_nope_g"], w["w_uk"], cfg["tm_rows"], cfg["absorbed"])
            if cfg["absorbed"]:
                mix_out = _mla_attention_absorbed(q_out[2], q_out[3], past_latent, new_lat_pad, past_krope_t,
                                                  new_kr_pad_t, r_t, w["w_uv"], length, lk_valid)
            else:
                qn = q_out[0].reshape(nseq, length, MLA_HEADS * NOPE_DIM)
                qr = q_out[1].reshape(nseq, length, MLA_HEADS * ROPE_DIM)
                mix_out = _mla_attention(qn, qr, kn, kr4, v_all, cfg["tq"])
        h, ctx = _mixout_ffn(h, mix_out, mem_out, w, layer, ffn_bf[layer], conv_ctx[layer], cfg["ffn_rows"])
        conv_out.append(ctx)
    return (h, new_latent, new_krope, jnp.stack(ssm_re_out), jnp.stack(ssm_im_out), jnp.stack(conv_out)), ffn_bf


PROMPT_CFG = dict(tm_mixin=1024, t_chunk=64, ffn_rows=512, tm_rows=1024, rope_rows=2048, tr_kv=1024,
                  tq=512, absorbed=False)
SAMPLE_CFG = dict(tm_mixin=32, t_chunk=32, ffn_rows=32, tm_rows=512, rope_rows=512, knorm_chunk=1024,
                  absorbed=True)


def kernel(x_prompt, x_sample, cache_mla_latent, cache_mla_krope, cache_mem_k, cache_mem_v, state_ssm_re, state_ssm_im, state_conv, mem_prompt, norm_mix_g, w_mix_in, w_mix_out, norm_ffn_g, w_ffn_in, ffn_conv_w, ffn_conv_b, w_ffn_out, mem_norm_g, w_mem_kv, mem_q_norm_g, mem_k_norm_g, ssm_a_re, ssm_a_im, ssm_log_dt, ssm_b_re, ssm_b_im, ssm_c_re, ssm_c_im, ssm_d, w_glu, b_glu, kv_norm_g, w_dkv, latent_norm_g, krope_norm_g, w_uk, w_uv, k_nope_norm_g, q_latent_norm_g, w_uq, q_nope_norm_g, q_rope_norm_g):
    bf = lambda t: t.astype(BF16)
    seg_mem = _seg_matrix(MEM_WIDTH, MEM_HEAD_DIM)
    lam_re, lam_im, s5_in, s5_out = _s5_prepare(ssm_a_re, ssm_a_im, ssm_log_dt, ssm_b_re, ssm_b_im,
                                                ssm_c_re, ssm_c_im)
    w = dict(
        norm_mix_g=norm_mix_g.reshape(DEPTH, 1, D_MODEL), w_mix_in=w_mix_in,
        norm_ffn_g=norm_ffn_g.reshape(DEPTH, 1, D_MODEL),
        ffn_conv_w=ffn_conv_w, ffn_conv_b=ffn_conv_b.reshape(DEPTH, 1, 2 * D_FF),
        w_mix_out_f32=w_mix_out, w_ffn_in_f32=w_ffn_in, w_ffn_out_f32=w_ffn_out,
        mem_q_g=jnp.tile(mem_q_norm_g, (1, MEM_HEADS)).reshape(DEPTH, 1, MEM_WIDTH),
        seg_mem=seg_mem,
        seg_nope=_seg_matrix(MLA_HEADS * NOPE_DIM, NOPE_DIM),
        seg_rope=_seg_matrix(MLA_HEADS * ROPE_DIM, ROPE_DIM),
        lam_re=lam_re, lam_im=lam_im, s5_in=s5_in, s5_out=s5_out,
        ssm_d=ssm_d, w_glu=w_glu, b_glu=b_glu,
        kv_norm_g=kv_norm_g, w_dkv=bf(w_dkv), latent_norm_g=latent_norm_g, krope_norm_g=krope_norm_g,
        w_uk=bf(w_uk), w_uk_t=bf(w_uk).T, w_uv=bf(w_uv),
        k_nope_g=jnp.tile(k_nope_norm_g.reshape(1, NOPE_DIM), (1, MLA_HEADS)),
        q_latent_norm_g=q_latent_norm_g, w_uq=bf(w_uq), q_nope_norm_g=q_nope_norm_g,
        q_rope_norm_g=q_rope_norm_g,
    )
    bsz = mem_prompt.shape[0]
    mem_k_p, mem_v_p, mem_k_bf, mem_v_bf = _memory_kv(mem_prompt, mem_norm_g, w_mem_kv, mem_k_norm_g, seg_mem)
    (y_prompt, lat_p, krope_p, ssm_re_p, ssm_im_p, conv_p), ffn_bf = _trunk(
        x_prompt, mem_k_bf, mem_v_bf, None, None, None, None, None, w, PROMPT_CFG)
    dec = cache_mem_k.shape[1]
    (y_sample, lat_s, krope_s, ssm_re_s, ssm_im_s, conv_s), _ = _trunk(
        x_sample, bf(cache_mem_k).reshape(DEPTH, dec, N_MEM, MEM_WIDTH),
        bf(cache_mem_v).reshape(DEPTH, dec, N_MEM, MEM_WIDTH),
        state_ssm_re, state_ssm_im, state_conv, cache_mla_latent, cache_mla_krope, w, SAMPLE_CFG, ffn_bf)
    shape5 = (DEPTH, bsz, N_MEM, MEM_HEADS, MEM_HEAD_DIM)
    return (y_prompt, y_sample, mem_k_p.reshape(shape5), mem_v_p.reshape(shape5), lat_p, krope_p,
            ssm_re_p, ssm_im_p, conv_p, lat_s, krope_s, ssm_re_s, ssm_im_s, conv_s)
```

```python
import functools
import math

import jax
import jax.numpy as jnp
from jax import lax
from jax.experimental import pallas as pl
from jax.experimental.pallas import tpu as pltpu

F32 = jnp.float32
BF16 = jnp.bfloat16

D_MODEL = 1024
DEPTH = 4
CHUNK = 64
N_A_LAYERS = DEPTH // 2
N_B_LAYERS = DEPTH - N_A_LAYERS
MIX_IN = 768
MEM_HEADS = 4
MEM_HEAD_DIM = 64
MEM_WIDTH = MEM_HEADS * MEM_HEAD_DIM
N_MEM = 256
SSM_GROUP = 16
SSM_GROUPS = MIX_IN // SSM_GROUP
SSM_STATE = 64
SSM_COLS = SSM_GROUPS * SSM_STATE
MLA_HEADS = 12
NOPE_DIM = 64
ROPE_DIM = 32
ROPE_HALF = ROPE_DIM // 2
V_DIM = 64
KV_LORA = 256
ROPE_BASE = 10000.0
MLA_SCALE = (NOPE_DIM + ROPE_DIM) ** -0.5
MEM_SCALE = MEM_HEAD_DIM ** -0.5
D_FF = 2816
CONV_W = 3
EPS = 1e-6
NEG_INF = -1e30

V7X_VMEM_LIMIT_BYTES = 56 * 1024 * 1024
LANES = 128
SUBLANES = 8

S5_BLOCKS = 3
S5_BLOCK_CH = MIX_IN // S5_BLOCKS
S5_BLOCK_ST = SSM_COLS // S5_BLOCKS
FFN_TF = 256
N_FF_TILES = D_FF // FFN_TF
FFN_PAD_SLOTS = 4


def _params(n_axes):
    return pltpu.CompilerParams(
        dimension_semantics=("arbitrary",) * n_axes,
        vmem_limit_bytes=V7X_VMEM_LIMIT_BYTES,
    )


def _dot(a, b):
    return jnp.dot(a, b, preferred_element_type=F32)


def _dot_nt(a, b):
    return lax.dot_general(a, b, (((1,), (1,)), ((), ())), preferred_element_type=F32)


def _rms_rows(x, g):
    ms = jnp.mean(x * x, axis=-1, keepdims=True)
    return x * lax.rsqrt(ms + EPS) * g


def _seg_rms(x, seg_mat, g):
    ms = _dot((x * x).astype(BF16), seg_mat)
    return x * lax.rsqrt(ms + EPS) * g


def _seg_matrix(width, seg):
    idx = jnp.arange(width) // seg
    return jnp.where(idx[:, None] == idx[None, :], 1.0 / seg, 0.0).astype(BF16)


def _full(shape):
    nd = len(shape)
    return pl.BlockSpec(shape, lambda *_: (0,) * nd)


def _rope_table_kernel(pos_ref, inv_ref, cos_ref, sin_ref):
    ang = pos_ref[...] * inv_ref[...]
    cos_ref[...] = jnp.cos(ang)
    sin_ref[...] = jnp.sin(ang)


def _rope_tables(past, length):
    pos = (past + jnp.arange(length, dtype=jnp.int32)).astype(F32)[:, None]
    inv_freq = (1.0 / (ROPE_BASE ** (jnp.arange(0, ROPE_DIM, 2, dtype=F32) / ROPE_DIM)))[None, :]
    cos, sin = pl.pallas_call(
        _rope_table_kernel,
        out_shape=[jax.ShapeDtypeStruct((length, ROPE_HALF), F32)] * 2,
        name="rope_table",
    )(pos, inv_freq)
    cos_t = jnp.concatenate([cos, cos], axis=1)
    sin_t = jnp.concatenate([-sin, sin], axis=1)
    return cos_t, sin_t


def _memkv_kernel(mem_ref, g_ref, w_ref, kg_ref, seg_ref, k_ref, v_ref, kb_ref, vb_ref):
    xn = _rms_rows(mem_ref[0], g_ref[0]).astype(BF16)
    kv = _dot(xn, w_ref[0].astype(BF16))
    k = _seg_rms(kv[:, :MEM_WIDTH], seg_ref[...], kg_ref[0])
    v = kv[:, MEM_WIDTH:]
    k_ref[0, 0] = k
    v_ref[0, 0] = v
    kb_ref[0, 0] = k.astype(BF16)
    vb_ref[0, 0] = v.astype(BF16)


def _memory_kv(mem, mem_norm_g, w_mem_kv, mem_k_norm_g, seg_mem):
    bsz = mem.shape[0]
    kg = jnp.tile(mem_k_norm_g, (1, MEM_HEADS)).reshape(DEPTH, 1, MEM_WIDTH)
    out4 = lambda dt: jax.ShapeDtypeStruct((DEPTH, bsz, N_MEM, MEM_WIDTH), dt)
    spec4 = pl.BlockSpec((1, 1, N_MEM, MEM_WIDTH), lambda l, b: (l, b, 0, 0))
    return pl.pallas_call(
        _memkv_kernel,
        grid=(DEPTH, bsz),
        in_specs=[
            pl.BlockSpec((1, N_MEM, D_MODEL), lambda l, b: (b, 0, 0)),
            pl.BlockSpec((1, 1, D_MODEL), lambda l, b: (l, 0, 0)),
            pl.BlockSpec((1, D_MODEL, 2 * MEM_WIDTH), lambda l, b: (l, 0, 0)),
            pl.BlockSpec((1, 1, MEM_WIDTH), lambda l, b: (l, 0, 0)),
            _full((MEM_WIDTH, MEM_WIDTH)),
        ],
        out_specs=[spec4, spec4, spec4, spec4],
        out_shape=[out4(F32), out4(F32), out4(BF16), out4(BF16)],
        compiler_params=_params(2),
        name="memory_kv",
    )(mem, mem_norm_g.reshape(DEPTH, 1, D_MODEL), w_mem_kv, kg, seg_mem)


def _mixin_kernel(h_ref, g_ref, w_ref, qg_ref, seg_ref, k_ref, v_ref, *rest, seqs, rows_per_seq, n_cast):
    cast_in, (zmix_ref, mem_ref), cast_out = rest[:n_cast], rest[n_cast:n_cast + 2], rest[n_cast + 2:]
    for src, dst in zip(cast_in, cast_out):
        dst[...] = src[...].astype(BF16)
    xn = _rms_rows(h_ref[...], g_ref[...]).astype(BF16)
    z = _dot(xn, w_ref[...].astype(BF16))
    zmix_ref[...] = z[:, :MIX_IN]
    mq = _seg_rms(z[:, MIX_IN:], seg_ref[...], qg_ref[...]).astype(BF16)
    lane = lax.broadcasted_iota(jnp.int32, (1, MEM_WIDTH), 1)
    in_head = [(lane // MEM_HEAD_DIM) == head for head in range(MEM_HEADS)]
    for b in range(seqs):
        rows = slice(b * rows_per_seq, (b + 1) * rows_per_seq)
        mq_b = mq[rows]
        qs = jnp.concatenate([jnp.where(m, mq_b, jnp.zeros_like(mq_b)) for m in in_head], axis=0)
        s = _dot_nt(qs, k_ref[b]) * MEM_SCALE
        p = jnp.exp(s - jnp.max(s, axis=-1, keepdims=True))
        p = p * (1.0 / jnp.sum(p, axis=-1, keepdims=True))
        o = _dot(p.astype(BF16), v_ref[b])
        out = jnp.zeros(mq_b.shape, F32)
        for head, m in enumerate(in_head):
            out = jnp.where(m, o[head * rows_per_seq:(head + 1) * rows_per_seq], out)
        mem_ref[rows, :] = out.astype(BF16)


def _mixin(h, w, layer, k_bf, v_bf, rows_per_seq_tile, to_cast=()):
    nseq, length, _ = h.shape
    m_rows = nseq * length
    if rows_per_seq_tile >= length:
        seqs, rps, tps = nseq, length, 1
    else:
        seqs, rps, tps = 1, rows_per_seq_tile, length // rows_per_seq_tile
    tm = seqs * rps
    n_steps = m_rows // tm
    kern = functools.partial(_mixin_kernel, seqs=seqs, rows_per_seq=rps, n_cast=len(to_cast))
    row = lambda width: pl.BlockSpec((tm, width), lambda m: (m, 0))
    of_layer = lambda *shape: pl.BlockSpec((None,) + shape, lambda m: (layer,) + (0,) * len(shape))
    kv_spec = pl.BlockSpec((None, seqs, N_MEM, MEM_WIDTH), lambda m: (layer, m // tps, 0, 0))
    slab = lambda t: t.shape[1] // n_steps
    cast_in = [pl.BlockSpec((None, slab(t), t.shape[2]), lambda m: (layer, m, 0)) for t in to_cast]
    cast_out = [pl.BlockSpec((slab(t), t.shape[2]), lambda m: (m, 0)) for t in to_cast]
    outs = pl.pallas_call(
        kern,
        grid=(n_steps,),
        in_specs=[
            row(D_MODEL),
            of_layer(1, D_MODEL),
            of_layer(D_MODEL, MIX_IN + MEM_WIDTH),
            of_layer(1, MEM_WIDTH),
            _full((MEM_WIDTH, MEM_WIDTH)),
            kv_spec, kv_spec,
        ] + cast_in,
        out_specs=[row(MIX_IN), row(MEM_WIDTH)] + cast_out,
        out_shape=[
            jax.ShapeDtypeStruct((m_rows, MIX_IN), F32),
            jax.ShapeDtypeStruct((m_rows, MEM_WIDTH), BF16),
        ] + [jax.ShapeDtypeStruct(t.shape[1:], BF16) for t in to_cast],
        compiler_params=_params(1),
        name="mix_in_mem_attn",
    )(h.reshape(m_rows, D_MODEL), w["norm_mix_g"], w["w_mix_in"], w["mem_q_g"], w["seg_mem"], k_bf, v_bf,
      *to_cast)
    return outs[0].reshape(nseq, length, MIX_IN), outs[1].reshape(nseq, length, MEM_WIDTH), outs[2:]


def _s5_discretise_kernel(are_ref, aim_ref, ldt_ref, lre_ref, lim_ref, fre_ref, fim_ref):
    a_re, a_im = are_ref[...], aim_ref[...]
    dt = jnp.exp(ldt_ref[...])
    mag = jnp.exp(a_re * dt)
    lam_re = mag * jnp.cos(a_im * dt)
    lam_im = mag * jnp.sin(a_im * dt)
    den = a_re * a_re + a_im * a_im
    x_re = lam_re - 1.0
    lre_ref[...] = lam_re
    lim_ref[...] = lam_im
    fre_ref[...] = (x_re * a_re + lam_im * a_im) / den
    fim_ref[...] = (lam_im * a_re - x_re * a_im) / den


def _s5_input_scale_kernel(fre_ref, fim_ref, bre_ref, bim_ref, bbre_ref, bbim_ref):
    f_re, f_im = fre_ref[0], fim_ref[0]
    b_re, b_im = bre_ref[0], bim_ref[0]
    bbre_ref[0] = f_re * b_re - f_im * b_im
    bbim_ref[0] = f_re * b_im + f_im * b_re


def _s5_prepare(a_re, a_im, log_dt, b_re, b_im, c_re, c_im):
    n = a_re.shape[0]
    dense = lambda t: t.reshape(n, SSM_COLS // LANES, LANES)
    ldt = jnp.broadcast_to(log_dt[:, :, None], (n, SSM_GROUPS, SSM_STATE))
    lam_re, lam_im, f_re, f_im = pl.pallas_call(
        _s5_discretise_kernel,
        out_shape=[jax.ShapeDtypeStruct((n, SSM_COLS // LANES, LANES), F32)] * 4,
        name="s5_discretise",
    )(dense(a_re), dense(a_im), dense(ldt))
    col = lambda t: t.reshape(n, SSM_COLS, 1)
    col_spec = pl.BlockSpec((1, SSM_COLS, 1), lambda l: (l, 0, 0))
    b_spec = pl.BlockSpec((1, SSM_COLS, SSM_GROUP), lambda l: (l, 0, 0))
    bb_re, bb_im = pl.pallas_call(
        _s5_input_scale_kernel,
        grid=(n,),
        in_specs=[col_spec, col_spec, b_spec, b_spec],
        out_specs=[b_spec, b_spec],
        out_shape=[jax.ShapeDtypeStruct((n, SSM_COLS, SSM_GROUP), F32)] * 2,
        compiler_params=_params(1),
        name="s5_input_scale",
    )(col(f_re), col(f_im), b_re.reshape(n, SSM_COLS, SSM_GROUP), b_im.reshape(n, SSM_COLS, SSM_GROUP))
    gpb = SSM_GROUPS // S5_BLOCKS

    def block_diagonal(t, rows_per_group, cols_per_group):
        rows = gpb * rows_per_group
        same = (jnp.arange(rows) // rows_per_group)[:, None] == jnp.arange(gpb)[None, :]
        wide = jnp.broadcast_to(t[:, :, :, None, :], (n, S5_BLOCKS, rows, gpb, cols_per_group))
        return jnp.where(same[None, None, :, :, None], wide, 0.0).reshape(
            n, S5_BLOCKS, rows, gpb * cols_per_group)

    def in_blocks(bb):
        t = bb.reshape(n, S5_BLOCKS, gpb, SSM_STATE, SSM_GROUP).transpose(0, 1, 2, 4, 3)
        return block_diagonal(t.reshape(n, S5_BLOCKS, S5_BLOCK_CH, SSM_STATE), SSM_GROUP, SSM_STATE)

    def out_blocks(c):
        t = c.reshape(n, S5_BLOCKS, gpb, SSM_GROUP, SSM_STATE).transpose(0, 1, 2, 4, 3)
        return block_diagonal(t.reshape(n, S5_BLOCKS, S5_BLOCK_ST, SSM_GROUP), SSM_STATE, SSM_GROUP)

    w_in = jnp.concatenate([in_blocks(bb_re), in_blocks(bb_im)], axis=-1).astype(BF16)
    w_out = jnp.concatenate([out_blocks(c_re), -out_blocks(c_im)], axis=-2).astype(BF16)
    return lam_re.reshape(n, 1, SSM_COLS), lam_im.reshape(n, 1, SSM_COLS), w_in, w_out


def _s5_kernel(u_ref, h0re_ref, h0im_ref, lre_ref, lim_ref, win_ref, wout_ref, d_ref, wglu_ref, bglu_ref,
               out_ref, sre_out_ref, sim_out_ref, hb_ref, sre_ref, sim_ref, *, bsz, t_chunk, col_block):
    c = pl.program_id(0)
    rows = bsz * t_chunk

    @pl.when(c == 0)
    def _():
        sre_ref[...] = h0re_ref[...]
        sim_ref[...] = h0im_ref[...]

    u = u_ref[...]
    ut = jnp.swapaxes(u, 0, 1).reshape(rows, MIX_IN).astype(BF16)
    ys = []
    for j in range(S5_BLOCKS):
        cols = slice(j * 2 * S5_BLOCK_ST, (j + 1) * 2 * S5_BLOCK_ST)
        hb_ref[:, cols] = _dot(ut[:, j * S5_BLOCK_CH:(j + 1) * S5_BLOCK_CH], win_ref[j])
        for sub in range(S5_BLOCK_ST // col_block):
            nat = j * S5_BLOCK_ST + sub * col_block
            cre = j * 2 * S5_BLOCK_ST + sub * col_block
            cim = cre + S5_BLOCK_ST
            lam_r = jnp.broadcast_to(lre_ref[:, nat:nat + col_block], (bsz, col_block))
            lam_i = jnp.broadcast_to(lim_ref[:, nat:nat + col_block], (bsz, col_block))
            s_r = sre_ref[:, nat:nat + col_block]
            s_i = sim_ref[:, nat:nat + col_block]
            for t in range(t_chunk):
                rows_t = slice(t * bsz, (t + 1) * bsz)
                n_r = lam_r * s_r - lam_i * s_i + hb_ref[rows_t, cre:cre + col_block]
                n_i = lam_r * s_i + lam_i * s_r + hb_ref[rows_t, cim:cim + col_block]
                hb_ref[rows_t, cre:cre + col_block] = n_r
                hb_ref[rows_t, cim:cim + col_block] = n_i
                s_r, s_i = n_r, n_i
            sre_ref[:, nat:nat + col_block] = s_r
            sim_ref[:, nat:nat + col_block] = s_i
        ys.append(_dot(hb_ref[:, cols].astype(BF16), wout_ref[j]))
    yt = jnp.concatenate(ys, axis=1).reshape(t_chunk, bsz, MIX_IN)
    y = jnp.swapaxes(yt, 0, 1) + d_ref[...] * u
    y = jax.nn.gelu(y).reshape(rows, MIX_IN)
    gate = _dot(y.astype(BF16), wglu_ref[...].astype(BF16)) + bglu_ref[...]
    out_ref[...] = (y * jax.nn.sigmoid(gate)).reshape(bsz, t_chunk, MIX_IN).astype(BF16)

    @pl.when(c == pl.num_programs(0) - 1)
    def _():
        sre_out_ref[...] = sre_ref[...]
        sim_out_ref[...] = sim_ref[...]


def _s5_mixer(u, h0_re, h0_im, lam_re, lam_im, w_in, w_out, d_skip, w_glu, b_glu, t_chunk):
    bsz, length, _ = u.shape
    rows = bsz * t_chunk
    col_block = (SUBLANES * 512) // bsz
    kern = functools.partial(_s5_kernel, bsz=bsz, t_chunk=t_chunk, col_block=col_block)
    state = jax.ShapeDtypeStruct((bsz, SSM_COLS), F32)
    return pl.pallas_call(
        kern,
        grid=(length // t_chunk,),
        in_specs=[
            pl.BlockSpec((bsz, t_chunk, MIX_IN), lambda c: (0, c, 0)),
            _full((bsz, SSM_COLS)), _full((bsz, SSM_COLS)),
            _full((1, SSM_COLS)), _full((1, SSM_COLS)),
            _full((S5_BLOCKS, S5_BLOCK_CH, 2 * S5_BLOCK_ST)),
            _full((S5_BLOCKS, 2 * S5_BLOCK_ST, S5_BLOCK_CH)),
            _full((1, MIX_IN)), _full((MIX_IN, MIX_IN)), _full((1, MIX_IN)),
        ],
        out_specs=[
            pl.BlockSpec((bsz, t_chunk, MIX_IN), lambda c: (0, c, 0)),
            _full((bsz, SSM_COLS)), _full((bsz, SSM_COLS)),
        ],
        out_shape=[jax.ShapeDtypeStruct((bsz, length, MIX_IN), BF16), state, state],
        scratch_shapes=[
            pltpu.VMEM((rows, 2 * SSM_COLS), F32),
            pltpu.VMEM((bsz, SSM_COLS), F32),
            pltpu.VMEM((bsz, SSM_COLS), F32),
        ],
        compiler_params=_params(1),
        name="s5_mixer",
    )(u, h0_re, h0_im, lam_re, lam_im, w_in, w_out, d_skip, w_glu, b_glu)


def _ffn_kernel(h_ref, mix_ref, mem_ref, wo_ref, g_ref, win_ref, cw_ref, cb_ref, w2_ref, ctx_ref,
                out_ref, new_ref, act_ref, pad_ref, carry_ref, *, seqs, rows_per_seq, tiles_per_seq):
    m = pl.program_id(0)
    tm = seqs * rows_per_seq
    h1 = h_ref[...] + _dot(mix_ref[...], wo_ref[:MIX_IN, :]) + _dot(mem_ref[...], wo_ref[MIX_IN:, :])
    out_ref[...] = h1
    xn = _rms_rows(h1, g_ref[...]).astype(BF16)

    if tiles_per_seq > 1:
        @pl.when(m == 0)
        def _():
            carry_ref[...] = jnp.zeros(carry_ref.shape, F32)

    def causal_conv(col, slot):
        cols = slice(col, col + FFN_TF)
        u = _dot(xn, win_ref[:, cols])
        ctx = ctx_ref[:, :, cols]
        if tiles_per_seq > 1:
            ctx = jnp.where(m % tiles_per_seq == 0, ctx, carry_ref[:, cols][None])
        cw = cw_ref[:, cols]
        if seqs == 1:
            row = lax.broadcasted_iota(jnp.int32, (SUBLANES, 1), 0)
            back1 = pltpu.roll(u, 1, axis=0)
            back2 = pltpu.roll(u, 2, axis=0)
            head1 = jnp.where(row == 0, ctx[0, 1:2], back1[:SUBLANES])
            head2 = jnp.where(row == 0, ctx[0, 0:1], jnp.where(row == 1, ctx[0, 1:2], back2[:SUBLANES]))
            back1 = jnp.concatenate([head1, back1[SUBLANES:]], axis=0)
            back2 = jnp.concatenate([head2, back2[SUBLANES:]], axis=0)
            y = cb_ref[:, cols] + back2 * cw[0:1] + back1 * cw[1:2] + u * cw[2:3]
            last2 = u[tm - 2:][None]
        else:
            pad = pad_ref.at[slot]
            pad[:, 6:8, :] = ctx
            pad[:, 8:, :] = u.reshape(seqs, rows_per_seq, FFN_TF)
            y = (cb_ref[:, cols] + pad[:, 6:6 + rows_per_seq, :] * cw[0:1]
                 + pad[:, 7:7 + rows_per_seq, :] * cw[1:2]
                 + pad[:, 8:8 + rows_per_seq, :] * cw[2:3]).reshape(tm, FFN_TF)
            last2 = pad[:, rows_per_seq + 6:rows_per_seq + 8, :]
        new_ref[:, :, cols] = last2
        if tiles_per_seq > 1:
            carry_ref[:, cols] = last2[0]
        return y

    for f in range(N_FF_TILES):
        ya = causal_conv(f * FFN_TF, (2 * f) % FFN_PAD_SLOTS)
        yg = causal_conv(D_FF + f * FFN_TF, (2 * f + 1) % FFN_PAD_SLOTS)
        act_ref[:, f * FFN_TF:(f + 1) * FFN_TF] = (jax.nn.silu(yg) * ya).astype(BF16)
    out_ref[...] += _dot(act_ref[...], w2_ref[...])


def _mixout_ffn(h, mix, mem, w, layer, ffn_w, ctx, rows_per_seq_tile):
    nseq, length, _ = h.shape
    m_rows = nseq * length
    if rows_per_seq_tile >= length:
        seqs, rps, tps = nseq, length, 1
    else:
        seqs, rps, tps = 1, rows_per_seq_tile, length // rows_per_seq_tile
    tm = seqs * rps
    n_m = m_rows // tm
    kern = functools.partial(_ffn_kernel, seqs=seqs, rows_per_seq=rps, tiles_per_seq=tps)
    row = lambda width: pl.BlockSpec((tm, width), lambda m: (m, 0))
    resident = lambda *shape: pl.BlockSpec(shape, lambda m: (0,) * len(shape), pipeline_mode=pl.Buffered(1))
    of_layer = lambda *shape: pl.BlockSpec((None,) + shape, lambda m: (layer,) + (0,) * len(shape),
                                           pipeline_mode=pl.Buffered(1))
    out, new = pl.pallas_call(
        kern,
        grid=(n_m,),
        in_specs=[
            row(D_MODEL), row(MIX_IN), row(MEM_WIDTH),
            resident(MIX_IN + MEM_WIDTH, D_MODEL),
            of_layer(1, D_MODEL),
            resident(D_MODEL, 2 * D_FF),
            of_layer(CONV_W, 2 * D_FF),
            of_layer(1, 2 * D_FF),
            resident(D_FF, D_MODEL),
            pl.BlockSpec((seqs, CONV_W - 1, 2 * D_FF), lambda m: (m // tps, 0, 0)),
        ],
        out_specs=[row(D_MODEL), pl.BlockSpec((seqs, CONV_W - 1, 2 * D_FF), lambda m: (m, 0, 0))],
        out_shape=[
            jax.ShapeDtypeStruct((m_rows, D_MODEL), F32),
            jax.ShapeDtypeStruct((n_m * seqs, CONV_W - 1, 2 * D_FF), F32),
        ],
        scratch_shapes=[
            pltpu.VMEM((tm, D_FF), BF16),
            pltpu.VMEM((FFN_PAD_SLOTS, seqs, rps + SUBLANES, FFN_TF) if seqs > 1 else (1, 1, SUBLANES, LANES), F32),
            pltpu.VMEM((CONV_W - 1, 2 * D_FF), F32),
        ],
        compiler_params=_params(1),
        name="mix_out_conv_ffn",
    )(h.reshape(m_rows, D_MODEL), mix.reshape(m_rows, MIX_IN), mem.reshape(m_rows, MEM_WIDTH),
      ffn_w[0], w["norm_ffn_g"], ffn_w[1], w["ffn_conv_w"], w["ffn_conv_b"], ffn_w[2], ctx)
    new_ctx = new.reshape(nseq, tps, CONV_W - 1, 2 * D_FF)[:, -1]
    return out.reshape(nseq, length, D_MODEL), new_ctx


def _dkv_kernel(h_ref, g_ref, wl_ref, wr_ref, wrr_ref, lg_ref, kg_ref, kgr_ref, cos_ref, sin_ref,
                lat_ref, kr_ref):
    xn = _rms_rows(h_ref[...], g_ref[...]).astype(BF16)
    lat_ref[...] = _rms_rows(_dot(xn, wl_ref[...]), lg_ref[...])
    kr = _dot(xn, wr_ref[...])
    kr_rot = _dot(xn, wrr_ref[...])
    r = lax.rsqrt(jnp.mean(kr * kr, axis=-1, keepdims=True) + EPS)
    kr_ref[...] = (kr * r * kg_ref[...]) * cos_ref[...] + (kr_rot * r * kgr_ref[...]) * sin_ref[...]


def _swap_halves(t, axis=-1):
    a, b = jnp.split(t, 2, axis=axis)
    return jnp.concatenate([b, a], axis=axis)


def _shared_kv_down(h, kv_norm_g, w_dkv_bf, latent_norm_g, krope_norm_g, cos_rows, sin_rows, tm):
    nseq, length, _ = h.shape
    m_rows = nseq * length
    tab_blocks = cos_rows.shape[0] // tm
    w_l = w_dkv_bf[:, :KV_LORA]
    w_r = w_dkv_bf[:, KV_LORA:]
    kg = krope_norm_g.reshape(1, ROPE_DIM)
    lat, kr = pl.pallas_call(
        _dkv_kernel,
        grid=(m_rows // tm,),
        in_specs=[
            pl.BlockSpec((tm, D_MODEL), lambda m: (m, 0)),
            _full((1, D_MODEL)),
            _full((D_MODEL, KV_LORA)), _full((D_MODEL, ROPE_DIM)), _full((D_MODEL, ROPE_DIM)),
            _full((1, KV_LORA)), _full((1, ROPE_DIM)), _full((1, ROPE_DIM)),
            pl.BlockSpec((tm, ROPE_DIM), lambda m: (m % tab_blocks, 0)),
            pl.BlockSpec((tm, ROPE_DIM), lambda m: (m % tab_blocks, 0)),
        ],
        out_specs=[pl.BlockSpec((tm, KV_LORA), lambda m: (m, 0)),
                   pl.BlockSpec((tm, ROPE_DIM), lambda m: (m, 0))],
        out_shape=[jax.ShapeDtypeStruct((m_rows, KV_LORA), F32),
                   jax.ShapeDtypeStruct((m_rows, ROPE_DIM), F32)],
        compiler_params=_params(1),
        name="shared_kv_down",
    )(h.reshape(m_rows, D_MODEL), kv_norm_g.reshape(1, D_MODEL), w_l, w_r, _swap_halves(w_r),
      latent_norm_g.reshape(1, KV_LORA), kg, _swap_halves(kg), cos_rows, sin_rows)
    return lat.reshape(nseq, length, KV_LORA), kr.reshape(nseq, length, ROPE_DIM)


def _kv_up_kernel(lat_ref, wk_ref, wv_ref, seg_ref, g_ref, k_ref, v_ref):
    lat = lat_ref[...].astype(BF16)
    k_ref[...] = _seg_rms(_dot(lat, wk_ref[...]), seg_ref[...], g_ref[...]).astype(BF16)
    v_ref[...] = _dot(lat, wv_ref[...]).astype(BF16)


def _kv_up(latent_rows, w_uk_bf, w_uv_bf, seg_nope, k_g_tiled, tr):
    rows = latent_rows.shape[0]
    width = MLA_HEADS * NOPE_DIM
    return pl.pallas_call(
        _kv_up_kernel,
        grid=(rows // tr,),
        in_specs=[
            pl.BlockSpec((tr, KV_LORA), lambda r: (r, 0)),
            _full((KV_LORA, width)), _full((KV_LORA, width)),
            _full((width, width)), _full((1, width)),
        ],
        out_specs=[pl.BlockSpec((tr, width), lambda r: (r, 0))] * 2,
        out_shape=[jax.ShapeDtypeStruct((rows, width), BF16)] * 2,
        compiler_params=_params(1),
        name="kv_up",
    )(latent_rows, w_uk_bf, w_uv_bf, seg_nope, k_g_tiled)


Q_PRESCALE = MLA_SCALE * math.log2(math.e)


def _q_kernel(z_ref, g_ref, wn_ref, wa_ref, segn_ref, segr_ref, gn_ref, ga_ref,
              cos_ref, sin_ref, kg_ref, wuk_ref, qn_ref, qr_ref, *maybe_absorbed_refs):
    xn = _rms_rows(z_ref[...], g_ref[...]).astype(BF16)
    qn = _seg_rms(_dot(xn, wn_ref[...]), segn_ref[...], gn_ref[...]) * Q_PRESCALE
    qn_ref[...] = qn.astype(BF16)
    a = _dot(xn, wa_ref[...])
    an = a * lax.rsqrt(_dot((a * a).astype(BF16), segr_ref[...]) + EPS) * ga_ref[...]
    width = an.shape[1]
    lane = lax.broadcasted_iota(jnp.int32, (1, width), 1)
    partner = jnp.where(lane % ROPE_DIM < ROPE_HALF,
                        pltpu.roll(an, width - ROPE_HALF, axis=1), pltpu.roll(an, ROPE_HALF, axis=1))
    rot = an * cos_ref[...] + partner * sin_ref[...]
    qr_ref[...] = (rot * Q_PRESCALE).astype(BF16)
    if maybe_absorbed_refs:
        qp_ref, qrh_ref = maybe_absorbed_refs
        lane = lax.broadcasted_iota(jnp.int32, (1, MLA_HEADS * NOPE_DIM), 1)
        qg = qn * kg_ref[...]
        rot_bf = (rot * Q_PRESCALE).astype(BF16)
        for head in range(MLA_HEADS):
            qh = jnp.where(lane // NOPE_DIM == head, qg, 0.0).astype(BF16)
            qp_ref[head] = _dot_nt(qh, wuk_ref[...]).astype(BF16)
            qrh_ref[head] = rot_bf[:, head * ROPE_DIM:(head + 1) * ROPE_DIM]


def _q_side(z_mix, q_latent_g, w_uq_bf, q_nope_g, q_rope_g, seg_nope, seg_rope, cos_rows, sin_rows,
            k_g_tiled, w_uk_bf, tm, absorbed):
    nseq, length, _ = z_mix.shape
    m_rows = nseq * length
    tab_blocks = cos_rows.shape[0] // tm
    wn_width = MLA_HEADS * NOPE_DIM
    wr_width = MLA_HEADS * ROPE_DIM
    w3 = w_uq_bf.reshape(MIX_IN, MLA_HEADS, NOPE_DIM + ROPE_DIM)
    w_n = w3[:, :, :NOPE_DIM].reshape(MIX_IN, wn_width)
    w_a = w3[:, :, NOPE_DIM:].reshape(MIX_IN, wr_width)
    g_n = jnp.tile(q_nope_g.reshape(1, NOPE_DIM), (1, MLA_HEADS))
    g_a = jnp.tile(q_rope_g.reshape(1, ROPE_DIM), (1, MLA_HEADS))
    cos_q = jnp.tile(cos_rows, (1, MLA_HEADS))
    sin_q = jnp.tile(sin_rows, (1, MLA_HEADS))
    out_specs = [pl.BlockSpec((tm, wd), lambda m: (m, 0)) for wd in (wn_width, wr_width)]
    out_shape = [jax.ShapeDtypeStruct((m_rows, wd), BF16) for wd in (wn_width, wr_width)]
    if absorbed:
        for wd in (KV_LORA, ROPE_DIM):
            out_specs.append(pl.BlockSpec((MLA_HEADS, tm, wd), lambda m: (0, m, 0)))
            out_shape.append(jax.ShapeDtypeStruct((MLA_HEADS, m_rows, wd), BF16))
    outs = pl.pallas_call(
        _q_kernel,
        grid=(m_rows // tm,),
        in_specs=[
            pl.BlockSpec((tm, MIX_IN), lambda m: (m, 0)),
            _full((1, MIX_IN)),
            _full((MIX_IN, wn_width)), _full((MIX_IN, wr_width)),
            _full((wn_width, wn_width)), _full((wr_width, wr_width)),
            _full((1, wn_width)), _full((1, wr_width)),
            pl.BlockSpec((tm, wr_width), lambda m: (m % tab_blocks, 0)),
            pl.BlockSpec((tm, wr_width), lambda m: (m % tab_blocks, 0)),
            _full((1, wn_width)), _full((KV_LORA, wn_width)),
        ],
        out_specs=out_specs,
        out_shape=out_shape,
        compiler_params=_params(1),
        name="mla_query",
    )(z_mix.reshape(m_rows, MIX_IN), q_latent_g.reshape(1, MIX_IN), w_n, w_a, seg_nope, seg_rope,
      g_n, g_a, cos_q, sin_q, k_g_tiled, w_uk_bf)
    return outs


ATTN_HEADS_PER_STEP = 4


def _attn_kernel(qn_ref, qr_ref, kn_ref, kr_ref, v_ref, o_ref, *, tile, n_tiles):
    qi = pl.program_id(2)
    pairs = ATTN_HEADS_PER_STEP // 2
    lane = lax.broadcasted_iota(jnp.int32, (1, LANES), 1)
    head_lanes = [(lane // NOPE_DIM) == j for j in range(2)]
    qr = qr_ref[0]
    qcat = []
    for h in range(ATTN_HEADS_PER_STEP):
        qn = qn_ref[0, :, (h // 2) * LANES:(h // 2 + 1) * LANES]
        qcat.append(jnp.concatenate(
            [jnp.where(head_lanes[h % 2], qn, jnp.zeros_like(qn)),
             jnp.where((lane // ROPE_DIM) == h, qr, jnp.zeros_like(qr))], axis=1))
    row_chunk = lax.broadcasted_iota(jnp.int32, (tile, 1), 0) // CHUNK
    col_chunk = lax.broadcasted_iota(jnp.int32, (1, tile), 1) // CHUNK
    diag_visible = col_chunk <= row_chunk

    def one_block(kb, carry, masked):
        rows_k = slice(kb * tile, (kb + 1) * tile)
        kr = kr_ref[0, rows_k, :]
        new = []
        for h in range(ATTN_HEADS_PER_STEP):
            lanes_p = slice((h // 2) * LANES, (h // 2 + 1) * LANES)
            m_i, acc = carry[h]
            kcat = jnp.concatenate([kn_ref[0, rows_k, lanes_p], kr], axis=1)
            s = _dot_nt(qcat[h], kcat)
            if masked:
                s = jnp.where(diag_visible, s, NEG_INF)
            m_new = jnp.maximum(m_i, jnp.max(s, axis=-1, keepdims=True))
            alpha = jnp.exp2(m_i - m_new)
            p = jnp.exp2(s - m_new).astype(BF16)
            vb = v_ref[0, rows_k, lanes_p]
            v_h = jnp.where(head_lanes[h % 2], vb, jnp.ones_like(vb))
            new.append((m_new, alpha * acc + _dot(p, v_h)))
        return tuple(new)

    def query_tile(n_full):
        carry = tuple((jnp.full((tile, 1), NEG_INF, F32), jnp.zeros((tile, LANES), F32))
                      for _ in range(ATTN_HEADS_PER_STEP))
        for kb in range(n_full):
            carry = one_block(kb, carry, False)
        carry = one_block(n_full, carry, True)
        for p in range(pairs):
            out = jnp.zeros((tile, LANES), F32)
            for j in range(2):
                acc = carry[2 * p + j][1]
                row_sum = pltpu.roll(acc, NOPE_DIM, axis=1)
                out = jnp.where(head_lanes[j], acc * (1.0 / row_sum), out)
            o_ref[0, :, p * LANES:(p + 1) * LANES] = out.astype(BF16)

    for c in range(n_tiles):
        pl.when(qi == c)(functools.partial(query_tile, c))


def _mla_attention(qn, qr, kn, kr4, v, tile):
    nseq, length, _ = qn.shape
    kern = functools.partial(_attn_kernel, tile=tile, n_tiles=length // tile)
    width = (ATTN_HEADS_PER_STEP // 2) * LANES
    return pl.pallas_call(
        kern,
        grid=(nseq, MLA_HEADS // ATTN_HEADS_PER_STEP, length // tile),
        in_specs=[
            pl.BlockSpec((1, tile, width), lambda b, g, i: (b, i, g)),
            pl.BlockSpec((1, tile, LANES), lambda b, g, i: (b, i, g)),
            pl.BlockSpec((1, length, width), lambda b, g, i: (b, 0, g)),
            pl.BlockSpec((1, length, LANES), lambda b, g, i: (b, 0, 0)),
            pl.BlockSpec((1, length, width), lambda b, g, i: (b, 0, g)),
        ],
        out_specs=pl.BlockSpec((1, tile, width), lambda b, g, i: (b, i, g)),
        out_shape=jax.ShapeDtypeStruct((nseq, length, MLA_HEADS * V_DIM), BF16),
        compiler_params=_params(3),
        name="mla_attention",
    )(qn, qr, kn, kr4, v)


KNORM_ROWS = 16


def _key_norm_kernel(past_ref, new_ref, wukt_ref, rt_ref, *, n_chunks, chunk):
    def norms(lat):
        keys = lat.shape[0]
        kt = _dot_nt(wukt_ref[...], lat)
        ss = jnp.sum((kt * kt).reshape(MLA_HEADS, NOPE_DIM, keys), axis=1)
        r = lax.rsqrt(ss * (1.0 / NOPE_DIM) + EPS)
        return jnp.concatenate([r, jnp.ones((KNORM_ROWS - MLA_HEADS, keys), F32)], axis=0)

    for c in range(n_chunks):
        rt_ref[0, :, c * chunk:(c + 1) * chunk] = norms(past_ref[0, c * chunk:(c + 1) * chunk, :].astype(BF16))
    rt_ref[0, :, n_chunks * chunk:] = norms(new_ref[0])


def _key_norms(past_latent, new_lat_pad, w_ukt_bf, chunk):
    nseq, past, _ = past_latent.shape
    new_pad = new_lat_pad.shape[1]
    lk_pad = past + new_pad
    kern = functools.partial(_key_norm_kernel, n_chunks=past // chunk, chunk=chunk)
    return pl.pallas_call(
        kern,
        grid=(nseq,),
        in_specs=[pl.BlockSpec((1, past, KV_LORA), lambda b: (b, 0, 0)),
                  pl.BlockSpec((1, new_pad, KV_LORA), lambda b: (b, 0, 0)),
                  _full((MLA_HEADS * NOPE_DIM, KV_LORA))],
        out_specs=pl.BlockSpec((1, KNORM_ROWS, lk_pad), lambda b: (b, 0, 0)),
        out_shape=jax.ShapeDtypeStruct((nseq, KNORM_ROWS, lk_pad), F32),
        compiler_params=_params(1),
        name="mla_key_norms",
    )(past_latent, new_lat_pad, w_ukt_bf)


def _attn_absorbed_kernel(qs_ref, qrs_ref, plat_ref, nlat_ref, pkrt_ref, nkrt_ref, rt_ref, wuv_ref, o_ref, op_ref,
                          *, tq, q_off, lk_valid):
    b = pl.program_id(0)
    nseq = pl.num_programs(0)
    rows = MLA_HEADS * tq
    lat = jnp.concatenate([plat_ref[0].astype(BF16), nlat_ref[0]], axis=0)
    kr_t = jnp.concatenate([pkrt_ref[0].astype(BF16), nkrt_ref[0]], axis=1)
    lk_pad = lat.shape[0]
    rt = rt_ref[0]
    knorm = jnp.concatenate([jnp.broadcast_to(rt[h:h + 1, :], (tq, lk_pad)) for h in range(MLA_HEADS)], axis=0)
    qs = qs_ref[...].reshape(rows, KV_LORA)
    qrs = qrs_ref[...].reshape(rows, ROPE_DIM)
    s = _dot_nt(qs, lat) * knorm + _dot(qrs, kr_t)
    k_pos = lax.broadcasted_iota(jnp.int32, (1, lk_pad), 1)
    if (lk_valid - 1) // CHUNK > q_off // CHUNK:
        q_chunk = (q_off + lax.broadcasted_iota(jnp.int32, (rows, 1), 0) % tq) // CHUNK
        s = jnp.where(k_pos // CHUNK <= q_chunk, s, NEG_INF)
    s = jnp.where(k_pos < lk_valid, s, NEG_INF)
    p = jnp.exp2(s - jnp.max(s, axis=-1, keepdims=True))
    l = jnp.sum(p, axis=-1, keepdims=True)
    op_ref[b] = (_dot(p.astype(BF16), lat) * (1.0 / l)).astype(BF16)

    @pl.when(b == nseq - 1)
    def _():
        n_all = op_ref.shape[0]
        lane_o = lax.broadcasted_iota(jnp.int32, (1, MLA_HEADS * V_DIM), 1)
        out = jnp.zeros((n_all * tq, MLA_HEADS * V_DIM), F32)
        for h in range(MLA_HEADS):
            x = op_ref[:, h * tq:(h + 1) * tq, :].reshape(n_all * tq, KV_LORA)
            out = jnp.where(lane_o // V_DIM == h, _dot(x, wuv_ref[...]), out)
        o_ref[...] = out.astype(BF16)


def _mla_attention_absorbed(qp, qrh, past_latent, new_lat_pad, past_krope_t, new_kr_pad_t, r_t, w_uv_bf,
                            tq, lk_valid):
    nseq, past, _ = past_latent.shape
    new_pad = new_lat_pad.shape[1]
    kern = functools.partial(_attn_absorbed_kernel, tq=tq, q_off=past, lk_valid=lk_valid)
    per_seq = lambda n, width: pl.BlockSpec((1, n, width), lambda b: (b, 0, 0))
    heads_of_seq = lambda width: pl.BlockSpec((MLA_HEADS, tq, width), lambda b: (0, b, 0))
    out = pl.pallas_call(
        kern,
        grid=(nseq,),
        in_specs=[
            heads_of_seq(KV_LORA), heads_of_seq(ROPE_DIM),
            per_seq(past, KV_LORA), per_seq(new_pad, KV_LORA),
            per_seq(ROPE_DIM, past), per_seq(ROPE_DIM, new_pad),
            pl.BlockSpec((1, KNORM_ROWS, past + new_pad), lambda b: (b, 0, 0)),
            _full((KV_LORA, MLA_HEADS * V_DIM)),
        ],
        out_specs=_full((nseq * tq, MLA_HEADS * V_DIM)),
        out_shape=jax.ShapeDtypeStruct((nseq * tq, MLA_HEADS * V_DIM), BF16),
        scratch_shapes=[pltpu.VMEM((nseq, MLA_HEADS * tq, KV_LORA), BF16)],
        compiler_params=_params(1),
        name="mla_attention_absorbed",
    )(qp, qrh, past_latent, new_lat_pad, past_krope_t, new_kr_pad_t, r_t, w_uv_bf)
    return out.reshape(nseq, tq, MLA_HEADS * V_DIM)


def _trunk(x, mem_k_bf, mem_v_bf, ssm_h0_re, ssm_h0_im, conv_ctx, past_latent, past_krope, w, cfg, ffn_bf=None):
    nseq, length, _ = x.shape
    past = 0 if past_latent is None else past_latent.shape[1]
    cos_t, sin_t = _rope_tables(past, length)
    reps = cfg["rope_rows"] // length
    cos_rows = jnp.tile(cos_t, (reps, 1))
    sin_rows = jnp.tile(sin_t, (reps, 1))
    if conv_ctx is None:
        conv_ctx = jnp.zeros((DEPTH, nseq, CONV_W - 1, 2 * D_FF), F32)
    if ssm_h0_re is None:
        ssm_h0_re = jnp.zeros((N_A_LAYERS, nseq, SSM_GROUPS, SSM_STATE), F32)
        ssm_h0_im = ssm_h0_re
    h = x
    ssm_re_out, ssm_im_out, conv_out = [], [], []
    make_ffn_bf = ffn_bf is None
    if make_ffn_bf:
        ffn_bf = []
    for layer in range(DEPTH):
        to_cast = (w["w_mix_out_f32"], w["w_ffn_in_f32"], w["w_ffn_out_f32"]) if make_ffn_bf else ()
        z_mix, mem_out, cast = _mixin(h, w, layer, mem_k_bf, mem_v_bf, cfg["tm_mixin"], to_cast)
        if make_ffn_bf:
            ffn_bf.append(cast)
        if layer < N_A_LAYERS:
            i = layer
            mix_out, s_re, s_im = _s5_mixer(
                z_mix, ssm_h0_re[i].reshape(nseq, SSM_COLS), ssm_h0_im[i].reshape(nseq, SSM_COLS),
                w["lam_re"][i], w["lam_im"][i], w["s5_in"][i], w["s5_out"][i],
                w["ssm_d"][i][None], w["w_glu"][i], w["b_glu"][i][None], cfg["t_chunk"])
            ssm_re_out.append(s_re.reshape(nseq, SSM_GROUPS, SSM_STATE))
            ssm_im_out.append(s_im.reshape(nseq, SSM_GROUPS, SSM_STATE))
        else:
            if layer == N_A_LAYERS:
                new_latent, new_krope = _shared_kv_down(
                    h, w["kv_norm_g"], w["w_dkv"], w["latent_norm_g"], w["krope_norm_g"],
                    cos_rows, sin_rows, cfg["tm_rows"])
                if cfg["absorbed"]:
                    assert past % LANES == 0
                    lk_valid = past + length
                    new_pad = -(-length // LANES) * LANES
                    pad_rows = lambda t: jnp.pad(t.astype(BF16), ((0, 0), (0, new_pad - length), (0, 0)))
                    new_lat_pad = pad_rows(new_latent)
                    new_kr_pad_t = jnp.swapaxes(pad_rows(new_krope), 1, 2)
                    past_krope_t = jnp.swapaxes(past_krope, 1, 2)
                    r_t = _key_norms(past_latent, new_lat_pad, w["w_uk_t"], cfg["knorm_chunk"])
                else:
                    assert past_latent is None
                    kn, v_all = _kv_up(new_latent.reshape(nseq * length, KV_LORA), w["w_uk"], w["w_uv"],
                                       w["seg_nope"], w["k_nope_g"], cfg["tr_kv"])
                    kn = kn.reshape(nseq, length, MLA_HEADS * NOPE_DIM)
                    v_all = v_all.reshape(nseq, length, MLA_HEADS * V_DIM)
                    kr4 = jnp.tile(new_krope.astype(BF16), (1, 1, LANES // ROPE_DIM))
            j = layer - N_A_LAYERS
            q_out = _q_side(z_mix, w["q_latent_norm_g"][j], w["w_uq"][j], w["q_nope_norm_g"][j],
                            w["q_rope_norm_g"][j], w["seg_nope"], w["seg_rope"], cos_rows, sin_rows,
                            w["k_nope_g"], w["w_uk"], cfg["tm_rows"], cfg["absorbed"])
            if cfg["absorbed"]:
                mix_out = _mla_attention_absorbed(q_out[2], q_out[3], past_latent, new_lat_pad, past_krope_t,
                                                  new_kr_pad_t, r_t, w["w_uv"], length, lk_valid)
            else:
                qn = q_out[0].reshape(nseq, length, MLA_HEADS * NOPE_DIM)
                qr = q_out[1].reshape(nseq, length, MLA_HEADS * ROPE_DIM)
                mix_out = _mla_attention(qn, qr, kn, kr4, v_all, cfg["tq"])
        h, ctx = _mixout_ffn(h, mix_out, mem_out, w, layer, ffn_bf[layer], conv_ctx[layer], cfg["ffn_rows"])
        conv_out.append(ctx)
    return (h, new_latent, new_krope, jnp.stack(ssm_re_out), jnp.stack(ssm_im_out), jnp.stack(conv_out)), ffn_bf


PROMPT_CFG = dict(tm_mixin=1024, t_chunk=64, ffn_rows=512, tm_rows=1024, rope_rows=2048, tr_kv=1024,
                  tq=512, absorbed=False)
SAMPLE_CFG = dict(tm_mixin=32, t_chunk=32, ffn_rows=32, tm_rows=512, rope_rows=512, knorm_chunk=1024,
                  absorbed=True)


def kernel(x_prompt, x_sample, cache_mla_latent, cache_mla_krope, cache_mem_k, cache_mem_v, state_ssm_re, state_ssm_im, state_conv, mem_prompt, norm_mix_g, w_mix_in, w_mix_out, norm_ffn_g, w_ffn_in, ffn_conv_w, ffn_conv_b, w_ffn_out, mem_norm_g, w_mem_kv, mem_q_norm_g, mem_k_norm_g, ssm_a_re, ssm_a_im, ssm_log_dt, ssm_b_re, ssm_b_im, ssm_c_re, ssm_c_im, ssm_d, w_glu, b_glu, kv_norm_g, w_dkv, latent_norm_g, krope_norm_g, w_uk, w_uv, k_nope_norm_g, q_latent_norm_g, w_uq, q_nope_norm_g, q_rope_norm_g):
    bf = lambda t: t.astype(BF16)
    seg_mem = _seg_matrix(MEM_WIDTH, MEM_HEAD_DIM)
    lam_re, lam_im, s5_in, s5_out = _s5_prepare(ssm_a_re, ssm_a_im, ssm_log_dt, ssm_b_re, ssm_b_im,
                                                ssm_c_re, ssm_c_im)
    w = dict(
        norm_mix_g=norm_mix_g.reshape(DEPTH, 1, D_MODEL), w_mix_in=w_mix_in,
        norm_ffn_g=norm_ffn_g.reshape(DEPTH, 1, D_MODEL),
        ffn_conv_w=ffn_conv_w, ffn_conv_b=ffn_conv_b.reshape(DEPTH, 1, 2 * D_FF),
        w_mix_out_f32=w_mix_out, w_ffn_in_f32=w_ffn_in, w_ffn_out_f32=w_ffn_out,
        mem_q_g=jnp.tile(mem_q_norm_g, (1, MEM_HEADS)).reshape(DEPTH, 1, MEM_WIDTH),
        seg_mem=seg_mem,
        seg_nope=_seg_matrix(MLA_HEADS * NOPE_DIM, NOPE_DIM),
        seg_rope=_seg_matrix(MLA_HEADS * ROPE_DIM, ROPE_DIM),
        lam_re=lam_re, lam_im=lam_im, s5_in=s5_in, s5_out=s5_out,
        ssm_d=ssm_d, w_glu=w_glu, b_glu=b_glu,
        kv_norm_g=kv_norm_g, w_dkv=bf(w_dkv), latent_norm_g=latent_norm_g, krope_norm_g=krope_norm_g,
        w_uk=bf(w_uk), w_uk_t=bf(w_uk).T, w_uv=bf(w_uv),
        k_nope_g=jnp.tile(k_nope_norm_g.reshape(1, NOPE_DIM), (1, MLA_HEADS)),
        q_latent_norm_g=q_latent_norm_g, w_uq=bf(w_uq), q_nope_norm_g=q_nope_norm_g,
        q_rope_norm_g=q_rope_norm_g,
    )
    bsz = mem_prompt.shape[0]
    mem_k_p, mem_v_p, mem_k_bf, mem_v_bf = _memory_kv(mem_prompt, mem_norm_g, w_mem_kv, mem_k_norm_g, seg_mem)
    (y_prompt, lat_p, krope_p, ssm_re_p, ssm_im_p, conv_p), ffn_bf = _trunk(
        x_prompt, mem_k_bf, mem_v_bf, None, None, None, None, None, w, PROMPT_CFG)
    dec = cache_mem_k.shape[1]
    (y_sample, lat_s, krope_s, ssm_re_s, ssm_im_s, conv_s), _ = _trunk(
        x_sample, bf(cache_mem_k).reshape(DEPTH, dec, N_MEM, MEM_WIDTH),
        bf(cache_mem_v).reshape(DEPTH, dec, N_MEM, MEM_WIDTH),
        state_ssm_re, state_ssm_im, state_conv, cache_mla_latent, cache_mla_krope, w, SAMPLE_CFG, ffn_bf)
    shape5 = (DEPTH, bsz, N_MEM, MEM_HEADS, MEM_HEAD_DIM)
    return (y_prompt, y_sample, mem_k_p.reshape(shape5), mem_v_p.reshape(shape5), lat_p, krope_p,
            ssm_re_p, ssm_im_p, conv_p, lat_s, krope_s, ssm_re_s, ssm_im_s, conv_s)
```

```python
import functools
import math

import jax
import jax.numpy as jnp
from jax import lax
from jax.experimental import pallas as pl
from jax.experimental.pallas import tpu as pltpu

F32 = jnp.float32
BF16 = jnp.bfloat16

D_MODEL = 1024
DEPTH = 4
CHUNK = 64
N_A_LAYERS = DEPTH // 2
N_B_LAYERS = DEPTH - N_A_LAYERS
MIX_IN = 768
MEM_HEADS = 4
MEM_HEAD_DIM = 64
MEM_WIDTH = MEM_HEADS * MEM_HEAD_DIM
N_MEM = 256
SSM_GROUP = 16
SSM_GROUPS = MIX_IN // SSM_GROUP
SSM_STATE = 64
SSM_COLS = SSM_GROUPS * SSM_STATE
MLA_HEADS = 12
NOPE_DIM = 64
ROPE_DIM = 32
ROPE_HALF = ROPE_DIM // 2
V_DIM = 64
KV_LORA = 256
ROPE_BASE = 10000.0
MLA_SCALE = (NOPE_DIM + ROPE_DIM) ** -0.5
MEM_SCALE = MEM_HEAD_DIM ** -0.5
D_FF = 2816
CONV_W = 3
EPS = 1e-6
NEG_INF = -1e30

V7X_VMEM_LIMIT_BYTES = 56 * 1024 * 1024
LANES = 128
SUBLANES = 8
V7X_MXU_DIM = 256

S5_BLOCKS = 3
S5_BLOCK_CH = MIX_IN // S5_BLOCKS
S5_BLOCK_ST = SSM_COLS // S5_BLOCKS
FFN_TF = 256
N_FF_TILES = D_FF // FFN_TF
FFN_PAD_SLOTS = 4


def _params(n_axes):
    return pltpu.CompilerParams(
        dimension_semantics=("arbitrary",) * n_axes,
        vmem_limit_bytes=V7X_VMEM_LIMIT_BYTES,
    )


def _dot(a, b):
    return jnp.dot(a, b, preferred_element_type=F32)


def _dot_nt(a, b):
    return lax.dot_general(a, b, (((1,), (1,)), ((), ())), preferred_element_type=F32)


def _two_row_halves(dot_fn, a, b):
    rows = a.shape[0]
    if rows < V7X_MXU_DIM or rows % 32:
        return dot_fn(a, b)
    return jnp.concatenate([dot_fn(a[:rows // 2], b), dot_fn(a[rows // 2:], b)], axis=0)


def _rms_rows(x, g):
    ms = jnp.mean(x * x, axis=-1, keepdims=True)
    return x * lax.rsqrt(ms + EPS) * g


def _seg_rms(x, seg_mat, g):
    ms = _dot((x * x).astype(BF16), seg_mat)
    return x * lax.rsqrt(ms + EPS) * g


def _seg_matrix(width, seg):
    idx = jnp.arange(width) // seg
    return jnp.where(idx[:, None] == idx[None, :], 1.0 / seg, 0.0).astype(BF16)


def _full(shape):
    nd = len(shape)
    return pl.BlockSpec(shape, lambda *_: (0,) * nd)


def _rope_table_kernel(pos_ref, inv_ref, cos_ref, sin_ref):
    ang = pos_ref[...] * inv_ref[...]
    cos_ref[...] = jnp.cos(ang)
    sin_ref[...] = jnp.sin(ang)


def _rope_tables(past, length):
    pos = (past + jnp.arange(length, dtype=jnp.int32)).astype(F32)[:, None]
    inv_freq = (1.0 / (ROPE_BASE ** (jnp.arange(0, ROPE_DIM, 2, dtype=F32) / ROPE_DIM)))[None, :]
    cos, sin = pl.pallas_call(
        _rope_table_kernel,
        out_shape=[jax.ShapeDtypeStruct((length, ROPE_HALF), F32)] * 2,
        name="rope_table",
    )(pos, inv_freq)
    cos_t = jnp.concatenate([cos, cos], axis=1)
    sin_t = jnp.concatenate([-sin, sin], axis=1)
    return cos_t, sin_t


def _memkv_kernel(mem_ref, g_ref, w_ref, kg_ref, seg_ref, k_ref, v_ref, kb_ref, vb_ref):
    xn = _rms_rows(mem_ref[0], g_ref[0]).astype(BF16)
    kv = _dot(xn, w_ref[0].astype(BF16))
    k = _seg_rms(kv[:, :MEM_WIDTH], seg_ref[...], kg_ref[0])
    v = kv[:, MEM_WIDTH:]
    k_ref[0, 0] = k
    v_ref[0, 0] = v
    kb_ref[0, 0] = k.astype(BF16)
    vb_ref[0, 0] = v.astype(BF16)


def _memory_kv(mem, mem_norm_g, w_mem_kv, mem_k_norm_g, seg_mem):
    bsz = mem.shape[0]
    kg = jnp.tile(mem_k_norm_g, (1, MEM_HEADS)).reshape(DEPTH, 1, MEM_WIDTH)
    out4 = lambda dt: jax.ShapeDtypeStruct((DEPTH, bsz, N_MEM, MEM_WIDTH), dt)
    spec4 = pl.BlockSpec((1, 1, N_MEM, MEM_WIDTH), lambda l, b: (l, b, 0, 0))
    return pl.pallas_call(
        _memkv_kernel,
        grid=(DEPTH, bsz),
        in_specs=[
            pl.BlockSpec((1, N_MEM, D_MODEL), lambda l, b: (b, 0, 0)),
            pl.BlockSpec((1, 1, D_MODEL), lambda l, b: (l, 0, 0)),
            pl.BlockSpec((1, D_MODEL, 2 * MEM_WIDTH), lambda l, b: (l, 0, 0)),
            pl.BlockSpec((1, 1, MEM_WIDTH), lambda l, b: (l, 0, 0)),
            _full((MEM_WIDTH, MEM_WIDTH)),
        ],
        out_specs=[spec4, spec4, spec4, spec4],
        out_shape=[out4(F32), out4(F32), out4(BF16), out4(BF16)],
        compiler_params=_params(2),
        name="memory_kv",
    )(mem, mem_norm_g.reshape(DEPTH, 1, D_MODEL), w_mem_kv, kg, seg_mem)


def _mixin_kernel(h_ref, g_ref, w_ref, qg_ref, seg_ref, k_ref, v_ref, *rest, seqs, rows_per_seq, n_cast):
    cast_in, (zmix_ref, mem_ref), cast_out = rest[:n_cast], rest[n_cast:n_cast + 2], rest[n_cast + 2:]
    for src, dst in zip(cast_in, cast_out):
        dst[...] = src[...].astype(BF16)
    xn = _rms_rows(h_ref[...], g_ref[...]).astype(BF16)
    z = _dot(xn, w_ref[...].astype(BF16))
    zmix_ref[...] = z[:, :MIX_IN]
    mq = _seg_rms(z[:, MIX_IN:], seg_ref[...], qg_ref[...]).astype(BF16)
    lane = lax.broadcasted_iota(jnp.int32, (1, MEM_WIDTH), 1)
    in_head = [(lane // MEM_HEAD_DIM) == head for head in range(MEM_HEADS)]
    for b in range(seqs):
        rows = slice(b * rows_per_seq, (b + 1) * rows_per_seq)
        mq_b = mq[rows]
        qs = jnp.concatenate([jnp.where(m, mq_b, jnp.zeros_like(mq_b)) for m in in_head], axis=0)
        s = _two_row_halves(_dot_nt, qs, k_ref[b]) * MEM_SCALE
        p = jnp.exp(s - jnp.max(s, axis=-1, keepdims=True))
        p = p * (1.0 / jnp.sum(p, axis=-1, keepdims=True))
        o = _two_row_halves(_dot, p.astype(BF16), v_ref[b])
        out = jnp.zeros(mq_b.shape, F32)
        for head, m in enumerate(in_head):
            out = jnp.where(m, o[head * rows_per_seq:(head + 1) * rows_per_seq], out)
        mem_ref[rows, :] = out.astype(BF16)


def _mixin(h, w, layer, k_bf, v_bf, rows_per_seq_tile, to_cast=()):
    nseq, length, _ = h.shape
    m_rows = nseq * length
    if rows_per_seq_tile >= length:
        seqs, rps, tps = nseq, length, 1
    else:
        seqs, rps, tps = 1, rows_per_seq_tile, length // rows_per_seq_tile
    tm = seqs * rps
    n_steps = m_rows // tm
    kern = functools.partial(_mixin_kernel, seqs=seqs, rows_per_seq=rps, n_cast=len(to_cast))
    row = lambda width: pl.BlockSpec((tm, width), lambda m: (m, 0))
    of_layer = lambda *shape: pl.BlockSpec((None,) + shape, lambda m: (layer,) + (0,) * len(shape))
    kv_spec = pl.BlockSpec((None, seqs, N_MEM, MEM_WIDTH), lambda m: (layer, m // tps, 0, 0))
    slab = lambda t: t.shape[1] // n_steps
    cast_in = [pl.BlockSpec((None, slab(t), t.shape[2]), lambda m: (layer, m, 0)) for t in to_cast]
    cast_out = [pl.BlockSpec((slab(t), t.shape[2]), lambda m: (m, 0)) for t in to_cast]
    outs = pl.pallas_call(
        kern,
        grid=(n_steps,),
        in_specs=[
            row(D_MODEL),
            of_layer(1, D_MODEL),
            of_layer(D_MODEL, MIX_IN + MEM_WIDTH),
            of_layer(1, MEM_WIDTH),
            _full((MEM_WIDTH, MEM_WIDTH)),
            kv_spec, kv_spec,
        ] + cast_in,
        out_specs=[row(MIX_IN), row(MEM_WIDTH)] + cast_out,
        out_shape=[
            jax.ShapeDtypeStruct((m_rows, MIX_IN), F32),
            jax.ShapeDtypeStruct((m_rows, MEM_WIDTH), BF16),
        ] + [jax.ShapeDtypeStruct(t.shape[1:], BF16) for t in to_cast],
        compiler_params=_params(1),
        name="mix_in_mem_attn",
    )(h.reshape(m_rows, D_MODEL), w["norm_mix_g"], w["w_mix_in"], w["mem_q_g"], w["seg_mem"], k_bf, v_bf,
      *to_cast)
    return outs[0].reshape(nseq, length, MIX_IN), outs[1].reshape(nseq, length, MEM_WIDTH), outs[2:]


def _s5_discretise_kernel(are_ref, aim_ref, ldt_ref, lre_ref, lim_ref, fre_ref, fim_ref):
    a_re, a_im = are_ref[...], aim_ref[...]
    dt = jnp.exp(ldt_ref[...])
    mag = jnp.exp(a_re * dt)
    lam_re = mag * jnp.cos(a_im * dt)
    lam_im = mag * jnp.sin(a_im * dt)
    den = a_re * a_re + a_im * a_im
    x_re = lam_re - 1.0
    lre_ref[...] = lam_re
    lim_ref[...] = lam_im
    fre_ref[...] = (x_re * a_re + lam_im * a_im) / den
    fim_ref[...] = (lam_im * a_re - x_re * a_im) / den


def _s5_input_scale_kernel(fre_ref, fim_ref, bre_ref, bim_ref, bbre_ref, bbim_ref):
    f_re, f_im = fre_ref[0], fim_ref[0]
    b_re, b_im = bre_ref[0], bim_ref[0]
    bbre_ref[0] = f_re * b_re - f_im * b_im
    bbim_ref[0] = f_re * b_im + f_im * b_re


def _s5_prepare(a_re, a_im, log_dt, b_re, b_im, c_re, c_im):
    n = a_re.shape[0]
    dense = lambda t: t.reshape(n, SSM_COLS // LANES, LANES)
    ldt = jnp.broadcast_to(log_dt[:, :, None], (n, SSM_GROUPS, SSM_STATE))
    lam_re, lam_im, f_re, f_im = pl.pallas_call(
        _s5_discretise_kernel,
        out_shape=[jax.ShapeDtypeStruct((n, SSM_COLS // LANES, LANES), F32)] * 4,
        name="s5_discretise",
    )(dense(a_re), dense(a_im), dense(ldt))
    col = lambda t: t.reshape(n, SSM_COLS, 1)
    col_spec = pl.BlockSpec((1, SSM_COLS, 1), lambda l: (l, 0, 0))
    b_spec = pl.BlockSpec((1, SSM_COLS, SSM_GROUP), lambda l: (l, 0, 0))
    bb_re, bb_im = pl.pallas_call(
        _s5_input_scale_kernel,
        grid=(n,),
        in_specs=[col_spec, col_spec, b_spec, b_spec],
        out_specs=[b_spec, b_spec],
        out_shape=[jax.ShapeDtypeStruct((n, SSM_COLS, SSM_GROUP), F32)] * 2,
        compiler_params=_params(1),
        name="s5_input_scale",
    )(col(f_re), col(f_im), b_re.reshape(n, SSM_COLS, SSM_GROUP), b_im.reshape(n, SSM_COLS, SSM_GROUP))
    gpb = SSM_GROUPS // S5_BLOCKS

    def block_diagonal(t, rows_per_group, cols_per_group):
        rows = gpb * rows_per_group
        same = (jnp.arange(rows) // rows_per_group)[:, None] == jnp.arange(gpb)[None, :]
        wide = jnp.broadcast_to(t[:, :, :, None, :], (n, S5_BLOCKS, rows, gpb, cols_per_group))
        return jnp.where(same[None, None, :, :, None], wide, 0.0).reshape(
            n, S5_BLOCKS, rows, gpb * cols_per_group)

    def in_blocks(bb):
        t = bb.reshape(n, S5_BLOCKS, gpb, SSM_STATE, SSM_GROUP).transpose(0, 1, 2, 4, 3)
        return block_diagonal(t.reshape(n, S5_BLOCKS, S5_BLOCK_CH, SSM_STATE), SSM_GROUP, SSM_STATE)

    def out_blocks(c):
        t = c.reshape(n, S5_BLOCKS, gpb, SSM_GROUP, SSM_STATE).transpose(0, 1, 2, 4, 3)
        return block_diagonal(t.reshape(n, S5_BLOCKS, S5_BLOCK_ST, SSM_GROUP), SSM_STATE, SSM_GROUP)

    w_in = jnp.concatenate([in_blocks(bb_re), in_blocks(bb_im)], axis=-1).astype(BF16)
    w_out = jnp.concatenate([out_blocks(c_re), -out_blocks(c_im)], axis=-2).astype(BF16)
    return lam_re.reshape(n, 1, SSM_COLS), lam_im.reshape(n, 1, SSM_COLS), w_in, w_out


def _s5_kernel(u_ref, h0re_ref, h0im_ref, lre_ref, lim_ref, win_ref, wout_ref, d_ref, wglu_ref, bglu_ref,
               out_ref, sre_out_ref, sim_out_ref, hb_ref, sre_ref, sim_ref, *, bsz, t_chunk, col_block):
    c = pl.program_id(0)
    rows = bsz * t_chunk

    @pl.when(c == 0)
    def _():
        sre_ref[...] = h0re_ref[...]
        sim_ref[...] = h0im_ref[...]

    u = u_ref[...]
    ut = jnp.swapaxes(u, 0, 1).reshape(rows, MIX_IN).astype(BF16)
    ys = []
    for j in range(S5_BLOCKS):
        cols = slice(j * 2 * S5_BLOCK_ST, (j + 1) * 2 * S5_BLOCK_ST)
        hb_ref[:, cols] = _dot(ut[:, j * S5_BLOCK_CH:(j + 1) * S5_BLOCK_CH], win_ref[j])
        for sub in range(S5_BLOCK_ST // col_block):
            nat = j * S5_BLOCK_ST + sub * col_block
            cre = j * 2 * S5_BLOCK_ST + sub * col_block
            cim = cre + S5_BLOCK_ST
            lam_r = jnp.broadcast_to(lre_ref[:, nat:nat + col_block], (bsz, col_block))
            lam_i = jnp.broadcast_to(lim_ref[:, nat:nat + col_block], (bsz, col_block))
            s_r = sre_ref[:, nat:nat + col_block]
            s_i = sim_ref[:, nat:nat + col_block]
            for t in range(t_chunk):
                rows_t = slice(t * bsz, (t + 1) * bsz)
                n_r = lam_r * s_r - lam_i * s_i + hb_ref[rows_t, cre:cre + col_block]
                n_i = lam_r * s_i + lam_i * s_r + hb_ref[rows_t, cim:cim + col_block]
                hb_ref[rows_t, cre:cre + col_block] = n_r
                hb_ref[rows_t, cim:cim + col_block] = n_i
                s_r, s_i = n_r, n_i
            sre_ref[:, nat:nat + col_block] = s_r
            sim_ref[:, nat:nat + col_block] = s_i
        ys.append(_two_row_halves(_dot, hb_ref[:, cols].astype(BF16), wout_ref[j]))
    yt = jnp.concatenate(ys, axis=1).reshape(t_chunk, bsz, MIX_IN)
    y = jnp.swapaxes(yt, 0, 1) + d_ref[...] * u
    y = jax.nn.gelu(y).reshape(rows, MIX_IN)
    gate = _dot(y.astype(BF16), wglu_ref[...].astype(BF16)) + bglu_ref[...]
    out_ref[...] = (y * jax.nn.sigmoid(gate)).reshape(bsz, t_chunk, MIX_IN).astype(BF16)

    @pl.when(c == pl.num_programs(0) - 1)
    def _():
        sre_out_ref[...] = sre_ref[...]
        sim_out_ref[...] = sim_ref[...]


def _s5_mixer(u, h0_re, h0_im, lam_re, lam_im, w_in, w_out, d_skip, w_glu, b_glu, t_chunk):
    bsz, length, _ = u.shape
    rows = bsz * t_chunk
    col_block = (SUBLANES * 512) // bsz
    kern = functools.partial(_s5_kernel, bsz=bsz, t_chunk=t_chunk, col_block=col_block)
    state = jax.ShapeDtypeStruct((bsz, SSM_COLS), F32)
    return pl.pallas_call(
        kern,
        grid=(length // t_chunk,),
        in_specs=[
            pl.BlockSpec((bsz, t_chunk, MIX_IN), lambda c: (0, c, 0)),
            _full((bsz, SSM_COLS)), _full((bsz, SSM_COLS)),
            _full((1, SSM_COLS)), _full((1, SSM_COLS)),
            _full((S5_BLOCKS, S5_BLOCK_CH, 2 * S5_BLOCK_ST)),
            _full((S5_BLOCKS, 2 * S5_BLOCK_ST, S5_BLOCK_CH)),
            _full((1, MIX_IN)), _full((MIX_IN, MIX_IN)), _full((1, MIX_IN)),
        ],
        out_specs=[
            pl.BlockSpec((bsz, t_chunk, MIX_IN), lambda c: (0, c, 0)),
            _full((bsz, SSM_COLS)), _full((bsz, SSM_COLS)),
        ],
        out_shape=[jax.ShapeDtypeStruct((bsz, length, MIX_IN), BF16), state, state],
        scratch_shapes=[
            pltpu.VMEM((rows, 2 * SSM_COLS), F32),
            pltpu.VMEM((bsz, SSM_COLS), F32),
            pltpu.VMEM((bsz, SSM_COLS), F32),
        ],
        compiler_params=_params(1),
        name="s5_mixer",
    )(u, h0_re, h0_im, lam_re, lam_im, w_in, w_out, d_skip, w_glu, b_glu)


def _ffn_kernel(h_ref, mix_ref, mem_ref, wo_ref, g_ref, win_ref, cw_ref, cb_ref, w2_ref, ctx_ref,
                out_ref, new_ref, act_ref, pad_ref, carry_ref, *, seqs, rows_per_seq, tiles_per_seq):
    m = pl.program_id(0)
    tm = seqs * rows_per_seq
    h1 = h_ref[...] + _dot(mix_ref[...], wo_ref[:MIX_IN, :]) + _dot(mem_ref[...], wo_ref[MIX_IN:, :])
    out_ref[...] = h1
    xn = _rms_rows(h1, g_ref[...]).astype(BF16)

    if tiles_per_seq > 1:
        @pl.when(m == 0)
        def _():
            carry_ref[...] = jnp.zeros(carry_ref.shape, F32)

    def causal_conv(col, slot):
        cols = slice(col, col + FFN_TF)
        u = _dot(xn, win_ref[:, cols])
        ctx = ctx_ref[:, :, cols]
        if tiles_per_seq > 1:
            ctx = jnp.where(m % tiles_per_seq == 0, ctx, carry_ref[:, cols][None])
        cw = cw_ref[:, cols]
        if seqs == 1:
            row = lax.broadcasted_iota(jnp.int32, (SUBLANES, 1), 0)
            back1 = pltpu.roll(u, 1, axis=0)
            back2 = pltpu.roll(u, 2, axis=0)
            head1 = jnp.where(row == 0, ctx[0, 1:2], back1[:SUBLANES])
            head2 = jnp.where(row == 0, ctx[0, 0:1], jnp.where(row == 1, ctx[0, 1:2], back2[:SUBLANES]))
            back1 = jnp.concatenate([head1, back1[SUBLANES:]], axis=0)
            back2 = jnp.concatenate([head2, back2[SUBLANES:]], axis=0)
            y = cb_ref[:, cols] + back2 * cw[0:1] + back1 * cw[1:2] + u * cw[2:3]
            last2 = u[tm - 2:][None]
        else:
            pad = pad_ref.at[slot]
            pad[:, 6:8, :] = ctx
            pad[:, 8:, :] = u.reshape(seqs, rows_per_seq, FFN_TF)
            y = (cb_ref[:, cols] + pad[:, 6:6 + rows_per_seq, :] * cw[0:1]
                 + pad[:, 7:7 + rows_per_seq, :] * cw[1:2]
                 + pad[:, 8:8 + rows_per_seq, :] * cw[2:3]).reshape(tm, FFN_TF)
            last2 = pad[:, rows_per_seq + 6:rows_per_seq + 8, :]
        new_ref[:, :, cols] = last2
        if tiles_per_seq > 1:
            carry_ref[:, cols] = last2[0]
        return y

    for f in range(N_FF_TILES):
        ya = causal_conv(f * FFN_TF, (2 * f) % FFN_PAD_SLOTS)
        yg = causal_conv(D_FF + f * FFN_TF, (2 * f + 1) % FFN_PAD_SLOTS)
        act_ref[:, f * FFN_TF:(f + 1) * FFN_TF] = (jax.nn.silu(yg) * ya).astype(BF16)
    out_ref[...] += _dot(act_ref[...], w2_ref[...])


def _mixout_ffn(h, mix, mem, w, layer, ffn_w, ctx, rows_per_seq_tile):
    nseq, length, _ = h.shape
    m_rows = nseq * length
    if rows_per_seq_tile >= length:
        seqs, rps, tps = nseq, length, 1
    else:
        seqs, rps, tps = 1, rows_per_seq_tile, length // rows_per_seq_tile
    tm = seqs * rps
    n_m = m_rows // tm
    kern = functools.partial(_ffn_kernel, seqs=seqs, rows_per_seq=rps, tiles_per_seq=tps)
    row = lambda width: pl.BlockSpec((tm, width), lambda m: (m, 0))
    resident = lambda *shape: pl.BlockSpec(shape, lambda m: (0,) * len(shape), pipeline_mode=pl.Buffered(1))
    of_layer = lambda *shape: pl.BlockSpec((None,) + shape, lambda m: (layer,) + (0,) * len(shape),
                                           pipeline_mode=pl.Buffered(1))
    out, new = pl.pallas_call(
        kern,
        grid=(n_m,),
        in_specs=[
            row(D_MODEL), row(MIX_IN), row(MEM_WIDTH),
            resident(MIX_IN + MEM_WIDTH, D_MODEL),
            of_layer(1, D_MODEL),
            resident(D_MODEL, 2 * D_FF),
            of_layer(CONV_W, 2 * D_FF),
            of_layer(1, 2 * D_FF),
            resident(D_FF, D_MODEL),
            pl.BlockSpec((seqs, CONV_W - 1, 2 * D_FF), lambda m: (m // tps, 0, 0)),
        ],
        out_specs=[row(D_MODEL), pl.BlockSpec((seqs, CONV_W - 1, 2 * D_FF), lambda m: (m, 0, 0))],
        out_shape=[
            jax.ShapeDtypeStruct((m_rows, D_MODEL), F32),
            jax.ShapeDtypeStruct((n_m * seqs, CONV_W - 1, 2 * D_FF), F32),
        ],
        scratch_shapes=[
            pltpu.VMEM((tm, D_FF), BF16),
            pltpu.VMEM((FFN_PAD_SLOTS, seqs, rps + SUBLANES, FFN_TF) if seqs > 1 else (1, 1, SUBLANES, LANES), F32),
            pltpu.VMEM((CONV_W - 1, 2 * D_FF), F32),
        ],
        compiler_params=_params(1),
        name="mix_out_conv_ffn",
    )(h.reshape(m_rows, D_MODEL), mix.reshape(m_rows, MIX_IN), mem.reshape(m_rows, MEM_WIDTH),
      ffn_w[0], w["norm_ffn_g"], ffn_w[1], w["ffn_conv_w"], w["ffn_conv_b"], ffn_w[2], ctx)
    new_ctx = new.reshape(nseq, tps, CONV_W - 1, 2 * D_FF)[:, -1]
    return out.reshape(nseq, length, D_MODEL), new_ctx


def _dkv_kernel(h_ref, g_ref, wl_ref, wr_ref, wrr_ref, lg_ref, kg_ref, kgr_ref, cos_ref, sin_ref,
                lat_ref, kr_ref):
    xn = _rms_rows(h_ref[...], g_ref[...]).astype(BF16)
    lat_ref[...] = _rms_rows(_two_row_halves(_dot, xn, wl_ref[...]), lg_ref[...])
    kr = _dot(xn, wr_ref[...])
    kr_rot = _dot(xn, wrr_ref[...])
    r = lax.rsqrt(jnp.mean(kr * kr, axis=-1, keepdims=True) + EPS)
    kr_ref[...] = (kr * r * kg_ref[...]) * cos_ref[...] + (kr_rot * r * kgr_ref[...]) * sin_ref[...]


def _swap_halves(t, axis=-1):
    a, b = jnp.split(t, 2, axis=axis)
    return jnp.concatenate([b, a], axis=axis)


def _shared_kv_down(h, kv_norm_g, w_dkv_bf, latent_norm_g, krope_norm_g, cos_rows, sin_rows, tm):
    nseq, length, _ = h.shape
    m_rows = nseq * length
    tab_blocks = cos_rows.shape[0] // tm
    w_l = w_dkv_bf[:, :KV_LORA]
    w_r = w_dkv_bf[:, KV_LORA:]
    kg = krope_norm_g.reshape(1, ROPE_DIM)
    lat, kr = pl.pallas_call(
        _dkv_kernel,
        grid=(m_rows // tm,),
        in_specs=[
            pl.BlockSpec((tm, D_MODEL), lambda m: (m, 0)),
            _full((1, D_MODEL)),
            _full((D_MODEL, KV_LORA)), _full((D_MODEL, ROPE_DIM)), _full((D_MODEL, ROPE_DIM)),
            _full((1, KV_LORA)), _full((1, ROPE_DIM)), _full((1, ROPE_DIM)),
            pl.BlockSpec((tm, ROPE_DIM), lambda m: (m % tab_blocks, 0)),
            pl.BlockSpec((tm, ROPE_DIM), lambda m: (m % tab_blocks, 0)),
        ],
        out_specs=[pl.BlockSpec((tm, KV_LORA), lambda m: (m, 0)),
                   pl.BlockSpec((tm, ROPE_DIM), lambda m: (m, 0))],
        out_shape=[jax.ShapeDtypeStruct((m_rows, KV_LORA), F32),
                   jax.ShapeDtypeStruct((m_rows, ROPE_DIM), F32)],
        compiler_params=_params(1),
        name="shared_kv_down",
    )(h.reshape(m_rows, D_MODEL), kv_norm_g.reshape(1, D_MODEL), w_l, w_r, _swap_halves(w_r),
      latent_norm_g.reshape(1, KV_LORA), kg, _swap_halves(kg), cos_rows, sin_rows)
    return lat.reshape(nseq, length, KV_LORA), kr.reshape(nseq, length, ROPE_DIM)


def _kv_up_kernel(lat_ref, wk_ref, wv_ref, seg_ref, g_ref, k_ref, v_ref):
    lat = lat_ref[...].astype(BF16)
    k_ref[...] = _seg_rms(_dot(lat, wk_ref[...]), seg_ref[...], g_ref[...]).astype(BF16)
    v_ref[...] = _dot(lat, wv_ref[...]).astype(BF16)


def _kv_up(latent_rows, w_uk_bf, w_uv_bf, seg_nope, k_g_tiled, tr):
    rows = latent_rows.shape[0]
    width = MLA_HEADS * NOPE_DIM
    return pl.pallas_call(
        _kv_up_kernel,
        grid=(rows // tr,),
        in_specs=[
            pl.BlockSpec((tr, KV_LORA), lambda r: (r, 0)),
            _full((KV_LORA, width)), _full((KV_LORA, width)),
            _full((width, width)), _full((1, width)),
        ],
        out_specs=[pl.BlockSpec((tr, width), lambda r: (r, 0))] * 2,
        out_shape=[jax.ShapeDtypeStruct((rows, width), BF16)] * 2,
        compiler_params=_params(1),
        name="kv_up",
    )(latent_rows, w_uk_bf, w_uv_bf, seg_nope, k_g_tiled)


Q_PRESCALE = MLA_SCALE * math.log2(math.e)


def _q_kernel(z_ref, g_ref, wn_ref, wa_ref, segn_ref, segr_ref, gn_ref, ga_ref,
              cos_ref, sin_ref, kg_ref, wuk_ref, qn_ref, qr_ref, *maybe_absorbed_refs):
    xn = _rms_rows(z_ref[...], g_ref[...]).astype(BF16)
    qn = _seg_rms(_dot(xn, wn_ref[...]), segn_ref[...], gn_ref[...]) * Q_PRESCALE
    qn_ref[...] = qn.astype(BF16)
    a = _dot(xn, wa_ref[...])
    an = a * lax.rsqrt(_dot((a * a).astype(BF16), segr_ref[...]) + EPS) * ga_ref[...]
    width = an.shape[1]
    lane = lax.broadcasted_iota(jnp.int32, (1, width), 1)
    partner = jnp.where(lane % ROPE_DIM < ROPE_HALF,
                        pltpu.roll(an, width - ROPE_HALF, axis=1), pltpu.roll(an, ROPE_HALF, axis=1))
    rot = an * cos_ref[...] + partner * sin_ref[...]
    qr_ref[...] = (rot * Q_PRESCALE).astype(BF16)
    if maybe_absorbed_refs:
        qp_ref, qrh_ref = maybe_absorbed_refs
        lane = lax.broadcasted_iota(jnp.int32, (1, MLA_HEADS * NOPE_DIM), 1)
        qg = qn * kg_ref[...]
        rot_bf = (rot * Q_PRESCALE).astype(BF16)
        for head in range(MLA_HEADS):
            qh = jnp.where(lane // NOPE_DIM == head, qg, 0.0).astype(BF16)
            qp_ref[head] = _dot_nt(qh, wuk_ref[...]).astype(BF16)
            qrh_ref[head] = rot_bf[:, head * ROPE_DIM:(head + 1) * ROPE_DIM]


def _q_side(z_mix, q_latent_g, w_uq_bf, q_nope_g, q_rope_g, seg_nope, seg_rope, cos_rows, sin_rows,
            k_g_tiled, w_uk_bf, tm, absorbed):
    nseq, length, _ = z_mix.shape
    m_rows = nseq * length
    tab_blocks = cos_rows.shape[0] // tm
    wn_width = MLA_HEADS * NOPE_DIM
    wr_width = MLA_HEADS * ROPE_DIM
    w3 = w_uq_bf.reshape(MIX_IN, MLA_HEADS, NOPE_DIM + ROPE_DIM)
    w_n = w3[:, :, :NOPE_DIM].reshape(MIX_IN, wn_width)
    w_a = w3[:, :, NOPE_DIM:].reshape(MIX_IN, wr_width)
    g_n = jnp.tile(q_nope_g.reshape(1, NOPE_DIM), (1, MLA_HEADS))
    g_a = jnp.tile(q_rope_g.reshape(1, ROPE_DIM), (1, MLA_HEADS))
    cos_q = jnp.tile(cos_rows, (1, MLA_HEADS))
    sin_q = jnp.tile(sin_rows, (1, MLA_HEADS))
    out_specs = [pl.BlockSpec((tm, wd), lambda m: (m, 0)) for wd in (wn_width, wr_width)]
    out_shape = [jax.ShapeDtypeStruct((m_rows, wd), BF16) for wd in (wn_width, wr_width)]
    if absorbed:
        for wd in (KV_LORA, ROPE_DIM):
            out_specs.append(pl.BlockSpec((MLA_HEADS, tm, wd), lambda m: (0, m, 0)))
            out_shape.append(jax.ShapeDtypeStruct((MLA_HEADS, m_rows, wd), BF16))
    outs = pl.pallas_call(
        _q_kernel,
        grid=(m_rows // tm,),
        in_specs=[
            pl.BlockSpec((tm, MIX_IN), lambda m: (m, 0)),
            _full((1, MIX_IN)),
            _full((MIX_IN, wn_width)), _full((MIX_IN, wr_width)),
            _full((wn_width, wn_width)), _full((wr_width, wr_width)),
            _full((1, wn_width)), _full((1, wr_width)),
            pl.BlockSpec((tm, wr_width), lambda m: (m % tab_blocks, 0)),
            pl.BlockSpec((tm, wr_width), lambda m: (m % tab_blocks, 0)),
            _full((1, wn_width)), _full((KV_LORA, wn_width)),
        ],
        out_specs=out_specs,
        out_shape=out_shape,
        compiler_params=_params(1),
        name="mla_query",
    )(z_mix.reshape(m_rows, MIX_IN), q_latent_g.reshape(1, MIX_IN), w_n, w_a, seg_nope, seg_rope,
      g_n, g_a, cos_q, sin_q, k_g_tiled, w_uk_bf)
    return outs


ATTN_HEADS_PER_STEP = 4


def _attn_kernel(qn_ref, qr_ref, kn_ref, kr_ref, v_ref, o_ref, *, tile, n_tiles):
    qi = pl.program_id(2)
    pairs = ATTN_HEADS_PER_STEP // 2
    lane = lax.broadcasted_iota(jnp.int32, (1, LANES), 1)
    head_lanes = [(lane // NOPE_DIM) == j for j in range(2)]
    qr = qr_ref[0]
    qcat = []
    for h in range(ATTN_HEADS_PER_STEP):
        qn = qn_ref[0, :, (h // 2) * LANES:(h // 2 + 1) * LANES]
        qcat.append(jnp.concatenate(
            [jnp.where(head_lanes[h % 2], qn, jnp.zeros_like(qn)),
             jnp.where((lane // ROPE_DIM) == h, qr, jnp.zeros_like(qr))], axis=1))
    row_chunk = lax.broadcasted_iota(jnp.int32, (tile, 1), 0) // CHUNK
    col_chunk = lax.broadcasted_iota(jnp.int32, (1, tile), 1) // CHUNK
    diag_visible = col_chunk <= row_chunk

    def one_block(kb, carry, masked):
        rows_k = slice(kb * tile, (kb + 1) * tile)
        kr = kr_ref[0, rows_k, :]
        new = []
        for h in range(ATTN_HEADS_PER_STEP):
            lanes_p = slice((h // 2) * LANES, (h // 2 + 1) * LANES)
            m_i, acc = carry[h]
            kcat = jnp.concatenate([kn_ref[0, rows_k, lanes_p], kr], axis=1)
            s = _dot_nt(qcat[h], kcat)
            if masked:
                s = jnp.where(diag_visible, s, NEG_INF)
            m_new = jnp.maximum(m_i, jnp.max(s, axis=-1, keepdims=True))
            alpha = jnp.exp2(m_i - m_new)
            p = jnp.exp2(s - m_new).astype(BF16)
            vb = v_ref[0, rows_k, lanes_p]
            v_h = jnp.where(head_lanes[h % 2], vb, jnp.ones_like(vb))
            new.append((m_new, alpha * acc + _dot(p, v_h)))
        return tuple(new)

    def query_tile(n_full):
        carry = tuple((jnp.full((tile, 1), NEG_INF, F32), jnp.zeros((tile, LANES), F32))
                      for _ in range(ATTN_HEADS_PER_STEP))
        for kb in range(n_full):
            carry = one_block(kb, carry, False)
        carry = one_block(n_full, carry, True)
        for p in range(pairs):
            out = jnp.zeros((tile, LANES), F32)
            for j in range(2):
                acc = carry[2 * p + j][1]
                row_sum = pltpu.roll(acc, NOPE_DIM, axis=1)
                out = jnp.where(head_lanes[j], acc * (1.0 / row_sum), out)
            o_ref[0, :, p * LANES:(p + 1) * LANES] = out.astype(BF16)

    for c in range(n_tiles):
        pl.when(qi == c)(functools.partial(query_tile, c))


def _mla_attention(qn, qr, kn, kr4, v, tile):
    nseq, length, _ = qn.shape
    kern = functools.partial(_attn_kernel, tile=tile, n_tiles=length // tile)
    width = (ATTN_HEADS_PER_STEP // 2) * LANES
    return pl.pallas_call(
        kern,
        grid=(nseq, MLA_HEADS // ATTN_HEADS_PER_STEP, length // tile),
        in_specs=[
            pl.BlockSpec((1, tile, width), lambda b, g, i: (b, i, g)),
            pl.BlockSpec((1, tile, LANES), lambda b, g, i: (b, i, g)),
            pl.BlockSpec((1, length, width), lambda b, g, i: (b, 0, g)),
            pl.BlockSpec((1, length, LANES), lambda b, g, i: (b, 0, 0)),
            pl.BlockSpec((1, length, width), lambda b, g, i: (b, 0, g)),
        ],
        out_specs=pl.BlockSpec((1, tile, width), lambda b, g, i: (b, i, g)),
        out_shape=jax.ShapeDtypeStruct((nseq, length, MLA_HEADS * V_DIM), BF16),
        compiler_params=_params(3),
        name="mla_attention",
    )(qn, qr, kn, kr4, v)


KNORM_ROWS = 16


def _key_norm_kernel(past_ref, new_ref, wukt_ref, rt_ref, *, n_chunks, chunk):
    def norms(lat):
        keys = lat.shape[0]
        kt = _dot_nt(wukt_ref[...], lat)
        ss = jnp.sum((kt * kt).reshape(MLA_HEADS, NOPE_DIM, keys), axis=1)
        r = lax.rsqrt(ss * (1.0 / NOPE_DIM) + EPS)
        return jnp.concatenate([r, jnp.ones((KNORM_ROWS - MLA_HEADS, keys), F32)], axis=0)

    for c in range(n_chunks):
        rt_ref[0, :, c * chunk:(c + 1) * chunk] = norms(past_ref[0, c * chunk:(c + 1) * chunk, :].astype(BF16))
    rt_ref[0, :, n_chunks * chunk:] = norms(new_ref[0])


def _key_norms(past_latent, new_lat_pad, w_ukt_bf, chunk):
    nseq, past, _ = past_latent.shape
    new_pad = new_lat_pad.shape[1]
    lk_pad = past + new_pad
    kern = functools.partial(_key_norm_kernel, n_chunks=past // chunk, chunk=chunk)
    return pl.pallas_call(
        kern,
        grid=(nseq,),
        in_specs=[pl.BlockSpec((1, past, KV_LORA), lambda b: (b, 0, 0)),
                  pl.BlockSpec((1, new_pad, KV_LORA), lambda b: (b, 0, 0)),
                  _full((MLA_HEADS * NOPE_DIM, KV_LORA))],
        out_specs=pl.BlockSpec((1, KNORM_ROWS, lk_pad), lambda b: (b, 0, 0)),
        out_shape=jax.ShapeDtypeStruct((nseq, KNORM_ROWS, lk_pad), F32),
        compiler_params=_params(1),
        name="mla_key_norms",
    )(past_latent, new_lat_pad, w_ukt_bf)


def _attn_absorbed_kernel(qs_ref, qrs_ref, plat_ref, nlat_ref, pkrt_ref, nkrt_ref, rt_ref, wuv_ref, o_ref, op_ref,
                          *, tq, q_off, lk_valid):
    b = pl.program_id(0)
    nseq = pl.num_programs(0)
    rows = MLA_HEADS * tq
    lat = jnp.concatenate([plat_ref[0].astype(BF16), nlat_ref[0]], axis=0)
    kr_t = jnp.concatenate([pkrt_ref[0].astype(BF16), nkrt_ref[0]], axis=1)
    lk_pad = lat.shape[0]
    rt = rt_ref[0]
    knorm = jnp.concatenate([jnp.broadcast_to(rt[h:h + 1, :], (tq, lk_pad)) for h in range(MLA_HEADS)], axis=0)
    qs = qs_ref[...].reshape(rows, KV_LORA)
    qrs = qrs_ref[...].reshape(rows, ROPE_DIM)
    s = _dot_nt(qs, lat) * knorm + _dot(qrs, kr_t)
    k_pos = lax.broadcasted_iota(jnp.int32, (1, lk_pad), 1)
    if (lk_valid - 1) // CHUNK > q_off // CHUNK:
        q_chunk = (q_off + lax.broadcasted_iota(jnp.int32, (rows, 1), 0) % tq) // CHUNK
        s = jnp.where(k_pos // CHUNK <= q_chunk, s, NEG_INF)
    s = jnp.where(k_pos < lk_valid, s, NEG_INF)
    p = jnp.exp2(s - jnp.max(s, axis=-1, keepdims=True))
    l = jnp.sum(p, axis=-1, keepdims=True)
    o = _two_row_halves(_dot, p.astype(BF16), lat)
    op_ref[b] = (o * (1.0 / l)).astype(BF16)

    @pl.when(b == nseq - 1)
    def _():
        n_all = op_ref.shape[0]
        lane_o = lax.broadcasted_iota(jnp.int32, (1, MLA_HEADS * V_DIM), 1)
        out = jnp.zeros((n_all * tq, MLA_HEADS * V_DIM), F32)
        for h in range(MLA_HEADS):
            x = op_ref[:, h * tq:(h + 1) * tq, :].reshape(n_all * tq, KV_LORA)
            out = jnp.where(lane_o // V_DIM == h, _dot(x, wuv_ref[...]), out)
        o_ref[...] = out.astype(BF16)


def _mla_attention_absorbed(qp, qrh, past_latent, new_lat_pad, past_krope_t, new_kr_pad_t, r_t, w_uv_bf,
                            tq, lk_valid):
    nseq, past, _ = past_latent.shape
    new_pad = new_lat_pad.shape[1]
    kern = functools.partial(_attn_absorbed_kernel, tq=tq, q_off=past, lk_valid=lk_valid)
    per_seq = lambda n, width: pl.BlockSpec((1, n, width), lambda b: (b, 0, 0))
    heads_of_seq = lambda width: pl.BlockSpec((MLA_HEADS, tq, width), lambda b: (0, b, 0))
    out = pl.pallas_call(
        kern,
        grid=(nseq,),
        in_specs=[
            heads_of_seq(KV_LORA), heads_of_seq(ROPE_DIM),
            per_seq(past, KV_LORA), per_seq(new_pad, KV_LORA),
            per_seq(ROPE_DIM, past), per_seq(ROPE_DIM, new_pad),
            pl.BlockSpec((1, KNORM_ROWS, past + new_pad), lambda b: (b, 0, 0)),
            _full((KV_LORA, MLA_HEADS * V_DIM)),
        ],
        out_specs=_full((nseq * tq, MLA_HEADS * V_DIM)),
        out_shape=jax.ShapeDtypeStruct((nseq * tq, MLA_HEADS * V_DIM), BF16),
        scratch_shapes=[pltpu.VMEM((nseq, MLA_HEADS * tq, KV_LORA), BF16)],
        compiler_params=_params(1),
        name="mla_attention_absorbed",
    )(qp, qrh, past_latent, new_lat_pad, past_krope_t, new_kr_pad_t, r_t, w_uv_bf)
    return out.reshape(nseq, tq, MLA_HEADS * V_DIM)


def _trunk(x, mem_k_bf, mem_v_bf, ssm_h0_re, ssm_h0_im, conv_ctx, past_latent, past_krope, w, cfg, ffn_bf=None):
    nseq, length, _ = x.shape
    past = 0 if past_latent is None else past_latent.shape[1]
    cos_t, sin_t = _rope_tables(past, length)
    reps = cfg["rope_rows"] // length
    cos_rows = jnp.tile(cos_t, (reps, 1))
    sin_rows = jnp.tile(sin_t, (reps, 1))
    if conv_ctx is None:
        conv_ctx = jnp.zeros((DEPTH, nseq, CONV_W - 1, 2 * D_FF), F32)
    if ssm_h0_re is None:
        ssm_h0_re = jnp.zeros((N_A_LAYERS, nseq, SSM_GROUPS, SSM_STATE), F32)
        ssm_h0_im = ssm_h0_re
    h = x
    ssm_re_out, ssm_im_out, conv_out = [], [], []
    make_ffn_bf = ffn_bf is None
    if make_ffn_bf:
        ffn_bf = []
    for layer in range(DEPTH):
        to_cast = (w["w_mix_out_f32"], w["w_ffn_in_f32"], w["w_ffn_out_f32"]) if make_ffn_bf else ()
        z_mix, mem_out, cast = _mixin(h, w, layer, mem_k_bf, mem_v_bf, cfg["tm_mixin"], to_cast)
        if make_ffn_bf:
            ffn_bf.append(cast)
        if layer < N_A_LAYERS:
            i = layer
            mix_out, s_re, s_im = _s5_mixer(
                z_mix, ssm_h0_re[i].reshape(nseq, SSM_COLS), ssm_h0_im[i].reshape(nseq, SSM_COLS),
                w["lam_re"][i], w["lam_im"][i], w["s5_in"][i], w["s5_out"][i],
                w["ssm_d"][i][None], w["w_glu"][i], w["b_glu"][i][None], cfg["t_chunk"])
            ssm_re_out.append(s_re.reshape(nseq, SSM_GROUPS, SSM_STATE))
            ssm_im_out.append(s_im.reshape(nseq, SSM_GROUPS, SSM_STATE))
        else:
            if layer == N_A_LAYERS:
                new_latent, new_krope = _shared_kv_down(
                    h, w["kv_norm_g"], w["w_dkv"], w["latent_norm_g"], w["krope_norm_g"],
                    cos_rows, sin_rows, cfg["tm_rows"])
                if cfg["absorbed"]:
                    assert past % LANES == 0
                    lk_valid = past + length
                    new_pad = -(-length // LANES) * LANES
                    pad_rows = lambda t: jnp.pad(t.astype(BF16), ((0, 0), (0, new_pad - length), (0, 0)))
                    new_lat_pad = pad_rows(new_latent)
                    new_kr_pad_t = jnp.swapaxes(pad_rows(new_krope), 1, 2)
                    past_krope_t = jnp.swapaxes(past_krope, 1, 2)
                    r_t = _key_norms(past_latent, new_lat_pad, w["w_uk_t"], cfg["knorm_chunk"])
                else:
                    assert past_latent is None
                    kn, v_all = _kv_up(new_latent.reshape(nseq * length, KV_LORA), w["w_uk"], w["w_uv"],
                                       w["seg_nope"], w["k_nope_g"], cfg["tr_kv"])
                    kn = kn.reshape(nseq, length, MLA_HEADS * NOPE_DIM)
                    v_all = v_all.reshape(nseq, length, MLA_HEADS * V_DIM)
                    kr4 = jnp.tile(new_krope.astype(BF16), (1, 1, LANES // ROPE_DIM))
            j = layer - N_A_LAYERS
            q_out = _q_side(z_mix, w["q_latent_norm_g"][j], w["w_uq"][j], w["q_nope_norm_g"][j],
                            w["q_rope_norm_g"][j], w["seg_nope"], w["seg_rope"], cos_rows, sin_rows,
                            w["k_nope_g"], w["w_uk"], cfg["tm_rows"], cfg["absorbed"])
            if cfg["absorbed"]:
                mix_out = _mla_attention_absorbed(q_out[2], q_out[3], past_latent, new_lat_pad, past_krope_t,
                                                  new_kr_pad_t, r_t, w["w_uv"], length, lk_valid)
            else:
                qn = q_out[0].reshape(nseq, length, MLA_HEADS * NOPE_DIM)
                qr = q_out[1].reshape(nseq, length, MLA_HEADS * ROPE_DIM)
                mix_out = _mla_attention(qn, qr, kn, kr4, v_all, cfg["tq"])
        h, ctx = _mixout_ffn(h, mix_out, mem_out, w, layer, ffn_bf[layer], conv_ctx[layer], cfg["ffn_rows"])
        conv_out.append(ctx)
    return (h, new_latent, new_krope, jnp.stack(ssm_re_out), jnp.stack(ssm_im_out), jnp.stack(conv_out)), ffn_bf


PROMPT_CFG = dict(tm_mixin=1024, t_chunk=64, ffn_rows=512, tm_rows=1024, rope_rows=2048, tr_kv=1024,
                  tq=512, absorbed=False)
SAMPLE_CFG = dict(tm_mixin=32, t_chunk=32, ffn_rows=32, tm_rows=512, rope_rows=512, knorm_chunk=1024,
                  absorbed=True)


def kernel(x_prompt, x_sample, cache_mla_latent, cache_mla_krope, cache_mem_k, cache_mem_v, state_ssm_re, state_ssm_im, state_conv, mem_prompt, norm_mix_g, w_mix_in, w_mix_out, norm_ffn_g, w_ffn_in, ffn_conv_w, ffn_conv_b, w_ffn_out, mem_norm_g, w_mem_kv, mem_q_norm_g, mem_k_norm_g, ssm_a_re, ssm_a_im, ssm_log_dt, ssm_b_re, ssm_b_im, ssm_c_re, ssm_c_im, ssm_d, w_glu, b_glu, kv_norm_g, w_dkv, latent_norm_g, krope_norm_g, w_uk, w_uv, k_nope_norm_g, q_latent_norm_g, w_uq, q_nope_norm_g, q_rope_norm_g):
    bf = lambda t: t.astype(BF16)
    seg_mem = _seg_matrix(MEM_WIDTH, MEM_HEAD_DIM)
    lam_re, lam_im, s5_in, s5_out = _s5_prepare(ssm_a_re, ssm_a_im, ssm_log_dt, ssm_b_re, ssm_b_im,
                                                ssm_c_re, ssm_c_im)
    w = dict(
        norm_mix_g=norm_mix_g.reshape(DEPTH, 1, D_MODEL), w_mix_in=w_mix_in,
        norm_ffn_g=norm_ffn_g.reshape(DEPTH, 1, D_MODEL),
        ffn_conv_w=ffn_conv_w, ffn_conv_b=ffn_conv_b.reshape(DEPTH, 1, 2 * D_FF),
        w_mix_out_f32=w_mix_out, w_ffn_in_f32=w_ffn_in, w_ffn_out_f32=w_ffn_out,
        mem_q_g=jnp.tile(mem_q_norm_g, (1, MEM_HEADS)).reshape(DEPTH, 1, MEM_WIDTH),
        seg_mem=seg_mem,
        seg_nope=_seg_matrix(MLA_HEADS * NOPE_DIM, NOPE_DIM),
        seg_rope=_seg_matrix(MLA_HEADS * ROPE_DIM, ROPE_DIM),
        lam_re=lam_re, lam_im=lam_im, s5_in=s5_in, s5_out=s5_out,
        ssm_d=ssm_d, w_glu=w_glu, b_glu=b_glu,
        kv_norm_g=kv_norm_g, w_dkv=bf(w_dkv), latent_norm_g=latent_norm_g, krope_norm_g=krope_norm_g,
        w_uk=bf(w_uk), w_uk_t=bf(w_uk).T, w_uv=bf(w_uv),
        k_nope_g=jnp.tile(k_nope_norm_g.reshape(1, NOPE_DIM), (1, MLA_HEADS)),
        q_latent_norm_g=q_latent_norm_g, w_uq=bf(w_uq), q_nope_norm_g=q_nope_norm_g,
        q_rope_norm_g=q_rope_norm_g,
    )
    bsz = mem_prompt.shape[0]
    mem_k_p, mem_v_p, mem_k_bf, mem_v_bf = _memory_kv(mem_prompt, mem_norm_g, w_mem_kv, mem_k_norm_g, seg_mem)
    (y_prompt, lat_p, krope_p, ssm_re_p, ssm_im_p, conv_p), ffn_bf = _trunk(
        x_prompt, mem_k_bf, mem_v_bf, None, None, None, None, None, w, PROMPT_CFG)
    dec = cache_mem_k.shape[1]
    (y_sample, lat_s, krope_s, ssm_re_s, ssm_im_s, conv_s), _ = _trunk(
        x_sample, bf(cache_mem_k).reshape(DEPTH, dec, N_MEM, MEM_WIDTH),
        bf(cache_mem_v).reshape(DEPTH, dec, N_MEM, MEM_WIDTH),
        state_ssm_re, state_ssm_im, state_conv, cache_mla_latent, cache_mla_krope, w, SAMPLE_CFG, ffn_bf)
    shape5 = (DEPTH, bsz, N_MEM, MEM_HEADS, MEM_HEAD_DIM)
    return (y_prompt, y_sample, mem_k_p.reshape(shape5), mem_v_p.reshape(shape5), lat_p, krope_p,
            ssm_re_p, ssm_im_p, conv_p, lat_s, krope_s, ssm_re_s, ssm_im_s, conv_s)
```

```python
import functools
import math

import jax
import jax.numpy as jnp
from jax import lax
from jax.experimental import pallas as pl
from jax.experimental.pallas import tpu as pltpu

F32 = jnp.float32
BF16 = jnp.bfloat16

D_MODEL = 1024
DEPTH = 4
CHUNK = 64
N_A_LAYERS = DEPTH // 2
N_B_LAYERS = DEPTH - N_A_LAYERS
MIX_IN = 768
MEM_HEADS = 4
MEM_HEAD_DIM = 64
MEM_WIDTH = MEM_HEADS * MEM_HEAD_DIM
N_MEM = 256
SSM_GROUP = 16
SSM_GROUPS = MIX_IN // SSM_GROUP
SSM_STATE = 64
SSM_COLS = SSM_GROUPS * SSM_STATE
MLA_HEADS = 12
NOPE_DIM = 64
ROPE_DIM = 32
ROPE_HALF = ROPE_DIM // 2
V_DIM = 64
KV_LORA = 256
ROPE_BASE = 10000.0
MLA_SCALE = (NOPE_DIM + ROPE_DIM) ** -0.5
MEM_SCALE = MEM_HEAD_DIM ** -0.5
D_FF = 2816
CONV_W = 3
EPS = 1e-6
NEG_INF = -1e30

V7X_VMEM_LIMIT_BYTES = 56 * 1024 * 1024
LANES = 128
SUBLANES = 8
V7X_MXU_DIM = 256

S5_BLOCKS = 3
S5_BLOCK_CH = MIX_IN // S5_BLOCKS
S5_BLOCK_ST = SSM_COLS // S5_BLOCKS
FFN_TF = 256
N_FF_TILES = D_FF // FFN_TF
FFN_PAD_SLOTS = 4


def _params(n_axes):
    return pltpu.CompilerParams(
        dimension_semantics=("arbitrary",) * n_axes,
        vmem_limit_bytes=V7X_VMEM_LIMIT_BYTES,
    )


def _dot(a, b):
    return jnp.dot(a, b, preferred_element_type=F32)


def _dot_nt(a, b):
    return lax.dot_general(a, b, (((1,), (1,)), ((), ())), preferred_element_type=F32)


def _two_row_halves(dot_fn, a, b):
    rows = a.shape[0]
    if rows < V7X_MXU_DIM or rows % 32:
        return dot_fn(a, b)
    return jnp.concatenate([dot_fn(a[:rows // 2], b), dot_fn(a[rows // 2:], b)], axis=0)


def _rms_rows(x, g):
    ms = jnp.mean(x * x, axis=-1, keepdims=True)
    return x * lax.rsqrt(ms + EPS) * g


def _seg_rms(x, seg_mat, g):
    ms = _dot((x * x).astype(BF16), seg_mat)
    return x * lax.rsqrt(ms + EPS) * g


def _seg_matrix(width, seg):
    idx = jnp.arange(width) // seg
    return jnp.where(idx[:, None] == idx[None, :], 1.0 / seg, 0.0).astype(BF16)


def _full(shape):
    nd = len(shape)
    return pl.BlockSpec(shape, lambda *_: (0,) * nd)


def _rope_table_kernel(pos_ref, inv_ref, cos_ref, sin_ref):
    ang = pos_ref[...] * inv_ref[...]
    cos_ref[...] = jnp.cos(ang)
    sin_ref[...] = jnp.sin(ang)


def _rope_tables(past, length):
    pos = (past + jnp.arange(length, dtype=jnp.int32)).astype(F32)[:, None]
    inv_freq = (1.0 / (ROPE_BASE ** (jnp.arange(0, ROPE_DIM, 2, dtype=F32) / ROPE_DIM)))[None, :]
    cos, sin = pl.pallas_call(
        _rope_table_kernel,
        out_shape=[jax.ShapeDtypeStruct((length, ROPE_HALF), F32)] * 2,
        name="rope_table",
    )(pos, inv_freq)
    cos_t = jnp.concatenate([cos, cos], axis=1)
    sin_t = jnp.concatenate([-sin, sin], axis=1)
    return cos_t, sin_t


def _memkv_kernel(mem_ref, g_ref, w_ref, kg_ref, seg_ref, k_ref, v_ref, kb_ref, vb_ref):
    bsz = mem_ref.shape[0]
    mem = mem_ref[...].reshape(bsz * N_MEM, D_MODEL)
    xn = _rms_rows(mem, g_ref[0]).astype(BF16)
    kv = _dot(xn, w_ref[0].astype(BF16))
    k = _seg_rms(kv[:, :MEM_WIDTH], seg_ref[...], kg_ref[0]).reshape(bsz, N_MEM, MEM_WIDTH)
    v = kv[:, MEM_WIDTH:].reshape(bsz, N_MEM, MEM_WIDTH)
    k_ref[0] = k
    v_ref[0] = v
    kb_ref[0] = k.astype(BF16)
    vb_ref[0] = v.astype(BF16)


def _memory_kv(mem, mem_norm_g, w_mem_kv, mem_k_norm_g, seg_mem):
    bsz = mem.shape[0]
    kg = jnp.tile(mem_k_norm_g, (1, MEM_HEADS)).reshape(DEPTH, 1, MEM_WIDTH)
    out4 = lambda dt: jax.ShapeDtypeStruct((DEPTH, bsz, N_MEM, MEM_WIDTH), dt)
    spec4 = pl.BlockSpec((1, bsz, N_MEM, MEM_WIDTH), lambda l: (l, 0, 0, 0))
    return pl.pallas_call(
        _memkv_kernel,
        grid=(DEPTH,),
        in_specs=[
            _full((bsz, N_MEM, D_MODEL)),
            pl.BlockSpec((1, 1, D_MODEL), lambda l: (l, 0, 0)),
            pl.BlockSpec((1, D_MODEL, 2 * MEM_WIDTH), lambda l: (l, 0, 0)),
            pl.BlockSpec((1, 1, MEM_WIDTH), lambda l: (l, 0, 0)),
            _full((MEM_WIDTH, MEM_WIDTH)),
        ],
        out_specs=[spec4, spec4, spec4, spec4],
        out_shape=[out4(F32), out4(F32), out4(BF16), out4(BF16)],
        compiler_params=_params(1),
        name="memory_kv",
    )(mem, mem_norm_g.reshape(DEPTH, 1, D_MODEL), w_mem_kv, kg, seg_mem)


def _mixin_kernel(h_ref, g_ref, w_ref, qg_ref, seg_ref, k_ref, v_ref, *rest, seqs, rows_per_seq, n_cast):
    cast_in, (zmix_ref, mem_ref), cast_out = rest[:n_cast], rest[n_cast:n_cast + 2], rest[n_cast + 2:]
    for src, dst in zip(cast_in, cast_out):
        dst[...] = src[...].astype(BF16)
    xn = _rms_rows(h_ref[...], g_ref[...]).astype(BF16)
    z = _dot(xn, w_ref[...].astype(BF16))
    zmix_ref[...] = z[:, :MIX_IN]
    mq = _seg_rms(z[:, MIX_IN:], seg_ref[...], qg_ref[...]).astype(BF16)
    lane = lax.broadcasted_iota(jnp.int32, (1, MEM_WIDTH), 1)
    in_head = [(lane // MEM_HEAD_DIM) == head for head in range(MEM_HEADS)]
    for b in range(seqs):
        rows = slice(b * rows_per_seq, (b + 1) * rows_per_seq)
        mq_b = mq[rows]
        qs = jnp.concatenate([jnp.where(m, mq_b, jnp.zeros_like(mq_b)) for m in in_head], axis=0)
        s = _two_row_halves(_dot_nt, qs, k_ref[b]) * MEM_SCALE
        p = jnp.exp(s - jnp.max(s, axis=-1, keepdims=True))
        p = p * (1.0 / jnp.sum(p, axis=-1, keepdims=True))
        o = _two_row_halves(_dot, p.astype(BF16), v_ref[b])
        out = jnp.zeros(mq_b.shape, F32)
        for head, m in enumerate(in_head):
            out = jnp.where(m, o[head * rows_per_seq:(head + 1) * rows_per_seq], out)
        mem_ref[rows, :] = out.astype(BF16)


def _mixin(h, w, layer, k_bf, v_bf, rows_per_seq_tile, to_cast=()):
    nseq, length, _ = h.shape
    m_rows = nseq * length
    if rows_per_seq_tile >= length:
        seqs, rps, tps = nseq, length, 1
    else:
        seqs, rps, tps = 1, rows_per_seq_tile, length // rows_per_seq_tile
    tm = seqs * rps
    n_steps = m_rows // tm
    kern = functools.partial(_mixin_kernel, seqs=seqs, rows_per_seq=rps, n_cast=len(to_cast))
    row = lambda width: pl.BlockSpec((tm, width), lambda m: (m, 0))
    of_layer = lambda *shape: pl.BlockSpec((None,) + shape, lambda m: (layer,) + (0,) * len(shape))
    kv_spec = pl.BlockSpec((None, seqs, N_MEM, MEM_WIDTH), lambda m: (layer, m // tps, 0, 0))
    slab = lambda t: t.shape[1] // n_steps
    cast_in = [pl.BlockSpec((None, slab(t), t.shape[2]), lambda m: (layer, m, 0)) for t in to_cast]
    cast_out = [pl.BlockSpec((slab(t), t.shape[2]), lambda m: (m, 0)) for t in to_cast]
    outs = pl.pallas_call(
        kern,
        grid=(n_steps,),
        in_specs=[
            row(D_MODEL),
            of_layer(1, D_MODEL),
            of_layer(D_MODEL, MIX_IN + MEM_WIDTH),
            of_layer(1, MEM_WIDTH),
            _full((MEM_WIDTH, MEM_WIDTH)),
            kv_spec, kv_spec,
        ] + cast_in,
        out_specs=[row(MIX_IN), row(MEM_WIDTH)] + cast_out,
        out_shape=[
            jax.ShapeDtypeStruct((m_rows, MIX_IN), F32),
            jax.ShapeDtypeStruct((m_rows, MEM_WIDTH), BF16),
        ] + [jax.ShapeDtypeStruct(t.shape[1:], BF16) for t in to_cast],
        compiler_params=_params(1),
        name="mix_in_mem_attn",
    )(h.reshape(m_rows, D_MODEL), w["norm_mix_g"], w["w_mix_in"], w["mem_q_g"], w["seg_mem"], k_bf, v_bf,
      *to_cast)
    return outs[0].reshape(nseq, length, MIX_IN), outs[1].reshape(nseq, length, MEM_WIDTH), outs[2:]


def _s5_discretise_kernel(are_ref, aim_ref, ldt_ref, lre_ref, lim_ref, fre_ref, fim_ref):
    a_re, a_im = are_ref[...], aim_ref[...]
    dt = jnp.exp(ldt_ref[...])
    mag = jnp.exp(a_re * dt)
    lam_re = mag * jnp.cos(a_im * dt)
    lam_im = mag * jnp.sin(a_im * dt)
    den = a_re * a_re + a_im * a_im
    x_re = lam_re - 1.0
    lre_ref[...] = lam_re
    lim_ref[...] = lam_im
    fre_ref[...] = (x_re * a_re + lam_im * a_im) / den
    fim_ref[...] = (lam_im * a_re - x_re * a_im) / den


def _s5_input_scale_kernel(fre_ref, fim_ref, bre_ref, bim_ref, bbre_ref, bbim_ref):
    f_re, f_im = fre_ref[0], fim_ref[0]
    b_re, b_im = bre_ref[0], bim_ref[0]
    bbre_ref[0] = f_re * b_re - f_im * b_im
    bbim_ref[0] = f_re * b_im + f_im * b_re


def _s5_prepare(a_re, a_im, log_dt, b_re, b_im, c_re, c_im):
    n = a_re.shape[0]
    dense = lambda t: t.reshape(n, SSM_COLS // LANES, LANES)
    ldt = jnp.broadcast_to(log_dt[:, :, None], (n, SSM_GROUPS, SSM_STATE))
    lam_re, lam_im, f_re, f_im = pl.pallas_call(
        _s5_discretise_kernel,
        out_shape=[jax.ShapeDtypeStruct((n, SSM_COLS // LANES, LANES), F32)] * 4,
        name="s5_discretise",
    )(dense(a_re), dense(a_im), dense(ldt))
    col = lambda t: t.reshape(n, SSM_COLS, 1)
    col_spec = pl.BlockSpec((1, SSM_COLS, 1), lambda l: (l, 0, 0))
    b_spec = pl.BlockSpec((1, SSM_COLS, SSM_GROUP), lambda l: (l, 0, 0))
    bb_re, bb_im = pl.pallas_call(
        _s5_input_scale_kernel,
        grid=(n,),
        in_specs=[col_spec, col_spec, b_spec, b_spec],
        out_specs=[b_spec, b_spec],
        out_shape=[jax.ShapeDtypeStruct((n, SSM_COLS, SSM_GROUP), F32)] * 2,
        compiler_params=_params(1),
        name="s5_input_scale",
    )(col(f_re), col(f_im), b_re.reshape(n, SSM_COLS, SSM_GROUP), b_im.reshape(n, SSM_COLS, SSM_GROUP))
    gpb = SSM_GROUPS // S5_BLOCKS

    def block_diagonal(t, rows_per_group, cols_per_group):
        rows = gpb * rows_per_group
        same = (jnp.arange(rows) // rows_per_group)[:, None] == jnp.arange(gpb)[None, :]
        wide = jnp.broadcast_to(t[:, :, :, None, :], (n, S5_BLOCKS, rows, gpb, cols_per_group))
        return jnp.where(same[None, None, :, :, None], wide, 0.0).reshape(
            n, S5_BLOCKS, rows, gpb * cols_per_group)

    def in_blocks(bb):
        t = bb.reshape(n, S5_BLOCKS, gpb, SSM_STATE, SSM_GROUP).transpose(0, 1, 2, 4, 3)
        return block_diagonal(t.reshape(n, S5_BLOCKS, S5_BLOCK_CH, SSM_STATE), SSM_GROUP, SSM_STATE)

    def out_blocks(c):
        t = c.reshape(n, S5_BLOCKS, gpb, SSM_GROUP, SSM_STATE).transpose(0, 1, 2, 4, 3)
        return block_diagonal(t.reshape(n, S5_BLOCKS, S5_BLOCK_ST, SSM_GROUP), SSM_STATE, SSM_GROUP)

    w_in = jnp.concatenate([in_blocks(bb_re), in_blocks(bb_im)], axis=-1).astype(BF16)
    w_out = jnp.concatenate([out_blocks(c_re), -out_blocks(c_im)], axis=-2).astype(BF16)
    return lam_re.reshape(n, 1, SSM_COLS), lam_im.reshape(n, 1, SSM_COLS), w_in, w_out


def _s5_kernel(u_ref, h0re_ref, h0im_ref, lre_ref, lim_ref, win_ref, wout_ref, d_ref, wglu_ref, bglu_ref,
               out_ref, sre_out_ref, sim_out_ref, hb_ref, sre_ref, sim_ref, *, bsz, t_chunk, col_block):
    c = pl.program_id(0)
    rows = bsz * t_chunk

    @pl.when(c == 0)
    def _():
        sre_ref[...] = h0re_ref[...]
        sim_ref[...] = h0im_ref[...]

    u = u_ref[...]
    ut = jnp.swapaxes(u, 0, 1).reshape(rows, MIX_IN).astype(BF16)
    ys = []
    for j in range(S5_BLOCKS):
        cols = slice(j * 2 * S5_BLOCK_ST, (j + 1) * 2 * S5_BLOCK_ST)
        hb_ref[:, cols] = _dot(ut[:, j * S5_BLOCK_CH:(j + 1) * S5_BLOCK_CH], win_ref[j])
        for sub in range(S5_BLOCK_ST // col_block):
            nat = j * S5_BLOCK_ST + sub * col_block
            cre = j * 2 * S5_BLOCK_ST + sub * col_block
            cim = cre + S5_BLOCK_ST
            lam_r = jnp.broadcast_to(lre_ref[:, nat:nat + col_block], (bsz, col_block))
            lam_i = jnp.broadcast_to(lim_ref[:, nat:nat + col_block], (bsz, col_block))
            s_r = sre_ref[:, nat:nat + col_block]
            s_i = sim_ref[:, nat:nat + col_block]
            for t in range(t_chunk):
                rows_t = slice(t * bsz, (t + 1) * bsz)
                n_r = lam_r * s_r - lam_i * s_i + hb_ref[rows_t, cre:cre + col_block]
                n_i = lam_r * s_i + lam_i * s_r + hb_ref[rows_t, cim:cim + col_block]
                hb_ref[rows_t, cre:cre + col_block] = n_r
                hb_ref[rows_t, cim:cim + col_block] = n_i
                s_r, s_i = n_r, n_i
            sre_ref[:, nat:nat + col_block] = s_r
            sim_ref[:, nat:nat + col_block] = s_i
        ys.append(_two_row_halves(_dot, hb_ref[:, cols].astype(BF16), wout_ref[j]))
    yt = jnp.concatenate(ys, axis=1).reshape(t_chunk, bsz, MIX_IN)
    y = jnp.swapaxes(yt, 0, 1) + d_ref[...] * u
    y = jax.nn.gelu(y).reshape(rows, MIX_IN)
    gate = _dot(y.astype(BF16), wglu_ref[...].astype(BF16)) + bglu_ref[...]
    out_ref[...] = (y * jax.nn.sigmoid(gate)).reshape(bsz, t_chunk, MIX_IN).astype(BF16)

    @pl.when(c == pl.num_programs(0) - 1)
    def _():
        sre_out_ref[...] = sre_ref[...]
        sim_out_ref[...] = sim_ref[...]


def _s5_mixer(u, h0_re, h0_im, lam_re, lam_im, w_in, w_out, d_skip, w_glu, b_glu, t_chunk):
    bsz, length, _ = u.shape
    rows = bsz * t_chunk
    col_block = (SUBLANES * 512) // bsz
    kern = functools.partial(_s5_kernel, bsz=bsz, t_chunk=t_chunk, col_block=col_block)
    state = jax.ShapeDtypeStruct((bsz, SSM_COLS), F32)
    return pl.pallas_call(
        kern,
        grid=(length // t_chunk,),
        in_specs=[
            pl.BlockSpec((bsz, t_chunk, MIX_IN), lambda c: (0, c, 0)),
            _full((bsz, SSM_COLS)), _full((bsz, SSM_COLS)),
            _full((1, SSM_COLS)), _full((1, SSM_COLS)),
            _full((S5_BLOCKS, S5_BLOCK_CH, 2 * S5_BLOCK_ST)),
            _full((S5_BLOCKS, 2 * S5_BLOCK_ST, S5_BLOCK_CH)),
            _full((1, MIX_IN)), _full((MIX_IN, MIX_IN)), _full((1, MIX_IN)),
        ],
        out_specs=[
            pl.BlockSpec((bsz, t_chunk, MIX_IN), lambda c: (0, c, 0)),
            _full((bsz, SSM_COLS)), _full((bsz, SSM_COLS)),
        ],
        out_shape=[jax.ShapeDtypeStruct((bsz, length, MIX_IN), BF16), state, state],
        scratch_shapes=[
            pltpu.VMEM((rows, 2 * SSM_COLS), F32),
            pltpu.VMEM((bsz, SSM_COLS), F32),
            pltpu.VMEM((bsz, SSM_COLS), F32),
        ],
        compiler_params=_params(1),
        name="s5_mixer",
    )(u, h0_re, h0_im, lam_re, lam_im, w_in, w_out, d_skip, w_glu, b_glu)


def _ffn_kernel(h_ref, mix_ref, mem_ref, wo_ref, g_ref, win_ref, cw_ref, cb_ref, w2_ref, ctx_ref,
                out_ref, new_ref, act_ref, pad_ref, carry_ref, *, seqs, rows_per_seq, tiles_per_seq):
    m = pl.program_id(0)
    tm = seqs * rows_per_seq
    h1 = h_ref[...] + _dot(mix_ref[...], wo_ref[:MIX_IN, :]) + _dot(mem_ref[...], wo_ref[MIX_IN:, :])
    out_ref[...] = h1
    xn = _rms_rows(h1, g_ref[...]).astype(BF16)

    if tiles_per_seq > 1:
        @pl.when(m == 0)
        def _():
            carry_ref[...] = jnp.zeros(carry_ref.shape, F32)

    def causal_conv(col, slot):
        cols = slice(col, col + FFN_TF)
        u = _dot(xn, win_ref[:, cols])
        ctx = ctx_ref[:, :, cols]
        if tiles_per_seq > 1:
            ctx = jnp.where(m % tiles_per_seq == 0, ctx, carry_ref[:, cols][None])
        cw = cw_ref[:, cols]
        if seqs == 1:
            row = lax.broadcasted_iota(jnp.int32, (SUBLANES, 1), 0)
            back1 = pltpu.roll(u, 1, axis=0)
            back2 = pltpu.roll(u, 2, axis=0)
            head1 = jnp.where(row == 0, ctx[0, 1:2], back1[:SUBLANES])
            head2 = jnp.where(row == 0, ctx[0, 0:1], jnp.where(row == 1, ctx[0, 1:2], back2[:SUBLANES]))
            back1 = jnp.concatenate([head1, back1[SUBLANES:]], axis=0)
            back2 = jnp.concatenate([head2, back2[SUBLANES:]], axis=0)
            y = cb_ref[:, cols] + back2 * cw[0:1] + back1 * cw[1:2] + u * cw[2:3]
            last2 = u[tm - 2:][None]
        else:
            pad = pad_ref.at[slot]
            pad[:, 6:8, :] = ctx
            pad[:, 8:, :] = u.reshape(seqs, rows_per_seq, FFN_TF)
            y = (cb_ref[:, cols] + pad[:, 6:6 + rows_per_seq, :] * cw[0:1]
                 + pad[:, 7:7 + rows_per_seq, :] * cw[1:2]
                 + pad[:, 8:8 + rows_per_seq, :] * cw[2:3]).reshape(tm, FFN_TF)
            last2 = pad[:, rows_per_seq + 6:rows_per_seq + 8, :]
        new_ref[:, :, cols] = last2
        if tiles_per_seq > 1:
            carry_ref[:, cols] = last2[0]
        return y

    for f in range(N_FF_TILES):
        ya = causal_conv(f * FFN_TF, (2 * f) % FFN_PAD_SLOTS)
        yg = causal_conv(D_FF + f * FFN_TF, (2 * f + 1) % FFN_PAD_SLOTS)
        act_ref[:, f * FFN_TF:(f + 1) * FFN_TF] = (jax.nn.silu(yg) * ya).astype(BF16)
    out_ref[...] += _dot(act_ref[...], w2_ref[...])


def _mixout_ffn(h, mix, mem, w, layer, ffn_w, ctx, rows_per_seq_tile):
    nseq, length, _ = h.shape
    m_rows = nseq * length
    if rows_per_seq_tile >= length:
        seqs, rps, tps = nseq, length, 1
    else:
        seqs, rps, tps = 1, rows_per_seq_tile, length // rows_per_seq_tile
    tm = seqs * rps
    n_m = m_rows // tm
    kern = functools.partial(_ffn_kernel, seqs=seqs, rows_per_seq=rps, tiles_per_seq=tps)
    row = lambda width: pl.BlockSpec((tm, width), lambda m: (m, 0))
    resident = lambda *shape: pl.BlockSpec(shape, lambda m: (0,) * len(shape), pipeline_mode=pl.Buffered(1))
    of_layer = lambda *shape: pl.BlockSpec((None,) + shape, lambda m: (layer,) + (0,) * len(shape),
                                           pipeline_mode=pl.Buffered(1))
    out, new = pl.pallas_call(
        kern,
        grid=(n_m,),
        in_specs=[
            row(D_MODEL), row(MIX_IN), row(MEM_WIDTH),
            resident(MIX_IN + MEM_WIDTH, D_MODEL),
            of_layer(1, D_MODEL),
            resident(D_MODEL, 2 * D_FF),
            of_layer(CONV_W, 2 * D_FF),
            of_layer(1, 2 * D_FF),
            resident(D_FF, D_MODEL),
            pl.BlockSpec((seqs, CONV_W - 1, 2 * D_FF), lambda m: (m // tps, 0, 0)),
        ],
        out_specs=[row(D_MODEL), pl.BlockSpec((seqs, CONV_W - 1, 2 * D_FF), lambda m: (m, 0, 0))],
        out_shape=[
            jax.ShapeDtypeStruct((m_rows, D_MODEL), F32),
            jax.ShapeDtypeStruct((n_m * seqs, CONV_W - 1, 2 * D_FF), F32),
        ],
        scratch_shapes=[
            pltpu.VMEM((tm, D_FF), BF16),
            pltpu.VMEM((FFN_PAD_SLOTS, seqs, rps + SUBLANES, FFN_TF) if seqs > 1 else (1, 1, SUBLANES, LANES), F32),
            pltpu.VMEM((CONV_W - 1, 2 * D_FF), F32),
        ],
        compiler_params=_params(1),
        name="mix_out_conv_ffn",
    )(h.reshape(m_rows, D_MODEL), mix.reshape(m_rows, MIX_IN), mem.reshape(m_rows, MEM_WIDTH),
      ffn_w[0], w["norm_ffn_g"], ffn_w[1], w["ffn_conv_w"], w["ffn_conv_b"], ffn_w[2], ctx)
    new_ctx = new.reshape(nseq, tps, CONV_W - 1, 2 * D_FF)[:, -1]
    return out.reshape(nseq, length, D_MODEL), new_ctx


def _dkv_kernel(h_ref, g_ref, wl_ref, wrr_ref, lg_ref, kg_ref, kgr_ref, cos_ref, sin_ref, lat_ref, kr_ref):
    xn = _rms_rows(h_ref[...], g_ref[...]).astype(BF16)
    lat_ref[...] = _rms_rows(_two_row_halves(_dot, xn, wl_ref[...]), lg_ref[...])
    both = _dot(xn, wrr_ref[...])
    kr, kr_rot = both[:, :ROPE_DIM], both[:, ROPE_DIM:]
    r = lax.rsqrt(jnp.mean(kr * kr, axis=-1, keepdims=True) + EPS)
    kr_ref[...] = (kr * r * kg_ref[...]) * cos_ref[...] + (kr_rot * r * kgr_ref[...]) * sin_ref[...]


def _swap_halves(t, axis=-1):
    a, b = jnp.split(t, 2, axis=axis)
    return jnp.concatenate([b, a], axis=axis)


def _shared_kv_down(h, kv_norm_g, w_dkv_bf, latent_norm_g, krope_norm_g, cos_rows, sin_rows, tm):
    nseq, length, _ = h.shape
    m_rows = nseq * length
    tab_blocks = cos_rows.shape[0] // tm
    w_l = w_dkv_bf[:, :KV_LORA]
    w_r = w_dkv_bf[:, KV_LORA:]
    kg = krope_norm_g.reshape(1, ROPE_DIM)
    lat, kr = pl.pallas_call(
        _dkv_kernel,
        grid=(m_rows // tm,),
        in_specs=[
            pl.BlockSpec((tm, D_MODEL), lambda m: (m, 0)),
            _full((1, D_MODEL)),
            _full((D_MODEL, KV_LORA)), _full((D_MODEL, 2 * ROPE_DIM)),
            _full((1, KV_LORA)), _full((1, ROPE_DIM)), _full((1, ROPE_DIM)),
            pl.BlockSpec((tm, ROPE_DIM), lambda m: (m % tab_blocks, 0)),
            pl.BlockSpec((tm, ROPE_DIM), lambda m: (m % tab_blocks, 0)),
        ],
        out_specs=[pl.BlockSpec((tm, KV_LORA), lambda m: (m, 0)),
                   pl.BlockSpec((tm, ROPE_DIM), lambda m: (m, 0))],
        out_shape=[jax.ShapeDtypeStruct((m_rows, KV_LORA), F32),
                   jax.ShapeDtypeStruct((m_rows, ROPE_DIM), F32)],
        compiler_params=_params(1),
        name="shared_kv_down",
    )(h.reshape(m_rows, D_MODEL), kv_norm_g.reshape(1, D_MODEL), w_l,
      jnp.concatenate([w_r, _swap_halves(w_r)], axis=1),
      latent_norm_g.reshape(1, KV_LORA), kg, _swap_halves(kg), cos_rows, sin_rows)
    return lat.reshape(nseq, length, KV_LORA), kr.reshape(nseq, length, ROPE_DIM)


def _kv_up_kernel(lat_ref, wk_ref, wv_ref, seg_ref, g_ref, k_ref, v_ref):
    lat = lat_ref[...].astype(BF16)
    k_ref[...] = _seg_rms(_dot(lat, wk_ref[...]), seg_ref[...], g_ref[...]).astype(BF16)
    v_ref[...] = _dot(lat, wv_ref[...]).astype(BF16)


def _kv_up(latent_rows, w_uk_bf, w_uv_bf, seg_nope, k_g_tiled, tr):
    rows = latent_rows.shape[0]
    width = MLA_HEADS * NOPE_DIM
    return pl.pallas_call(
        _kv_up_kernel,
        grid=(rows // tr,),
        in_specs=[
            pl.BlockSpec((tr, KV_LORA), lambda r: (r, 0)),
            _full((KV_LORA, width)), _full((KV_LORA, width)),
            _full((width, width)), _full((1, width)),
        ],
        out_specs=[pl.BlockSpec((tr, width), lambda r: (r, 0))] * 2,
        out_shape=[jax.ShapeDtypeStruct((rows, width), BF16)] * 2,
        compiler_params=_params(1),
        name="kv_up",
    )(latent_rows, w_uk_bf, w_uv_bf, seg_nope, k_g_tiled)


Q_PRESCALE = MLA_SCALE * math.log2(math.e)


def _q_kernel(z_ref, g_ref, wn_ref, wa_ref, segn_ref, segr_ref, gn_ref, ga_ref,
              cos_ref, sin_ref, kg_ref, wuk_ref, qn_ref, qr_ref, *maybe_absorbed_refs):
    xn = _rms_rows(z_ref[...], g_ref[...]).astype(BF16)
    qn = _seg_rms(_dot(xn, wn_ref[...]), segn_ref[...], gn_ref[...]) * Q_PRESCALE
    qn_ref[...] = qn.astype(BF16)
    a = _dot(xn, wa_ref[...])
    an = a * lax.rsqrt(_dot((a * a).astype(BF16), segr_ref[...]) + EPS) * ga_ref[...]
    width = an.shape[1]
    lane = lax.broadcasted_iota(jnp.int32, (1, width), 1)
    partner = jnp.where(lane % ROPE_DIM < ROPE_HALF,
                        pltpu.roll(an, width - ROPE_HALF, axis=1), pltpu.roll(an, ROPE_HALF, axis=1))
    rot = an * cos_ref[...] + partner * sin_ref[...]
    qr_ref[...] = (rot * Q_PRESCALE).astype(BF16)
    if maybe_absorbed_refs:
        qp_ref, qrh_ref = maybe_absorbed_refs
        lane = lax.broadcasted_iota(jnp.int32, (1, MLA_HEADS * NOPE_DIM), 1)
        qg = qn * kg_ref[...]
        rot_bf = (rot * Q_PRESCALE).astype(BF16)
        for head in range(MLA_HEADS):
            qh = jnp.where(lane // NOPE_DIM == head, qg, 0.0).astype(BF16)
            qp_ref[head] = _dot_nt(qh, wuk_ref[...]).astype(BF16)
            qrh_ref[head] = rot_bf[:, head * ROPE_DIM:(head + 1) * ROPE_DIM]


def _q_side(z_mix, q_latent_g, w_uq_bf, q_nope_g, q_rope_g, seg_nope, seg_rope, cos_rows, sin_rows,
            k_g_tiled, w_uk_bf, tm, absorbed):
    nseq, length, _ = z_mix.shape
    m_rows = nseq * length
    tab_blocks = cos_rows.shape[0] // tm
    wn_width = MLA_HEADS * NOPE_DIM
    wr_width = MLA_HEADS * ROPE_DIM
    w3 = w_uq_bf.reshape(MIX_IN, MLA_HEADS, NOPE_DIM + ROPE_DIM)
    w_n = w3[:, :, :NOPE_DIM].reshape(MIX_IN, wn_width)
    w_a = w3[:, :, NOPE_DIM:].reshape(MIX_IN, wr_width)
    g_n = jnp.tile(q_nope_g.reshape(1, NOPE_DIM), (1, MLA_HEADS))
    g_a = jnp.tile(q_rope_g.reshape(1, ROPE_DIM), (1, MLA_HEADS))
    cos_q = jnp.tile(cos_rows, (1, MLA_HEADS))
    sin_q = jnp.tile(sin_rows, (1, MLA_HEADS))
    out_specs = [pl.BlockSpec((tm, wd), lambda m: (m, 0)) for wd in (wn_width, wr_width)]
    out_shape = [jax.ShapeDtypeStruct((m_rows, wd), BF16) for wd in (wn_width, wr_width)]
    if absorbed:
        for wd in (KV_LORA, ROPE_DIM):
            out_specs.append(pl.BlockSpec((MLA_HEADS, tm, wd), lambda m: (0, m, 0)))
            out_shape.append(jax.ShapeDtypeStruct((MLA_HEADS, m_rows, wd), BF16))
    outs = pl.pallas_call(
        _q_kernel,
        grid=(m_rows // tm,),
        in_specs=[
            pl.BlockSpec((tm, MIX_IN), lambda m: (m, 0)),
            _full((1, MIX_IN)),
            _full((MIX_IN, wn_width)), _full((MIX_IN, wr_width)),
            _full((wn_width, wn_width)), _full((wr_width, wr_width)),
            _full((1, wn_width)), _full((1, wr_width)),
            pl.BlockSpec((tm, wr_width), lambda m: (m % tab_blocks, 0)),
            pl.BlockSpec((tm, wr_width), lambda m: (m % tab_blocks, 0)),
            _full((1, wn_width)), _full((KV_LORA, wn_width)),
        ],
        out_specs=out_specs,
        out_shape=out_shape,
        compiler_params=_params(1),
        name="mla_query",
    )(z_mix.reshape(m_rows, MIX_IN), q_latent_g.reshape(1, MIX_IN), w_n, w_a, seg_nope, seg_rope,
      g_n, g_a, cos_q, sin_q, k_g_tiled, w_uk_bf)
    return outs


ATTN_HEADS_PER_STEP = 4


def _attn_kernel(qn_ref, qr_ref, kn_ref, kr_ref, v_ref, o_ref, *, tile, n_tiles):
    qi = pl.program_id(2)
    pairs = ATTN_HEADS_PER_STEP // 2
    lane = lax.broadcasted_iota(jnp.int32, (1, LANES), 1)
    head_lanes = [(lane // NOPE_DIM) == j for j in range(2)]
    qr = qr_ref[0]
    qcat = []
    for h in range(ATTN_HEADS_PER_STEP):
        qn = qn_ref[0, :, (h // 2) * LANES:(h // 2 + 1) * LANES]
        qcat.append(jnp.concatenate(
            [jnp.where(head_lanes[h % 2], qn, jnp.zeros_like(qn)),
             jnp.where((lane // ROPE_DIM) == h, qr, jnp.zeros_like(qr))], axis=1))
    row_chunk = lax.broadcasted_iota(jnp.int32, (tile, 1), 0) // CHUNK
    col_chunk = lax.broadcasted_iota(jnp.int32, (1, tile), 1) // CHUNK
    diag_visible = col_chunk <= row_chunk

    def one_block(kb, carry, masked):
        rows_k = slice(kb * tile, (kb + 1) * tile)
        kr = kr_ref[0, rows_k, :]
        new = []
        for h in range(ATTN_HEADS_PER_STEP):
            lanes_p = slice((h // 2) * LANES, (h // 2 + 1) * LANES)
            m_i, acc = carry[h]
            kcat = jnp.concatenate([kn_ref[0, rows_k, lanes_p], kr], axis=1)
            s = _dot_nt(qcat[h], kcat)
            if masked:
                s = jnp.where(diag_visible, s, NEG_INF)
            m_new = jnp.maximum(m_i, jnp.max(s, axis=-1, keepdims=True))
            alpha = jnp.exp2(m_i - m_new)
            p = jnp.exp2(s - m_new).astype(BF16)
            vb = v_ref[0, rows_k, lanes_p]
            v_h = jnp.where(head_lanes[h % 2], vb, jnp.ones_like(vb))
            new.append((m_new, alpha * acc + _dot(p, v_h)))
        return tuple(new)

    def query_tile(n_full):
        carry = tuple((jnp.full((tile, 1), NEG_INF, F32), jnp.zeros((tile, LANES), F32))
                      for _ in range(ATTN_HEADS_PER_STEP))
        for kb in range(n_full):
            carry = one_block(kb, carry, False)
        carry = one_block(n_full, carry, True)
        for p in range(pairs):
            out = jnp.zeros((tile, LANES), F32)
            for j in range(2):
                acc = carry[2 * p + j][1]
                row_sum = pltpu.roll(acc, NOPE_DIM, axis=1)
                out = jnp.where(head_lanes[j], acc * (1.0 / row_sum), out)
            o_ref[0, :, p * LANES:(p + 1) * LANES] = out.astype(BF16)

    for c in range(n_tiles):
        pl.when(qi == c)(functools.partial(query_tile, c))


def _mla_attention(qn, qr, kn, kr4, v, tile):
    nseq, length, _ = qn.shape
    kern = functools.partial(_attn_kernel, tile=tile, n_tiles=length // tile)
    width = (ATTN_HEADS_PER_STEP // 2) * LANES
    return pl.pallas_call(
        kern,
        grid=(nseq, MLA_HEADS // ATTN_HEADS_PER_STEP, length // tile),
        in_specs=[
            pl.BlockSpec((1, tile, width), lambda b, g, i: (b, i, g)),
            pl.BlockSpec((1, tile, LANES), lambda b, g, i: (b, i, g)),
            pl.BlockSpec((1, length, width), lambda b, g, i: (b, 0, g)),
            pl.BlockSpec((1, length, LANES), lambda b, g, i: (b, 0, 0)),
            pl.BlockSpec((1, length, width), lambda b, g, i: (b, 0, g)),
        ],
        out_specs=pl.BlockSpec((1, tile, width), lambda b, g, i: (b, i, g)),
        out_shape=jax.ShapeDtypeStruct((nseq, length, MLA_HEADS * V_DIM), BF16),
        compiler_params=_params(3),
        name="mla_attention",
    )(qn, qr, kn, kr4, v)


KNORM_ROWS = 16


def _key_norm_kernel(past_ref, new_ref, wukt_ref, rt_ref, *, n_chunks, chunk):
    def norms(lat):
        keys = lat.shape[0]
        kt = _dot_nt(wukt_ref[...], lat)
        ss = jnp.sum((kt * kt).reshape(MLA_HEADS, NOPE_DIM, keys), axis=1)
        r = lax.rsqrt(ss * (1.0 / NOPE_DIM) + EPS)
        return jnp.concatenate([r, jnp.ones((KNORM_ROWS - MLA_HEADS, keys), F32)], axis=0)

    for c in range(n_chunks):
        rt_ref[0, :, c * chunk:(c + 1) * chunk] = norms(past_ref[0, c * chunk:(c + 1) * chunk, :].astype(BF16))
    rt_ref[0, :, n_chunks * chunk:] = norms(new_ref[0])


def _key_norms(past_latent, new_lat_pad, w_ukt_bf, chunk):
    nseq, past, _ = past_latent.shape
    new_pad = new_lat_pad.shape[1]
    lk_pad = past + new_pad
    kern = functools.partial(_key_norm_kernel, n_chunks=past // chunk, chunk=chunk)
    return pl.pallas_call(
        kern,
        grid=(nseq,),
        in_specs=[pl.BlockSpec((1, past, KV_LORA), lambda b: (b, 0, 0)),
                  pl.BlockSpec((1, new_pad, KV_LORA), lambda b: (b, 0, 0)),
                  _full((MLA_HEADS * NOPE_DIM, KV_LORA))],
        out_specs=pl.BlockSpec((1, KNORM_ROWS, lk_pad), lambda b: (b, 0, 0)),
        out_shape=jax.ShapeDtypeStruct((nseq, KNORM_ROWS, lk_pad), F32),
        compiler_params=_params(1),
        name="mla_key_norms",
    )(past_latent, new_lat_pad, w_ukt_bf)


def _attn_absorbed_kernel(qs_ref, qrs_ref, plat_ref, nlat_ref, pkrt_ref, nkrt_ref, rt_ref, wuv_ref, o_ref, op_ref,
                          *, tq, q_off, lk_valid):
    b = pl.program_id(0)
    nseq = pl.num_programs(0)
    rows = MLA_HEADS * tq
    lat = jnp.concatenate([plat_ref[0].astype(BF16), nlat_ref[0]], axis=0)
    kr_t = jnp.concatenate([pkrt_ref[0].astype(BF16), nkrt_ref[0]], axis=1)
    lk_pad = lat.shape[0]
    rt = rt_ref[0]
    knorm = jnp.concatenate([jnp.broadcast_to(rt[h:h + 1, :], (tq, lk_pad)) for h in range(MLA_HEADS)], axis=0)
    qs = qs_ref[...].reshape(rows, KV_LORA)
    qrs = qrs_ref[...].reshape(rows, ROPE_DIM)
    s = _dot_nt(qs, lat) * knorm + _dot(qrs, kr_t)
    k_pos = lax.broadcasted_iota(jnp.int32, (1, lk_pad), 1)
    if (lk_valid - 1) // CHUNK > q_off // CHUNK:
        q_chunk = (q_off + lax.broadcasted_iota(jnp.int32, (rows, 1), 0) % tq) // CHUNK
        s = jnp.where(k_pos // CHUNK <= q_chunk, s, NEG_INF)
    s = jnp.where(k_pos < lk_valid, s, NEG_INF)
    p = jnp.exp2(s - jnp.max(s, axis=-1, keepdims=True))
    l = jnp.sum(p, axis=-1, keepdims=True)
    o = _two_row_halves(_dot, p.astype(BF16), lat)
    op_ref[b] = (o * (1.0 / l)).astype(BF16)

    @pl.when(b == nseq - 1)
    def _():
        n_all = op_ref.shape[0]
        lane_o = lax.broadcasted_iota(jnp.int32, (1, MLA_HEADS * V_DIM), 1)
        out = jnp.zeros((n_all * tq, MLA_HEADS * V_DIM), F32)
        for h in range(MLA_HEADS):
            x = op_ref[:, h * tq:(h + 1) * tq, :].reshape(n_all * tq, KV_LORA)
            out = jnp.where(lane_o // V_DIM == h, _dot(x, wuv_ref[...]), out)
        o_ref[...] = out.astype(BF16)


def _mla_attention_absorbed(qp, qrh, past_latent, new_lat_pad, past_krope_t, new_kr_pad_t, r_t, w_uv_bf,
                            tq, lk_valid):
    nseq, past, _ = past_latent.shape
    new_pad = new_lat_pad.shape[1]
    kern = functools.partial(_attn_absorbed_kernel, tq=tq, q_off=past, lk_valid=lk_valid)
    per_seq = lambda n, width: pl.BlockSpec((1, n, width), lambda b: (b, 0, 0))
    heads_of_seq = lambda width: pl.BlockSpec((MLA_HEADS, tq, width), lambda b: (0, b, 0))
    out = pl.pallas_call(
        kern,
        grid=(nseq,),
        in_specs=[
            heads_of_seq(KV_LORA), heads_of_seq(ROPE_DIM),
            per_seq(past, KV_LORA), per_seq(new_pad, KV_LORA),
            per_seq(ROPE_DIM, past), per_seq(ROPE_DIM, new_pad),
            pl.BlockSpec((1, KNORM_ROWS, past + new_pad), lambda b: (b, 0, 0)),
            _full((KV_LORA, MLA_HEADS * V_DIM)),
        ],
        out_specs=_full((nseq * tq, MLA_HEADS * V_DIM)),
        out_shape=jax.ShapeDtypeStruct((nseq * tq, MLA_HEADS * V_DIM), BF16),
        scratch_shapes=[pltpu.VMEM((nseq, MLA_HEADS * tq, KV_LORA), BF16)],
        compiler_params=_params(1),
        name="mla_attention_absorbed",
    )(qp, qrh, past_latent, new_lat_pad, past_krope_t, new_kr_pad_t, r_t, w_uv_bf)
    return out.reshape(nseq, tq, MLA_HEADS * V_DIM)


def _trunk(x, mem_k_bf, mem_v_bf, ssm_h0_re, ssm_h0_im, conv_ctx, past_latent, past_krope, w, cfg, ffn_bf=None):
    nseq, length, _ = x.shape
    past = 0 if past_latent is None else past_latent.shape[1]
    cos_t, sin_t = _rope_tables(past, length)
    reps = cfg["rope_rows"] // length
    cos_rows = jnp.tile(cos_t, (reps, 1))
    sin_rows = jnp.tile(sin_t, (reps, 1))
    if conv_ctx is None:
        conv_ctx = jnp.zeros((DEPTH, nseq, CONV_W - 1, 2 * D_FF), F32)
    if ssm_h0_re is None:
        ssm_h0_re = jnp.zeros((N_A_LAYERS, nseq, SSM_GROUPS, SSM_STATE), F32)
        ssm_h0_im = ssm_h0_re
    h = x
    ssm_re_out, ssm_im_out, conv_out = [], [], []
    make_ffn_bf = ffn_bf is None
    if make_ffn_bf:
        ffn_bf = []
    for layer in range(DEPTH):
        to_cast = (w["w_mix_out_f32"], w["w_ffn_in_f32"], w["w_ffn_out_f32"]) if make_ffn_bf else ()
        z_mix, mem_out, cast = _mixin(h, w, layer, mem_k_bf, mem_v_bf, cfg["tm_mixin"], to_cast)
        if make_ffn_bf:
            ffn_bf.append(cast)
        if layer < N_A_LAYERS:
            i = layer
            mix_out, s_re, s_im = _s5_mixer(
                z_mix, ssm_h0_re[i].reshape(nseq, SSM_COLS), ssm_h0_im[i].reshape(nseq, SSM_COLS),
                w["lam_re"][i], w["lam_im"][i], w["s5_in"][i], w["s5_out"][i],
                w["ssm_d"][i][None], w["w_glu"][i], w["b_glu"][i][None], cfg["t_chunk"])
            ssm_re_out.append(s_re.reshape(nseq, SSM_GROUPS, SSM_STATE))
            ssm_im_out.append(s_im.reshape(nseq, SSM_GROUPS, SSM_STATE))
        else:
            if layer == N_A_LAYERS:
                new_latent, new_krope = _shared_kv_down(
                    h, w["kv_norm_g"], w["w_dkv"], w["latent_norm_g"], w["krope_norm_g"],
                    cos_rows, sin_rows, cfg["tm_rows"])
                if cfg["absorbed"]:
                    assert past % LANES == 0
                    lk_valid = past + length
                    new_pad = -(-length // LANES) * LANES
                    pad_rows = lambda t: jnp.pad(t.astype(BF16), ((0, 0), (0, new_pad - length), (0, 0)))
                    new_lat_pad = pad_rows(new_latent)
                    new_kr_pad_t = jnp.swapaxes(pad_rows(new_krope), 1, 2)
                    past_krope_t = jnp.swapaxes(past_krope, 1, 2)
                    r_t = _key_norms(past_latent, new_lat_pad, w["w_uk_t"], cfg["knorm_chunk"])
                else:
                    assert past_latent is None
                    kn, v_all = _kv_up(new_latent.reshape(nseq * length, KV_LORA), w["w_uk"], w["w_uv"],
                                       w["seg_nope"], w["k_nope_g"], cfg["tr_kv"])
                    kn = kn.reshape(nseq, length, MLA_HEADS * NOPE_DIM)
                    v_all = v_all.reshape(nseq, length, MLA_HEADS * V_DIM)
                    kr4 = jnp.tile(new_krope.astype(BF16), (1, 1, LANES // ROPE_DIM))
            j = layer - N_A_LAYERS
            q_out = _q_side(z_mix, w["q_latent_norm_g"][j], w["w_uq"][j], w["q_nope_norm_g"][j],
                            w["q_rope_norm_g"][j], w["seg_nope"], w["seg_rope"], cos_rows, sin_rows,
                            w["k_nope_g"], w["w_uk"], cfg["tm_rows"], cfg["absorbed"])
            if cfg["absorbed"]:
                mix_out = _mla_attention_absorbed(q_out[2], q_out[3], past_latent, new_lat_pad, past_krope_t,
                                                  new_kr_pad_t, r_t, w["w_uv"], length, lk_valid)
            else:
                qn = q_out[0].reshape(nseq, length, MLA_HEADS * NOPE_DIM)
                qr = q_out[1].reshape(nseq, length, MLA_HEADS * ROPE_DIM)
                mix_out = _mla_attention(qn, qr, kn, kr4, v_all, cfg["tq"])
        h, ctx = _mixout_ffn(h, mix_out, mem_out, w, layer, ffn_bf[layer], conv_ctx[layer], cfg["ffn_rows"])
        conv_out.append(ctx)
    return (h, new_latent, new_krope, jnp.stack(ssm_re_out), jnp.stack(ssm_im_out), jnp.stack(conv_out)), ffn_bf


PROMPT_CFG = dict(tm_mixin=1024, t_chunk=64, ffn_rows=512, tm_rows=1024, rope_rows=2048, tr_kv=1024,
                  tq=512, absorbed=False)
SAMPLE_CFG = dict(tm_mixin=32, t_chunk=32, ffn_rows=32, tm_rows=512, rope_rows=512, knorm_chunk=1024,
                  absorbed=True)


def kernel(x_prompt, x_sample, cache_mla_latent, cache_mla_krope, cache_mem_k, cache_mem_v, state_ssm_re, state_ssm_im, state_conv, mem_prompt, norm_mix_g, w_mix_in, w_mix_out, norm_ffn_g, w_ffn_in, ffn_conv_w, ffn_conv_b, w_ffn_out, mem_norm_g, w_mem_kv, mem_q_norm_g, mem_k_norm_g, ssm_a_re, ssm_a_im, ssm_log_dt, ssm_b_re, ssm_b_im, ssm_c_re, ssm_c_im, ssm_d, w_glu, b_glu, kv_norm_g, w_dkv, latent_norm_g, krope_norm_g, w_uk, w_uv, k_nope_norm_g, q_latent_norm_g, w_uq, q_nope_norm_g, q_rope_norm_g):
    bf = lambda t: t.astype(BF16)
    seg_mem = _seg_matrix(MEM_WIDTH, MEM_HEAD_DIM)
    lam_re, lam_im, s5_in, s5_out = _s5_prepare(ssm_a_re, ssm_a_im, ssm_log_dt, ssm_b_re, ssm_b_im,
                                                ssm_c_re, ssm_c_im)
    w = dict(
        norm_mix_g=norm_mix_g.reshape(DEPTH, 1, D_MODEL), w_mix_in=w_mix_in,
        norm_ffn_g=norm_ffn_g.reshape(DEPTH, 1, D_MODEL),
        ffn_conv_w=ffn_conv_w, ffn_conv_b=ffn_conv_b.reshape(DEPTH, 1, 2 * D_FF),
        w_mix_out_f32=w_mix_out, w_ffn_in_f32=w_ffn_in, w_ffn_out_f32=w_ffn_out,
        mem_q_g=jnp.tile(mem_q_norm_g, (1, MEM_HEADS)).reshape(DEPTH, 1, MEM_WIDTH),
        seg_mem=seg_mem,
        seg_nope=_seg_matrix(MLA_HEADS * NOPE_DIM, NOPE_DIM),
        seg_rope=_seg_matrix(MLA_HEADS * ROPE_DIM, ROPE_DIM),
        lam_re=lam_re, lam_im=lam_im, s5_in=s5_in, s5_out=s5_out,
        ssm_d=ssm_d, w_glu=w_glu, b_glu=b_glu,
        kv_norm_g=kv_norm_g, w_dkv=bf(w_dkv), latent_norm_g=latent_norm_g, krope_norm_g=krope_norm_g,
        w_uk=bf(w_uk), w_uk_t=bf(w_uk).T, w_uv=bf(w_uv),
        k_nope_g=jnp.tile(k_nope_norm_g.reshape(1, NOPE_DIM), (1, MLA_HEADS)),
        q_latent_norm_g=q_latent_norm_g, w_uq=bf(w_uq), q_nope_norm_g=q_nope_norm_g,
        q_rope_norm_g=q_rope_norm_g,
    )
    bsz = mem_prompt.shape[0]
    mem_k_p, mem_v_p, mem_k_bf, mem_v_bf = _memory_kv(mem_prompt, mem_norm_g, w_mem_kv, mem_k_norm_g, seg_mem)
    (y_prompt, lat_p, krope_p, ssm_re_p, ssm_im_p, conv_p), ffn_bf = _trunk(
        x_prompt, mem_k_bf, mem_v_bf, None, None, None, None, None, w, PROMPT_CFG)
    dec = cache_mem_k.shape[1]
    (y_sample, lat_s, krope_s, ssm_re_s, ssm_im_s, conv_s), _ = _trunk(
        x_sample, bf(cache_mem_k).reshape(DEPTH, dec, N_MEM, MEM_WIDTH),
        bf(cache_mem_v).reshape(DEPTH, dec, N_MEM, MEM_WIDTH),
        state_ssm_re, state_ssm_im, state_conv, cache_mla_latent, cache_mla_krope, w, SAMPLE_CFG, ffn_bf)
    shape5 = (DEPTH, bsz, N_MEM, MEM_HEADS, MEM_HEAD_DIM)
    return (y_prompt, y_sample, mem_k_p.reshape(shape5), mem_v_p.reshape(shape5), lat_p, krope_p,
            ssm_re_p, ssm_im_p, conv_p, lat_s, krope_s, ssm_re_s, ssm_im_s, conv_s)
```

```python
import functools
import math

import jax
import jax.numpy as jnp
from jax import lax
from jax.experimental import pallas as pl
from jax.experimental.pallas import tpu as pltpu

F32 = jnp.float32
BF16 = jnp.bfloat16

D_MODEL = 1024
DEPTH = 4
CHUNK = 64
N_A_LAYERS = DEPTH // 2
N_B_LAYERS = DEPTH - N_A_LAYERS
MIX_IN = 768
MEM_HEADS = 4
MEM_HEAD_DIM = 64
MEM_WIDTH = MEM_HEADS * MEM_HEAD_DIM
N_MEM = 256
SSM_GROUP = 16
SSM_GROUPS = MIX_IN // SSM_GROUP
SSM_STATE = 64
SSM_COLS = SSM_GROUPS * SSM_STATE
MLA_HEADS = 12
NOPE_DIM = 64
ROPE_DIM = 32
ROPE_HALF = ROPE_DIM // 2
V_DIM = 64
KV_LORA = 256
ROPE_BASE = 10000.0
MLA_SCALE = (NOPE_DIM + ROPE_DIM) ** -0.5
MEM_SCALE = MEM_HEAD_DIM ** -0.5
D_FF = 2816
CONV_W = 3
EPS = 1e-6
NEG_INF = -1e30

V7X_VMEM_LIMIT_BYTES = 56 * 1024 * 1024
LANES = 128
SUBLANES = 8
V7X_MXU_DIM = 256

S5_BLOCKS = 3
S5_BLOCK_CH = MIX_IN // S5_BLOCKS
S5_BLOCK_ST = SSM_COLS // S5_BLOCKS
FFN_TF = 256
N_FF_TILES = D_FF // FFN_TF
FFN_PAD_SLOTS = 4


def _params(n_axes):
    return pltpu.CompilerParams(
        dimension_semantics=("arbitrary",) * n_axes,
        vmem_limit_bytes=V7X_VMEM_LIMIT_BYTES,
    )


def _dot(a, b):
    return jnp.dot(a, b, preferred_element_type=F32)


def _dot_nt(a, b):
    return lax.dot_general(a, b, (((1,), (1,)), ((), ())), preferred_element_type=F32)


def _two_row_halves(dot_fn, a, b):
    rows = a.shape[0]
    if rows < V7X_MXU_DIM or rows % 32:
        return dot_fn(a, b)
    return jnp.concatenate([dot_fn(a[:rows // 2], b), dot_fn(a[rows // 2:], b)], axis=0)


def _rms_rows(x, g):
    ms = jnp.mean(x * x, axis=-1, keepdims=True)
    return x * lax.rsqrt(ms + EPS) * g


def _seg_rms(x, seg_mat, g):
    ms = _dot((x * x).astype(BF16), seg_mat)
    return x * lax.rsqrt(ms + EPS) * g


def _seg_matrix(width, seg):
    idx = jnp.arange(width) // seg
    return jnp.where(idx[:, None] == idx[None, :], 1.0 / seg, 0.0).astype(BF16)


def _full(shape):
    nd = len(shape)
    return pl.BlockSpec(shape, lambda *_: (0,) * nd)


def _rope_table_kernel(pos_ref, inv_ref, cos_ref, sin_ref):
    ang = pos_ref[...] * inv_ref[...]
    cos_ref[...] = jnp.cos(ang)
    sin_ref[...] = jnp.sin(ang)


def _rope_tables(past, length):
    pos = (past + jnp.arange(length, dtype=jnp.int32)).astype(F32)[:, None]
    inv_freq = (1.0 / (ROPE_BASE ** (jnp.arange(0, ROPE_DIM, 2, dtype=F32) / ROPE_DIM)))[None, :]
    cos, sin = pl.pallas_call(
        _rope_table_kernel,
        out_shape=[jax.ShapeDtypeStruct((length, ROPE_HALF), F32)] * 2,
        name="rope_table",
    )(pos, inv_freq)
    cos_t = jnp.concatenate([cos, cos], axis=1)
    sin_t = jnp.concatenate([-sin, sin], axis=1)
    return cos_t, sin_t


def _memkv_kernel(mem_ref, g_ref, w_ref, kg_ref, seg_ref, k_ref, v_ref, kb_ref, vb_ref):
    bsz = mem_ref.shape[0]
    mem = mem_ref[...].reshape(bsz * N_MEM, D_MODEL)
    xn = _rms_rows(mem, g_ref[0]).astype(BF16)
    kv = _dot(xn, w_ref[0].astype(BF16))
    k = _seg_rms(kv[:, :MEM_WIDTH], seg_ref[...], kg_ref[0]).reshape(bsz, N_MEM, MEM_WIDTH)
    v = kv[:, MEM_WIDTH:].reshape(bsz, N_MEM, MEM_WIDTH)
    k_ref[0] = k
    v_ref[0] = v
    kb_ref[0] = k.astype(BF16)
    vb_ref[0] = v.astype(BF16)


def _memory_kv(mem, mem_norm_g, w_mem_kv, mem_k_norm_g, seg_mem):
    bsz = mem.shape[0]
    kg = jnp.tile(mem_k_norm_g, (1, MEM_HEADS)).reshape(DEPTH, 1, MEM_WIDTH)
    out4 = lambda dt: jax.ShapeDtypeStruct((DEPTH, bsz, N_MEM, MEM_WIDTH), dt)
    spec4 = pl.BlockSpec((1, bsz, N_MEM, MEM_WIDTH), lambda l: (l, 0, 0, 0))
    return pl.pallas_call(
        _memkv_kernel,
        grid=(DEPTH,),
        in_specs=[
            _full((bsz, N_MEM, D_MODEL)),
            pl.BlockSpec((1, 1, D_MODEL), lambda l: (l, 0, 0)),
            pl.BlockSpec((1, D_MODEL, 2 * MEM_WIDTH), lambda l: (l, 0, 0)),
            pl.BlockSpec((1, 1, MEM_WIDTH), lambda l: (l, 0, 0)),
            _full((MEM_WIDTH, MEM_WIDTH)),
        ],
        out_specs=[spec4, spec4, spec4, spec4],
        out_shape=[out4(F32), out4(F32), out4(BF16), out4(BF16)],
        compiler_params=_params(1),
        name="memory_kv",
    )(mem, mem_norm_g.reshape(DEPTH, 1, D_MODEL), w_mem_kv, kg, seg_mem)


def _mixin_kernel(h_ref, g_ref, w_ref, qg_ref, seg_ref, k_ref, v_ref, *rest, seqs, rows_per_seq, n_cast):
    cast_in, (zmix_ref, mem_ref), cast_out = rest[:n_cast], rest[n_cast:n_cast + 2], rest[n_cast + 2:]
    for src, dst in zip(cast_in, cast_out):
        dst[...] = src[...].astype(BF16)
    xn = _rms_rows(h_ref[...], g_ref[...]).astype(BF16)
    z = _dot(xn, w_ref[...].astype(BF16))
    zmix_ref[...] = z[:, :MIX_IN]
    mq = _seg_rms(z[:, MIX_IN:], seg_ref[...], qg_ref[...]).astype(BF16)
    lane = lax.broadcasted_iota(jnp.int32, (1, MEM_WIDTH), 1)
    in_head = [(lane // MEM_HEAD_DIM) == head for head in range(MEM_HEADS)]
    for b in range(seqs):
        rows = slice(b * rows_per_seq, (b + 1) * rows_per_seq)
        mq_b = mq[rows]
        qs = jnp.concatenate([jnp.where(m, mq_b, jnp.zeros_like(mq_b)) for m in in_head], axis=0)
        s = _two_row_halves(_dot_nt, qs, k_ref[b]) * MEM_SCALE
        p = jnp.exp(s - jnp.max(s, axis=-1, keepdims=True))
        p = p * (1.0 / jnp.sum(p, axis=-1, keepdims=True))
        o = _two_row_halves(_dot, p.astype(BF16), v_ref[b])
        out = jnp.zeros(mq_b.shape, F32)
        for head, m in enumerate(in_head):
            out = jnp.where(m, o[head * rows_per_seq:(head + 1) * rows_per_seq], out)
        mem_ref[rows, :] = out.astype(BF16)


def _mixin(h, w, layer, k_bf, v_bf, rows_per_seq_tile, to_cast=()):
    nseq, length, _ = h.shape
    m_rows = nseq * length
    if rows_per_seq_tile >= length:
        seqs, rps, tps = nseq, length, 1
    else:
        seqs, rps, tps = 1, rows_per_seq_tile, length // rows_per_seq_tile
    tm = seqs * rps
    n_steps = m_rows // tm
    kern = functools.partial(_mixin_kernel, seqs=seqs, rows_per_seq=rps, n_cast=len(to_cast))
    row = lambda width: pl.BlockSpec((tm, width), lambda m: (m, 0))
    of_layer = lambda *shape: pl.BlockSpec((None,) + shape, lambda m: (layer,) + (0,) * len(shape))
    kv_spec = pl.BlockSpec((None, seqs, N_MEM, MEM_WIDTH), lambda m: (layer, m // tps, 0, 0))
    slab = lambda t: t.shape[1] // n_steps
    cast_in = [pl.BlockSpec((None, slab(t), t.shape[2]), lambda m: (layer, m, 0)) for t in to_cast]
    cast_out = [pl.BlockSpec((slab(t), t.shape[2]), lambda m: (m, 0)) for t in to_cast]
    outs = pl.pallas_call(
        kern,
        grid=(n_steps,),
        in_specs=[
            row(D_MODEL),
            of_layer(1, D_MODEL),
            of_layer(D_MODEL, MIX_IN + MEM_WIDTH),
            of_layer(1, MEM_WIDTH),
            _full((MEM_WIDTH, MEM_WIDTH)),
            kv_spec, kv_spec,
        ] + cast_in,
        out_specs=[row(MIX_IN), row(MEM_WIDTH)] + cast_out,
        out_shape=[
            jax.ShapeDtypeStruct((m_rows, MIX_IN), F32),
            jax.ShapeDtypeStruct((m_rows, MEM_WIDTH), BF16),
        ] + [jax.ShapeDtypeStruct(t.shape[1:], BF16) for t in to_cast],
        compiler_params=_params(1),
        name="mix_in_mem_attn",
    )(h.reshape(m_rows, D_MODEL), w["norm_mix_g"], w["w_mix_in"], w["mem_q_g"], w["seg_mem"], k_bf, v_bf,
      *to_cast)
    return outs[0].reshape(nseq, length, MIX_IN), outs[1].reshape(nseq, length, MEM_WIDTH), outs[2:]


def _s5_discretise_kernel(are_ref, aim_ref, ldt_ref, lre_ref, lim_ref, fre_ref, fim_ref):
    a_re, a_im = are_ref[...], aim_ref[...]
    dt = jnp.exp(ldt_ref[...])
    mag = jnp.exp(a_re * dt)
    lam_re = mag * jnp.cos(a_im * dt)
    lam_im = mag * jnp.sin(a_im * dt)
    den = a_re * a_re + a_im * a_im
    x_re = lam_re - 1.0
    lre_ref[...] = lam_re
    lim_ref[...] = lam_im
    fre_ref[...] = (x_re * a_re + lam_im * a_im) / den
    fim_ref[...] = (lam_im * a_re - x_re * a_im) / den


def _s5_input_scale_kernel(fre_ref, fim_ref, bre_ref, bim_ref, bbre_ref, bbim_ref):
    f_re, f_im = fre_ref[0], fim_ref[0]
    b_re, b_im = bre_ref[0], bim_ref[0]
    bbre_ref[0] = f_re * b_re - f_im * b_im
    bbim_ref[0] = f_re * b_im + f_im * b_re


def _s5_prepare(a_re, a_im, log_dt, b_re, b_im, c_re, c_im):
    n = a_re.shape[0]
    dense = lambda t: t.reshape(n, SSM_COLS // LANES, LANES)
    ldt = jnp.broadcast_to(log_dt[:, :, None], (n, SSM_GROUPS, SSM_STATE))
    lam_re, lam_im, f_re, f_im = pl.pallas_call(
        _s5_discretise_kernel,
        out_shape=[jax.ShapeDtypeStruct((n, SSM_COLS // LANES, LANES), F32)] * 4,
        name="s5_discretise",
    )(dense(a_re), dense(a_im), dense(ldt))
    col = lambda t: t.reshape(n, SSM_COLS, 1)
    col_spec = pl.BlockSpec((1, SSM_COLS, 1), lambda l: (l, 0, 0))
    b_spec = pl.BlockSpec((1, SSM_COLS, SSM_GROUP), lambda l: (l, 0, 0))
    bb_re, bb_im = pl.pallas_call(
        _s5_input_scale_kernel,
        grid=(n,),
        in_specs=[col_spec, col_spec, b_spec, b_spec],
        out_specs=[b_spec, b_spec],
        out_shape=[jax.ShapeDtypeStruct((n, SSM_COLS, SSM_GROUP), F32)] * 2,
        compiler_params=_params(1),
        name="s5_input_scale",
    )(col(f_re), col(f_im), b_re.reshape(n, SSM_COLS, SSM_GROUP), b_im.reshape(n, SSM_COLS, SSM_GROUP))
    gpb = SSM_GROUPS // S5_BLOCKS

    def block_diagonal(t, rows_per_group, cols_per_group):
        rows = gpb * rows_per_group
        same = (jnp.arange(rows) // rows_per_group)[:, None] == jnp.arange(gpb)[None, :]
        wide = jnp.broadcast_to(t[:, :, :, None, :], (n, S5_BLOCKS, rows, gpb, cols_per_group))
        return jnp.where(same[None, None, :, :, None], wide, 0.0).reshape(
            n, S5_BLOCKS, rows, gpb * cols_per_group)

    def in_blocks(bb):
        t = bb.reshape(n, S5_BLOCKS, gpb, SSM_STATE, SSM_GROUP).transpose(0, 1, 2, 4, 3)
        return block_diagonal(t.reshape(n, S5_BLOCKS, S5_BLOCK_CH, SSM_STATE), SSM_GROUP, SSM_STATE)

    def out_blocks(c):
        t = c.reshape(n, S5_BLOCKS, gpb, SSM_GROUP, SSM_STATE).transpose(0, 1, 2, 4, 3)
        return block_diagonal(t.reshape(n, S5_BLOCKS, S5_BLOCK_ST, SSM_GROUP), SSM_STATE, SSM_GROUP)

    w_in = jnp.concatenate([in_blocks(bb_re), in_blocks(bb_im)], axis=-1).astype(BF16)
    w_out = jnp.concatenate([out_blocks(c_re), -out_blocks(c_im)], axis=-2).astype(BF16)
    return lam_re.reshape(n, 1, SSM_COLS), lam_im.reshape(n, 1, SSM_COLS), w_in, w_out


def _s5_kernel(u_ref, h0re_ref, h0im_ref, lre_ref, lim_ref, win_ref, wout_ref, d_ref, wglu_ref, bglu_ref,
               out_ref, sre_out_ref, sim_out_ref, hb_ref, sre_ref, sim_ref, *, bsz, t_chunk, col_block):
    c = pl.program_id(0)
    rows = bsz * t_chunk

    @pl.when(c == 0)
    def _():
        sre_ref[...] = h0re_ref[...]
        sim_ref[...] = h0im_ref[...]

    u = u_ref[...]
    ut = jnp.swapaxes(u, 0, 1).reshape(rows, MIX_IN).astype(BF16)
    ys = []
    for j in range(S5_BLOCKS):
        cols = slice(j * 2 * S5_BLOCK_ST, (j + 1) * 2 * S5_BLOCK_ST)
        hb_ref[:, cols] = _dot(ut[:, j * S5_BLOCK_CH:(j + 1) * S5_BLOCK_CH], win_ref[j])
        for sub in range(S5_BLOCK_ST // col_block):
            nat = j * S5_BLOCK_ST + sub * col_block
            cre = j * 2 * S5_BLOCK_ST + sub * col_block
            cim = cre + S5_BLOCK_ST
            lam_r = jnp.broadcast_to(lre_ref[:, nat:nat + col_block], (bsz, col_block))
            lam_i = jnp.broadcast_to(lim_ref[:, nat:nat + col_block], (bsz, col_block))
            s_r = sre_ref[:, nat:nat + col_block]
            s_i = sim_ref[:, nat:nat + col_block]
            for t in range(t_chunk):
                rows_t = slice(t * bsz, (t + 1) * bsz)
                n_r = lam_r * s_r - lam_i * s_i + hb_ref[rows_t, cre:cre + col_block]
                n_i = lam_r * s_i + lam_i * s_r + hb_ref[rows_t, cim:cim + col_block]
                hb_ref[rows_t, cre:cre + col_block] = n_r
                hb_ref[rows_t, cim:cim + col_block] = n_i
                s_r, s_i = n_r, n_i
            sre_ref[:, nat:nat + col_block] = s_r
            sim_ref[:, nat:nat + col_block] = s_i
        ys.append(_two_row_halves(_dot, hb_ref[:, cols].astype(BF16), wout_ref[j]))
    yt = jnp.concatenate(ys, axis=1).reshape(t_chunk, bsz, MIX_IN)
    y = jnp.swapaxes(yt, 0, 1) + d_ref[...] * u
    y = jax.nn.gelu(y).reshape(rows, MIX_IN)
    gate = _dot(y.astype(BF16), wglu_ref[...].astype(BF16)) + bglu_ref[...]
    out_ref[...] = (y * jax.nn.sigmoid(gate)).reshape(bsz, t_chunk, MIX_IN).astype(BF16)

    @pl.when(c == pl.num_programs(0) - 1)
    def _():
        sre_out_ref[...] = sre_ref[...]
        sim_out_ref[...] = sim_ref[...]


def _s5_mixer(u, h0_re, h0_im, lam_re, lam_im, w_in, w_out, d_skip, w_glu, b_glu, t_chunk):
    bsz, length, _ = u.shape
    rows = bsz * t_chunk
    col_block = (SUBLANES * 512) // bsz
    kern = functools.partial(_s5_kernel, bsz=bsz, t_chunk=t_chunk, col_block=col_block)
    state = jax.ShapeDtypeStruct((bsz, SSM_COLS), F32)
    return pl.pallas_call(
        kern,
        grid=(length // t_chunk,),
        in_specs=[
            pl.BlockSpec((bsz, t_chunk, MIX_IN), lambda c: (0, c, 0)),
            _full((bsz, SSM_COLS)), _full((bsz, SSM_COLS)),
            _full((1, SSM_COLS)), _full((1, SSM_COLS)),
            _full((S5_BLOCKS, S5_BLOCK_CH, 2 * S5_BLOCK_ST)),
            _full((S5_BLOCKS, 2 * S5_BLOCK_ST, S5_BLOCK_CH)),
            _full((1, MIX_IN)), _full((MIX_IN, MIX_IN)), _full((1, MIX_IN)),
        ],
        out_specs=[
            pl.BlockSpec((bsz, t_chunk, MIX_IN), lambda c: (0, c, 0)),
            _full((bsz, SSM_COLS)), _full((bsz, SSM_COLS)),
        ],
        out_shape=[jax.ShapeDtypeStruct((bsz, length, MIX_IN), BF16), state, state],
        scratch_shapes=[
            pltpu.VMEM((rows, 2 * SSM_COLS), F32),
            pltpu.VMEM((bsz, SSM_COLS), F32),
            pltpu.VMEM((bsz, SSM_COLS), F32),
        ],
        compiler_params=_params(1),
        name="s5_mixer",
    )(u, h0_re, h0_im, lam_re, lam_im, w_in, w_out, d_skip, w_glu, b_glu)


def _ffn_kernel(h_ref, mix_ref, mem_ref, wo_ref, g_ref, win_ref, cw_ref, cb_ref, w2_ref, ctx_ref,
                out_ref, new_ref, act_ref, pad_ref, carry_ref, *, seqs, rows_per_seq, tiles_per_seq):
    m = pl.program_id(0)
    tm = seqs * rows_per_seq
    h1 = h_ref[...] + _dot(mix_ref[...], wo_ref[:MIX_IN, :]) + _dot(mem_ref[...], wo_ref[MIX_IN:, :])
    out_ref[...] = h1
    xn = _rms_rows(h1, g_ref[...]).astype(BF16)

    if tiles_per_seq > 1:
        @pl.when(m == 0)
        def _():
            carry_ref[...] = jnp.zeros(carry_ref.shape, F32)

    def causal_conv(col, slot):
        cols = slice(col, col + FFN_TF)
        u = _two_row_halves(_dot, xn, win_ref[:, cols])
        ctx = ctx_ref[:, :, cols]
        if tiles_per_seq > 1:
            ctx = jnp.where(m % tiles_per_seq == 0, ctx, carry_ref[:, cols][None])
        cw = cw_ref[:, cols]
        if seqs == 1:
            row = lax.broadcasted_iota(jnp.int32, (SUBLANES, 1), 0)
            back1 = pltpu.roll(u, 1, axis=0)
            back2 = pltpu.roll(u, 2, axis=0)
            head1 = jnp.where(row == 0, ctx[0, 1:2], back1[:SUBLANES])
            head2 = jnp.where(row == 0, ctx[0, 0:1], jnp.where(row == 1, ctx[0, 1:2], back2[:SUBLANES]))
            back1 = jnp.concatenate([head1, back1[SUBLANES:]], axis=0)
            back2 = jnp.concatenate([head2, back2[SUBLANES:]], axis=0)
            y = cb_ref[:, cols] + back2 * cw[0:1] + back1 * cw[1:2] + u * cw[2:3]
            last2 = u[tm - 2:][None]
        else:
            pad = pad_ref.at[slot]
            pad[:, 6:8, :] = ctx
            pad[:, 8:, :] = u.reshape(seqs, rows_per_seq, FFN_TF)
            y = (cb_ref[:, cols] + pad[:, 6:6 + rows_per_seq, :] * cw[0:1]
                 + pad[:, 7:7 + rows_per_seq, :] * cw[1:2]
                 + pad[:, 8:8 + rows_per_seq, :] * cw[2:3]).reshape(tm, FFN_TF)
            last2 = pad[:, rows_per_seq + 6:rows_per_seq + 8, :]
        new_ref[:, :, cols] = last2
        if tiles_per_seq > 1:
            carry_ref[:, cols] = last2[0]
        return y

    for f in range(N_FF_TILES):
        ya = causal_conv(f * FFN_TF, (2 * f) % FFN_PAD_SLOTS)
        yg = causal_conv(D_FF + f * FFN_TF, (2 * f + 1) % FFN_PAD_SLOTS)
        act_ref[:, f * FFN_TF:(f + 1) * FFN_TF] = (jax.nn.silu(yg) * ya).astype(BF16)
    out_ref[...] += _dot(act_ref[...], w2_ref[...])


def _mixout_ffn(h, mix, mem, w, layer, ffn_w, ctx, rows_per_seq_tile):
    nseq, length, _ = h.shape
    m_rows = nseq * length
    if rows_per_seq_tile >= length:
        seqs, rps, tps = nseq, length, 1
    else:
        seqs, rps, tps = 1, rows_per_seq_tile, length // rows_per_seq_tile
    tm = seqs * rps
    n_m = m_rows // tm
    kern = functools.partial(_ffn_kernel, seqs=seqs, rows_per_seq=rps, tiles_per_seq=tps)
    row = lambda width: pl.BlockSpec((tm, width), lambda m: (m, 0))
    resident = lambda *shape: pl.BlockSpec(shape, lambda m: (0,) * len(shape), pipeline_mode=pl.Buffered(1))
    of_layer = lambda *shape: pl.BlockSpec((None,) + shape, lambda m: (layer,) + (0,) * len(shape),
                                           pipeline_mode=pl.Buffered(1))
    out, new = pl.pallas_call(
        kern,
        grid=(n_m,),
        in_specs=[
            row(D_MODEL), row(MIX_IN), row(MEM_WIDTH),
            resident(MIX_IN + MEM_WIDTH, D_MODEL),
            of_layer(1, D_MODEL),
            resident(D_MODEL, 2 * D_FF),
            of_layer(CONV_W, 2 * D_FF),
            of_layer(1, 2 * D_FF),
            resident(D_FF, D_MODEL),
            pl.BlockSpec((seqs, CONV_W - 1, 2 * D_FF), lambda m: (m // tps, 0, 0)),
        ],
        out_specs=[row(D_MODEL), pl.BlockSpec((seqs, CONV_W - 1, 2 * D_FF), lambda m: (m, 0, 0))],
        out_shape=[
            jax.ShapeDtypeStruct((m_rows, D_MODEL), F32),
            jax.ShapeDtypeStruct((n_m * seqs, CONV_W - 1, 2 * D_FF), F32),
        ],
        scratch_shapes=[
            pltpu.VMEM((tm, D_FF), BF16),
            pltpu.VMEM((FFN_PAD_SLOTS, seqs, rps + SUBLANES, FFN_TF) if seqs > 1 else (1, 1, SUBLANES, LANES), F32),
            pltpu.VMEM((CONV_W - 1, 2 * D_FF), F32),
        ],
        compiler_params=_params(1),
        name="mix_out_conv_ffn",
    )(h.reshape(m_rows, D_MODEL), mix.reshape(m_rows, MIX_IN), mem.reshape(m_rows, MEM_WIDTH),
      ffn_w[0], w["norm_ffn_g"], ffn_w[1], w["ffn_conv_w"], w["ffn_conv_b"], ffn_w[2], ctx)
    new_ctx = new.reshape(nseq, tps, CONV_W - 1, 2 * D_FF)[:, -1]
    return out.reshape(nseq, length, D_MODEL), new_ctx


def _dkv_kernel(h_ref, g_ref, wl_ref, wr_ref, wrr_ref, lg_ref, kg_ref, kgr_ref, cos_ref, sin_ref,
                lat_ref, kr_ref):
    xn = _rms_rows(h_ref[...], g_ref[...]).astype(BF16)
    lat_ref[...] = _rms_rows(_two_row_halves(_dot, xn, wl_ref[...]), lg_ref[...])
    kr = _dot(xn, wr_ref[...])
    kr_rot = _dot(xn, wrr_ref[...])
    r = lax.rsqrt(jnp.mean(kr * kr, axis=-1, keepdims=True) + EPS)
    kr_ref[...] = (kr * r * kg_ref[...]) * cos_ref[...] + (kr_rot * r * kgr_ref[...]) * sin_ref[...]


def _swap_halves(t, axis=-1):
    a, b = jnp.split(t, 2, axis=axis)
    return jnp.concatenate([b, a], axis=axis)


def _shared_kv_down(h, kv_norm_g, w_dkv_bf, latent_norm_g, krope_norm_g, cos_rows, sin_rows, tm):
    nseq, length, _ = h.shape
    m_rows = nseq * length
    tab_blocks = cos_rows.shape[0] // tm
    w_l = w_dkv_bf[:, :KV_LORA]
    w_r = w_dkv_bf[:, KV_LORA:]
    kg = krope_norm_g.reshape(1, ROPE_DIM)
    lat, kr = pl.pallas_call(
        _dkv_kernel,
        grid=(m_rows // tm,),
        in_specs=[
            pl.BlockSpec((tm, D_MODEL), lambda m: (m, 0)),
            _full((1, D_MODEL)),
            _full((D_MODEL, KV_LORA)), _full((D_MODEL, ROPE_DIM)), _full((D_MODEL, ROPE_DIM)),
            _full((1, KV_LORA)), _full((1, ROPE_DIM)), _full((1, ROPE_DIM)),
            pl.BlockSpec((tm, ROPE_DIM), lambda m: (m % tab_blocks, 0)),
            pl.BlockSpec((tm, ROPE_DIM), lambda m: (m % tab_blocks, 0)),
        ],
        out_specs=[pl.BlockSpec((tm, KV_LORA), lambda m: (m, 0)),
                   pl.BlockSpec((tm, ROPE_DIM), lambda m: (m, 0))],
        out_shape=[jax.ShapeDtypeStruct((m_rows, KV_LORA), F32),
                   jax.ShapeDtypeStruct((m_rows, ROPE_DIM), F32)],
        compiler_params=_params(1),
        name="shared_kv_down",
    )(h.reshape(m_rows, D_MODEL), kv_norm_g.reshape(1, D_MODEL), w_l, w_r, _swap_halves(w_r),
      latent_norm_g.reshape(1, KV_LORA), kg, _swap_halves(kg), cos_rows, sin_rows)
    return lat.reshape(nseq, length, KV_LORA), kr.reshape(nseq, length, ROPE_DIM)


def _kv_up_kernel(lat_ref, wk_ref, wv_ref, seg_ref, g_ref, k_ref, v_ref):
    lat = lat_ref[...].astype(BF16)
    k_ref[...] = _seg_rms(_dot(lat, wk_ref[...]), seg_ref[...], g_ref[...]).astype(BF16)
    v_ref[...] = _dot(lat, wv_ref[...]).astype(BF16)


def _kv_up(latent_rows, w_uk_bf, w_uv_bf, seg_nope, k_g_tiled, tr):
    rows = latent_rows.shape[0]
    width = MLA_HEADS * NOPE_DIM
    return pl.pallas_call(
        _kv_up_kernel,
        grid=(rows // tr,),
        in_specs=[
            pl.BlockSpec((tr, KV_LORA), lambda r: (r, 0)),
            _full((KV_LORA, width)), _full((KV_LORA, width)),
            _full((width, width)), _full((1, width)),
        ],
        out_specs=[pl.BlockSpec((tr, width), lambda r: (r, 0))] * 2,
        out_shape=[jax.ShapeDtypeStruct((rows, width), BF16)] * 2,
        compiler_params=_params(1),
        name="kv_up",
    )(latent_rows, w_uk_bf, w_uv_bf, seg_nope, k_g_tiled)


Q_PRESCALE = MLA_SCALE * math.log2(math.e)


def _q_kernel(z_ref, g_ref, wn_ref, wa_ref, segn_ref, segr_ref, gn_ref, ga_ref,
              cos_ref, sin_ref, kg_ref, wuk_ref, qn_ref, qr_ref, *maybe_absorbed_refs):
    xn = _rms_rows(z_ref[...], g_ref[...]).astype(BF16)
    qn = _seg_rms(_dot(xn, wn_ref[...]), segn_ref[...], gn_ref[...]) * Q_PRESCALE
    qn_ref[...] = qn.astype(BF16)
    a = _dot(xn, wa_ref[...])
    an = a * lax.rsqrt(_dot((a * a).astype(BF16), segr_ref[...]) + EPS) * ga_ref[...]
    width = an.shape[1]
    lane = lax.broadcasted_iota(jnp.int32, (1, width), 1)
    partner = jnp.where(lane % ROPE_DIM < ROPE_HALF,
                        pltpu.roll(an, width - ROPE_HALF, axis=1), pltpu.roll(an, ROPE_HALF, axis=1))
    rot = an * cos_ref[...] + partner * sin_ref[...]
    qr_ref[...] = (rot * Q_PRESCALE).astype(BF16)
    if maybe_absorbed_refs:
        qp_ref, qrh_ref = maybe_absorbed_refs
        lane = lax.broadcasted_iota(jnp.int32, (1, MLA_HEADS * NOPE_DIM), 1)
        qg = qn * kg_ref[...]
        rot_bf = (rot * Q_PRESCALE).astype(BF16)
        for head in range(MLA_HEADS):
            qh = jnp.where(lane // NOPE_DIM == head, qg, 0.0).astype(BF16)
            qp_ref[head] = _dot_nt(qh, wuk_ref[...]).astype(BF16)
            qrh_ref[head] = rot_bf[:, head * ROPE_DIM:(head + 1) * ROPE_DIM]


def _q_side(z_mix, q_latent_g, w_uq_bf, q_nope_g, q_rope_g, seg_nope, seg_rope, cos_rows, sin_rows,
            k_g_tiled, w_uk_bf, tm, absorbed):
    nseq, length, _ = z_mix.shape
    m_rows = nseq * length
    tab_blocks = cos_rows.shape[0] // tm
    wn_width = MLA_HEADS * NOPE_DIM
    wr_width = MLA_HEADS * ROPE_DIM
    w3 = w_uq_bf.reshape(MIX_IN, MLA_HEADS, NOPE_DIM + ROPE_DIM)
    w_n = w3[:, :, :NOPE_DIM].reshape(MIX_IN, wn_width)
    w_a = w3[:, :, NOPE_DIM:].reshape(MIX_IN, wr_width)
    g_n = jnp.tile(q_nope_g.reshape(1, NOPE_DIM), (1, MLA_HEADS))
    g_a = jnp.tile(q_rope_g.reshape(1, ROPE_DIM), (1, MLA_HEADS))
    cos_q = jnp.tile(cos_rows, (1, MLA_HEADS))
    sin_q = jnp.tile(sin_rows, (1, MLA_HEADS))
    out_specs = [pl.BlockSpec((tm, wd), lambda m: (m, 0)) for wd in (wn_width, wr_width)]
    out_shape = [jax.ShapeDtypeStruct((m_rows, wd), BF16) for wd in (wn_width, wr_width)]
    if absorbed:
        for wd in (KV_LORA, ROPE_DIM):
            out_specs.append(pl.BlockSpec((MLA_HEADS, tm, wd), lambda m: (0, m, 0)))
            out_shape.append(jax.ShapeDtypeStruct((MLA_HEADS, m_rows, wd), BF16))
    outs = pl.pallas_call(
        _q_kernel,
        grid=(m_rows // tm,),
        in_specs=[
            pl.BlockSpec((tm, MIX_IN), lambda m: (m, 0)),
            _full((1, MIX_IN)),
            _full((MIX_IN, wn_width)), _full((MIX_IN, wr_width)),
            _full((wn_width, wn_width)), _full((wr_width, wr_width)),
            _full((1, wn_width)), _full((1, wr_width)),
            pl.BlockSpec((tm, wr_width), lambda m: (m % tab_blocks, 0)),
            pl.BlockSpec((tm, wr_width), lambda m: (m % tab_blocks, 0)),
            _full((1, wn_width)), _full((KV_LORA, wn_width)),
        ],
        out_specs=out_specs,
        out_shape=out_shape,
        compiler_params=_params(1),
        name="mla_query",
    )(z_mix.reshape(m_rows, MIX_IN), q_latent_g.reshape(1, MIX_IN), w_n, w_a, seg_nope, seg_rope,
      g_n, g_a, cos_q, sin_q, k_g_tiled, w_uk_bf)
    return outs


ATTN_HEADS_PER_STEP = 4


def _attn_kernel(qn_ref, qr_ref, kn_ref, kr_ref, v_ref, o_ref, *, tile, n_tiles):
    qi = pl.program_id(2)
    pairs = ATTN_HEADS_PER_STEP // 2
    lane = lax.broadcasted_iota(jnp.int32, (1, LANES), 1)
    head_lanes = [(lane // NOPE_DIM) == j for j in range(2)]
    qr = qr_ref[0]
    qcat = []
    for h in range(ATTN_HEADS_PER_STEP):
        qn = qn_ref[0, :, (h // 2) * LANES:(h // 2 + 1) * LANES]
        qcat.append(jnp.concatenate(
            [jnp.where(head_lanes[h % 2], qn, jnp.zeros_like(qn)),
             jnp.where((lane // ROPE_DIM) == h, qr, jnp.zeros_like(qr))], axis=1))
    row_chunk = lax.broadcasted_iota(jnp.int32, (tile, 1), 0) // CHUNK
    col_chunk = lax.broadcasted_iota(jnp.int32, (1, tile), 1) // CHUNK
    diag_visible = col_chunk <= row_chunk

    def one_block(kb, carry, masked):
        rows_k = slice(kb * tile, (kb + 1) * tile)
        kr = kr_ref[0, rows_k, :]
        new = []
        for h in range(ATTN_HEADS_PER_STEP):
            lanes_p = slice((h // 2) * LANES, (h // 2 + 1) * LANES)
            m_i, acc = carry[h]
            kcat = jnp.concatenate([kn_ref[0, rows_k, lanes_p], kr], axis=1)
            s = _dot_nt(qcat[h], kcat)
            if masked:
                s = jnp.where(diag_visible, s, NEG_INF)
            m_new = jnp.maximum(m_i, jnp.max(s, axis=-1, keepdims=True))
            alpha = jnp.exp2(m_i - m_new)
            p = jnp.exp2(s - m_new).astype(BF16)
            vb = v_ref[0, rows_k, lanes_p]
            v_h = jnp.where(head_lanes[h % 2], vb, jnp.ones_like(vb))
            new.append((m_new, alpha * acc + _dot(p, v_h)))
        return tuple(new)

    def query_tile(n_full):
        carry = tuple((jnp.full((tile, 1), NEG_INF, F32), jnp.zeros((tile, LANES), F32))
                      for _ in range(ATTN_HEADS_PER_STEP))
        for kb in range(n_full):
            carry = one_block(kb, carry, False)
        carry = one_block(n_full, carry, True)
        for p in range(pairs):
            out = jnp.zeros((tile, LANES), F32)
            for j in range(2):
                acc = carry[2 * p + j][1]
                row_sum = pltpu.roll(acc, NOPE_DIM, axis=1)
                out = jnp.where(head_lanes[j], acc * (1.0 / row_sum), out)
            o_ref[0, :, p * LANES:(p + 1) * LANES] = out.astype(BF16)

    for c in range(n_tiles):
        pl.when(qi == c)(functools.partial(query_tile, c))


def _mla_attention(qn, qr, kn, kr4, v, tile):
    nseq, length, _ = qn.shape
    kern = functools.partial(_attn_kernel, tile=tile, n_tiles=length // tile)
    width = (ATTN_HEADS_PER_STEP // 2) * LANES
    return pl.pallas_call(
        kern,
        grid=(nseq, MLA_HEADS // ATTN_HEADS_PER_STEP, length // tile),
        in_specs=[
            pl.BlockSpec((1, tile, width), lambda b, g, i: (b, i, g)),
            pl.BlockSpec((1, tile, LANES), lambda b, g, i: (b, i, g)),
            pl.BlockSpec((1, length, width), lambda b, g, i: (b, 0, g)),
            pl.BlockSpec((1, length, LANES), lambda b, g, i: (b, 0, 0)),
            pl.BlockSpec((1, length, width), lambda b, g, i: (b, 0, g)),
        ],
        out_specs=pl.BlockSpec((1, tile, width), lambda b, g, i: (b, i, g)),
        out_shape=jax.ShapeDtypeStruct((nseq, length, MLA_HEADS * V_DIM), BF16),
        compiler_params=_params(3),
        name="mla_attention",
    )(qn, qr, kn, kr4, v)


KNORM_ROWS = 16


def _key_norm_kernel(past_ref, new_ref, wukt_ref, rt_ref, *, n_chunks, chunk):
    def norms(lat):
        keys = lat.shape[0]
        kt = _dot_nt(wukt_ref[...], lat)
        ss = jnp.sum((kt * kt).reshape(MLA_HEADS, NOPE_DIM, keys), axis=1)
        r = lax.rsqrt(ss * (1.0 / NOPE_DIM) + EPS)
        return jnp.concatenate([r, jnp.ones((KNORM_ROWS - MLA_HEADS, keys), F32)], axis=0)

    for c in range(n_chunks):
        rt_ref[0, :, c * chunk:(c + 1) * chunk] = norms(past_ref[0, c * chunk:(c + 1) * chunk, :].astype(BF16))
    rt_ref[0, :, n_chunks * chunk:] = norms(new_ref[0])


def _key_norms(past_latent, new_lat_pad, w_ukt_bf, chunk):
    nseq, past, _ = past_latent.shape
    new_pad = new_lat_pad.shape[1]
    lk_pad = past + new_pad
    kern = functools.partial(_key_norm_kernel, n_chunks=past // chunk, chunk=chunk)
    return pl.pallas_call(
        kern,
        grid=(nseq,),
        in_specs=[pl.BlockSpec((1, past, KV_LORA), lambda b: (b, 0, 0)),
                  pl.BlockSpec((1, new_pad, KV_LORA), lambda b: (b, 0, 0)),
                  _full((MLA_HEADS * NOPE_DIM, KV_LORA))],
        out_specs=pl.BlockSpec((1, KNORM_ROWS, lk_pad), lambda b: (b, 0, 0)),
        out_shape=jax.ShapeDtypeStruct((nseq, KNORM_ROWS, lk_pad), F32),
        compiler_params=_params(1),
        name="mla_key_norms",
    )(past_latent, new_lat_pad, w_ukt_bf)


def _attn_absorbed_kernel(qs_ref, qrs_ref, plat_ref, nlat_ref, pkrt_ref, nkrt_ref, rt_ref, wuv_ref, o_ref, op_ref,
                          *, tq, q_off, lk_valid):
    b = pl.program_id(0)
    nseq = pl.num_programs(0)
    rows = MLA_HEADS * tq
    lat = jnp.concatenate([plat_ref[0].astype(BF16), nlat_ref[0]], axis=0)
    kr_t = jnp.concatenate([pkrt_ref[0].astype(BF16), nkrt_ref[0]], axis=1)
    lk_pad = lat.shape[0]
    rt = rt_ref[0]
    knorm = jnp.concatenate([jnp.broadcast_to(rt[h:h + 1, :], (tq, lk_pad)) for h in range(MLA_HEADS)], axis=0)
    qs = qs_ref[...].reshape(rows, KV_LORA)
    qrs = qrs_ref[...].reshape(rows, ROPE_DIM)
    s = _dot_nt(qs, lat) * knorm + _dot(qrs, kr_t)
    k_pos = lax.broadcasted_iota(jnp.int32, (1, lk_pad), 1)
    if (lk_valid - 1) // CHUNK > q_off // CHUNK:
        q_chunk = (q_off + lax.broadcasted_iota(jnp.int32, (rows, 1), 0) % tq) // CHUNK
        s = jnp.where(k_pos // CHUNK <= q_chunk, s, NEG_INF)
    s = jnp.where(k_pos < lk_valid, s, NEG_INF)
    p = jnp.exp2(s - jnp.max(s, axis=-1, keepdims=True))
    l = jnp.sum(p, axis=-1, keepdims=True)
    o = _two_row_halves(_dot, p.astype(BF16), lat)
    op_ref[b] = (o * (1.0 / l)).astype(BF16)

    @pl.when(b == nseq - 1)
    def _():
        n_all = op_ref.shape[0]
        lane_o = lax.broadcasted_iota(jnp.int32, (1, MLA_HEADS * V_DIM), 1)
        out = jnp.zeros((n_all * tq, MLA_HEADS * V_DIM), F32)
        for h in range(MLA_HEADS):
            x = op_ref[:, h * tq:(h + 1) * tq, :].reshape(n_all * tq, KV_LORA)
            out = jnp.where(lane_o // V_DIM == h, _dot(x, wuv_ref[...]), out)
        o_ref[...] = out.astype(BF16)


def _mla_attention_absorbed(qp, qrh, past_latent, new_lat_pad, past_krope_t, new_kr_pad_t, r_t, w_uv_bf,
                            tq, lk_valid):
    nseq, past, _ = past_latent.shape
    new_pad = new_lat_pad.shape[1]
    kern = functools.partial(_attn_absorbed_kernel, tq=tq, q_off=past, lk_valid=lk_valid)
    per_seq = lambda n, width: pl.BlockSpec((1, n, width), lambda b: (b, 0, 0))
    heads_of_seq = lambda width: pl.BlockSpec((MLA_HEADS, tq, width), lambda b: (0, b, 0))
    out = pl.pallas_call(
        kern,
        grid=(nseq,),
        in_specs=[
            heads_of_seq(KV_LORA), heads_of_seq(ROPE_DIM),
            per_seq(past, KV_LORA), per_seq(new_pad, KV_LORA),
            per_seq(ROPE_DIM, past), per_seq(ROPE_DIM, new_pad),
            pl.BlockSpec((1, KNORM_ROWS, past + new_pad), lambda b: (b, 0, 0)),
            _full((KV_LORA, MLA_HEADS * V_DIM)),
        ],
        out_specs=_full((nseq * tq, MLA_HEADS * V_DIM)),
        out_shape=jax.ShapeDtypeStruct((nseq * tq, MLA_HEADS * V_DIM), BF16),
        scratch_shapes=[pltpu.VMEM((nseq, MLA_HEADS * tq, KV_LORA), BF16)],
        compiler_params=_params(1),
        name="mla_attention_absorbed",
    )(qp, qrh, past_latent, new_lat_pad, past_krope_t, new_kr_pad_t, r_t, w_uv_bf)
    return out.reshape(nseq, tq, MLA_HEADS * V_DIM)


def _trunk(x, mem_k_bf, mem_v_bf, ssm_h0_re, ssm_h0_im, conv_ctx, past_latent, past_krope, w, cfg, ffn_bf=None):
    nseq, length, _ = x.shape
    past = 0 if past_latent is None else past_latent.shape[1]
    cos_t, sin_t = _rope_tables(past, length)
    reps = cfg["rope_rows"] // length
    cos_rows = jnp.tile(cos_t, (reps, 1))
    sin_rows = jnp.tile(sin_t, (reps, 1))
    if conv_ctx is None:
        conv_ctx = jnp.zeros((DEPTH, nseq, CONV_W - 1, 2 * D_FF), F32)
    if ssm_h0_re is None:
        ssm_h0_re = jnp.zeros((N_A_LAYERS, nseq, SSM_GROUPS, SSM_STATE), F32)
        ssm_h0_im = ssm_h0_re
    h = x
    ssm_re_out, ssm_im_out, conv_out = [], [], []
    make_ffn_bf = ffn_bf is None
    if make_ffn_bf:
        ffn_bf = []
    for layer in range(DEPTH):
        to_cast = (w["w_mix_out_f32"], w["w_ffn_in_f32"], w["w_ffn_out_f32"]) if make_ffn_bf else ()
        z_mix, mem_out, cast = _mixin(h, w, layer, mem_k_bf, mem_v_bf, cfg["tm_mixin"], to_cast)
        if make_ffn_bf:
            ffn_bf.append(cast)
        if layer < N_A_LAYERS:
            i = layer
            mix_out, s_re, s_im = _s5_mixer(
                z_mix, ssm_h0_re[i].reshape(nseq, SSM_COLS), ssm_h0_im[i].reshape(nseq, SSM_COLS),
                w["lam_re"][i], w["lam_im"][i], w["s5_in"][i], w["s5_out"][i],
                w["ssm_d"][i][None], w["w_glu"][i], w["b_glu"][i][None], cfg["t_chunk"])
            ssm_re_out.append(s_re.reshape(nseq, SSM_GROUPS, SSM_STATE))
            ssm_im_out.append(s_im.reshape(nseq, SSM_GROUPS, SSM_STATE))
        else:
            if layer == N_A_LAYERS:
                new_latent, new_krope = _shared_kv_down(
                    h, w["kv_norm_g"], w["w_dkv"], w["latent_norm_g"], w["krope_norm_g"],
                    cos_rows, sin_rows, cfg["tm_rows"])
                if cfg["absorbed"]:
                    assert past % LANES == 0
                    lk_valid = past + length
                    new_pad = -(-length // LANES) * LANES
                    pad_rows = lambda t: jnp.pad(t.astype(BF16), ((0, 0), (0, new_pad - length), (0, 0)))
                    new_lat_pad = pad_rows(new_latent)
                    new_kr_pad_t = jnp.swapaxes(pad_rows(new_krope), 1, 2)
                    past_krope_t = jnp.swapaxes(past_krope, 1, 2)
                    r_t = _key_norms(past_latent, new_lat_pad, w["w_uk_t"], cfg["knorm_chunk"])
                else:
                    assert past_latent is None
                    kn, v_all = _kv_up(new_latent.reshape(nseq * length, KV_LORA), w["w_uk"], w["w_uv"],
                                       w["seg_nope"], w["k_nope_g"], cfg["tr_kv"])
                    kn = kn.reshape(nseq, length, MLA_HEADS * NOPE_DIM)
                    v_all = v_all.reshape(nseq, length, MLA_HEADS * V_DIM)
                    kr4 = jnp.tile(new_krope.astype(BF16), (1, 1, LANES // ROPE_DIM))
            j = layer - N_A_LAYERS
            q_out = _q_side(z_mix, w["q_latent_norm_g"][j], w["w_uq"][j], w["q_nope_norm_g"][j],
                            w["q_rope_norm_g"][j], w["seg_nope"], w["seg_rope"], cos_rows, sin_rows,
                            w["k_nope_g"], w["w_uk"], cfg["tm_rows"], cfg["absorbed"])
            if cfg["absorbed"]:
                mix_out = _mla_attention_absorbed(q_out[2], q_out[3], past_latent, new_lat_pad, past_krope_t,
                                                  new_kr_pad_t, r_t, w["w_uv"], length, lk_valid)
            else:
                qn = q_out[0].reshape(nseq, length, MLA_HEADS * NOPE_DIM)
                qr = q_out[1].reshape(nseq, length, MLA_HEADS * ROPE_DIM)
                mix_out = _mla_attention(qn, qr, kn, kr4, v_all, cfg["tq"])
        h, ctx = _mixout_ffn(h, mix_out, mem_out, w, layer, ffn_bf[layer], conv_ctx[layer], cfg["ffn_rows"])
        conv_out.append(ctx)
    return (h, new_latent, new_krope, jnp.stack(ssm_re_out), jnp.stack(ssm_im_out), jnp.stack(conv_out)), ffn_bf


PROMPT_CFG = dict(tm_mixin=1024, t_chunk=64, ffn_rows=512, tm_rows=1024, rope_rows=2048, tr_kv=1024,
                  tq=512, absorbed=False)
SAMPLE_CFG = dict(tm_mixin=32, t_chunk=32, ffn_rows=32, tm_rows=512, rope_rows=512, knorm_chunk=1024,
                  absorbed=True)


def kernel(x_prompt, x_sample, cache_mla_latent, cache_mla_krope, cache_mem_k, cache_mem_v, state_ssm_re, state_ssm_im, state_conv, mem_prompt, norm_mix_g, w_mix_in, w_mix_out, norm_ffn_g, w_ffn_in, ffn_conv_w, ffn_conv_b, w_ffn_out, mem_norm_g, w_mem_kv, mem_q_norm_g, mem_k_norm_g, ssm_a_re, ssm_a_im, ssm_log_dt, ssm_b_re, ssm_b_im, ssm_c_re, ssm_c_im, ssm_d, w_glu, b_glu, kv_norm_g, w_dkv, latent_norm_g, krope_norm_g, w_uk, w_uv, k_nope_norm_g, q_latent_norm_g, w_uq, q_nope_norm_g, q_rope_norm_g):
    bf = lambda t: t.astype(BF16)
    seg_mem = _seg_matrix(MEM_WIDTH, MEM_HEAD_DIM)
    lam_re, lam_im, s5_in, s5_out = _s5_prepare(ssm_a_re, ssm_a_im, ssm_log_dt, ssm_b_re, ssm_b_im,
                                                ssm_c_re, ssm_c_im)
    w = dict(
        norm_mix_g=norm_mix_g.reshape(DEPTH, 1, D_MODEL), w_mix_in=w_mix_in,
        norm_ffn_g=norm_ffn_g.reshape(DEPTH, 1, D_MODEL),
        ffn_conv_w=ffn_conv_w, ffn_conv_b=ffn_conv_b.reshape(DEPTH, 1, 2 * D_FF),
        w_mix_out_f32=w_mix_out, w_ffn_in_f32=w_ffn_in, w_ffn_out_f32=w_ffn_out,
        mem_q_g=jnp.tile(mem_q_norm_g, (1, MEM_HEADS)).reshape(DEPTH, 1, MEM_WIDTH),
        seg_mem=seg_mem,
        seg_nope=_seg_matrix(MLA_HEADS * NOPE_DIM, NOPE_DIM),
        seg_rope=_seg_matrix(MLA_HEADS * ROPE_DIM, ROPE_DIM),
        lam_re=lam_re, lam_im=lam_im, s5_in=s5_in, s5_out=s5_out,
        ssm_d=ssm_d, w_glu=w_glu, b_glu=b_glu,
        kv_norm_g=kv_norm_g, w_dkv=bf(w_dkv), latent_norm_g=latent_norm_g, krope_norm_g=krope_norm_g,
        w_uk=bf(w_uk), w_uk_t=bf(w_uk).T, w_uv=bf(w_uv),
        k_nope_g=jnp.tile(k_nope_norm_g.reshape(1, NOPE_DIM), (1, MLA_HEADS)),
        q_latent_norm_g=q_latent_norm_g, w_uq=bf(w_uq), q_nope_norm_g=q_nope_norm_g,
        q_rope_norm_g=q_rope_norm_g,
    )
    bsz = mem_prompt.shape[0]
    mem_k_p, mem_v_p, mem_k_bf, mem_v_bf = _memory_kv(mem_prompt, mem_norm_g, w_mem_kv, mem_k_norm_g, seg_mem)
    (y_prompt, lat_p, krope_p, ssm_re_p, ssm_im_p, conv_p), ffn_bf = _trunk(
        x_prompt, mem_k_bf, mem_v_bf, None, None, None, None, None, w, PROMPT_CFG)
    dec = cache_mem_k.shape[1]
    (y_sample, lat_s, krope_s, ssm_re_s, ssm_im_s, conv_s), _ = _trunk(
        x_sample, bf(cache_mem_k).reshape(DEPTH, dec, N_MEM, MEM_WIDTH),
        bf(cache_mem_v).reshape(DEPTH, dec, N_MEM, MEM_WIDTH),
        state_ssm_re, state_ssm_im, state_conv, cache_mla_latent, cache_mla_krope, w, SAMPLE_CFG, ffn_bf)
    shape5 = (DEPTH, bsz, N_MEM, MEM_HEADS, MEM_HEAD_DIM)
    return (y_prompt, y_sample, mem_k_p.reshape(shape5), mem_v_p.reshape(shape5), lat_p, krope_p,
            ssm_re_p, ssm_im_p, conv_p, lat_s, krope_s, ssm_re_s, ssm_im_s, conv_s)
```

```python
import functools
import math

import jax
import jax.numpy as jnp
from jax import lax
from jax.experimental import pallas as pl
from jax.experimental.pallas import tpu as pltpu

F32 = jnp.float32
BF16 = jnp.bfloat16

D_MODEL = 1024
DEPTH = 4
CHUNK = 64
N_A_LAYERS = DEPTH // 2
N_B_LAYERS = DEPTH - N_A_LAYERS
MIX_IN = 768
MEM_HEADS = 4
MEM_HEAD_DIM = 64
MEM_WIDTH = MEM_HEADS * MEM_HEAD_DIM
N_MEM = 256
SSM_GROUP = 16
SSM_GROUPS = MIX_IN // SSM_GROUP
SSM_STATE = 64
SSM_COLS = SSM_GROUPS * SSM_STATE
MLA_HEADS = 12
NOPE_DIM = 64
ROPE_DIM = 32
ROPE_HALF = ROPE_DIM // 2
V_DIM = 64
KV_LORA = 256
ROPE_BASE = 10000.0
MLA_SCALE = (NOPE_DIM + ROPE_DIM) ** -0.5
MEM_SCALE = MEM_HEAD_DIM ** -0.5
D_FF = 2816
CONV_W = 3
EPS = 1e-6
NEG_INF = -1e30

V7X_VMEM_LIMIT_BYTES = 56 * 1024 * 1024
LANES = 128
SUBLANES = 8
V7X_MXU_DIM = 256

S5_BLOCKS = 3
S5_BLOCK_CH = MIX_IN // S5_BLOCKS
S5_BLOCK_ST = SSM_COLS // S5_BLOCKS
FFN_TF = 256
N_FF_TILES = D_FF // FFN_TF
FFN_PAD_SLOTS = 4


def _params(n_axes):
    return pltpu.CompilerParams(
        dimension_semantics=("arbitrary",) * n_axes,
        vmem_limit_bytes=V7X_VMEM_LIMIT_BYTES,
    )


def _dot(a, b):
    return jnp.dot(a, b, preferred_element_type=F32)


def _dot_nt(a, b):
    return lax.dot_general(a, b, (((1,), (1,)), ((), ())), preferred_element_type=F32)


def _two_row_halves(dot_fn, a, b):
    rows = a.shape[0]
    if rows < V7X_MXU_DIM or rows % 32:
        return dot_fn(a, b)
    return jnp.concatenate([dot_fn(a[:rows // 2], b), dot_fn(a[rows // 2:], b)], axis=0)


def _rms_rows(x, g):
    ms = jnp.mean(x * x, axis=-1, keepdims=True)
    return x * lax.rsqrt(ms + EPS) * g


def _seg_rms(x, seg_mat, g):
    ms = _dot((x * x).astype(BF16), seg_mat)
    return x * lax.rsqrt(ms + EPS) * g


def _seg_matrix(width, seg):
    idx = jnp.arange(width) // seg
    return jnp.where(idx[:, None] == idx[None, :], 1.0 / seg, 0.0).astype(BF16)


def _full(shape):
    nd = len(shape)
    return pl.BlockSpec(shape, lambda *_: (0,) * nd)


def _rope_table_kernel(pos_ref, inv_ref, cos_ref, sin_ref):
    ang = pos_ref[...] * inv_ref[...]
    cos_ref[...] = jnp.cos(ang)
    sin_ref[...] = jnp.sin(ang)


def _rope_tables(past, length):
    pos = (past + jnp.arange(length, dtype=jnp.int32)).astype(F32)[:, None]
    inv_freq = (1.0 / (ROPE_BASE ** (jnp.arange(0, ROPE_DIM, 2, dtype=F32) / ROPE_DIM)))[None, :]
    cos, sin = pl.pallas_call(
        _rope_table_kernel,
        out_shape=[jax.ShapeDtypeStruct((length, ROPE_HALF), F32)] * 2,
        name="rope_table",
    )(pos, inv_freq)
    cos_t = jnp.concatenate([cos, cos], axis=1)
    sin_t = jnp.concatenate([-sin, sin], axis=1)
    return cos_t, sin_t


def _memkv_kernel(mem_ref, g_ref, w_ref, kg_ref, seg_ref, k_ref, v_ref, kb_ref, vb_ref):
    bsz = mem_ref.shape[0]
    mem = mem_ref[...].reshape(bsz * N_MEM, D_MODEL)
    xn = _rms_rows(mem, g_ref[0]).astype(BF16)
    kv = _dot(xn, w_ref[0].astype(BF16))
    k = _seg_rms(kv[:, :MEM_WIDTH], seg_ref[...], kg_ref[0]).reshape(bsz, N_MEM, MEM_WIDTH)
    v = kv[:, MEM_WIDTH:].reshape(bsz, N_MEM, MEM_WIDTH)
    k_ref[0] = k
    v_ref[0] = v
    kb_ref[0] = k.astype(BF16)
    vb_ref[0] = v.astype(BF16)


def _memory_kv(mem, mem_norm_g, w_mem_kv, mem_k_norm_g, seg_mem):
    bsz = mem.shape[0]
    kg = jnp.tile(mem_k_norm_g, (1, MEM_HEADS)).reshape(DEPTH, 1, MEM_WIDTH)
    out4 = lambda dt: jax.ShapeDtypeStruct((DEPTH, bsz, N_MEM, MEM_WIDTH), dt)
    spec4 = pl.BlockSpec((1, bsz, N_MEM, MEM_WIDTH), lambda l: (l, 0, 0, 0))
    return pl.pallas_call(
        _memkv_kernel,
        grid=(DEPTH,),
        in_specs=[
            _full((bsz, N_MEM, D_MODEL)),
            pl.BlockSpec((1, 1, D_MODEL), lambda l: (l, 0, 0)),
            pl.BlockSpec((1, D_MODEL, 2 * MEM_WIDTH), lambda l: (l, 0, 0)),
            pl.BlockSpec((1, 1, MEM_WIDTH), lambda l: (l, 0, 0)),
            _full((MEM_WIDTH, MEM_WIDTH)),
        ],
        out_specs=[spec4, spec4, spec4, spec4],
        out_shape=[out4(F32), out4(F32), out4(BF16), out4(BF16)],
        compiler_params=_params(1),
        name="memory_kv",
    )(mem, mem_norm_g.reshape(DEPTH, 1, D_MODEL), w_mem_kv, kg, seg_mem)


def _mixin_kernel(h_ref, g_ref, w_ref, qg_ref, seg_ref, k_ref, v_ref, *rest, seqs, rows_per_seq, n_cast):
    cast_in, (zmix_ref, mem_ref), cast_out = rest[:n_cast], rest[n_cast:n_cast + 2], rest[n_cast + 2:]
    for src, dst in zip(cast_in, cast_out):
        dst[...] = src[...].astype(BF16)
    xn = _rms_rows(h_ref[...], g_ref[...]).astype(BF16)
    z = _dot(xn, w_ref[...].astype(BF16))
    zmix_ref[...] = z[:, :MIX_IN]
    mq = _seg_rms(z[:, MIX_IN:], seg_ref[...], qg_ref[...]).astype(BF16)
    lane = lax.broadcasted_iota(jnp.int32, (1, MEM_WIDTH), 1)
    in_head = [(lane // MEM_HEAD_DIM) == head for head in range(MEM_HEADS)]
    for b in range(seqs):
        rows = slice(b * rows_per_seq, (b + 1) * rows_per_seq)
        mq_b = mq[rows]
        qs = jnp.concatenate([jnp.where(m, mq_b, jnp.zeros_like(mq_b)) for m in in_head], axis=0)
        s = _two_row_halves(_dot_nt, qs, k_ref[b]) * MEM_SCALE
        p = jnp.exp(s - jnp.max(s, axis=-1, keepdims=True))
        p = p * (1.0 / jnp.sum(p, axis=-1, keepdims=True))
        o = _two_row_halves(_dot, p.astype(BF16), v_ref[b])
        out = jnp.zeros(mq_b.shape, F32)
        for head, m in enumerate(in_head):
            out = jnp.where(m, o[head * rows_per_seq:(head + 1) * rows_per_seq], out)
        mem_ref[rows, :] = out.astype(BF16)


def _mixin(h, w, layer, k_bf, v_bf, rows_per_seq_tile, to_cast=()):
    nseq, length, _ = h.shape
    m_rows = nseq * length
    if rows_per_seq_tile >= length:
        seqs, rps, tps = nseq, length, 1
    else:
        seqs, rps, tps = 1, rows_per_seq_tile, length // rows_per_seq_tile
    tm = seqs * rps
    n_steps = m_rows // tm
    kern = functools.partial(_mixin_kernel, seqs=seqs, rows_per_seq=rps, n_cast=len(to_cast))
    row = lambda width: pl.BlockSpec((tm, width), lambda m: (m, 0))
    of_layer = lambda *shape: pl.BlockSpec((None,) + shape, lambda m: (layer,) + (0,) * len(shape))
    kv_spec = pl.BlockSpec((None, seqs, N_MEM, MEM_WIDTH), lambda m: (layer, m // tps, 0, 0))
    slab = lambda t: t.shape[1] // n_steps
    cast_in = [pl.BlockSpec((None, slab(t), t.shape[2]), lambda m: (layer, m, 0)) for t in to_cast]
    cast_out = [pl.BlockSpec((slab(t), t.shape[2]), lambda m: (m, 0)) for t in to_cast]
    outs = pl.pallas_call(
        kern,
        grid=(n_steps,),
        in_specs=[
            row(D_MODEL),
            of_layer(1, D_MODEL),
            of_layer(D_MODEL, MIX_IN + MEM_WIDTH),
            of_layer(1, MEM_WIDTH),
            _full((MEM_WIDTH, MEM_WIDTH)),
            kv_spec, kv_spec,
        ] + cast_in,
        out_specs=[row(MIX_IN), row(MEM_WIDTH)] + cast_out,
        out_shape=[
            jax.ShapeDtypeStruct((m_rows, MIX_IN), F32),
            jax.ShapeDtypeStruct((m_rows, MEM_WIDTH), BF16),
        ] + [jax.ShapeDtypeStruct(t.shape[1:], BF16) for t in to_cast],
        compiler_params=_params(1),
        name="mix_in_mem_attn",
    )(h.reshape(m_rows, D_MODEL), w["norm_mix_g"], w["w_mix_in"], w["mem_q_g"], w["seg_mem"], k_bf, v_bf,
      *to_cast)
    return outs[0].reshape(nseq, length, MIX_IN), outs[1].reshape(nseq, length, MEM_WIDTH), outs[2:]


def _s5_discretise_kernel(are_ref, aim_ref, ldt_ref, lre_ref, lim_ref, fre_ref, fim_ref):
    a_re, a_im = are_ref[...], aim_ref[...]
    dt = jnp.exp(ldt_ref[...])
    mag = jnp.exp(a_re * dt)
    lam_re = mag * jnp.cos(a_im * dt)
    lam_im = mag * jnp.sin(a_im * dt)
    den = a_re * a_re + a_im * a_im
    x_re = lam_re - 1.0
    lre_ref[...] = lam_re
    lim_ref[...] = lam_im
    fre_ref[...] = (x_re * a_re + lam_im * a_im) / den
    fim_ref[...] = (lam_im * a_re - x_re * a_im) / den


def _s5_input_scale_kernel(fre_ref, fim_ref, bre_ref, bim_ref, bbre_ref, bbim_ref):
    f_re, f_im = fre_ref[0], fim_ref[0]
    b_re, b_im = bre_ref[0], bim_ref[0]
    bbre_ref[0] = f_re * b_re - f_im * b_im
    bbim_ref[0] = f_re * b_im + f_im * b_re


def _s5_prepare(a_re, a_im, log_dt, b_re, b_im, c_re, c_im):
    n = a_re.shape[0]
    dense = lambda t: t.reshape(n, SSM_COLS // LANES, LANES)
    ldt = jnp.broadcast_to(log_dt[:, :, None], (n, SSM_GROUPS, SSM_STATE))
    lam_re, lam_im, f_re, f_im = pl.pallas_call(
        _s5_discretise_kernel,
        out_shape=[jax.ShapeDtypeStruct((n, SSM_COLS // LANES, LANES), F32)] * 4,
        name="s5_discretise",
    )(dense(a_re), dense(a_im), dense(ldt))
    col = lambda t: t.reshape(n, SSM_COLS, 1)
    col_spec = pl.BlockSpec((1, SSM_COLS, 1), lambda l: (l, 0, 0))
    b_spec = pl.BlockSpec((1, SSM_COLS, SSM_GROUP), lambda l: (l, 0, 0))
    bb_re, bb_im = pl.pallas_call(
        _s5_input_scale_kernel,
        grid=(n,),
        in_specs=[col_spec, col_spec, b_spec, b_spec],
        out_specs=[b_spec, b_spec],
        out_shape=[jax.ShapeDtypeStruct((n, SSM_COLS, SSM_GROUP), F32)] * 2,
        compiler_params=_params(1),
        name="s5_input_scale",
    )(col(f_re), col(f_im), b_re.reshape(n, SSM_COLS, SSM_GROUP), b_im.reshape(n, SSM_COLS, SSM_GROUP))
    gpb = SSM_GROUPS // S5_BLOCKS

    def block_diagonal(t, rows_per_group, cols_per_group):
        rows = gpb * rows_per_group
        same = (jnp.arange(rows) // rows_per_group)[:, None] == jnp.arange(gpb)[None, :]
        wide = jnp.broadcast_to(t[:, :, :, None, :], (n, S5_BLOCKS, rows, gpb, cols_per_group))
        return jnp.where(same[None, None, :, :, None], wide, 0.0).reshape(
            n, S5_BLOCKS, rows, gpb * cols_per_group)

    def in_blocks(bb):
        t = bb.reshape(n, S5_BLOCKS, gpb, SSM_STATE, SSM_GROUP).transpose(0, 1, 2, 4, 3)
        return block_diagonal(t.reshape(n, S5_BLOCKS, S5_BLOCK_CH, SSM_STATE), SSM_GROUP, SSM_STATE)

    def out_blocks(c):
        t = c.reshape(n, S5_BLOCKS, gpb, SSM_GROUP, SSM_STATE).transpose(0, 1, 2, 4, 3)
        return block_diagonal(t.reshape(n, S5_BLOCKS, S5_BLOCK_ST, SSM_GROUP), SSM_STATE, SSM_GROUP)

    w_in = jnp.concatenate([in_blocks(bb_re), in_blocks(bb_im)], axis=-1).astype(BF16)
    w_out = jnp.concatenate([out_blocks(c_re), -out_blocks(c_im)], axis=-2).astype(BF16)
    return lam_re.reshape(n, 1, SSM_COLS), lam_im.reshape(n, 1, SSM_COLS), w_in, w_out


def _s5_kernel(u_ref, h0re_ref, h0im_ref, lre_ref, lim_ref, win_ref, wout_ref, d_ref, wglu_ref, bglu_ref,
               out_ref, sre_out_ref, sim_out_ref, hb_ref, sre_ref, sim_ref, *, bsz, t_chunk, col_block):
    c = pl.program_id(0)
    rows = bsz * t_chunk

    @pl.when(c == 0)
    def _():
        sre_ref[...] = h0re_ref[...]
        sim_ref[...] = h0im_ref[...]

    u = u_ref[...]
    ut = jnp.swapaxes(u, 0, 1).reshape(rows, MIX_IN).astype(BF16)
    ys = []
    for j in range(S5_BLOCKS):
        cols = slice(j * 2 * S5_BLOCK_ST, (j + 1) * 2 * S5_BLOCK_ST)
        hb_ref[:, cols] = _dot(ut[:, j * S5_BLOCK_CH:(j + 1) * S5_BLOCK_CH], win_ref[j])
        for sub in range(S5_BLOCK_ST // col_block):
            nat = j * S5_BLOCK_ST + sub * col_block
            cre = j * 2 * S5_BLOCK_ST + sub * col_block
            cim = cre + S5_BLOCK_ST
            lam_r = jnp.broadcast_to(lre_ref[:, nat:nat + col_block], (bsz, col_block))
            lam_i = jnp.broadcast_to(lim_ref[:, nat:nat + col_block], (bsz, col_block))
            s_r = sre_ref[:, nat:nat + col_block]
            s_i = sim_ref[:, nat:nat + col_block]
            for t in range(t_chunk):
                rows_t = slice(t * bsz, (t + 1) * bsz)
                n_r = lam_r * s_r - lam_i * s_i + hb_ref[rows_t, cre:cre + col_block]
                n_i = lam_r * s_i + lam_i * s_r + hb_ref[rows_t, cim:cim + col_block]
                hb_ref[rows_t, cre:cre + col_block] = n_r
                hb_ref[rows_t, cim:cim + col_block] = n_i
                s_r, s_i = n_r, n_i
            sre_ref[:, nat:nat + col_block] = s_r
            sim_ref[:, nat:nat + col_block] = s_i
        ys.append(_two_row_halves(_dot, hb_ref[:, cols].astype(BF16), wout_ref[j]))
    yt = jnp.concatenate(ys, axis=1).reshape(t_chunk, bsz, MIX_IN)
    y = jnp.swapaxes(yt, 0, 1) + d_ref[...] * u
    y = jax.nn.gelu(y).reshape(rows, MIX_IN)
    gate = _dot(y.astype(BF16), wglu_ref[...].astype(BF16)) + bglu_ref[...]
    out_ref[...] = (y * jax.nn.sigmoid(gate)).reshape(bsz, t_chunk, MIX_IN).astype(BF16)

    @pl.when(c == pl.num_programs(0) - 1)
    def _():
        sre_out_ref[...] = sre_ref[...]
        sim_out_ref[...] = sim_ref[...]


def _s5_mixer(u, h0_re, h0_im, lam_re, lam_im, w_in, w_out, d_skip, w_glu, b_glu, t_chunk):
    bsz, length, _ = u.shape
    rows = bsz * t_chunk
    col_block = (SUBLANES * 512) // bsz
    kern = functools.partial(_s5_kernel, bsz=bsz, t_chunk=t_chunk, col_block=col_block)
    state = jax.ShapeDtypeStruct((bsz, SSM_COLS), F32)
    return pl.pallas_call(
        kern,
        grid=(length // t_chunk,),
        in_specs=[
            pl.BlockSpec((bsz, t_chunk, MIX_IN), lambda c: (0, c, 0)),
            _full((bsz, SSM_COLS)), _full((bsz, SSM_COLS)),
            _full((1, SSM_COLS)), _full((1, SSM_COLS)),
            _full((S5_BLOCKS, S5_BLOCK_CH, 2 * S5_BLOCK_ST)),
            _full((S5_BLOCKS, 2 * S5_BLOCK_ST, S5_BLOCK_CH)),
            _full((1, MIX_IN)), _full((MIX_IN, MIX_IN)), _full((1, MIX_IN)),
        ],
        out_specs=[
            pl.BlockSpec((bsz, t_chunk, MIX_IN), lambda c: (0, c, 0)),
            _full((bsz, SSM_COLS)), _full((bsz, SSM_COLS)),
        ],
        out_shape=[jax.ShapeDtypeStruct((bsz, length, MIX_IN), BF16), state, state],
        scratch_shapes=[
            pltpu.VMEM((rows, 2 * SSM_COLS), F32),
            pltpu.VMEM((bsz, SSM_COLS), F32),
            pltpu.VMEM((bsz, SSM_COLS), F32),
        ],
        compiler_params=_params(1),
        name="s5_mixer",
    )(u, h0_re, h0_im, lam_re, lam_im, w_in, w_out, d_skip, w_glu, b_glu)


def _ffn_kernel(h_ref, mix_ref, mem_ref, wo_ref, g_ref, win_ref, cw_ref, cb_ref, w2_ref, ctx_ref,
                out_ref, new_ref, act_ref, pad_ref, carry_ref, *, seqs, rows_per_seq, tiles_per_seq):
    m = pl.program_id(0)
    tm = seqs * rows_per_seq
    h1 = h_ref[...] + _dot(mix_ref[...], wo_ref[:MIX_IN, :]) + _dot(mem_ref[...], wo_ref[MIX_IN:, :])
    out_ref[...] = h1
    xn = _rms_rows(h1, g_ref[...]).astype(BF16)

    if tiles_per_seq > 1:
        @pl.when(m == 0)
        def _():
            carry_ref[...] = jnp.zeros(carry_ref.shape, F32)

    def causal_conv(col, slot):
        cols = slice(col, col + FFN_TF)
        u = _two_row_halves(_dot, xn, win_ref[:, cols])
        ctx = ctx_ref[:, :, cols]
        if tiles_per_seq > 1:
            ctx = jnp.where(m % tiles_per_seq == 0, ctx, carry_ref[:, cols][None])
        cw = cw_ref[:, cols]
        if seqs == 1:
            row = lax.broadcasted_iota(jnp.int32, (SUBLANES, 1), 0)
            back1 = pltpu.roll(u, 1, axis=0)
            back2 = pltpu.roll(u, 2, axis=0)
            head1 = jnp.where(row == 0, ctx[0, 1:2], back1[:SUBLANES])
            head2 = jnp.where(row == 0, ctx[0, 0:1], jnp.where(row == 1, ctx[0, 1:2], back2[:SUBLANES]))
            back1 = jnp.concatenate([head1, back1[SUBLANES:]], axis=0)
            back2 = jnp.concatenate([head2, back2[SUBLANES:]], axis=0)
            y = cb_ref[:, cols] + back2 * cw[0:1] + back1 * cw[1:2] + u * cw[2:3]
            last2 = u[tm - 2:][None]
        else:
            pad = pad_ref.at[slot]
            pad[:, 6:8, :] = ctx
            pad[:, 8:, :] = u.reshape(seqs, rows_per_seq, FFN_TF)
            y = (cb_ref[:, cols] + pad[:, 6:6 + rows_per_seq, :] * cw[0:1]
                 + pad[:, 7:7 + rows_per_seq, :] * cw[1:2]
                 + pad[:, 8:8 + rows_per_seq, :] * cw[2:3]).reshape(tm, FFN_TF)
            last2 = pad[:, rows_per_seq + 6:rows_per_seq + 8, :]
        new_ref[:, :, cols] = last2
        if tiles_per_seq > 1:
            carry_ref[:, cols] = last2[0]
        return y

    for f in range(N_FF_TILES):
        ya = causal_conv(f * FFN_TF, (2 * f) % FFN_PAD_SLOTS)
        yg = causal_conv(D_FF + f * FFN_TF, (2 * f + 1) % FFN_PAD_SLOTS)
        act_ref[:, f * FFN_TF:(f + 1) * FFN_TF] = (jax.nn.silu(yg) * ya).astype(BF16)
    out_ref[...] += _dot(act_ref[...], w2_ref[...])


def _mixout_ffn(h, mix, mem, w, layer, ffn_w, ctx, rows_per_seq_tile):
    nseq, length, _ = h.shape
    m_rows = nseq * length
    if rows_per_seq_tile >= length:
        seqs, rps, tps = nseq, length, 1
    else:
        seqs, rps, tps = 1, rows_per_seq_tile, length // rows_per_seq_tile
    tm = seqs * rps
    n_m = m_rows // tm
    kern = functools.partial(_ffn_kernel, seqs=seqs, rows_per_seq=rps, tiles_per_seq=tps)
    row = lambda width: pl.BlockSpec((tm, width), lambda m: (m, 0))
    resident = lambda *shape: pl.BlockSpec(shape, lambda m: (0,) * len(shape), pipeline_mode=pl.Buffered(1))
    of_layer = lambda *shape: pl.BlockSpec((None,) + shape, lambda m: (layer,) + (0,) * len(shape),
                                           pipeline_mode=pl.Buffered(1))
    out, new = pl.pallas_call(
        kern,
        grid=(n_m,),
        in_specs=[
            row(D_MODEL), row(MIX_IN), row(MEM_WIDTH),
            resident(MIX_IN + MEM_WIDTH, D_MODEL),
            of_layer(1, D_MODEL),
            resident(D_MODEL, 2 * D_FF),
            of_layer(CONV_W, 2 * D_FF),
            of_layer(1, 2 * D_FF),
            resident(D_FF, D_MODEL),
            pl.BlockSpec((seqs, CONV_W - 1, 2 * D_FF), lambda m: (m // tps, 0, 0)),
        ],
        out_specs=[row(D_MODEL), pl.BlockSpec((seqs, CONV_W - 1, 2 * D_FF), lambda m: (m, 0, 0))],
        out_shape=[
            jax.ShapeDtypeStruct((m_rows, D_MODEL), F32),
            jax.ShapeDtypeStruct((n_m * seqs, CONV_W - 1, 2 * D_FF), F32),
        ],
        scratch_shapes=[
            pltpu.VMEM((tm, D_FF), BF16),
            pltpu.VMEM((FFN_PAD_SLOTS, seqs, rps + SUBLANES, FFN_TF) if seqs > 1 else (1, 1, SUBLANES, LANES), F32),
            pltpu.VMEM((CONV_W - 1, 2 * D_FF), F32),
        ],
        compiler_params=_params(1),
        name="mix_out_conv_ffn",
    )(h.reshape(m_rows, D_MODEL), mix.reshape(m_rows, MIX_IN), mem.reshape(m_rows, MEM_WIDTH),
      ffn_w[0], w["norm_ffn_g"], ffn_w[1], w["ffn_conv_w"], w["ffn_conv_b"], ffn_w[2], ctx)
    new_ctx = new.reshape(nseq, tps, CONV_W - 1, 2 * D_FF)[:, -1]
    return out.reshape(nseq, length, D_MODEL), new_ctx


def _dkv_kernel(h_ref, g_ref, wl_ref, wr_ref, wrr_ref, lg_ref, kg_ref, kgr_ref, cos_ref, sin_ref, *rest):
    fused_up = len(rest) > 2
    lat_ref, kr_ref = rest[4:6] if fused_up else rest
    xn = _rms_rows(h_ref[...], g_ref[...]).astype(BF16)
    lat = _rms_rows(_two_row_halves(_dot, xn, wl_ref[...]), lg_ref[...])
    lat_ref[...] = lat
    if fused_up:
        wk_ref, wv_ref, seg_ref, kng_ref = rest[:4]
        k_ref, v_ref = rest[6:]
        lat_bf = lat.astype(BF16)
        k_ref[...] = _seg_rms(_dot(lat_bf, wk_ref[...]), seg_ref[...], kng_ref[...]).astype(BF16)
        v_ref[...] = _dot(lat_bf, wv_ref[...]).astype(BF16)
    kr = _dot(xn, wr_ref[...])
    kr_rot = _dot(xn, wrr_ref[...])
    r = lax.rsqrt(jnp.mean(kr * kr, axis=-1, keepdims=True) + EPS)
    kr_ref[...] = (kr * r * kg_ref[...]) * cos_ref[...] + (kr_rot * r * kgr_ref[...]) * sin_ref[...]


def _swap_halves(t, axis=-1):
    a, b = jnp.split(t, 2, axis=axis)
    return jnp.concatenate([b, a], axis=axis)


def _shared_kv_down(h, kv_norm_g, w_dkv_bf, latent_norm_g, krope_norm_g, cos_rows, sin_rows, tm, up=None):
    nseq, length, _ = h.shape
    m_rows = nseq * length
    tab_blocks = cos_rows.shape[0] // tm
    w_l = w_dkv_bf[:, :KV_LORA]
    w_r = w_dkv_bf[:, KV_LORA:]
    kg = krope_norm_g.reshape(1, ROPE_DIM)
    width = MLA_HEADS * NOPE_DIM
    row = lambda wd: pl.BlockSpec((tm, wd), lambda m: (m, 0))
    in_specs = [
        row(D_MODEL),
        _full((1, D_MODEL)),
        _full((D_MODEL, KV_LORA)), _full((D_MODEL, ROPE_DIM)), _full((D_MODEL, ROPE_DIM)),
        _full((1, KV_LORA)), _full((1, ROPE_DIM)), _full((1, ROPE_DIM)),
        pl.BlockSpec((tm, ROPE_DIM), lambda m: (m % tab_blocks, 0)),
        pl.BlockSpec((tm, ROPE_DIM), lambda m: (m % tab_blocks, 0)),
    ]
    out_specs = [row(KV_LORA), row(ROPE_DIM)]
    out_shape = [jax.ShapeDtypeStruct((m_rows, KV_LORA), F32), jax.ShapeDtypeStruct((m_rows, ROPE_DIM), F32)]
    extra = ()
    if up is not None:
        in_specs += [_full((KV_LORA, width)), _full((KV_LORA, width)), _full((width, width)), _full((1, width))]
        out_specs += [row(width), row(width)]
        out_shape += [jax.ShapeDtypeStruct((m_rows, width), BF16)] * 2
        extra = tuple(up)
    outs = pl.pallas_call(
        _dkv_kernel,
        grid=(m_rows // tm,),
        in_specs=in_specs,
        out_specs=out_specs,
        out_shape=out_shape,
        compiler_params=_params(1),
        name="shared_kv_down",
    )(h.reshape(m_rows, D_MODEL), kv_norm_g.reshape(1, D_MODEL), w_l, w_r, _swap_halves(w_r),
      latent_norm_g.reshape(1, KV_LORA), kg, _swap_halves(kg), cos_rows, sin_rows, *extra)
    lat, kr = outs[0].reshape(nseq, length, KV_LORA), outs[1].reshape(nseq, length, ROPE_DIM)
    return (lat, kr) + tuple(o.reshape(nseq, length, width) for o in outs[2:])


Q_PRESCALE = MLA_SCALE * math.log2(math.e)


def _q_kernel(z_ref, g_ref, wn_ref, wa_ref, segn_ref, segr_ref, gn_ref, ga_ref,
              cos_ref, sin_ref, kg_ref, wuk_ref, qn_ref, qr_ref, *maybe_absorbed_refs):
    xn = _rms_rows(z_ref[...], g_ref[...]).astype(BF16)
    qn = _seg_rms(_dot(xn, wn_ref[...]), segn_ref[...], gn_ref[...]) * Q_PRESCALE
    qn_ref[...] = qn.astype(BF16)
    a = _dot(xn, wa_ref[...])
    an = a * lax.rsqrt(_dot((a * a).astype(BF16), segr_ref[...]) + EPS) * ga_ref[...]
    width = an.shape[1]
    lane = lax.broadcasted_iota(jnp.int32, (1, width), 1)
    partner = jnp.where(lane % ROPE_DIM < ROPE_HALF,
                        pltpu.roll(an, width - ROPE_HALF, axis=1), pltpu.roll(an, ROPE_HALF, axis=1))
    rot = an * cos_ref[...] + partner * sin_ref[...]
    qr_ref[...] = (rot * Q_PRESCALE).astype(BF16)
    if maybe_absorbed_refs:
        qp_ref, qrh_ref = maybe_absorbed_refs
        lane = lax.broadcasted_iota(jnp.int32, (1, MLA_HEADS * NOPE_DIM), 1)
        qg = qn * kg_ref[...]
        rot_bf = (rot * Q_PRESCALE).astype(BF16)
        for head in range(MLA_HEADS):
            qh = jnp.where(lane // NOPE_DIM == head, qg, 0.0).astype(BF16)
            qp_ref[head] = _dot_nt(qh, wuk_ref[...]).astype(BF16)
            qrh_ref[head] = rot_bf[:, head * ROPE_DIM:(head + 1) * ROPE_DIM]


def _q_side(z_mix, q_latent_g, w_uq_bf, q_nope_g, q_rope_g, seg_nope, seg_rope, cos_rows, sin_rows,
            k_g_tiled, w_uk_bf, tm, absorbed):
    nseq, length, _ = z_mix.shape
    m_rows = nseq * length
    tab_blocks = cos_rows.shape[0] // tm
    wn_width = MLA_HEADS * NOPE_DIM
    wr_width = MLA_HEADS * ROPE_DIM
    w3 = w_uq_bf.reshape(MIX_IN, MLA_HEADS, NOPE_DIM + ROPE_DIM)
    w_n = w3[:, :, :NOPE_DIM].reshape(MIX_IN, wn_width)
    w_a = w3[:, :, NOPE_DIM:].reshape(MIX_IN, wr_width)
    g_n = jnp.tile(q_nope_g.reshape(1, NOPE_DIM), (1, MLA_HEADS))
    g_a = jnp.tile(q_rope_g.reshape(1, ROPE_DIM), (1, MLA_HEADS))
    cos_q = jnp.tile(cos_rows, (1, MLA_HEADS))
    sin_q = jnp.tile(sin_rows, (1, MLA_HEADS))
    out_specs = [pl.BlockSpec((tm, wd), lambda m: (m, 0)) for wd in (wn_width, wr_width)]
    out_shape = [jax.ShapeDtypeStruct((m_rows, wd), BF16) for wd in (wn_width, wr_width)]
    if absorbed:
        for wd in (KV_LORA, ROPE_DIM):
            out_specs.append(pl.BlockSpec((MLA_HEADS, tm, wd), lambda m: (0, m, 0)))
            out_shape.append(jax.ShapeDtypeStruct((MLA_HEADS, m_rows, wd), BF16))
    outs = pl.pallas_call(
        _q_kernel,
        grid=(m_rows // tm,),
        in_specs=[
            pl.BlockSpec((tm, MIX_IN), lambda m: (m, 0)),
            _full((1, MIX_IN)),
            _full((MIX_IN, wn_width)), _full((MIX_IN, wr_width)),
            _full((wn_width, wn_width)), _full((wr_width, wr_width)),
            _full((1, wn_width)), _full((1, wr_width)),
            pl.BlockSpec((tm, wr_width), lambda m: (m % tab_blocks, 0)),
            pl.BlockSpec((tm, wr_width), lambda m: (m % tab_blocks, 0)),
            _full((1, wn_width)), _full((KV_LORA, wn_width)),
        ],
        out_specs=out_specs,
        out_shape=out_shape,
        compiler_params=_params(1),
        name="mla_query",
    )(z_mix.reshape(m_rows, MIX_IN), q_latent_g.reshape(1, MIX_IN), w_n, w_a, seg_nope, seg_rope,
      g_n, g_a, cos_q, sin_q, k_g_tiled, w_uk_bf)
    return outs


ATTN_HEADS_PER_STEP = 4


def _attn_kernel(qn_ref, qr_ref, kn_ref, kr_ref, v_ref, o_ref, *, tile, n_tiles):
    qi = pl.program_id(2)
    pairs = ATTN_HEADS_PER_STEP // 2
    lane = lax.broadcasted_iota(jnp.int32, (1, LANES), 1)
    head_lanes = [(lane // NOPE_DIM) == j for j in range(2)]
    qr = qr_ref[0]
    qcat = []
    for h in range(ATTN_HEADS_PER_STEP):
        qn = qn_ref[0, :, (h // 2) * LANES:(h // 2 + 1) * LANES]
        qcat.append(jnp.concatenate(
            [jnp.where(head_lanes[h % 2], qn, jnp.zeros_like(qn)),
             jnp.where((lane // ROPE_DIM) == h, qr, jnp.zeros_like(qr))], axis=1))
    row_chunk = lax.broadcasted_iota(jnp.int32, (tile, 1), 0) // CHUNK
    col_chunk = lax.broadcasted_iota(jnp.int32, (1, tile), 1) // CHUNK
    diag_visible = col_chunk <= row_chunk

    def one_block(kb, carry, masked):
        rows_k = slice(kb * tile, (kb + 1) * tile)
        kr = kr_ref[0, rows_k, :]
        new = []
        for h in range(ATTN_HEADS_PER_STEP):
            lanes_p = slice((h // 2) * LANES, (h // 2 + 1) * LANES)
            m_i, acc = carry[h]
            kcat = jnp.concatenate([kn_ref[0, rows_k, lanes_p], kr], axis=1)
            s = _dot_nt(qcat[h], kcat)
            if masked:
                s = jnp.where(diag_visible, s, NEG_INF)
            m_new = jnp.maximum(m_i, jnp.max(s, axis=-1, keepdims=True))
            alpha = jnp.exp2(m_i - m_new)
            p = jnp.exp2(s - m_new).astype(BF16)
            vb = v_ref[0, rows_k, lanes_p]
            v_h = jnp.where(head_lanes[h % 2], vb, jnp.ones_like(vb))
            new.append((m_new, alpha * acc + _dot(p, v_h)))
        return tuple(new)

    def query_tile(n_full):
        carry = tuple((jnp.full((tile, 1), NEG_INF, F32), jnp.zeros((tile, LANES), F32))
                      for _ in range(ATTN_HEADS_PER_STEP))
        for kb in range(n_full):
            carry = one_block(kb, carry, False)
        carry = one_block(n_full, carry, True)
        for p in range(pairs):
            out = jnp.zeros((tile, LANES), F32)
            for j in range(2):
                acc = carry[2 * p + j][1]
                row_sum = pltpu.roll(acc, NOPE_DIM, axis=1)
                out = jnp.where(head_lanes[j], acc * (1.0 / row_sum), out)
            o_ref[0, :, p * LANES:(p + 1) * LANES] = out.astype(BF16)

    for c in range(n_tiles):
        pl.when(qi == c)(functools.partial(query_tile, c))


def _mla_attention(qn, qr, kn, kr4, v, tile):
    nseq, length, _ = qn.shape
    kern = functools.partial(_attn_kernel, tile=tile, n_tiles=length // tile)
    width = (ATTN_HEADS_PER_STEP // 2) * LANES
    return pl.pallas_call(
        kern,
        grid=(nseq, MLA_HEADS // ATTN_HEADS_PER_STEP, length // tile),
        in_specs=[
            pl.BlockSpec((1, tile, width), lambda b, g, i: (b, i, g)),
            pl.BlockSpec((1, tile, LANES), lambda b, g, i: (b, i, g)),
            pl.BlockSpec((1, length, width), lambda b, g, i: (b, 0, g)),
            pl.BlockSpec((1, length, LANES), lambda b, g, i: (b, 0, 0)),
            pl.BlockSpec((1, length, width), lambda b, g, i: (b, 0, g)),
        ],
        out_specs=pl.BlockSpec((1, tile, width), lambda b, g, i: (b, i, g)),
        out_shape=jax.ShapeDtypeStruct((nseq, length, MLA_HEADS * V_DIM), BF16),
        compiler_params=_params(3),
        name="mla_attention",
    )(qn, qr, kn, kr4, v)


KNORM_ROWS = 16


def _key_norm_kernel(past_ref, new_ref, wukt_ref, rt_ref, *, n_chunks, chunk):
    def norms(lat):
        keys = lat.shape[0]
        kt = _dot_nt(wukt_ref[...], lat)
        ss = jnp.sum((kt * kt).reshape(MLA_HEADS, NOPE_DIM, keys), axis=1)
        r = lax.rsqrt(ss * (1.0 / NOPE_DIM) + EPS)
        return jnp.concatenate([r, jnp.ones((KNORM_ROWS - MLA_HEADS, keys), F32)], axis=0)

    for c in range(n_chunks):
        rt_ref[0, :, c * chunk:(c + 1) * chunk] = norms(past_ref[0, c * chunk:(c + 1) * chunk, :].astype(BF16))
    rt_ref[0, :, n_chunks * chunk:] = norms(new_ref[0])


def _key_norms(past_latent, new_lat_pad, w_ukt_bf, chunk):
    nseq, past, _ = past_latent.shape
    new_pad = new_lat_pad.shape[1]
    lk_pad = past + new_pad
    kern = functools.partial(_key_norm_kernel, n_chunks=past // chunk, chunk=chunk)
    return pl.pallas_call(
        kern,
        grid=(nseq,),
        in_specs=[pl.BlockSpec((1, past, KV_LORA), lambda b: (b, 0, 0)),
                  pl.BlockSpec((1, new_pad, KV_LORA), lambda b: (b, 0, 0)),
                  _full((MLA_HEADS * NOPE_DIM, KV_LORA))],
        out_specs=pl.BlockSpec((1, KNORM_ROWS, lk_pad), lambda b: (b, 0, 0)),
        out_shape=jax.ShapeDtypeStruct((nseq, KNORM_ROWS, lk_pad), F32),
        compiler_params=_params(1),
        name="mla_key_norms",
    )(past_latent, new_lat_pad, w_ukt_bf)


def _attn_absorbed_kernel(qs_ref, qrs_ref, plat_ref, nlat_ref, pkrt_ref, nkrt_ref, rt_ref, wuv_ref, o_ref, op_ref,
                          *, tq, q_off, lk_valid):
    b = pl.program_id(0)
    nseq = pl.num_programs(0)
    rows = MLA_HEADS * tq
    lat = jnp.concatenate([plat_ref[0].astype(BF16), nlat_ref[0]], axis=0)
    kr_t = jnp.concatenate([pkrt_ref[0].astype(BF16), nkrt_ref[0]], axis=1)
    lk_pad = lat.shape[0]
    rt = rt_ref[0]
    knorm = jnp.concatenate([jnp.broadcast_to(rt[h:h + 1, :], (tq, lk_pad)) for h in range(MLA_HEADS)], axis=0)
    qs = qs_ref[...].reshape(rows, KV_LORA)
    qrs = qrs_ref[...].reshape(rows, ROPE_DIM)
    s = _dot_nt(qs, lat) * knorm + _dot(qrs, kr_t)
    k_pos = lax.broadcasted_iota(jnp.int32, (1, lk_pad), 1)
    if (lk_valid - 1) // CHUNK > q_off // CHUNK:
        q_chunk = (q_off + lax.broadcasted_iota(jnp.int32, (rows, 1), 0) % tq) // CHUNK
        s = jnp.where(k_pos // CHUNK <= q_chunk, s, NEG_INF)
    s = jnp.where(k_pos < lk_valid, s, NEG_INF)
    p = jnp.exp2(s - jnp.max(s, axis=-1, keepdims=True))
    l = jnp.sum(p, axis=-1, keepdims=True)
    o = _two_row_halves(_dot, p.astype(BF16), lat)
    op_ref[b] = (o * (1.0 / l)).astype(BF16)

    @pl.when(b == nseq - 1)
    def _():
        n_all = op_ref.shape[0]
        lane_o = lax.broadcasted_iota(jnp.int32, (1, MLA_HEADS * V_DIM), 1)
        out = jnp.zeros((n_all * tq, MLA_HEADS * V_DIM), F32)
        for h in range(MLA_HEADS):
            x = op_ref[:, h * tq:(h + 1) * tq, :].reshape(n_all * tq, KV_LORA)
            out = jnp.where(lane_o // V_DIM == h, _dot(x, wuv_ref[...]), out)
        o_ref[...] = out.astype(BF16)


def _mla_attention_absorbed(qp, qrh, past_latent, new_lat_pad, past_krope_t, new_kr_pad_t, r_t, w_uv_bf,
                            tq, lk_valid):
    nseq, past, _ = past_latent.shape
    new_pad = new_lat_pad.shape[1]
    kern = functools.partial(_attn_absorbed_kernel, tq=tq, q_off=past, lk_valid=lk_valid)
    per_seq = lambda n, width: pl.BlockSpec((1, n, width), lambda b: (b, 0, 0))
    heads_of_seq = lambda width: pl.BlockSpec((MLA_HEADS, tq, width), lambda b: (0, b, 0))
    out = pl.pallas_call(
        kern,
        grid=(nseq,),
        in_specs=[
            heads_of_seq(KV_LORA), heads_of_seq(ROPE_DIM),
            per_seq(past, KV_LORA), per_seq(new_pad, KV_LORA),
            per_seq(ROPE_DIM, past), per_seq(ROPE_DIM, new_pad),
            pl.BlockSpec((1, KNORM_ROWS, past + new_pad), lambda b: (b, 0, 0)),
            _full((KV_LORA, MLA_HEADS * V_DIM)),
        ],
        out_specs=_full((nseq * tq, MLA_HEADS * V_DIM)),
        out_shape=jax.ShapeDtypeStruct((nseq * tq, MLA_HEADS * V_DIM), BF16),
        scratch_shapes=[pltpu.VMEM((nseq, MLA_HEADS * tq, KV_LORA), BF16)],
        compiler_params=_params(1),
        name="mla_attention_absorbed",
    )(qp, qrh, past_latent, new_lat_pad, past_krope_t, new_kr_pad_t, r_t, w_uv_bf)
    return out.reshape(nseq, tq, MLA_HEADS * V_DIM)


def _trunk(x, mem_k_bf, mem_v_bf, ssm_h0_re, ssm_h0_im, conv_ctx, past_latent, past_krope, w, cfg, ffn_bf=None):
    nseq, length, _ = x.shape
    past = 0 if past_latent is None else past_latent.shape[1]
    cos_t, sin_t = _rope_tables(past, length)
    reps = cfg["rope_rows"] // length
    cos_rows = jnp.tile(cos_t, (reps, 1))
    sin_rows = jnp.tile(sin_t, (reps, 1))
    if conv_ctx is None:
        conv_ctx = jnp.zeros((DEPTH, nseq, CONV_W - 1, 2 * D_FF), F32)
    if ssm_h0_re is None:
        ssm_h0_re = jnp.zeros((N_A_LAYERS, nseq, SSM_GROUPS, SSM_STATE), F32)
        ssm_h0_im = ssm_h0_re
    h = x
    ssm_re_out, ssm_im_out, conv_out = [], [], []
    make_ffn_bf = ffn_bf is None
    if make_ffn_bf:
        ffn_bf = []
    for layer in range(DEPTH):
        to_cast = (w["w_mix_out_f32"], w["w_ffn_in_f32"], w["w_ffn_out_f32"]) if make_ffn_bf else ()
        z_mix, mem_out, cast = _mixin(h, w, layer, mem_k_bf, mem_v_bf, cfg["tm_mixin"], to_cast)
        if make_ffn_bf:
            ffn_bf.append(cast)
        if layer < N_A_LAYERS:
            i = layer
            mix_out, s_re, s_im = _s5_mixer(
                z_mix, ssm_h0_re[i].reshape(nseq, SSM_COLS), ssm_h0_im[i].reshape(nseq, SSM_COLS),
                w["lam_re"][i], w["lam_im"][i], w["s5_in"][i], w["s5_out"][i],
                w["ssm_d"][i][None], w["w_glu"][i], w["b_glu"][i][None], cfg["t_chunk"])
            ssm_re_out.append(s_re.reshape(nseq, SSM_GROUPS, SSM_STATE))
            ssm_im_out.append(s_im.reshape(nseq, SSM_GROUPS, SSM_STATE))
        else:
            if layer == N_A_LAYERS:
                up = None if cfg["absorbed"] else (w["w_uk"], w["w_uv"], w["seg_nope"], w["k_nope_g"])
                new_latent, new_krope, *kv_heads = _shared_kv_down(
                    h, w["kv_norm_g"], w["w_dkv"], w["latent_norm_g"], w["krope_norm_g"],
                    cos_rows, sin_rows, cfg["tm_rows"], up)
                if cfg["absorbed"]:
                    assert past % LANES == 0
                    lk_valid = past + length
                    new_pad = -(-length // LANES) * LANES
                    pad_rows = lambda t: jnp.pad(t.astype(BF16), ((0, 0), (0, new_pad - length), (0, 0)))
                    new_lat_pad = pad_rows(new_latent)
                    new_kr_pad_t = jnp.swapaxes(pad_rows(new_krope), 1, 2)
                    past_krope_t = jnp.swapaxes(past_krope, 1, 2)
                    r_t = _key_norms(past_latent, new_lat_pad, w["w_uk_t"], cfg["knorm_chunk"])
                else:
                    assert past_latent is None
                    kn, v_all = kv_heads
                    kr4 = jnp.tile(new_krope.astype(BF16), (1, 1, LANES // ROPE_DIM))
            j = layer - N_A_LAYERS
            q_out = _q_side(z_mix, w["q_latent_norm_g"][j], w["w_uq"][j], w["q_nope_norm_g"][j],
                            w["q_rope_norm_g"][j], w["seg_nope"], w["seg_rope"], cos_rows, sin_rows,
                            w["k_nope_g"], w["w_uk"], cfg["tm_rows"], cfg["absorbed"])
            if cfg["absorbed"]:
                mix_out = _mla_attention_absorbed(q_out[2], q_out[3], past_latent, new_lat_pad, past_krope_t,
                                                  new_kr_pad_t, r_t, w["w_uv"], length, lk_valid)
            else:
                qn = q_out[0].reshape(nseq, length, MLA_HEADS * NOPE_DIM)
                qr = q_out[1].reshape(nseq, length, MLA_HEADS * ROPE_DIM)
                mix_out = _mla_attention(qn, qr, kn, kr4, v_all, cfg["tq"])
        h, ctx = _mixout_ffn(h, mix_out, mem_out, w, layer, ffn_bf[layer], conv_ctx[layer], cfg["ffn_rows"])
        conv_out.append(ctx)
    return (h, new_latent, new_krope, jnp.stack(ssm_re_out), jnp.stack(ssm_im_out), jnp.stack(conv_out)), ffn_bf


PROMPT_CFG = dict(tm_mixin=1024, t_chunk=64, ffn_rows=512, tm_rows=1024, rope_rows=2048,
                  tq=512, absorbed=False)
SAMPLE_CFG = dict(tm_mixin=32, t_chunk=32, ffn_rows=32, tm_rows=512, rope_rows=512, knorm_chunk=1024,
                  absorbed=True)


def kernel(x_prompt, x_sample, cache_mla_latent, cache_mla_krope, cache_mem_k, cache_mem_v, state_ssm_re, state_ssm_im, state_conv, mem_prompt, norm_mix_g, w_mix_in, w_mix_out, norm_ffn_g, w_ffn_in, ffn_conv_w, ffn_conv_b, w_ffn_out, mem_norm_g, w_mem_kv, mem_q_norm_g, mem_k_norm_g, ssm_a_re, ssm_a_im, ssm_log_dt, ssm_b_re, ssm_b_im, ssm_c_re, ssm_c_im, ssm_d, w_glu, b_glu, kv_norm_g, w_dkv, latent_norm_g, krope_norm_g, w_uk, w_uv, k_nope_norm_g, q_latent_norm_g, w_uq, q_nope_norm_g, q_rope_norm_g):
    bf = lambda t: t.astype(BF16)
    seg_mem = _seg_matrix(MEM_WIDTH, MEM_HEAD_DIM)
    lam_re, lam_im, s5_in, s5_out = _s5_prepare(ssm_a_re, ssm_a_im, ssm_log_dt, ssm_b_re, ssm_b_im,
                                                ssm_c_re, ssm_c_im)
    w = dict(
        norm_mix_g=norm_mix_g.reshape(DEPTH, 1, D_MODEL), w_mix_in=w_mix_in,
        norm_ffn_g=norm_ffn_g.reshape(DEPTH, 1, D_MODEL),
        ffn_conv_w=ffn_conv_w, ffn_conv_b=ffn_conv_b.reshape(DEPTH, 1, 2 * D_FF),
        w_mix_out_f32=w_mix_out, w_ffn_in_f32=w_ffn_in, w_ffn_out_f32=w_ffn_out,
        mem_q_g=jnp.tile(mem_q_norm_g, (1, MEM_HEADS)).reshape(DEPTH, 1, MEM_WIDTH),
        seg_mem=seg_mem,
        seg_nope=_seg_matrix(MLA_HEADS * NOPE_DIM, NOPE_DIM),
        seg_rope=_seg_matrix(MLA_HEADS * ROPE_DIM, ROPE_DIM),
        lam_re=lam_re, lam_im=lam_im, s5_in=s5_in, s5_out=s5_out,
        ssm_d=ssm_d, w_glu=w_glu, b_glu=b_glu,
        kv_norm_g=kv_norm_g, w_dkv=bf(w_dkv), latent_norm_g=latent_norm_g, krope_norm_g=krope_norm_g,
        w_uk=bf(w_uk), w_uk_t=bf(w_uk).T, w_uv=bf(w_uv),
        k_nope_g=jnp.tile(k_nope_norm_g.reshape(1, NOPE_DIM), (1, MLA_HEADS)),
        q_latent_norm_g=q_latent_norm_g, w_uq=bf(w_uq), q_nope_norm_g=q_nope_norm_g,
        q_rope_norm_g=q_rope_norm_g,
    )
    bsz = mem_prompt.shape[0]
    mem_k_p, mem_v_p, mem_k_bf, mem_v_bf = _memory_kv(mem_prompt, mem_norm_g, w_mem_kv, mem_k_norm_g, seg_mem)
    (y_prompt, lat_p, krope_p, ssm_re_p, ssm_im_p, conv_p), ffn_bf = _trunk(
        x_prompt, mem_k_bf, mem_v_bf, None, None, None, None, None, w, PROMPT_CFG)
    dec = cache_mem_k.shape[1]
    (y_sample, lat_s, krope_s, ssm_re_s, ssm_im_s, conv_s), _ = _trunk(
        x_sample, bf(cache_mem_k).reshape(DEPTH, dec, N_MEM, MEM_WIDTH),
        bf(cache_mem_v).reshape(DEPTH, dec, N_MEM, MEM_WIDTH),
        state_ssm_re, state_ssm_im, state_conv, cache_mla_latent, cache_mla_krope, w, SAMPLE_CFG, ffn_bf)
    shape5 = (DEPTH, bsz, N_MEM, MEM_HEADS, MEM_HEAD_DIM)
    return (y_prompt, y_sample, mem_k_p.reshape(shape5), mem_v_p.reshape(shape5), lat_p, krope_p,
            ssm_re_p, ssm_im_p, conv_p, lat_s, krope_s, ssm_re_s, ssm_im_s, conv_s)
```
